```python
import math
import jax
import jax.numpy as jnp
from jax import lax
import numpy as np

D_MODEL = 1024
BATCH = 8
SEQ = 4096
DEPTH = 2

GRID_W = 64
CTX_LEN = 256
HEAD_DIM = 64
EPS = 1e-6
ATTN_HEADS = 8
ATTN_KV_HEADS = 2
ATTN_GROUP = ATTN_HEADS // ATTN_KV_HEADS
ATTN_WIDTH = ATTN_HEADS * HEAD_DIM
ATTN_KV_WIDTH = ATTN_KV_HEADS * HEAD_DIM
Q_BLOCK = 128
ROPE_THETA = 10000.0
HYENA_WIDTH = 256
HYENA_ORDER = 2
HYENA_SHORT = 3
HYENA_BANDS = 16
HYENA_POS_DIM = 1 + 2 * HYENA_BANDS
HYENA_FILTER_HIDDEN = 64
HYENA_DECAY_TARGET = 1e-2
HYENA_FAST_DECAY = 0.3
HYENA_SLOW_DECAY = 1.5
RWKV_HEADS = 4
RWKV_WIDTH = RWKV_HEADS * HEAD_DIM
RWKV_DECAY_RANK = 64
RWKV_ICLR_RANK = 64
RWKV_GATE_RANK = 128
RWKV_GN_EPS = 64e-5
RWKV_SPLITS = (RWKV_WIDTH, RWKV_WIDTH, RWKV_WIDTH, 2 * RWKV_DECAY_RANK, 2 * RWKV_ICLR_RANK, RWKV_GATE_RANK)
RWKV_PROJ = 3 * RWKV_WIDTH + 2 * RWKV_DECAY_RANK + 2 * RWKV_ICLR_RANK + RWKV_GATE_RANK
LRU_WIDTH = 256
LRU_BLOCKS = 4
LRU_BLOCK = LRU_WIDTH // LRU_BLOCKS
LRU_CONV = 4
LRU_C = 8.0
N_BRANCH = 4
IN_SPLITS = (ATTN_WIDTH, ATTN_KV_WIDTH, ATTN_KV_WIDTH, 3 * HYENA_WIDTH, RWKV_PROJ, 2 * LRU_WIDTH, N_BRANCH * D_MODEL)
N_IN = ATTN_WIDTH + 2 * ATTN_KV_WIDTH + 3 * HYENA_WIDTH + RWKV_PROJ + 2 * LRU_WIDTH + N_BRANCH * D_MODEL
N_GROUPS = 4
EXPERTS_PER_GROUP = 4
N_EXPERTS = N_GROUPS * EXPERTS_PER_GROUP
TOP_K_INNER = 2
EXPERT_HIDDEN = 512

kernel_name = 'hybrid_gated_mixers_hmoe_dit'


def _rms(x, eps=EPS):
    xf = x.astype(jnp.float32)
    return (xf * lax.rsqrt(jnp.mean(xf * xf, -1, keepdims=True) + eps)).astype(x.dtype)


def _modulate(x, shift, scale):
    return _rms(x) * (1 + scale) + shift


def _split_cols(p, sizes):
    idx = np.cumsum(sizes)[:-1].tolist()
    return jnp.split(p, idx, axis=-1)


def _dwconv(x, w, b, pad_left):
    k = w.shape[0]
    y = lax.conv_general_dilated(x, w[:, None, :].astype(x.dtype), window_strides=(1,),
                                 padding=[(pad_left, k - 1 - pad_left)],
                                 dimension_numbers=('NWC', 'WIO', 'NWC'),
                                 feature_group_count=x.shape[-1])
    return y + b


def _axial_rope_tables(n_tokens):
    rows = n_tokens // GRID_W
    row = jnp.repeat(jnp.arange(rows), GRID_W)
    col = jnp.tile(jnp.arange(GRID_W), rows)
    n_freq = HEAD_DIM // 4
    freqs = ROPE_THETA ** (-jnp.arange(n_freq, dtype=jnp.float32) / n_freq)
    pos = jnp.stack([row, col], -1).astype(jnp.float32)
    ang = pos[..., None] * freqs
    return jnp.cos(ang), jnp.sin(ang)


def _apply_rope(x, cos, sin):
    shp = x.shape
    xf = x.astype(jnp.float32).reshape(shp[0], shp[1], -1, 2, 2, HEAD_DIM // 4)
    x1, x2 = xf[..., 0, :], xf[..., 1, :]
    cs, sn = cos[None, :, None], sin[None, :, None]
    out = jnp.stack([x1 * cs - x2 * sn, x1 * sn + x2 * cs], axis=-2)
    return out.reshape(shp).astype(x.dtype)


def _qkv_heads(p_q, p_k, p_v, q_gain, k_gain):
    b, l, _ = p_q.shape
    q = _rms(p_q.reshape(b, l, ATTN_KV_HEADS, ATTN_GROUP, HEAD_DIM)) * q_gain
    k = _rms(p_k.reshape(b, l, ATTN_KV_HEADS, HEAD_DIM)) * k_gain
    v = p_v.reshape(b, l, ATTN_KV_HEADS, HEAD_DIM)
    return q, k, v


def _attend(q, k, v):
    s = jnp.einsum('bqhgd,bkhd->bhgqk', q, k).astype(jnp.float32) * (HEAD_DIM ** -0.5)
    p = jax.nn.softmax(s, axis=-1).astype(v.dtype)
    o = jnp.einsum('bhgqk,bkhd->bqhgd', p, v)
    return o.reshape(o.shape[0], o.shape[1], -1)


def _attention_mixer(pc, px, q_gain, k_gain, need_ctx):
    qc, kc, vc = _qkv_heads(pc[0], pc[1], pc[2], q_gain, k_gain)
    qx, kx, vx = _qkv_heads(px[0], px[1], px[2], q_gain, k_gain)
    b, l = qx.shape[:2]
    cos, sin = _axial_rope_tables(l)
    qx = _apply_rope(qx, cos, sin)
    kx = _apply_rope(kx, cos, sin)
    k_all = jnp.concatenate([kc, kx], axis=1)
    v_all = jnp.concatenate([vc, vx], axis=1)
    n_blk = l // Q_BLOCK
    qb = jnp.moveaxis(qx.reshape(b, n_blk, Q_BLOCK, ATTN_KV_HEADS, ATTN_GROUP, HEAD_DIM), 1, 0)
    yb = lax.map(lambda q_blk: _attend(q_blk, k_all, v_all), qb)
    yx = jnp.moveaxis(yb, 0, 1).reshape(b, l, ATTN_WIDTH)
    yc = _attend(qc, kc, vc) if need_ctx else None
    return yc, yx


def _hyena_filters(n, f1, fb1, f2, fb2, f3):
    t = jnp.arange(n, dtype=jnp.float32) / n
    bands = jnp.arange(1, HYENA_BANDS + 1, dtype=jnp.float32)
    ang = 2.0 * math.pi * t[:, None] * bands
    feat = jnp.concatenate([t[:, None], jnp.sin(ang), jnp.cos(ang)], axis=-1)
    h = jnp.sin(feat @ f1.astype(jnp.float32) + fb1.astype(jnp.float32))
    h = jnp.sin(h @ f2.astype(jnp.float32) + fb2.astype(jnp.float32))
    h = (h @ f3.astype(jnp.float32)).reshape(n, HYENA_ORDER, 2, HYENA_WIDTH)
    deltas = jnp.linspace(-math.log(HYENA_DECAY_TARGET) / HYENA_SLOW_DECAY,
                          -math.log(HYENA_DECAY_TARGET) / HYENA_FAST_DECAY, HYENA_WIDTH, dtype=jnp.float32)
    h = h * jnp.exp(-t[:, None] * deltas)[:, None, None, :]
    return h / jnp.sum(jnp.abs(h), axis=(0, 2), keepdims=True)


def _two_sided_longconv(z, h_fwd, h_bwd, skip):
    n = z.shape[1]
    filt2l = jnp.concatenate([h_fwd, jnp.zeros_like(h_fwd[:1]), h_bwd[:0:-1]], axis=0)
    zf = jnp.fft.rfft(z.astype(jnp.float32), n=2 * n, axis=1)
    hf = jnp.fft.rfft(filt2l, n=2 * n, axis=0)
    y = jnp.fft.irfft(zf * hf[None], n=2 * n, axis=1)[:, :n]
    return (y + z.astype(jnp.float32) * skip.astype(jnp.float32)).astype(z.dtype)


def _hyena_mixer(pc, px, conv_w, conv_b, f1, fb1, f2, fb2, f3, skip, need_ctx):
    def run(p):
        z = _dwconv(p, conv_w, conv_b, HYENA_SHORT // 2)
        v, x1, x2 = jnp.split(z, 3, axis=-1)
        h = _hyena_filters(p.shape[1], f1, fb1, f2, fb2, f3)
        y = v
        for o, gate in enumerate((x1, x2)):
            y = gate * _two_sided_longconv(y, h[:, o, 0], h[:, o, 1], skip[o])
        return y
    return (run(pc) if need_ctx else None), run(px)


def _shift_mix(p, mu):
    prev = jnp.pad(p[:, :-1], ((0, 0), (1, 0), (0, 0)))
    nxt = jnp.pad(p[:, 1:], ((0, 0), (0, 1), (0, 0)))
    return p + (prev - p) * mu[0] + (nxt - p) * mu[1]


def _rwkv_heads(t):
    return t.reshape(t.shape[:-1] + (RWKV_HEADS, HEAD_DIM))


def _rwkv_inputs(p, mu, w0, w2, a0, a2, g2, k_k, k_a):
    b, l, _ = p.shape
    r, k, v, w1, a1, g1 = _split_cols(_shift_mix(p, mu), RWKV_SPLITS)
    w1 = w1.reshape(b, l, 2, RWKV_DECAY_RANK)
    a1 = a1.reshape(b, l, 2, RWKV_ICLR_RANK)
    w = -jax.nn.softplus(-(w0 + jnp.einsum('bldr,drc->bldc', jnp.tanh(w1), w2))) - 0.5
    decay = _rwkv_heads(jnp.exp(-jnp.exp(w.astype(jnp.float32))))
    a = _rwkv_heads(jax.nn.sigmoid(a0 + jnp.einsum('bldr,drc->bldc', a1, a2)))
    g = jax.nn.sigmoid(g1) @ g2
    kk = _rwkv_heads(k * k_k).astype(jnp.float32)
    kk = kk * lax.rsqrt(jnp.sum(kk * kk, -1, keepdims=True) + 1e-12)
    k_dir = _rwkv_heads(k)[:, :, None] * (1 + (a - 1) * _rwkv_heads(k_a))
    return _rwkv_heads(r), _rwkv_heads(v), kk, g, decay, a, k_dir


def _rwkv_scan(s0, r, decay, k, v, kk, a, reverse, emit):
    def step(s, inp):
        r_t, w_t, k_t, v_t, kk_t, a_t = inp
        s = (s * w_t[:, :, None, :]
             - jnp.einsum('bhvk,bhk->bhv', s, kk_t)[..., None] * (kk_t * a_t)[:, :, None, :]
             + v_t[..., None] * k_t[:, :, None, :])
        return s, (jnp.einsum('bhvk,bhk->bhv', s, r_t) if emit else None)
    xs = tuple(jnp.moveaxis(t.astype(jnp.float32), 1, 0) for t in (r, decay, k, v, kk, a))
    s_fin, ys = lax.scan(step, s0, xs, reverse=reverse)
    return s_fin, (jnp.moveaxis(ys, 0, 1) if emit else None)


def _rwkv_readout(y, r, k_dir, v, g, r_k, ln_w, ln_b):
    b, l = y.shape[:2]
    mu = jnp.mean(y, -1, keepdims=True)
    var = jnp.mean(jnp.square(y - mu), -1, keepdims=True)
    yn = ((y - mu) * lax.rsqrt(var + RWKV_GN_EPS)).reshape(b, l, RWKV_WIDTH) * ln_w + ln_b
    bonus = jnp.sum(jnp.sum(r[:, :, None] * k_dir * r_k, -1, keepdims=True) * v[:, :, None], axis=2)
    return ((yn + bonus.reshape(b, l, RWKV_WIDTH)) * g).astype(g.dtype)


def _rwkv_mixer(pc, px, mu, w0, w2, a0, a2, g2, k_k, k_a, r_k, ln_w, ln_b, need_ctx):
    rc, vc, kkc, gc, dc, ac, kdc = _rwkv_inputs(pc, mu, w0, w2, a0, a2, g2, k_k, k_a)
    rx, vx, kkx, gx, dx, ax, kdx = _rwkv_inputs(px, mu, w0, w2, a0, a2, g2, k_k, k_a)
    s0 = jnp.zeros((px.shape[0], RWKV_HEADS, HEAD_DIM, HEAD_DIM), jnp.float32)
    yc = 0.0
    yx = 0.0
    for d, rev in enumerate((False, True)):
        s_ctx, yc_d = _rwkv_scan(s0, rc, dc[:, :, d], kdc[:, :, d], vc, kkc, ac[:, :, d], rev, need_ctx)
        _, yx_d = _rwkv_scan(s_ctx, rx, dx[:, :, d], kdx[:, :, d], vx, kkx, ax[:, :, d], rev, True)
        yx = yx + yx_d
        if need_ctx:
            yc = yc + yc_d
    out_x = _rwkv_readout(yx, rx, kdx, vx, gx, r_k, ln_w, ln_b)
    out_c = _rwkv_readout(yc, rc, kdc, vc, gc, r_k, ln_w, ln_b) if need_ctx else None
    return out_c, out_x


def _blockdiag(x, w, b):
    bsz, l, _ = x.shape
    y = jnp.einsum('blnc,ncd->blnd', x.reshape(bsz, l, LRU_BLOCKS, LRU_BLOCK), w)
    return y.reshape(bsz, l, LRU_WIDTH) + b


def _lru_coeffs(xc, wa, ba, wx, bx, lam):
    r = jax.nn.sigmoid(_blockdiag(xc, wa, ba)).astype(jnp.float32)
    i = jax.nn.sigmoid(_blockdiag(xc, wx, bx)).astype(jnp.float32)
    log_a = -LRU_C * r * jax.nn.softplus(-lam.astype(jnp.float32))
    a = jnp.exp(log_a)
    b = jnp.sqrt(-jnp.expm1(2.0 * log_a)) * (i * xc.astype(jnp.float32))
    return a, b


def _linear_scan(a, b, h0, reverse):
    def combine(e1, e2):
        a1, b1 = e1
        a2, b2 = e2
        return a1 * a2, a2 * b1 + b2
    a_cum, b_cum = lax.associative_scan(combine, (a, b), axis=1, reverse=reverse)
    return a_cum * h0[:, None] + b_cum


def _lru_mixer(pc, px, conv_w, conv_b, wa, ba, wx, bx, lam, need_ctx):
    gate_c, xin_c = jnp.split(pc, 2, axis=-1)
    gate_x, xin_x = jnp.split(px, 2, axis=-1)
    pad = (LRU_CONV - 1) // 2
    xc_c = _dwconv(xin_c, conv_w, conv_b, pad)
    xc_x = _dwconv(xin_x, conv_w, conv_b, pad)
    h_init = jnp.zeros((px.shape[0], LRU_WIDTH), jnp.float32)
    yc = 0.0
    yx = 0.0
    for d, rev in enumerate((False, True)):
        a_c, b_c = _lru_coeffs(xc_c, wa[d], ba[d], wx[d], bx[d], lam[d])
        h_c = _linear_scan(a_c, b_c, h_init, rev)
        a_x, b_x = _lru_coeffs(xc_x, wa[d], ba[d], wx[d], bx[d], lam[d])
        h_x = _linear_scan(a_x, b_x, h_c[:, 0] if rev else h_c[:, -1], rev)
        yx = yx + h_x
        if need_ctx:
            yc = yc + h_c
    out_x = (yx * jax.nn.gelu(gate_x.astype(jnp.float32))).astype(px.dtype)
    out_c = (yc * jax.nn.gelu(gate_c.astype(jnp.float32))).astype(pc.dtype) if need_ctx else None
    return out_c, out_x


def _merge(gates, ys, w_brs, w_out):
    b, l, _ = gates.shape
    g = jax.nn.sigmoid(gates.reshape(b, l, N_BRANCH, D_MODEL))
    m = g[:, :, 0] * (ys[0] @ w_brs[0])
    for i in range(1, N_BRANCH):
        m = m + g[:, :, i] * (ys[i] @ w_brs[i])
    return m @ w_out


def _hier_moe(h, w_grp, b_grp, w_rt, b_rt, w1, w3, w2):
    shp = h.shape
    t = h.reshape(-1, D_MODEL)
    gp = jax.nn.softmax((t @ w_grp + b_grp).astype(jnp.float32), axis=-1)
    g_val, g_idx = lax.top_k(gp, 1)
    el = (t @ w_rt + b_rt).astype(jnp.float32).reshape(-1, N_GROUPS, EXPERTS_PER_GROUP)
    el = jnp.take_along_axis(el, g_idx[:, :, None], axis=1)[:, 0]
    e_val, e_idx = lax.top_k(jax.nn.softmax(el, axis=-1), TOP_K_INNER)
    wts = g_val * e_val / jnp.sum(e_val, -1, keepdims=True)
    e_glob = g_idx * EXPERTS_PER_GROUP + e_idx
    combine = jnp.sum(jax.nn.one_hot(e_glob, N_EXPERTS, dtype=jnp.float32) * wts[..., None], axis=1)
    combine = combine.astype(t.dtype)
    out = jnp.zeros_like(t)
    for e in range(N_EXPERTS):
        y = (jax.nn.silu(t @ w1[e]) * (t @ w3[e])) @ w2[e]
        out = out + combine[:, e:e + 1] * y
    return out.reshape(shp)


def setup_inputs(seed: int = 0) -> dict:
    key = jax.random.key(seed)
    keys = iter(jax.random.split(key, 64))
    f32 = jnp.float32

    def nrm(shape, scale):
        return jax.random.normal(next(keys), shape, f32) * scale

    d = D_MODEL
    fh = HYENA_FILTER_HIDDEN
    x = nrm((BATCH, SEQ, d), 1.0)
    c = nrm((BATCH, d), 1.0)
    ctx = nrm((BATCH, CTX_LEN, d), 1.0)
    c_ctx = nrm((d,), 1.0)
    ada_w = nrm((DEPTH, d, 6 * d), d ** -0.5)
    ada_b = nrm((DEPTH, 6 * d), 0.01)
    w_in = nrm((DEPTH, d, N_IN), d ** -0.5)
    q_norm = 1.0 + nrm((DEPTH, HEAD_DIM), 0.02)
    k_norm = 1.0 + nrm((DEPTH, HEAD_DIM), 0.02)
    hy_conv_w = nrm((DEPTH, HYENA_SHORT, 3 * HYENA_WIDTH), HYENA_SHORT ** -0.5)
    hy_conv_b = nrm((DEPTH, 3 * HYENA_WIDTH), 0.01)
    hy_f1 = nrm((DEPTH, HYENA_POS_DIM, fh), HYENA_POS_DIM ** -0.5)
    hy_fb1 = nrm((DEPTH, fh), 0.01)
    hy_f2 = nrm((DEPTH, fh, fh), fh ** -0.5)
    hy_fb2 = nrm((DEPTH, fh), 0.01)
    hy_f3 = nrm((DEPTH, fh, HYENA_ORDER * 2 * HYENA_WIDTH), fh ** -0.5)
    hy_skip = nrm((DEPTH, HYENA_ORDER, HYENA_WIDTH), 1.0)
    rw_mu = jax.random.uniform(next(keys), (DEPTH, 2, RWKV_PROJ), f32, 0.0, 0.5)
    n_pos = jnp.arange(RWKV_WIDTH, dtype=f32) / (RWKV_WIDTH - 1)
    ratio = jnp.arange(DEPTH, dtype=f32) / max(DEPTH - 1, 1)
    w0_sched = -7.0 + 5.0 * n_pos[None] ** (0.85 + ratio[:, None] ** 0.5) + 0.5
    rw_w0 = w0_sched[:, None, :] + nrm((DEPTH, 2, RWKV_WIDTH), 0.1)
    rw_w2 = nrm((DEPTH, 2, RWKV_DECAY_RANK, RWKV_WIDTH), 0.1 * RWKV_DECAY_RANK ** -0.5)
    rw_a0 = nrm((DEPTH, 2, RWKV_WIDTH), 0.1)
    rw_a2 = nrm((DEPTH, 2, RWKV_ICLR_RANK, RWKV_WIDTH), 0.1 * RWKV_ICLR_RANK ** -0.5)
    rw_g2 = nrm((DEPTH, RWKV_GATE_RANK, RWKV_WIDTH), RWKV_GATE_RANK ** -0.5)
    rw_k_k = 0.85 + nrm((DEPTH, RWKV_WIDTH), 0.02)
    rw_k_a = 1.0 + nrm((DEPTH, RWKV_WIDTH), 0.02)
    rw_r_k = nrm((DEPTH, RWKV_HEADS, HEAD_DIM), 0.1)
    rw_ln_w = 1.0 + nrm((DEPTH, RWKV_WIDTH), 0.02)
    rw_ln_b = nrm((DEPTH, RWKV_WIDTH), 0.01)
    lru_conv_w = nrm((DEPTH, LRU_CONV, LRU_WIDTH), LRU_CONV ** -0.5)
    lru_conv_b = nrm((DEPTH, LRU_WIDTH), 0.01)
    lru_wa = nrm((DEPTH, 2, LRU_BLOCKS, LRU_BLOCK, LRU_BLOCK), LRU_BLOCK ** -0.5)
    lru_ba = nrm((DEPTH, 2, LRU_WIDTH), 0.01)
    lru_wx = nrm((DEPTH, 2, LRU_BLOCKS, LRU_BLOCK, LRU_BLOCK), LRU_BLOCK ** -0.5)
    lru_bx = nrm((DEPTH, 2, LRU_WIDTH), 0.01)
    u = jax.random.uniform(next(keys), (DEPTH, 2, LRU_WIDTH), f32, 0.9, 0.999)
    sig = u ** (1.0 / LRU_C)
    lru_lambda = jnp.log(sig) - jnp.log1p(-sig)
    w_br_attn = nrm((DEPTH, ATTN_WIDTH, d), ATTN_WIDTH ** -0.5)
    w_br_hyena = nrm((DEPTH, HYENA_WIDTH, d), HYENA_WIDTH ** -0.5)
    w_br_rwkv = nrm((DEPTH, RWKV_WIDTH, d), RWKV_WIDTH ** -0.5)
    w_br_lru = nrm((DEPTH, LRU_WIDTH, d), LRU_WIDTH ** -0.5)
    w_out = nrm((DEPTH, d, d), d ** -0.5)
    moe_w_grp = nrm((DEPTH, d, N_GROUPS), d ** -0.5)
    moe_b_grp = nrm((DEPTH, N_GROUPS), 0.01)
    moe_w_rt = nrm((DEPTH, d, N_EXPERTS), d ** -0.5)
    moe_b_rt = nrm((DEPTH, N_EXPERTS), 0.01)
    moe_w1 = nrm((DEPTH, N_EXPERTS, d, EXPERT_HIDDEN), d ** -0.5)
    moe_w3 = nrm((DEPTH, N_EXPERTS, d, EXPERT_HIDDEN), d ** -0.5)
    moe_w2 = nrm((DEPTH, N_EXPERTS, EXPERT_HIDDEN, d), EXPERT_HIDDEN ** -0.5)
    return {'x': x, 'c': c, 'ctx': ctx, 'c_ctx': c_ctx, 'ada_w': ada_w, 'ada_b': ada_b, 'w_in': w_in,
            'q_norm': q_norm, 'k_norm': k_norm, 'hy_conv_w': hy_conv_w, 'hy_conv_b': hy_conv_b,
            'hy_f1': hy_f1, 'hy_fb1': hy_fb1, 'hy_f2': hy_f2, 'hy_fb2': hy_fb2, 'hy_f3': hy_f3,
            'hy_skip': hy_skip, 'rw_mu': rw_mu, 'rw_w0': rw_w0, 'rw_w2': rw_w2, 'rw_a0': rw_a0,
            'rw_a2': rw_a2, 'rw_g2': rw_g2, 'rw_k_k': rw_k_k, 'rw_k_a': rw_k_a, 'rw_r_k': rw_r_k,
            'rw_ln_w': rw_ln_w, 'rw_ln_b': rw_ln_b, 'lru_conv_w': lru_conv_w, 'lru_conv_b': lru_conv_b,
            'lru_wa': lru_wa, 'lru_ba': lru_ba, 'lru_wx': lru_wx, 'lru_bx': lru_bx, 'lru_lambda': lru_lambda,
            'w_br_attn': w_br_attn, 'w_br_hyena': w_br_hyena, 'w_br_rwkv': w_br_rwkv, 'w_br_lru': w_br_lru,
            'w_out': w_out, 'moe_w_grp': moe_w_grp, 'moe_b_grp': moe_b_grp, 'moe_w_rt': moe_w_rt,
            'moe_b_rt': moe_b_rt, 'moe_w1': moe_w1, 'moe_w3': moe_w3, 'moe_w2': moe_w2}


def reference(x, c, ctx, c_ctx, ada_w, ada_b, w_in, q_norm, k_norm, hy_conv_w, hy_conv_b,
              hy_f1, hy_fb1, hy_f2, hy_fb2, hy_f3, hy_skip, rw_mu, rw_w0, rw_w2, rw_a0, rw_a2,
              rw_g2, rw_k_k, rw_k_a, rw_r_k, rw_ln_w, rw_ln_b, lru_conv_w, lru_conv_b, lru_wa,
              lru_ba, lru_wx, lru_bx, lru_lambda, w_br_attn, w_br_hyena, w_br_rwkv, w_br_lru,
              w_out, moe_w_grp, moe_b_grp, moe_w_rt, moe_b_rt, moe_w1, moe_w3, moe_w2):
    for i in range(DEPTH):
        need_ctx = i < DEPTH - 1
        mod_x = (jax.nn.silu(c) @ ada_w[i] + ada_b[i])[:, None, :]
        mod_c = (jax.nn.silu(c_ctx) @ ada_w[i] + ada_b[i])[None, None, :]
        sh1x, sc1x, g1x, sh2x, sc2x, g2x = jnp.split(mod_x, 6, axis=-1)
        sh1c, sc1c, g1c, sh2c, sc2c, g2c = jnp.split(mod_c, 6, axis=-1)

        hx = _modulate(x, sh1x, sc1x)
        hc = _modulate(ctx, sh1c, sc1c)
        qx, kx, vx, hyx, rwx, lrx, gtx = _split_cols(hx @ w_in[i], IN_SPLITS)
        qc, kc, vc, hyc, rwc, lrc, gtc = _split_cols(hc @ w_in[i], IN_SPLITS)

        att_c, att_x = _attention_mixer((qc, kc, vc), (qx, kx, vx), q_norm[i], k_norm[i], need_ctx)
        hy_c, hy_x = _hyena_mixer(hyc, hyx, hy_conv_w[i], hy_conv_b[i], hy_f1[i], hy_fb1[i],
                                  hy_f2[i], hy_fb2[i], hy_f3[i], hy_skip[i], need_ctx)
        rw_c, rw_x = _rwkv_mixer(rwc, rwx, rw_mu[i], rw_w0[i], rw_w2[i], rw_a0[i], rw_a2[i], rw_g2[i],
                                 rw_k_k[i], rw_k_a[i], rw_r_k[i], rw_ln_w[i], rw_ln_b[i], need_ctx)
        lr_c, lr_x = _lru_mixer(lrc, lrx, lru_conv_w[i], lru_conv_b[i], lru_wa[i], lru_ba[i],
                                lru_wx[i], lru_bx[i], lru_lambda[i], need_ctx)
        w_brs = (w_br_attn[i], w_br_hyena[i], w_br_rwkv[i], w_br_lru[i])
        x = x + g1x * _merge(gtx, (att_x, hy_x, rw_x, lr_x), w_brs, w_out[i])

        moe_args = (moe_w_grp[i], moe_b_grp[i], moe_w_rt[i], moe_b_rt[i], moe_w1[i], moe_w3[i], moe_w2[i])
        if need_ctx:
            ctx = ctx + g1c * _merge(gtc, (att_c, hy_c, rw_c, lr_c), w_brs, w_out[i])
            n_ctx = ctx.shape[1]
            h2 = jnp.concatenate([_modulate(ctx, sh2c, sc2c), _modulate(x, sh2x, sc2x)], axis=1)
            f2 = _hier_moe(h2, *moe_args)
            ctx = ctx + g2c * f2[:, :n_ctx]
            x = x + g2x * f2[:, n_ctx:]
        else:
            x = x + g2x * _hier_moe(_modulate(x, sh2x, sc2x), *moe_args)
    return x
```

```python
import functools
import math

import numpy as np
import jax
import jax.numpy as jnp
from jax import lax
from jax.experimental import pallas as pl
from jax.experimental.pallas import tpu as pltpu

F32 = jnp.float32
BF16 = jnp.bfloat16

HEAD_DIM = 64
GRID_W = 64
EPS = 1e-6
ATTN_HEADS = 8
ATTN_KV_HEADS = 2
ATTN_GROUP = ATTN_HEADS // ATTN_KV_HEADS
ATTN_WIDTH = ATTN_HEADS * HEAD_DIM
ATTN_KV_WIDTH = ATTN_KV_HEADS * HEAD_DIM
ROPE_THETA = 10000.0
HYENA_WIDTH = 256
HYENA_ORDER = 2
HYENA_BANDS = 16
HYENA_DECAY_TARGET = 1e-2
HYENA_FAST_DECAY = 0.3
HYENA_SLOW_DECAY = 1.5
RWKV_HEADS = 4
RWKV_WIDTH = RWKV_HEADS * HEAD_DIM
RWKV_DECAY_RANK = 64
RWKV_ICLR_RANK = 64
RWKV_GATE_RANK = 128
RWKV_GN_EPS = 64e-5
RWKV_PROJ = 3 * RWKV_WIDTH + 2 * RWKV_DECAY_RANK + 2 * RWKV_ICLR_RANK + RWKV_GATE_RANK
RWKV_CHUNK = 64
RWKV_INV_BASE = 16
LRU_WIDTH = 256
LRU_BLOCKS = 4
LRU_C = 8.0
N_BRANCH = 4
N_GROUPS = 4
EXPERTS_PER_GROUP = 4
N_EXPERTS = N_GROUPS * EXPERTS_PER_GROUP

V7X_VMEM_LIMIT_BYTES = 52 * 1024 * 1024
SUBLANES = 8
LANES = 128
MOD_ROWS = 16


def _cparams(*sem):
    return pltpu.CompilerParams(dimension_semantics=sem, vmem_limit_bytes=V7X_VMEM_LIMIT_BYTES)


def _dot(a, b):
    return jnp.dot(a, b, preferred_element_type=F32)


def _dot_nt(a, b):
    return lax.dot_general(a, b, (((1,), (1,)), ((), ())), preferred_element_type=F32)


def _split3(x):
    hi = x.astype(BF16)
    r1 = x - hi.astype(F32)
    mid = r1.astype(BF16)
    lo = (r1 - mid.astype(F32)).astype(BF16)
    return hi, mid, lo


def _dot_exact_lhs(m_bf16, x):
    hi, mid, lo = _split3(x)
    return _dot(m_bf16, hi) + _dot(m_bf16, mid) + _dot(m_bf16, lo)


def _dot_exact_rhs(x, m_bf16):
    hi, mid, lo = _split3(x)
    return _dot(hi, m_bf16) + _dot(mid, m_bf16) + _dot(lo, m_bf16)


def _dot_split(a, b):
    ah = a.astype(BF16)
    al = (a - ah.astype(F32)).astype(BF16)
    bh = b.astype(BF16)
    bl = (b - bh.astype(F32)).astype(BF16)
    return _dot(ah, bh) + (_dot(ah, bl) + _dot(al, bh))


def _sigmoid(x):
    return 1.0 / (1.0 + jnp.exp(-x))


def _softplus(x):
    return jnp.maximum(x, 0.0) + jnp.log(1.0 + jnp.exp(-jnp.abs(x)))


def _silu(x):
    return x * _sigmoid(x)


def _largest_tile(n, cap, mult):
    best = None
    for t in range(mult, min(n, cap) + 1, mult):
        if n % t == 0:
            best = t
    assert best is not None, (n, cap, mult)
    return best


def _head_ones(width):
    idx = np.arange(width) // HEAD_DIM
    return jnp.asarray((idx[:, None] == idx[None, :]).astype(np.float32), dtype=BF16)


def _row_ids(tile_rows, tile_idx):
    return tile_idx * tile_rows + lax.broadcasted_iota(jnp.int32, (tile_rows, 1), 0)


def _mod_rows(modx_ref, modc_ref, idx, is_ctx):
    return jnp.where(is_ctx, modc_ref[0, idx:idx + 1, :], modx_ref[0, idx:idx + 1, :])


def _rms_modulate(x, shift, scale):
    ms = jnp.mean(x * x, axis=-1, keepdims=True)
    return (x * lax.rsqrt(ms + EPS)) * (1.0 + scale) + shift


def _ada_kernel(c_ref, w_ref, b_ref, o_ref):
    s = _silu(c_ref[...])
    o_ref[0] = jnp.dot(s, w_ref[0], preferred_element_type=F32, precision=lax.Precision.HIGHEST) + b_ref[0]


def _ada_mod(cc, ada_w, ada_b):
    depth, d, n6 = ada_w.shape
    tn = _largest_tile(n6, 1024, LANES)
    return pl.pallas_call(
        _ada_kernel,
        out_shape=jax.ShapeDtypeStruct((depth, MOD_ROWS, n6), F32),
        grid=(depth, n6 // tn),
        in_specs=[pl.BlockSpec((MOD_ROWS, d), lambda i, j: (0, 0)),
                  pl.BlockSpec((1, d, tn), lambda i, j: (i, 0, j)),
                  pl.BlockSpec((1, 1, tn), lambda i, j: (i, 0, j))],
        out_specs=pl.BlockSpec((1, MOD_ROWS, tn), lambda i, j: (i, 0, j)),
        compiler_params=_cparams("parallel", "parallel"),
        name="ada_mod",
    )(cc, ada_w, ada_b.reshape(depth, 1, n6))


def _modmm_kernel(x_ref, modx_ref, modc_ref, w_ref, o_ref, h_scr, *, cl, tm):
    @pl.when(pl.program_id(2) == 0)
    def _():
        is_ctx = _row_ids(tm, pl.program_id(1)) < cl
        h = _rms_modulate(x_ref[0], _mod_rows(modx_ref, modc_ref, 0, is_ctx), _mod_rows(modx_ref, modc_ref, 1, is_ctx))
        h_scr[...] = h.astype(BF16)

    o_ref[0] = _dot(h_scr[...], w_ref[...]).astype(o_ref.dtype)


def _modmm(xc, mod, w, cl, out_dtype=BF16):
    b, lt, d = xc.shape
    n = w.shape[1]
    tm = _largest_tile(lt, 1088, 16)
    tn = n if n <= 1280 else _largest_tile(n, 1024, 2 * LANES)
    return pl.pallas_call(
        functools.partial(_modmm_kernel, cl=cl, tm=tm),
        out_shape=jax.ShapeDtypeStruct((b, lt, n), out_dtype),
        grid=(b, lt // tm, n // tn),
        in_specs=[pl.BlockSpec((1, tm, d), lambda i, r, j: (i, r, 0)),
                  pl.BlockSpec((1, 6, d), lambda i, r, j: (i, 0, 0)),
                  pl.BlockSpec((1, 6, d), lambda i, r, j: (b, 0, 0)),
                  pl.BlockSpec((d, tn), lambda i, r, j: (0, j))],
        out_specs=pl.BlockSpec((1, tm, tn), lambda i, r, j: (i, r, j)),
        scratch_shapes=[pltpu.VMEM((tm, d), BF16)],
        compiler_params=_cparams("parallel", "parallel", "arbitrary"),
        name="modmm",
    )(xc, mod, mod, w)


def _rope_tables(l, cl):
    n_freq = HEAD_DIM // 4
    t = jnp.arange(l)
    freqs = ROPE_THETA ** (-jnp.arange(n_freq, dtype=F32) / n_freq)
    pos = jnp.stack([t // GRID_W, t % GRID_W], -1).astype(F32)
    ang = pos[..., None] * freqs
    cos64 = jnp.stack([jnp.cos(ang), jnp.cos(ang)], axis=2).reshape(l, HEAD_DIM)
    sin64 = jnp.stack([-jnp.sin(ang), jnp.sin(ang)], axis=2).reshape(l, HEAD_DIM)
    cos64 = jnp.concatenate([jnp.ones((cl, HEAD_DIM), F32), cos64], 0)
    sin64 = jnp.concatenate([jnp.zeros((cl, HEAD_DIM), F32), sin64], 0)
    return jnp.tile(cos64, (1, 2)), jnp.tile(sin64, (1, 2))


def _head_rms(t, ones_ref):
    ms = _dot_exact_rhs(t * t, ones_ref[...]) * (1.0 / HEAD_DIM)
    return t * lax.rsqrt(ms + EPS)


def _rope(t, cos, sin):
    w = t.shape[-1]
    lane = lax.broadcasted_iota(jnp.int32, t.shape, 1)
    q4 = HEAD_DIM // 4
    first_half = (lane % (2 * q4)) < q4
    partner = jnp.where(first_half, pltpu.roll(t, w - q4, 1), pltpu.roll(t, q4, 1))
    return t * cos + partner * sin


def _attn_prep_kernel(p_ref, cos_ref, sin_ref, qg_ref, kg_ref, oq_ref, ok_ref, q_ref, kt_ref):
    p = p_ref[0].astype(F32)
    cos2, sin2 = cos_ref[...], sin_ref[...]
    reps = ATTN_WIDTH // (2 * HEAD_DIM)
    cos_q = jnp.concatenate([cos2] * reps, axis=1)
    sin_q = jnp.concatenate([sin2] * reps, axis=1)
    q = _head_rms(p[:, :ATTN_WIDTH], oq_ref) * qg_ref[...]
    q_ref[0] = _rope(q, cos_q, sin_q).astype(q_ref.dtype)
    k = _head_rms(p[:, ATTN_WIDTH:ATTN_WIDTH + ATTN_KV_WIDTH], ok_ref) * kg_ref[...]
    kt_ref[0] = _rope(k, cos2, sin2).T.astype(kt_ref.dtype)


def _attn_prep(pqkv, cos2, sin2, q_gain, k_gain):
    b, lt, wtot = pqkv.shape
    tr = _largest_tile(lt, 2176, LANES)
    qg = jnp.tile(q_gain * (HEAD_DIM ** -0.5), ATTN_HEADS).reshape(1, ATTN_WIDTH)
    kg = jnp.tile(k_gain, ATTN_KV_HEADS).reshape(1, ATTN_KV_WIDTH)
    return pl.pallas_call(
        _attn_prep_kernel,
        out_shape=(jax.ShapeDtypeStruct((b, lt, ATTN_WIDTH), BF16),
                   jax.ShapeDtypeStruct((b, ATTN_KV_WIDTH, lt), BF16)),
        grid=(b, lt // tr),
        in_specs=[pl.BlockSpec((1, tr, wtot), lambda i, r: (i, r, 0)),
                  pl.BlockSpec((tr, 2 * HEAD_DIM), lambda i, r: (r, 0)),
                  pl.BlockSpec((tr, 2 * HEAD_DIM), lambda i, r: (r, 0)),
                  pl.BlockSpec((1, ATTN_WIDTH), lambda i, r: (0, 0)),
                  pl.BlockSpec((1, ATTN_KV_WIDTH), lambda i, r: (0, 0)),
                  pl.BlockSpec((ATTN_WIDTH, ATTN_WIDTH), lambda i, r: (0, 0)),
                  pl.BlockSpec((ATTN_KV_WIDTH, ATTN_KV_WIDTH), lambda i, r: (0, 0))],
        out_specs=(pl.BlockSpec((1, tr, ATTN_WIDTH), lambda i, r: (i, r, 0)),
                   pl.BlockSpec((1, ATTN_KV_WIDTH, tr), lambda i, r: (i, 0, r))),
        compiler_params=_cparams("parallel", "parallel"),
        name="attn_prep",
    )(pqkv, cos2, sin2, qg, kg, _head_ones(ATTN_WIDTH), _head_ones(ATTN_KV_WIDTH))


def _attn_kernel(q_ref, kt_ref, v_ref, o_ref, *, n_ctx_tiles, cl, lt):
    def run(nk):
        outs = []
        for h in range(ATTN_HEADS):
            kv = h // ATTN_GROUP
            qh = q_ref[0, :, h * HEAD_DIM:(h + 1) * HEAD_DIM]
            kt = kt_ref[0, kv * HEAD_DIM:(kv + 1) * HEAD_DIM, :nk]
            s = _dot(qh, kt)
            m = jnp.max(s, axis=-1, keepdims=True)
            p = jnp.exp(s - m)
            l = jnp.sum(p, axis=-1, keepdims=True)
            o = _dot(p.astype(BF16), v_ref[0, :nk, :])
            outs.append(o[:, kv * HEAD_DIM:(kv + 1) * HEAD_DIM] / l)
        o_ref[0] = jnp.concatenate(outs, axis=-1).astype(o_ref.dtype)

    is_ctx_tile = pl.program_id(1) < n_ctx_tiles

    @pl.when(is_ctx_tile)
    def _():
        run(cl)

    @pl.when(jnp.logical_not(is_ctx_tile))
    def _():
        run(lt)


def _attention(qn, kt, pqkv, cl):
    b, lt, _ = qn.shape
    tq = 256 if (cl % 256 == 0 and lt % 256 == 0) else 128
    assert cl % tq == 0 and lt % tq == 0
    v_blk = (ATTN_WIDTH + ATTN_KV_WIDTH) // ATTN_KV_WIDTH
    return pl.pallas_call(
        functools.partial(_attn_kernel, n_ctx_tiles=cl // tq, cl=cl, lt=lt),
        out_shape=jax.ShapeDtypeStruct((b, lt, ATTN_WIDTH), BF16),
        grid=(b, lt // tq),
        in_specs=[pl.BlockSpec((1, tq, ATTN_WIDTH), lambda i, t: (i, t, 0)),
                  pl.BlockSpec((1, ATTN_KV_WIDTH, lt), lambda i, t: (i, 0, 0)),
                  pl.BlockSpec((1, lt, ATTN_KV_WIDTH), lambda i, t: (i, 0, v_blk))],
        out_specs=pl.BlockSpec((1, tq, ATTN_WIDTH), lambda i, t: (i, t, 0)),
        compiler_params=_cparams("parallel", "parallel"),
        name="attention",
    )(qn, kt, pqkv)


def _halo_specs(tr, lt, width, lead):
    per = tr // SUBLANES
    last = lt // SUBLANES - 1
    nlead = len(lead)

    def prev_map(*ids):
        return (*ids[:nlead], jnp.maximum(ids[nlead] * per - 1, 0), 0)

    def next_map(*ids):
        return (*ids[:nlead], jnp.minimum((ids[nlead] + 1) * per, last), 0)

    blk = (*lead, SUBLANES, width)
    return pl.BlockSpec(blk, prev_map), pl.BlockSpec(blk, next_map)


def _shift_rows(x, prev8, next8, offset, rows, cl, lt):
    tr = x.shape[0]
    local = lax.broadcasted_iota(jnp.int32, (tr, 1), 0)
    if offset < 0:
        y = pltpu.roll(x, -offset, 0)
        y = jnp.where(local == 0, prev8[SUBLANES - 1:SUBLANES, :], y)
        bad = (rows == 0) | (rows == cl)
    else:
        y = pltpu.roll(x, tr - offset, 0)
        for j in range(offset):
            y = jnp.where(local == tr - offset + j, next8[j:j + 1, :], y)
        bad = (rows >= lt - offset) | ((rows >= cl - offset) & (rows < cl))
    return jnp.where(bad, 0.0, y)


def _hyena_pre_kernel(p_ref, pv_ref, nx_ref, w_ref, b_ref, v_ref, x1_ref, x2_ref, *, cl, lt, tr):
    rows = _row_ids(tr, pl.program_id(1))
    p = p_ref[0].astype(F32)
    pm = _shift_rows(p, pv_ref[0].astype(F32), nx_ref[0].astype(F32), -1, rows, cl, lt)
    pp = _shift_rows(p, pv_ref[0].astype(F32), nx_ref[0].astype(F32), 1, rows, cl, lt)
    z = pm * w_ref[0:1, :] + p * w_ref[1:2, :] + pp * w_ref[2:3, :] + b_ref[...]
    c = HYENA_WIDTH
    v_ref[0] = z[:, :c].astype(v_ref.dtype)
    x1_ref[0] = z[:, c:2 * c].astype(x1_ref.dtype)
    x2_ref[0] = z[:, 2 * c:].astype(x2_ref.dtype)


def _hyena_pre(phy, conv_w, conv_b, cl):
    b, lt, w = phy.shape
    tr = _largest_tile(lt, 1088, 16)
    prev_spec, next_spec = _halo_specs(tr, lt, w, (1,))
    out = jax.ShapeDtypeStruct((b, lt, HYENA_WIDTH), BF16)
    ospec = pl.BlockSpec((1, tr, HYENA_WIDTH), lambda i, r: (i, r, 0))
    return pl.pallas_call(
        functools.partial(_hyena_pre_kernel, cl=cl, lt=lt, tr=tr),
        out_shape=(out, out, out),
        grid=(b, lt // tr),
        in_specs=[pl.BlockSpec((1, tr, w), lambda i, r: (i, r, 0)), prev_spec, next_spec,
                  pl.BlockSpec(conv_w.shape, lambda i, r: (0, 0)),
                  pl.BlockSpec((1, w), lambda i, r: (0, 0))],
        out_specs=(ospec, ospec, ospec),
        compiler_params=_cparams("parallel", "parallel"),
        name="hyena_pre",
    )(phy, phy, phy, conv_w, conv_b.reshape(1, w))


def _dft_mats(n):
    k = jnp.arange(n, dtype=jnp.int32)[:, None]
    s = jnp.arange(n, dtype=jnp.int32)[None, :]
    ang = ((k * s) % (2 * n)).astype(F32) * (math.pi / n)
    cm = jnp.cos(ang)
    sn = jnp.sin(ang)
    alt_s = (1 - 2 * (s % 2)).astype(F32)
    alt_k = (1 - 2 * (k % 2)).astype(F32)
    sm = jnp.where(k == 0, alt_s, sn)
    smt = jnp.where(s == 0, alt_k, sn)
    return cm.astype(BF16), sm.astype(BF16), smt.astype(BF16)


def _dft_raw_kernel(cm_ref, sm_ref, z_ref, zr_ref, zi_ref):
    zr_ref[...] = _dot(cm_ref[...], z_ref[...])
    zi_ref[...] = _dot(sm_ref[...], z_ref[...])


def _dft_raw(cm, sm, z):
    n, c = z.shape
    tm = _largest_tile(n, 256, 16)
    mat = pl.BlockSpec((tm, n), lambda j: (j, 0))
    out = jax.ShapeDtypeStruct((n, c), F32)
    ospec = pl.BlockSpec((tm, c), lambda j: (j, 0))
    return pl.pallas_call(
        _dft_raw_kernel, out_shape=(out, out), grid=(n // tm,),
        in_specs=[mat, mat, pl.BlockSpec((n, c), lambda j: (0, 0))],
        out_specs=(ospec, ospec),
        compiler_params=_cparams("parallel"), name="dft_raw",
    )(cm, sm, z)


def _dft_fwd_kernel(cm_ref, sm_ref, z_ref, hr_ref, hi_ref, yr_ref, yi_ref, *, bb, tm):
    is_row0 = _row_ids(tm, pl.program_id(1)) == 0
    hr, hi = hr_ref[...], hi_ref[...]
    for i in range(bb):
        z = z_ref[i]
        zr = _dot(cm_ref[...], z)
        zi = _dot(sm_ref[...], z)
        yr = jnp.where(is_row0, 0.5 * zr * hr, zr * hr + zi * hi)
        yi = jnp.where(is_row0, 0.5 * zi * hi, zi * hr - zr * hi)
        yr_ref[i] = yr.astype(yr_ref.dtype)
        yi_ref[i] = yi.astype(yi_ref.dtype)


def _dft_inv_kernel(cm_ref, smt_ref, yr_ref, yi_ref, z_ref, g_ref, skip_ref, o_ref, *, bb):
    for i in range(bb):
        y = _dot(cm_ref[...], yr_ref[i]) + _dot(smt_ref[...], yi_ref[i])
        y = y + z_ref[i].astype(F32) * skip_ref[...]
        o_ref[i] = (g_ref[i].astype(F32) * y).astype(o_ref.dtype)


def _longconv_gated(z, gate, hr, hi, skip, mats):
    cm, sm, smt = mats
    b, n, c = z.shape
    bb = 2 if b % 2 == 0 else 1
    tm = _largest_tile(n, 256, 16)
    mat = pl.BlockSpec((tm, n), lambda i, j: (j, 0))
    full = pl.BlockSpec((bb, n, c), lambda i, j: (i, 0, 0))
    tile = pl.BlockSpec((bb, tm, c), lambda i, j: (i, j, 0))
    filt = pl.BlockSpec((tm, c), lambda i, j: (j, 0))
    spec_shape = jax.ShapeDtypeStruct((b, n, c), BF16)
    yr, yi = pl.pallas_call(
        functools.partial(_dft_fwd_kernel, bb=bb, tm=tm),
        out_shape=(spec_shape, spec_shape), grid=(b // bb, n // tm),
        in_specs=[mat, mat, full, filt, filt], out_specs=(tile, tile),
        compiler_params=_cparams("parallel", "parallel"), name="dft_fwd",
    )(cm, sm, z, hr, hi)
    return pl.pallas_call(
        functools.partial(_dft_inv_kernel, bb=bb),
        out_shape=jax.ShapeDtypeStruct((b, n, c), BF16), grid=(b // bb, n // tm),
        in_specs=[mat, mat, full, full, tile, tile, pl.BlockSpec((1, c), lambda i, j: (0, 0))],
        out_specs=tile,
        compiler_params=_cparams("parallel", "parallel"), name="dft_inv",
    )(cm, smt, yr, yi, z, gate, skip.reshape(1, c))


def _hyena_filters(n, f1, fb1, f2, fb2, f3):
    t = jnp.arange(n, dtype=F32) / n
    bands = jnp.arange(1, HYENA_BANDS + 1, dtype=F32)
    ang = 2.0 * math.pi * t[:, None] * bands
    feat = jnp.concatenate([t[:, None], jnp.sin(ang), jnp.cos(ang)], axis=-1)
    hp = lax.Precision.HIGHEST
    h = jnp.sin(jnp.dot(feat, f1, precision=hp) + fb1)
    h = jnp.sin(jnp.dot(h, f2, precision=hp) + fb2)
    h = jnp.dot(h, f3, precision=hp).reshape(n, HYENA_ORDER, 2, HYENA_WIDTH)
    deltas = jnp.linspace(-math.log(HYENA_DECAY_TARGET) / HYENA_SLOW_DECAY,
                          -math.log(HYENA_DECAY_TARGET) / HYENA_FAST_DECAY, HYENA_WIDTH, dtype=F32)
    h = h * jnp.exp(-t[:, None] * deltas)[:, None, None, :]
    return h / jnp.sum(jnp.abs(h), axis=(0, 2), keepdims=True)


def _hyena_spectra(n, filt_params, mats):
    cm, sm, _ = mats
    h = _hyena_filters(n, *filt_params)
    hf = h[:, :, 0].reshape(n, HYENA_ORDER * HYENA_WIDTH)
    hb = h[:, :, 1].reshape(n, HYENA_ORDER * HYENA_WIDTH)
    hb = jnp.where(jnp.arange(n)[:, None] == 0, 0.0, hb)
    zr, zi = _dft_raw(cm, sm, jnp.concatenate([hf + hb, hb - hf], axis=1).astype(BF16))
    oc = HYENA_ORDER * HYENA_WIDTH
    hr = zr[:, :oc]
    hi = jnp.where(jnp.arange(n)[:, None] == 0, zi[:, :oc], zi[:, oc:])
    scale = 1.0 / n
    hr = (hr * scale).reshape(n, HYENA_ORDER, HYENA_WIDTH).transpose(1, 0, 2)
    hi = (hi * scale).reshape(n, HYENA_ORDER, HYENA_WIDTH).transpose(1, 0, 2)
    return hr, hi


def _hyena_run(v, x1, x2, spectra, skip, mats):
    hr, hi = spectra
    y = v
    for o, gate in enumerate((x1, x2)):
        y = _longconv_gated(y, gate, hr[o], hi[o], skip[o], mats)
    return y


def _blockdiag_dense(w):
    nb, blk = w.shape[1], w.shape[2]
    eye = jnp.eye(nb, dtype=w.dtype)
    return jnp.einsum('dncf,nm->dncmf', w, eye).reshape(w.shape[0], nb * blk, nb * blk)


def _lru_pre_kernel(p_ref, pv_ref, nx_ref, cw_ref, cb_ref, wa_ref, ba_ref, wx_ref, bx_ref, lam_ref,
                    a_ref, b_ref, *, cl, lt, tr):
    rows = _row_ids(tr, pl.program_id(1))
    c = LRU_WIDTH
    x = p_ref[0][:, c:].astype(F32)
    pv = pv_ref[0][:, c:].astype(F32)
    nx = nx_ref[0][:, c:].astype(F32)
    xc = (_shift_rows(x, pv, nx, -1, rows, cl, lt) * cw_ref[0:1, :] + x * cw_ref[1:2, :]
          + _shift_rows(x, pv, nx, 1, rows, cl, lt) * cw_ref[2:3, :]
          + _shift_rows(x, pv, nx, 2, rows, cl, lt) * cw_ref[3:4, :] + cb_ref[...])
    xcb = xc.astype(BF16)
    for d in range(2):
        r = _sigmoid(_dot(xcb, wa_ref[d]) + ba_ref[d])
        i = _sigmoid(_dot(xcb, wx_ref[d]) + bx_ref[d])
        log_a = -LRU_C * r * _softplus(-lam_ref[d])
        a_ref[d, 0] = jnp.exp(log_a)
        b_ref[d, 0] = jnp.sqrt(1.0 - jnp.exp(2.0 * log_a)) * (i * xc)


def _lru_pre(plr, conv_w, conv_b, wa, ba, wx, bx, lam, cl):
    b, lt, w = plr.shape
    c = LRU_WIDTH
    tr = _largest_tile(lt, 1088, 16)
    prev_spec, next_spec = _halo_specs(tr, lt, w, (1,))
    out = jax.ShapeDtypeStruct((2, b, lt, c), F32)
    ospec = pl.BlockSpec((2, 1, tr, c), lambda i, r: (0, i, r, 0))
    const2 = lambda shape: pl.BlockSpec(shape, lambda i, r: (0,) * len(shape))
    return pl.pallas_call(
        functools.partial(_lru_pre_kernel, cl=cl, lt=lt, tr=tr),
        out_shape=(out, out), grid=(b, lt // tr),
        in_specs=[pl.BlockSpec((1, tr, w), lambda i, r: (i, r, 0)), prev_spec, next_spec,
                  const2(conv_w.shape), const2((1, c)), const2((2, c, c)), const2((2, 1, c)),
                  const2((2, c, c)), const2((2, 1, c)), const2((2, 1, c))],
        out_specs=(ospec, ospec),
        compiler_params=_cparams("parallel", "parallel"), name="lru_pre",
    )(plr, plr, plr, conv_w, conv_b.reshape(1, c), _blockdiag_dense(wa).astype(BF16), ba.reshape(2, 1, c),
      _blockdiag_dense(wx).astype(BF16), bx.reshape(2, 1, c), lam.reshape(2, 1, c))


def _gelu_tanh(x):
    return 0.5 * x * (1.0 + jnp.tanh(math.sqrt(2.0 / math.pi) * (x + 0.044715 * (x * x * x))))


def _lru_scan_kernel(af_ref, bf_ref, ar_ref, br_ref, g_ref, o_ref, acc_ref, *, cl, lt):
    row = lax.broadcasted_iota(jnp.int32, (SUBLANES, LANES), 0)

    def group_scan(a, b, reverse):
        for s in (1, 2, 4):
            if reverse:
                keep = row < SUBLANES - s
                a_s = jnp.where(keep, pltpu.roll(a, SUBLANES - s, 0), 1.0)
                b_s = jnp.where(keep, pltpu.roll(b, SUBLANES - s, 0), 0.0)
            else:
                keep = row >= s
                a_s = jnp.where(keep, pltpu.roll(a, s, 0), 1.0)
                b_s = jnp.where(keep, pltpu.roll(b, s, 0), 0.0)
            b = a * b_s + b
            a = a * a_s
        return a, b

    def fwd_body(g, h):
        sl = pl.ds(pl.multiple_of(g * SUBLANES, SUBLANES), SUBLANES)
        a, b = group_scan(af_ref[0, 0, sl, :], bf_ref[0, 0, sl, :], False)
        hh = a * h + b
        acc_ref[sl, :] = hh
        return hh[SUBLANES - 1:SUBLANES, :]

    lax.fori_loop(0, lt // SUBLANES, fwd_body, jnp.zeros((1, LANES), F32))

    def rev_body(i, h, top):
        g = top - 1 - i
        sl = pl.ds(pl.multiple_of(g * SUBLANES, SUBLANES), SUBLANES)
        a, b = group_scan(ar_ref[0, 0, sl, :], br_ref[0, 0, sl, :], True)
        hh = a * h + b
        gate = g_ref[0, sl, :].astype(F32)
        o_ref[0, sl, :] = ((acc_ref[sl, :] + hh) * _gelu_tanh(gate)).astype(o_ref.dtype)
        return hh[0:1, :]

    h = lax.fori_loop(0, cl // SUBLANES, functools.partial(rev_body, top=cl // SUBLANES), jnp.zeros((1, LANES), F32))
    lax.fori_loop(0, (lt - cl) // SUBLANES, functools.partial(rev_body, top=lt // SUBLANES), h)


def _lru_scan(a, b_, plr, cl):
    _, b, lt, c = a.shape
    nl = c // LANES
    fwd = pl.BlockSpec((1, 1, lt, LANES), lambda i, j: (0, i, 0, j))
    rev = pl.BlockSpec((1, 1, lt, LANES), lambda i, j: (1, i, 0, j))
    return pl.pallas_call(
        functools.partial(_lru_scan_kernel, cl=cl, lt=lt),
        out_shape=jax.ShapeDtypeStruct((b, lt, c), BF16), grid=(b, nl),
        in_specs=[fwd, fwd, rev, rev, pl.BlockSpec((1, lt, LANES), lambda i, j: (i, 0, j))],
        out_specs=pl.BlockSpec((1, lt, LANES), lambda i, j: (i, 0, j)),
        scratch_shapes=[pltpu.VMEM((lt, LANES), F32)],
        compiler_params=_cparams("parallel", "parallel"), name="lru_scan",
    )(a, b_, a, b_, plr)


def _chunk_masks(tr):
    t = lax.broadcasted_iota(jnp.int32, (tr, tr), 0)
    s = lax.broadcasted_iota(jnp.int32, (tr, tr), 1)
    same = (t // RWKV_CHUNK) == (s // RWKV_CHUNK)
    return same, same & (s <= t), same & (s < t)


def _rwkv_prep_kernel(p_ref, pv_ref, nx_ref, mu_ref, w0_ref, w2_ref, a0_ref, a2_ref, g2_ref, kk_ref, ka_ref,
                      rk_ref, ones_ref,
                      aq_ref, vp_ref, y0_ref, rt_ref, mrb_ref, bht_ref, gm_ref, pc_ref, g_ref, bonus_ref,
                      *, cl, lt, tr):
    c = RWKV_WIDTH
    hd = HEAD_DIM
    ch = RWKV_CHUNK
    nch = tr // ch
    rows = _row_ids(tr, pl.program_id(2))
    p = p_ref[0, 0].astype(F32)
    pv = pv_ref[0, 0].astype(F32)
    nx = nx_ref[0, 0].astype(F32)
    prev = _shift_rows(p, pv, nx, -1, rows, cl, lt)
    nxt = _shift_rows(p, pv, nx, 1, rows, cl, lt)
    xm = p + (prev - p) * mu_ref[0, 0:1, :] + (nxt - p) * mu_ref[0, 1:2, :]
    r, k, v = xm[:, :c], xm[:, c:2 * c], xm[:, 2 * c:3 * c]
    o = 3 * c
    w1 = xm[:, o:o + 2 * RWKV_DECAY_RANK]
    a1 = xm[:, o + 2 * RWKV_DECAY_RANK:o + 2 * RWKV_DECAY_RANK + 2 * RWKV_ICLR_RANK]
    g1 = xm[:, o + 2 * RWKV_DECAY_RANK + 2 * RWKV_ICLR_RANK:]
    wlog = -_softplus(-(w0_ref[0] + _dot(jnp.tanh(w1).astype(BF16), w2_ref[0]))) - 0.5
    ld = -jnp.exp(wlog)
    a = _sigmoid(a0_ref[0] + _dot(a1.astype(BF16), a2_ref[0]))
    g_ref[0, 0] = _dot(_sigmoid(g1).astype(BF16), g2_ref[...])
    kk = k * kk_ref[...]
    kk = kk * lax.rsqrt(_dot_exact_rhs(kk * kk, ones_ref[...]) + 1e-12)
    kd = k * (1.0 + (a - 1.0) * ka_ref[...])
    bonus_ref[0, 0] = _dot_exact_rhs(r * kd * rk_ref[...], ones_ref[...]) * v
    beta = kk * a

    same, incl, strict = _chunk_masks(tr)
    cum = _dot_exact_lhs(jnp.where(incl, 1.0, 0.0).astype(BF16), ld)
    tot = _dot_exact_lhs(jnp.where(same, 1.0, 0.0).astype(BF16), ld)
    alpha_t = kk * jnp.exp(cum - ld)
    r_t = r * jnp.exp(cum)
    e_neg = jnp.exp(-cum)
    k_t = kd * e_neg
    b_t = beta * e_neg
    e_rem = jnp.exp(tot - cum)
    k_hat_t = (kd * e_rem).T
    b_hat_t = (beta * e_rem).T
    pc_t = jnp.exp(tot).T
    eye = lax.broadcasted_iota(jnp.int32, (tr, tr), 0) == lax.broadcasted_iota(jnp.int32, (tr, tr), 1)
    eye_f = jnp.where(eye, 1.0, 0.0)
    t_i = lax.broadcasted_iota(jnp.int32, (tr, tr), 0)
    s_i = lax.broadcasted_iota(jnp.int32, (tr, tr), 1)
    same_blk = []
    size = RWKV_INV_BASE
    while size <= ch:
        same_blk.append((t_i // size) == (s_i // size))
        size *= 2
    col_chunk = lax.broadcasted_iota(jnp.int32, (tr, nch * hd), 1) // hd
    row_chunk = lax.broadcasted_iota(jnp.int32, (tr, nch * hd), 0) // ch

    def diag_blocks(m):
        out = m[:, :ch]
        for j in range(1, nch):
            out = out + m[:, j * ch:(j + 1) * ch]
        return out

    aq, vp, y0, mrb, gms = [], [], [], [], []
    for h in range(RWKV_HEADS):
        hs = slice(h * hd, (h + 1) * hd)
        lhs = jnp.concatenate([alpha_t[:, hs], r_t[:, hs]], axis=0).astype(BF16)
        rhs = jnp.concatenate([b_t[:, hs], k_t[:, hs]], axis=0).astype(BF16)
        prod = _dot_nt(lhs, rhs)
        l_ab = jnp.where(strict, prod[:tr, :tr], 0.0)
        l_ak = jnp.where(strict, prod[:tr, tr:], 0.0)
        m_rb = jnp.where(incl, prod[tr:, :tr], 0.0)
        m_rk = jnp.where(incl, prod[tr:, tr:], 0.0)
        l_d = jnp.where(same_blk[0], l_ab, 0.0)
        t_inv = eye_f - l_d
        pw = l_d
        for _ in range(int(math.log2(RWKV_INV_BASE)) - 1):
            pw = _dot_split(pw, pw)
            t_inv = t_inv + _dot_split(t_inv, pw)
        for lvl in range(1, len(same_blk)):
            l_off = jnp.where(same_blk[lvl] & jnp.logical_not(same_blk[lvl - 1]), l_ab, 0.0)
            t_inv = t_inv - _dot_split(_dot_split(t_inv, l_off), t_inv)
        vh = v[:, hs].astype(BF16)
        rhs2 = jnp.concatenate([alpha_t[:, hs], _dot(l_ak.astype(BF16), vh)], axis=1)
        x = _dot(t_inv.astype(BF16), rhs2.astype(BF16))
        aq.append(x[:, :hd])
        vp.append(x[:, hd:])
        y0.append(_dot(m_rk.astype(BF16), vh))
        mrb.append(diag_blocks(m_rb))
        v_bd = jnp.where(col_chunk == row_chunk, jnp.concatenate([v[:, hs]] * nch, axis=1), 0.0)
        gms.append(_dot(k_hat_t[hs, :].astype(BF16), v_bd.astype(BF16)))
    aq_ref[0, 0] = jnp.concatenate(aq, axis=1).astype(aq_ref.dtype)
    vp_ref[0, 0] = jnp.concatenate(vp, axis=1)
    y0_ref[0, 0] = jnp.concatenate(y0, axis=1)
    rt_ref[0, 0] = r_t.astype(rt_ref.dtype)
    mrb_ref[0, 0] = jnp.concatenate(mrb, axis=1).astype(mrb_ref.dtype)
    for j in range(nch):
        cs = slice(j * ch, (j + 1) * ch)
        bht_ref[0, 0, j] = jnp.concatenate([b_hat_t[h * hd:(h + 1) * hd, cs] for h in range(RWKV_HEADS)],
                                           axis=1).astype(bht_ref.dtype)
        gm_ref[0, 0, j] = jnp.concatenate([gms[h][:, j * hd:(j + 1) * hd] for h in range(RWKV_HEADS)], axis=1)
        pc_ref[0, 0, j] = jnp.concatenate([pc_t[h * hd:(h + 1) * hd, cs] for h in range(RWKV_HEADS)], axis=1)


def _rwkv_prep(pd, mud, w0, w2p, a0, a2p, g2, k_k, k_a, r_k, cl):
    _, b, lt, w = pd.shape
    c = RWKV_WIDTH
    tr = 256 if lt % 256 == 0 else 128
    nch = tr // RWKV_CHUNK
    prev_spec, next_spec = _halo_specs(tr, lt, w, (1, 1))
    tile = pl.BlockSpec((1, 1, tr, c), lambda d, i, r: (d, i, r, 0))
    per_dir = lambda shape: pl.BlockSpec((1, *shape), lambda d, i, r: (d,) + (0,) * len(shape))
    const = lambda shape: pl.BlockSpec(shape, lambda d, i, r: (0,) * len(shape))
    seq = lambda dt: jax.ShapeDtypeStruct((2, b, lt, c), dt)
    chunked = jax.ShapeDtypeStruct((2, b, lt // RWKV_CHUNK, HEAD_DIM, c), F32)
    chunk_spec = pl.BlockSpec((1, 1, nch, HEAD_DIM, c), lambda d, i, r: (d, i, r, 0, 0))
    return pl.pallas_call(
        functools.partial(_rwkv_prep_kernel, cl=cl, lt=lt, tr=tr),
        out_shape=(seq(BF16), seq(F32), seq(F32), seq(BF16), seq(BF16),
                   jax.ShapeDtypeStruct(chunked.shape, BF16), chunked, chunked, seq(F32), seq(F32)),
        grid=(2, b, lt // tr),
        in_specs=[pl.BlockSpec((1, 1, tr, w), lambda d, i, r: (d, i, r, 0)), prev_spec, next_spec,
                  per_dir((2, w)), per_dir((1, c)), per_dir((2 * RWKV_DECAY_RANK, c)), per_dir((1, c)),
                  per_dir((2 * RWKV_ICLR_RANK, c)), const((RWKV_GATE_RANK, c)), const((1, c)), const((1, c)),
                  const((1, c)), const((c, c))],
        out_specs=(tile, tile, tile, tile, tile, chunk_spec, chunk_spec, chunk_spec, tile, tile),
        compiler_params=_cparams("parallel", "parallel", "parallel"), name="rwkv_prep",
    )(pd, pd, pd, mud, w0.reshape(2, 1, c), w2p, a0.reshape(2, 1, c), a2p, g2.astype(BF16), k_k.reshape(1, c),
      k_a.reshape(1, c), r_k.reshape(1, c), _head_ones(c))


def _rwkv_scan_kernel(aq_ref, vp_ref, y0_ref, rt_ref, mrb_ref, bht_ref, gm_ref, pc_ref, y_ref, s_ref, *, ns):
    @pl.when(pl.program_id(1) == 0)
    def _():
        s_ref[...] = jnp.zeros_like(s_ref)

    hd = HEAD_DIM
    for i in range(ns):
        ys, s_new = [], []
        for h in range(RWKV_HEADS):
            hs = slice(h * hd, (h + 1) * hd)
            s0 = s_ref[i, :, hs]
            s0b = s0.astype(BF16)
            u = _dot(aq_ref[i, 0, :, hs], s0b) + vp_ref[i, 0, :, hs]
            ub = u.astype(BF16)
            ys.append(_dot(rt_ref[i, 0, :, hs], s0b) + y0_ref[i, 0, :, hs] - _dot(mrb_ref[i, 0, :, hs], ub))
            s_new.append(pc_ref[i, 0, 0, :, hs] * s0 + gm_ref[i, 0, 0, :, hs] - _dot(bht_ref[i, 0, 0, :, hs], ub))
        y_ref[i, 0] = jnp.concatenate(ys, axis=1)
        s_ref[i] = jnp.concatenate(s_new, axis=1)


def _rwkv_scan(aq, vp, y0, rt, mrb, bht, gm, pc):
    _, b, lt, c = aq.shape
    nstream = 2 * b
    ns = 4 if nstream % 4 == 0 else 1
    ch = RWKV_CHUNK
    merge = lambda t: t.reshape(nstream, 1, *t.shape[2:])
    tile = pl.BlockSpec((ns, 1, ch, c), lambda s, j: (s, 0, j, 0))
    chunk_spec = pl.BlockSpec((ns, 1, 1, HEAD_DIM, c), lambda s, j: (s, 0, j, 0, 0))
    y = pl.pallas_call(
        functools.partial(_rwkv_scan_kernel, ns=ns),
        out_shape=jax.ShapeDtypeStruct((nstream, 1, lt, c), F32), grid=(nstream // ns, lt // ch),
        in_specs=[tile, tile, tile, tile, tile, chunk_spec, chunk_spec, chunk_spec],
        out_specs=tile,
        scratch_shapes=[pltpu.VMEM((ns, HEAD_DIM, c), F32)],
        compiler_params=_cparams("parallel", "arbitrary"), name="rwkv_scan",
    )(*(merge(t) for t in (aq, vp, y0, rt, mrb, bht, gm, pc)))
    return y.reshape(2, b, lt, c)


def _rwkv_readout_kernel(yf_ref, yr_ref, bf_ref, br_ref, g_ref, lw_ref, lb_ref, ones_ref, o_ref):
    y = yf_ref[0] + yr_ref[0]
    inv = 1.0 / HEAD_DIM
    mu = _dot_exact_rhs(y, ones_ref[...]) * inv
    yc = y - mu
    var = _dot_exact_rhs(yc * yc, ones_ref[...]) * inv
    yn = yc * lax.rsqrt(var + RWKV_GN_EPS) * lw_ref[...] + lb_ref[...]
    o_ref[0] = ((yn + bf_ref[0] + br_ref[0]) * g_ref[0]).astype(o_ref.dtype)


def _rwkv_readout(yf, yr, bf, br, g, ln_w, ln_b):
    b, lt, c = yf.shape
    tr = _largest_tile(lt, 1088, 16)
    tile = pl.BlockSpec((1, tr, c), lambda i, r: (i, r, 0))
    row = pl.BlockSpec((1, c), lambda i, r: (0, 0))
    return pl.pallas_call(
        _rwkv_readout_kernel, out_shape=jax.ShapeDtypeStruct((b, lt, c), BF16), grid=(b, lt // tr),
        in_specs=[tile, tile, tile, tile, tile, row, row, pl.BlockSpec((c, c), lambda i, r: (0, 0))],
        out_specs=tile, compiler_params=_cparams("parallel", "parallel"), name="rwkv_readout",
    )(yf, yr, bf, br, g, ln_w.reshape(1, c), ln_b.reshape(1, c), _head_ones(c))


def _flip_segments(t, cl, axis):
    idx = [slice(None)] * t.ndim
    idx[axis] = slice(0, cl)
    head = jnp.flip(t[tuple(idx)], axis)
    idx[axis] = slice(cl, None)
    tail = jnp.flip(t[tuple(idx)], axis)
    return jnp.concatenate([head, tail], axis)


def _pad_rank_rows(w):
    z = jnp.zeros_like(w[0])
    return jnp.stack([jnp.concatenate([w[0], z], 0), jnp.concatenate([z, w[1]], 0)])


def _rwkv_mixer(prw, mu, w0, w2, a0, a2, g2, k_k, k_a, r_k, ln_w, ln_b, cl):
    pd = jnp.stack([prw, _flip_segments(prw, cl, 1)])
    mud = jnp.stack([mu, mu[::-1]])
    outs = _rwkv_prep(pd, mud, w0, _pad_rank_rows(w2).astype(BF16), a0, _pad_rank_rows(a2).astype(BF16), g2,
                      k_k, k_a, r_k, cl)
    *scan_in, g, bonus = outs
    y = _rwkv_scan(*scan_in)
    return _rwkv_readout(y[0], _flip_segments(y[1], cl, 1), bonus[0], _flip_segments(bonus[1], cl, 1), g[0],
                         ln_w, ln_b)


def _merge_kernel(ya_ref, yh_ref, yr_ref, yl_ref, gt_ref, x_ref, modx_ref, modc_ref,
                  wa_ref, wh_ref, wr_ref, wl_ref, wo_ref, o_ref, *, cl, tm, d):
    is_ctx = _row_ids(tm, pl.program_id(1)) < cl
    m = None
    for i, (y_ref, w_ref) in enumerate(((ya_ref, wa_ref), (yh_ref, wh_ref), (yr_ref, wr_ref), (yl_ref, wl_ref))):
        gate = _sigmoid(gt_ref[0, :, i * d:(i + 1) * d].astype(F32))
        term = gate * _dot(y_ref[0], w_ref[...])
        m = term if m is None else m + term
    g1 = _mod_rows(modx_ref, modc_ref, 2, is_ctx)
    o_ref[0] = x_ref[0] + g1 * _dot(m.astype(BF16), wo_ref[...])


def _merge(ys, gates, xc, mod, w_brs, w_out, cl):
    b, lt, d = xc.shape
    tm = _largest_tile(lt, 544, 16)
    row = lambda w: pl.BlockSpec((1, tm, w), lambda i, r: (i, r, 0))
    const = lambda a: pl.BlockSpec(a.shape, lambda i, r: (0, 0))
    ws = [w.astype(BF16) for w in w_brs] + [w_out.astype(BF16)]
    return pl.pallas_call(
        functools.partial(_merge_kernel, cl=cl, tm=tm, d=d),
        out_shape=jax.ShapeDtypeStruct((b, lt, d), F32), grid=(b, lt // tm),
        in_specs=[row(y.shape[-1]) for y in ys] + [row(N_BRANCH * d), row(d),
                  pl.BlockSpec((1, 6, d), lambda i, r: (i, 0, 0)), pl.BlockSpec((1, 6, d), lambda i, r: (b, 0, 0))]
                 + [const(w) for w in ws],
        out_specs=row(d), compiler_params=_cparams("parallel", "parallel"), name="merge",
    )(*ys, gates, xc, mod, mod, *ws)


def _route_kernel(x_ref, modx_ref, modc_ref, wr_ref, br_ref, h_ref, cmb_ref, *, cl, tm):
    is_ctx = _row_ids(tm, pl.program_id(1)) < cl
    h = _rms_modulate(x_ref[0], _mod_rows(modx_ref, modc_ref, 3, is_ctx), _mod_rows(modx_ref, modc_ref, 4, is_ctx))
    h_ref[0] = h.astype(h_ref.dtype)
    lg = jnp.dot(h, wr_ref[...], preferred_element_type=F32, precision=lax.Precision.HIGHEST) + br_ref[...]
    lane = lax.broadcasted_iota(jnp.int32, lg.shape, 1)
    lane_f = lane.astype(F32)
    neg = -jnp.inf
    big = 1e9

    def first_lane(cond):
        return jnp.min(jnp.where(cond, lane_f, big), axis=-1, keepdims=True)

    is_grp = (lane >= N_EXPERTS) & (lane < N_EXPERTS + N_GROUPS)
    gl = jnp.where(is_grp, lg, neg)
    gmax = jnp.max(gl, axis=-1, keepdims=True)
    ge = jnp.where(is_grp, jnp.exp(gl - gmax), 0.0)
    gp = ge / jnp.sum(ge, axis=-1, keepdims=True)
    g_val = jnp.max(gp, axis=-1, keepdims=True)
    g_idx = first_lane(is_grp & (gp == g_val)) - N_EXPERTS
    lo = g_idx * EXPERTS_PER_GROUP
    in_grp = (lane_f >= lo) & (lane_f < lo + EXPERTS_PER_GROUP)
    el = jnp.where(in_grp, lg, neg)
    emax = jnp.max(el, axis=-1, keepdims=True)
    ee = jnp.where(in_grp, jnp.exp(el - emax), 0.0)
    pe = ee / jnp.sum(ee, axis=-1, keepdims=True)
    v1 = jnp.max(jnp.where(in_grp, pe, -1.0), axis=-1, keepdims=True)
    i1 = first_lane(in_grp & (pe == v1))
    rest = in_grp & (lane_f != i1)
    v2 = jnp.max(jnp.where(rest, pe, -1.0), axis=-1, keepdims=True)
    i2 = first_lane(rest & (pe == v2))
    den = v1 + v2
    cmb_ref[0] = jnp.where(lane_f == i1, g_val * v1 / den, 0.0) + jnp.where(lane_f == i2, g_val * v2 / den, 0.0)


def _route(xc, mod, w_grp, b_grp, w_rt, b_rt, cl):
    b, lt, d = xc.shape
    tm = _largest_tile(lt, 544, 16)
    wr = jnp.zeros((d, LANES), F32).at[:, :N_EXPERTS].set(w_rt).at[:, N_EXPERTS:N_EXPERTS + N_GROUPS].set(w_grp)
    br = jnp.zeros((1, LANES), F32).at[0, :N_EXPERTS].set(b_rt).at[0, N_EXPERTS:N_EXPERTS + N_GROUPS].set(b_grp)
    return pl.pallas_call(
        functools.partial(_route_kernel, cl=cl, tm=tm),
        out_shape=(jax.ShapeDtypeStruct((b, lt, d), BF16), jax.ShapeDtypeStruct((b, lt, LANES), F32)),
        grid=(b, lt // tm),
        in_specs=[pl.BlockSpec((1, tm, d), lambda i, r: (i, r, 0)),
                  pl.BlockSpec((1, 6, d), lambda i, r: (i, 0, 0)), pl.BlockSpec((1, 6, d), lambda i, r: (b, 0, 0)),
                  pl.BlockSpec((d, LANES), lambda i, r: (0, 0)), pl.BlockSpec((1, LANES), lambda i, r: (0, 0))],
        out_specs=(pl.BlockSpec((1, tm, d), lambda i, r: (i, r, 0)), pl.BlockSpec((1, tm, LANES), lambda i, r: (i, r, 0))),
        compiler_params=_cparams("parallel", "parallel"), name="moe_route",
    )(xc, mod, mod, wr, br)


def _moe_kernel(h_ref, cmb_ref, x_ref, modx_ref, modc_ref, w1_ref, w3_ref, w2_ref, o_ref, acc_ref, *, cl, tm):
    e = pl.program_id(2)

    @pl.when(e == 0)
    def _():
        acc_ref[...] = jnp.zeros_like(acc_ref)

    h = h_ref[0]
    t = _silu(_dot(h, w1_ref[0])) * _dot(h, w3_ref[0])
    y = _dot(t.astype(BF16), w2_ref[0])
    lane = lax.broadcasted_iota(jnp.int32, (tm, LANES), 1)
    wcol = jnp.sum(jnp.where(lane == e, cmb_ref[0], 0.0), axis=-1, keepdims=True)
    acc_ref[...] += wcol * y

    @pl.when(e == pl.num_programs(2) - 1)
    def _():
        is_ctx = _row_ids(tm, pl.program_id(1)) < cl
        o_ref[0] = x_ref[0] + _mod_rows(modx_ref, modc_ref, 5, is_ctx) * acc_ref[...]


def _moe(h2, cmb, xc, mod, w1, w3, w2, cl):
    b, lt, d = xc.shape
    ne, _, hid = w1.shape
    tm = _largest_tile(lt, 1088, 16)
    return pl.pallas_call(
        functools.partial(_moe_kernel, cl=cl, tm=tm),
        out_shape=jax.ShapeDtypeStruct((b, lt, d), F32), grid=(b, lt // tm, ne),
        in_specs=[pl.BlockSpec((1, tm, d), lambda i, r, e: (i, r, 0)),
                  pl.BlockSpec((1, tm, LANES), lambda i, r, e: (i, r, 0)),
                  pl.BlockSpec((1, tm, d), lambda i, r, e: (i, r, 0)),
                  pl.BlockSpec((1, 6, d), lambda i, r, e: (i, 0, 0)),
                  pl.BlockSpec((1, 6, d), lambda i, r, e: (b, 0, 0)),
                  pl.BlockSpec((1, d, hid), lambda i, r, e: (e, 0, 0)),
                  pl.BlockSpec((1, d, hid), lambda i, r, e: (e, 0, 0)),
                  pl.BlockSpec((1, hid, d), lambda i, r, e: (e, 0, 0))],
        out_specs=pl.BlockSpec((1, tm, d), lambda i, r, e: (i, r, 0)),
        scratch_shapes=[pltpu.VMEM((tm, d), F32)],
        compiler_params=_cparams("parallel", "parallel", "arbitrary"), name="moe_experts",
    )(h2, cmb, xc, mod, mod, w1.astype(BF16), w3.astype(BF16), w2.astype(BF16))


def kernel(x, c, ctx, c_ctx, ada_w, ada_b, w_in, q_norm, k_norm, hy_conv_w, hy_conv_b, hy_f1, hy_fb1, hy_f2, hy_fb2, hy_f3, hy_skip, rw_mu, rw_w0, rw_w2, rw_a0, rw_a2, rw_g2, rw_k_k, rw_k_a, rw_r_k, rw_ln_w, rw_ln_b, lru_conv_w, lru_conv_b, lru_wa, lru_ba, lru_wx, lru_bx, lru_lambda, w_br_attn, w_br_hyena, w_br_rwkv, w_br_lru, w_out, moe_w_grp, moe_b_grp, moe_w_rt, moe_b_rt, moe_w1, moe_w3, moe_w2):
    b, l, d = x.shape
    cl = ctx.shape[1]
    depth = ada_w.shape[0]
    assert b < MOD_ROWS and cl % RWKV_CHUNK == 0 and l % RWKV_CHUNK == 0

    xc = jnp.concatenate([ctx, x], axis=1)
    cc = jnp.zeros((MOD_ROWS, d), F32).at[:b].set(c).at[b].set(c_ctx)
    mod_all = _ada_mod(cc, ada_w, ada_b).reshape(depth, MOD_ROWS, 6, d)

    cos2, sin2 = _rope_tables(l, cl)
    mats_x = _dft_mats(l)
    mats_c = _dft_mats(cl)
    qkv_w = ATTN_WIDTH + 2 * ATTN_KV_WIDTH
    col = np.cumsum([0, qkv_w, 3 * HYENA_WIDTH, RWKV_PROJ, 2 * LRU_WIDTH, N_BRANCH * d])

    for i in range(depth):
        need_ctx = i < depth - 1
        mod = mod_all[i]
        w_i = w_in[i].astype(BF16)
        pqkv, phy, prw, plr, gates = (_modmm(xc, mod, w_i[:, col[j]:col[j + 1]], cl) for j in range(5))

        qn, kt = _attn_prep(pqkv, cos2, sin2, q_norm[i], k_norm[i])
        y_att = _attention(qn, kt, pqkv, cl)

        hv, hx1, hx2 = _hyena_pre(phy, hy_conv_w[i], hy_conv_b[i], cl)
        filt = (hy_f1[i], hy_fb1[i], hy_f2[i], hy_fb2[i], hy_f3[i])
        y_hx = _hyena_run(hv[:, cl:], hx1[:, cl:], hx2[:, cl:], _hyena_spectra(l, filt, mats_x), hy_skip[i], mats_x)
        if need_ctx:
            y_hc = _hyena_run(hv[:, :cl], hx1[:, :cl], hx2[:, :cl], _hyena_spectra(cl, filt, mats_c), hy_skip[i],
                              mats_c)
        else:
            y_hc = jnp.zeros((b, cl, HYENA_WIDTH), BF16)
        y_hy = jnp.concatenate([y_hc, y_hx], axis=1)

        y_rw = _rwkv_mixer(prw, rw_mu[i], rw_w0[i], rw_w2[i], rw_a0[i], rw_a2[i], rw_g2[i], rw_k_k[i], rw_k_a[i],
                           rw_r_k[i].reshape(-1), rw_ln_w[i], rw_ln_b[i], cl)

        la, lb = _lru_pre(plr, lru_conv_w[i], lru_conv_b[i], lru_wa[i], lru_ba[i], lru_wx[i], lru_bx[i],
                          lru_lambda[i], cl)
        y_lr = _lru_scan(la, lb, plr, cl)

        xc = _merge((y_att, y_hy, y_rw, y_lr), gates, xc, mod,
                    (w_br_attn[i], w_br_hyena[i], w_br_rwkv[i], w_br_lru[i]), w_out[i], cl)

        h2, cmb = _route(xc, mod, moe_w_grp[i], moe_b_grp[i], moe_w_rt[i], moe_b_rt[i], cl)
        xc = _moe(h2, cmb, xc, mod, moe_w1[i], moe_w3[i], moe_w2[i], cl)
    return xc[:, cl:]
```

```python
import functools
import math

import numpy as np
import jax
import jax.numpy as jnp
from jax import lax
from jax.experimental import pallas as pl
from jax.experimental.pallas import tpu as pltpu

F32 = jnp.float32
BF16 = jnp.bfloat16

HEAD_DIM = 64
GRID_W = 64
EPS = 1e-6
ATTN_HEADS = 8
ATTN_KV_HEADS = 2
ATTN_GROUP = ATTN_HEADS // ATTN_KV_HEADS
ATTN_WIDTH = ATTN_HEADS * HEAD_DIM
ATTN_KV_WIDTH = ATTN_KV_HEADS * HEAD_DIM
ROPE_THETA = 10000.0
HYENA_WIDTH = 256
HYENA_ORDER = 2
HYENA_BANDS = 16
HYENA_DECAY_TARGET = 1e-2
HYENA_FAST_DECAY = 0.3
HYENA_SLOW_DECAY = 1.5
RWKV_HEADS = 4
RWKV_WIDTH = RWKV_HEADS * HEAD_DIM
RWKV_DECAY_RANK = 64
RWKV_ICLR_RANK = 64
RWKV_GATE_RANK = 128
RWKV_GN_EPS = 64e-5
RWKV_PROJ = 3 * RWKV_WIDTH + 2 * RWKV_DECAY_RANK + 2 * RWKV_ICLR_RANK + RWKV_GATE_RANK
RWKV_CHUNK = 64
RWKV_INV_BASE = 4
RWKV_STAGE_ROWS = 128
LRU_WIDTH = 256
LRU_BLOCKS = 4
LRU_C = 8.0
N_BRANCH = 4
N_GROUPS = 4
EXPERTS_PER_GROUP = 4
N_EXPERTS = N_GROUPS * EXPERTS_PER_GROUP

V7X_VMEM_LIMIT_BYTES = 52 * 1024 * 1024
SUBLANES = 8
LANES = 128
MOD_ROWS = 16


def _cparams(*sem):
    return pltpu.CompilerParams(dimension_semantics=sem, vmem_limit_bytes=V7X_VMEM_LIMIT_BYTES)


def _dot(a, b):
    return jnp.dot(a, b, preferred_element_type=F32)


def _dot_nt(a, b):
    return lax.dot_general(a, b, (((1,), (1,)), ((), ())), preferred_element_type=F32)


def _split3(x):
    hi = x.astype(BF16)
    r1 = x - hi.astype(F32)
    mid = r1.astype(BF16)
    lo = (r1 - mid.astype(F32)).astype(BF16)
    return hi, mid, lo


def _dot_exact_lhs(m_bf16, x):
    hi, mid, lo = _split3(x)
    return _dot(m_bf16, hi) + _dot(m_bf16, mid) + _dot(m_bf16, lo)


def _dot_exact_rhs(x, m_bf16):
    hi, mid, lo = _split3(x)
    return _dot(hi, m_bf16) + _dot(mid, m_bf16) + _dot(lo, m_bf16)


def _dot_bf(a, b):
    return _dot(a.astype(BF16), b.astype(BF16))


def _sigmoid(x):
    return 1.0 / (1.0 + jnp.exp(-x))


def _softplus(x):
    return jnp.maximum(x, 0.0) + jnp.log(1.0 + jnp.exp(-jnp.abs(x)))


def _silu(x):
    return x * _sigmoid(x)


def _largest_tile(n, cap, mult):
    best = None
    for t in range(mult, min(n, cap) + 1, mult):
        if n % t == 0:
            best = t
    assert best is not None, (n, cap, mult)
    return best


def _head_ones(width):
    idx = np.arange(width) // HEAD_DIM
    return jnp.asarray((idx[:, None] == idx[None, :]).astype(np.float32), dtype=BF16)


def _row_ids(tile_rows, tile_idx):
    return tile_idx * tile_rows + lax.broadcasted_iota(jnp.int32, (tile_rows, 1), 0)


def _mod_rows(modx_ref, modc_ref, idx, is_ctx):
    return jnp.where(is_ctx, modc_ref[0, idx:idx + 1, :], modx_ref[0, idx:idx + 1, :])


def _rms_modulate(x, shift, scale):
    ms = jnp.mean(x * x, axis=-1, keepdims=True)
    return (x * lax.rsqrt(ms + EPS)) * (1.0 + scale) + shift


def _ada_kernel(c_ref, w_ref, b_ref, o_ref):
    s = _silu(c_ref[...])
    o_ref[0] = jnp.dot(s, w_ref[0], preferred_element_type=F32, precision=lax.Precision.HIGHEST) + b_ref[0]


def _ada_mod(cc, ada_w, ada_b):
    depth, d, n6 = ada_w.shape
    tn = _largest_tile(n6, 1024, LANES)
    return pl.pallas_call(
        _ada_kernel,
        out_shape=jax.ShapeDtypeStruct((depth, MOD_ROWS, n6), F32),
        grid=(depth, n6 // tn),
        in_specs=[pl.BlockSpec((MOD_ROWS, d), lambda i, j: (0, 0)),
                  pl.BlockSpec((1, d, tn), lambda i, j: (i, 0, j)),
                  pl.BlockSpec((1, 1, tn), lambda i, j: (i, 0, j))],
        out_specs=pl.BlockSpec((1, MOD_ROWS, tn), lambda i, j: (i, 0, j)),
        compiler_params=_cparams("parallel", "parallel"),
        name="ada_mod",
    )(cc, ada_w, ada_b.reshape(depth, 1, n6))


def _modmm_kernel(x_ref, modx_ref, modc_ref, w_ref, o_ref, h_scr, *, cl, tm):
    @pl.when(pl.program_id(2) == 0)
    def _():
        is_ctx = _row_ids(tm, pl.program_id(1)) < cl
        h = _rms_modulate(x_ref[0], _mod_rows(modx_ref, modc_ref, 0, is_ctx), _mod_rows(modx_ref, modc_ref, 1, is_ctx))
        h_scr[...] = h.astype(BF16)

    o_ref[0] = _dot(h_scr[...], w_ref[...]).astype(o_ref.dtype)


def _modmm(xc, mod, w, cl, out_dtype=BF16):
    b, lt, d = xc.shape
    n = w.shape[1]
    tm = _largest_tile(lt, 1088, 16)
    tn = n if n <= 1280 else _largest_tile(n, 1024, 2 * LANES)
    return pl.pallas_call(
        functools.partial(_modmm_kernel, cl=cl, tm=tm),
        out_shape=jax.ShapeDtypeStruct((b, lt, n), out_dtype),
        grid=(b, lt // tm, n // tn),
        in_specs=[pl.BlockSpec((1, tm, d), lambda i, r, j: (i, r, 0)),
                  pl.BlockSpec((1, 6, d), lambda i, r, j: (i, 0, 0)),
                  pl.BlockSpec((1, 6, d), lambda i, r, j: (b, 0, 0)),
                  pl.BlockSpec((d, tn), lambda i, r, j: (0, j))],
        out_specs=pl.BlockSpec((1, tm, tn), lambda i, r, j: (i, r, j)),
        scratch_shapes=[pltpu.VMEM((tm, d), BF16)],
        compiler_params=_cparams("parallel", "parallel", "arbitrary"),
        name="modmm",
    )(xc, mod, mod, w)


def _rope_tables(l, cl):
    n_freq = HEAD_DIM // 4
    t = jnp.arange(l)
    freqs = ROPE_THETA ** (-jnp.arange(n_freq, dtype=F32) / n_freq)
    pos = jnp.stack([t // GRID_W, t % GRID_W], -1).astype(F32)
    ang = pos[..., None] * freqs
    cos64 = jnp.stack([jnp.cos(ang), jnp.cos(ang)], axis=2).reshape(l, HEAD_DIM)
    sin64 = jnp.stack([-jnp.sin(ang), jnp.sin(ang)], axis=2).reshape(l, HEAD_DIM)
    cos64 = jnp.concatenate([jnp.ones((cl, HEAD_DIM), F32), cos64], 0)
    sin64 = jnp.concatenate([jnp.zeros((cl, HEAD_DIM), F32), sin64], 0)
    return jnp.tile(cos64, (1, 2)), jnp.tile(sin64, (1, 2))


def _head_rms(t, ones_ref):
    ms = _dot_exact_rhs(t * t, ones_ref[...]) * (1.0 / HEAD_DIM)
    return t * lax.rsqrt(ms + EPS)


def _rope(t, cos, sin):
    w = t.shape[-1]
    lane = lax.broadcasted_iota(jnp.int32, t.shape, 1)
    q4 = HEAD_DIM // 4
    first_half = (lane % (2 * q4)) < q4
    partner = jnp.where(first_half, pltpu.roll(t, w - q4, 1), pltpu.roll(t, q4, 1))
    return t * cos + partner * sin


def _attn_prep_kernel(p_ref, cos_ref, sin_ref, qg_ref, kg_ref, oq_ref, ok_ref, q_ref, kt_ref, vx_ref):
    p = p_ref[0].astype(F32)
    v = p[:, ATTN_WIDTH + ATTN_KV_WIDTH:]
    low = lax.broadcasted_iota(jnp.int32, v.shape, 1) < HEAD_DIM
    vx_ref[0, 0] = jnp.where(low, v, 1.0).astype(vx_ref.dtype)
    vx_ref[0, 1] = jnp.where(low, pltpu.roll(v, HEAD_DIM, 1), 1.0).astype(vx_ref.dtype)
    cos2, sin2 = cos_ref[...], sin_ref[...]
    reps = ATTN_WIDTH // (2 * HEAD_DIM)
    cos_q = jnp.concatenate([cos2] * reps, axis=1)
    sin_q = jnp.concatenate([sin2] * reps, axis=1)
    q = _head_rms(p[:, :ATTN_WIDTH], oq_ref) * qg_ref[...]
    q_ref[0] = _rope(q, cos_q, sin_q).astype(q_ref.dtype)
    k = _head_rms(p[:, ATTN_WIDTH:ATTN_WIDTH + ATTN_KV_WIDTH], ok_ref) * kg_ref[...]
    kt_ref[0] = _rope(k, cos2, sin2).T.astype(kt_ref.dtype)


def _attn_prep(pqkv, cos2, sin2, q_gain, k_gain):
    b, lt, wtot = pqkv.shape
    tr = _largest_tile(lt, 2176, LANES)
    qg = jnp.tile(q_gain * (HEAD_DIM ** -0.5 * math.log2(math.e)), ATTN_HEADS).reshape(1, ATTN_WIDTH)
    kg = jnp.tile(k_gain, ATTN_KV_HEADS).reshape(1, ATTN_KV_WIDTH)
    return pl.pallas_call(
        _attn_prep_kernel,
        out_shape=(jax.ShapeDtypeStruct((b, lt, ATTN_WIDTH), BF16),
                   jax.ShapeDtypeStruct((b, ATTN_KV_WIDTH, lt), BF16),
                   jax.ShapeDtypeStruct((b, ATTN_KV_HEADS, lt, 2 * HEAD_DIM), BF16)),
        grid=(b, lt // tr),
        in_specs=[pl.BlockSpec((1, tr, wtot), lambda i, r: (i, r, 0)),
                  pl.BlockSpec((tr, 2 * HEAD_DIM), lambda i, r: (r, 0)),
                  pl.BlockSpec((tr, 2 * HEAD_DIM), lambda i, r: (r, 0)),
                  pl.BlockSpec((1, ATTN_WIDTH), lambda i, r: (0, 0)),
                  pl.BlockSpec((1, ATTN_KV_WIDTH), lambda i, r: (0, 0)),
                  pl.BlockSpec((ATTN_WIDTH, ATTN_WIDTH), lambda i, r: (0, 0)),
                  pl.BlockSpec((ATTN_KV_WIDTH, ATTN_KV_WIDTH), lambda i, r: (0, 0))],
        out_specs=(pl.BlockSpec((1, tr, ATTN_WIDTH), lambda i, r: (i, r, 0)),
                   pl.BlockSpec((1, ATTN_KV_WIDTH, tr), lambda i, r: (i, 0, r)),
                   pl.BlockSpec((1, ATTN_KV_HEADS, tr, 2 * HEAD_DIM), lambda i, r: (i, 0, r, 0))),
        compiler_params=_cparams("parallel", "parallel"),
        name="attn_prep",
    )(pqkv, cos2, sin2, qg, kg, _head_ones(ATTN_WIDTH), _head_ones(ATTN_KV_WIDTH))


def _attn_kernel(q_ref, kt_ref, v_ref, o_ref, *, n_ctx_tiles, cl, lt):
    def run(nk):
        outs = []
        for h in range(ATTN_HEADS):
            kv = h // ATTN_GROUP
            qh = q_ref[0, :, h * HEAD_DIM:(h + 1) * HEAD_DIM]
            kt = kt_ref[0, kv * HEAD_DIM:(kv + 1) * HEAD_DIM, :nk]
            s = _dot(qh, kt)
            m = jnp.max(s, axis=-1, keepdims=True)
            p = jnp.exp2(s - m)
            o = _dot(p.astype(BF16), v_ref[0, kv, :nk, :])
            outs.append(o[:, :HEAD_DIM] / o[:, HEAD_DIM:])
        o_ref[0] = jnp.concatenate(outs, axis=-1).astype(o_ref.dtype)

    is_ctx_tile = pl.program_id(1) < n_ctx_tiles

    @pl.when(is_ctx_tile)
    def _():
        run(cl)

    @pl.when(jnp.logical_not(is_ctx_tile))
    def _():
        run(lt)


def _attention(qn, kt, vx, cl):
    b, lt, _ = qn.shape
    tq = 256 if (cl % 256 == 0 and lt % 256 == 0) else 128
    assert cl % tq == 0 and lt % tq == 0
    return pl.pallas_call(
        functools.partial(_attn_kernel, n_ctx_tiles=cl // tq, cl=cl, lt=lt),
        out_shape=jax.ShapeDtypeStruct((b, lt, ATTN_WIDTH), BF16),
        grid=(b, lt // tq),
        in_specs=[pl.BlockSpec((1, tq, ATTN_WIDTH), lambda i, t: (i, t, 0)),
                  pl.BlockSpec((1, ATTN_KV_WIDTH, lt), lambda i, t: (i, 0, 0)),
                  pl.BlockSpec((1, ATTN_KV_HEADS, lt, 2 * HEAD_DIM), lambda i, t: (i, 0, 0, 0))],
        out_specs=pl.BlockSpec((1, tq, ATTN_WIDTH), lambda i, t: (i, t, 0)),
        compiler_params=_cparams("parallel", "parallel"),
        name="attention",
    )(qn, kt, vx)


def _halo_specs(tr, lt, width, lead):
    per = tr // SUBLANES
    last = lt // SUBLANES - 1
    nlead = len(lead)

    def prev_map(*ids):
        return (*ids[:nlead], jnp.maximum(ids[nlead] * per - 1, 0), 0)

    def next_map(*ids):
        return (*ids[:nlead], jnp.minimum((ids[nlead] + 1) * per, last), 0)

    blk = (*lead, SUBLANES, width)
    return pl.BlockSpec(blk, prev_map), pl.BlockSpec(blk, next_map)


def _shift_rows(x, halo, offset, rows, cl, lt):
    tr = x.shape[0]
    local = lax.broadcasted_iota(jnp.int32, (tr, 1), 0)
    if offset < 0:
        y = pltpu.roll(x, -offset, 0)
        y = jnp.where(local == 0, halo, y)
        bad = (rows == 0) | (rows == cl)
    else:
        y = pltpu.roll(x, tr - offset, 0)
        for j in range(offset):
            y = jnp.where(local == tr - offset + j, halo[j:j + 1, :], y)
        bad = (rows >= lt - offset) | ((rows >= cl - offset) & (rows < cl))
    return jnp.where(bad, 0.0, y)


def _hyena_pre_kernel(p_ref, pv_ref, nx_ref, w_ref, b_ref, v_ref, x1_ref, x2_ref, *, cl, lt, tr):
    rows = _row_ids(tr, pl.program_id(1))
    p = p_ref[0].astype(F32)
    pm = _shift_rows(p, pv_ref[0, SUBLANES - 1:, :].astype(F32), -1, rows, cl, lt)
    pp = _shift_rows(p, nx_ref[0, :1, :].astype(F32), 1, rows, cl, lt)
    z = pm * w_ref[0:1, :] + p * w_ref[1:2, :] + pp * w_ref[2:3, :] + b_ref[...]
    c = HYENA_WIDTH
    v_ref[0] = z[:, :c].astype(v_ref.dtype)
    x1_ref[0] = z[:, c:2 * c].astype(x1_ref.dtype)
    x2_ref[0] = z[:, 2 * c:].astype(x2_ref.dtype)


def _hyena_pre(phy, conv_w, conv_b, cl):
    b, lt, w = phy.shape
    tr = _largest_tile(lt, 1088, 16)
    prev_spec, next_spec = _halo_specs(tr, lt, w, (1,))
    out = jax.ShapeDtypeStruct((b, lt, HYENA_WIDTH), BF16)
    ospec = pl.BlockSpec((1, tr, HYENA_WIDTH), lambda i, r: (i, r, 0))
    return pl.pallas_call(
        functools.partial(_hyena_pre_kernel, cl=cl, lt=lt, tr=tr),
        out_shape=(out, out, out),
        grid=(b, lt // tr),
        in_specs=[pl.BlockSpec((1, tr, w), lambda i, r: (i, r, 0)), prev_spec, next_spec,
                  pl.BlockSpec(conv_w.shape, lambda i, r: (0, 0)),
                  pl.BlockSpec((1, w), lambda i, r: (0, 0))],
        out_specs=(ospec, ospec, ospec),
        compiler_params=_cparams("parallel", "parallel"),
        name="hyena_pre",
    )(phy, phy, phy, conv_w, conv_b.reshape(1, w))


def _dft_mats(n):
    tm = _largest_tile(n, 256, 16)
    tn = _largest_tile(n, 1024, LANES)
    s = jnp.arange(n, dtype=jnp.int32)[None, :]

    def tables(k):
        ang = ((k[:, None] * s) % (2 * n)).astype(F32) * (math.pi / n)
        return jnp.cos(ang), jnp.sin(ang)

    cb, sb = tables(jnp.arange(0, n, tm, dtype=jnp.int32))
    co, so = tables(jnp.arange(tm, dtype=jnp.int32))
    base = pl.BlockSpec((1, 1, tn), lambda j, c: (j, 0, c))
    off = pl.BlockSpec((tm, tn), lambda j, c: (0, c))
    out = jax.ShapeDtypeStruct((n, n), BF16)
    ospec = pl.BlockSpec((tm, tn), lambda j, c: (j, c))
    return pl.pallas_call(
        functools.partial(_dft_mats_kernel, tm=tm, tn=tn), out_shape=(out, out, out), grid=(n // tm, n // tn),
        in_specs=[base, base, off, off], out_specs=(ospec, ospec, ospec),
        compiler_params=_cparams("parallel", "parallel"), name="dft_mats",
    )(cb.reshape(n // tm, 1, n), sb.reshape(n // tm, 1, n), co, so)


def _dft_mats_kernel(cb_ref, sb_ref, co_ref, so_ref, cm_ref, sm_ref, smt_ref, *, tm, tn):
    cb, sb, co, so = cb_ref[0], sb_ref[0], co_ref[...], so_ref[...]
    k = pl.program_id(0) * tm + lax.broadcasted_iota(jnp.int32, (tm, tn), 0)
    s = pl.program_id(1) * tn + lax.broadcasted_iota(jnp.int32, (tm, tn), 1)
    sn = sb * co + cb * so
    cm_ref[...] = (cb * co - sb * so).astype(cm_ref.dtype)
    sm_ref[...] = jnp.where(k == 0, (1 - 2 * (s % 2)).astype(F32), sn).astype(sm_ref.dtype)
    smt_ref[...] = jnp.where(s == 0, (1 - 2 * (k % 2)).astype(F32), sn).astype(smt_ref.dtype)


def _dft_raw_kernel(cm_ref, sm_ref, z_ref, zr_ref, zi_ref):
    zr_ref[...] = _dot(cm_ref[...], z_ref[...])
    zi_ref[...] = _dot(sm_ref[...], z_ref[...])


def _dft_raw(cm, sm, z):
    n, c = z.shape
    tm = _largest_tile(n, 256, 16)
    mat = pl.BlockSpec((tm, n), lambda j: (j, 0))
    out = jax.ShapeDtypeStruct((n, c), F32)
    ospec = pl.BlockSpec((tm, c), lambda j: (j, 0))
    return pl.pallas_call(
        _dft_raw_kernel, out_shape=(out, out), grid=(n // tm,),
        in_specs=[mat, mat, pl.BlockSpec((n, c), lambda j: (0, 0))],
        out_specs=(ospec, ospec),
        compiler_params=_cparams("parallel"), name="dft_raw",
    )(cm, sm, z)


def _dft_fwd_kernel(cm_ref, sm_ref, z_ref, hr_ref, hi_ref, yr_ref, yi_ref, *, bb, tm):
    is_row0 = _row_ids(tm, pl.program_id(1)) == 0
    hr, hi = hr_ref[...], hi_ref[...]
    for i in range(bb):
        z = z_ref[i]
        zr = _dot(cm_ref[...], z)
        zi = _dot(sm_ref[...], z)
        yr = jnp.where(is_row0, 0.5 * zr * hr, zr * hr + zi * hi)
        yi = jnp.where(is_row0, 0.5 * zi * hi, zi * hr - zr * hi)
        yr_ref[i] = yr.astype(yr_ref.dtype)
        yi_ref[i] = yi.astype(yi_ref.dtype)


def _dft_inv_kernel(cm_ref, smt_ref, yr_ref, yi_ref, z_ref, g_ref, skip_ref, o_ref, *, bb):
    for i in range(bb):
        y = _dot(cm_ref[...], yr_ref[i]) + _dot(smt_ref[...], yi_ref[i])
        y = y + z_ref[i].astype(F32) * skip_ref[...]
        o_ref[i] = (g_ref[i].astype(F32) * y).astype(o_ref.dtype)


def _longconv_gated(z, gate, hr, hi, skip, mats):
    cm, sm, smt = mats
    b, n, c = z.shape
    bb = 2 if b % 2 == 0 else 1
    tm = _largest_tile(n, 256, 16)
    mat = pl.BlockSpec((tm, n), lambda i, j: (j, 0))
    full = pl.BlockSpec((bb, n, c), lambda i, j: (i, 0, 0))
    tile = pl.BlockSpec((bb, tm, c), lambda i, j: (i, j, 0))
    filt = pl.BlockSpec((tm, c), lambda i, j: (j, 0))
    spec_shape = jax.ShapeDtypeStruct((b, n, c), BF16)
    yr, yi = pl.pallas_call(
        functools.partial(_dft_fwd_kernel, bb=bb, tm=tm),
        out_shape=(spec_shape, spec_shape), grid=(b // bb, n // tm),
        in_specs=[mat, mat, full, filt, filt], out_specs=(tile, tile),
        compiler_params=_cparams("parallel", "parallel"), name="dft_fwd",
    )(cm, sm, z, hr, hi)
    return pl.pallas_call(
        functools.partial(_dft_inv_kernel, bb=bb),
        out_shape=jax.ShapeDtypeStruct((b, n, c), BF16), grid=(b // bb, n // tm),
        in_specs=[mat, mat, full, full, tile, tile, pl.BlockSpec((1, c), lambda i, j: (0, 0))],
        out_specs=tile,
        compiler_params=_cparams("parallel", "parallel"), name="dft_inv",
    )(cm, smt, yr, yi, z, gate, skip.reshape(1, c))


def _hyena_filters(n, f1, fb1, f2, fb2, f3):
    t = jnp.arange(n, dtype=F32) / n
    bands = jnp.arange(1, HYENA_BANDS + 1, dtype=F32)
    ang = 2.0 * math.pi * t[:, None] * bands
    feat = jnp.concatenate([t[:, None], jnp.sin(ang), jnp.cos(ang)], axis=-1)
    hp = lax.Precision.HIGHEST
    h = jnp.sin(jnp.dot(feat, f1, precision=hp) + fb1)
    h = jnp.sin(jnp.dot(h, f2, precision=hp) + fb2)
    h = jnp.dot(h, f3, precision=hp).reshape(n, HYENA_ORDER, 2, HYENA_WIDTH)
    deltas = jnp.linspace(-math.log(HYENA_DECAY_TARGET) / HYENA_SLOW_DECAY,
                          -math.log(HYENA_DECAY_TARGET) / HYENA_FAST_DECAY, HYENA_WIDTH, dtype=F32)
    h = h * jnp.exp(-t[:, None] * deltas)[:, None, None, :]
    return h / jnp.sum(jnp.abs(h), axis=(0, 2), keepdims=True)


def _hyena_spectra(n, filt_params, mats):
    cm, sm, _ = mats
    h = _hyena_filters(n, *filt_params)
    hf = h[:, :, 0].reshape(n, HYENA_ORDER * HYENA_WIDTH)
    hb = h[:, :, 1].reshape(n, HYENA_ORDER * HYENA_WIDTH)
    hb = jnp.where(jnp.arange(n)[:, None] == 0, 0.0, hb)
    zr, zi = _dft_raw(cm, sm, jnp.concatenate([hf + hb, hb - hf], axis=1).astype(BF16))
    oc = HYENA_ORDER * HYENA_WIDTH
    hr = zr[:, :oc]
    hi = jnp.where(jnp.arange(n)[:, None] == 0, zi[:, :oc], zi[:, oc:])
    scale = 1.0 / n
    hr = (hr * scale).reshape(n, HYENA_ORDER, HYENA_WIDTH).transpose(1, 0, 2)
    hi = (hi * scale).reshape(n, HYENA_ORDER, HYENA_WIDTH).transpose(1, 0, 2)
    return hr, hi


def _hyena_run(v, x1, x2, spectra, skip, mats):
    hr, hi = spectra
    y = v
    for o, gate in enumerate((x1, x2)):
        y = _longconv_gated(y, gate, hr[o], hi[o], skip[o], mats)
    return y


def _blockdiag_dense(w):
    nb, blk = w.shape[1], w.shape[2]
    eye = jnp.eye(nb, dtype=w.dtype)
    return jnp.einsum('dncf,nm->dncmf', w, eye).reshape(w.shape[0], nb * blk, nb * blk)


def _lru_pre_kernel(p_ref, pv_ref, nx_ref, cw_ref, cb_ref, wa_ref, ba_ref, wx_ref, bx_ref, lam_ref,
                    a_ref, b_ref, *, cl, lt, tr):
    rows = _row_ids(tr, pl.program_id(1))
    c = LRU_WIDTH
    x = p_ref[0][:, c:].astype(F32)
    pv = pv_ref[0][SUBLANES - 1:, c:].astype(F32)
    nx = nx_ref[0][:, c:].astype(F32)
    xc = (_shift_rows(x, pv, -1, rows, cl, lt) * cw_ref[0:1, :] + x * cw_ref[1:2, :]
          + _shift_rows(x, nx[:1], 1, rows, cl, lt) * cw_ref[2:3, :]
          + _shift_rows(x, nx[:2], 2, rows, cl, lt) * cw_ref[3:4, :] + cb_ref[...])
    xcb = xc.astype(BF16)
    for d in range(2):
        r = _sigmoid(_dot(xcb, wa_ref[d]) + ba_ref[d])
        i = _sigmoid(_dot(xcb, wx_ref[d]) + bx_ref[d])
        log_a = -LRU_C * r * _softplus(-lam_ref[d])
        a_ref[d, 0] = jnp.exp(log_a)
        b_ref[d, 0] = jnp.sqrt(1.0 - jnp.exp(2.0 * log_a)) * (i * xc)


def _lru_pre(plr, conv_w, conv_b, wa, ba, wx, bx, lam, cl):
    b, lt, w = plr.shape
    c = LRU_WIDTH
    tr = _largest_tile(lt, 1088, 16)
    prev_spec, next_spec = _halo_specs(tr, lt, w, (1,))
    out = jax.ShapeDtypeStruct((2, b, lt, c), F32)
    ospec = pl.BlockSpec((2, 1, tr, c), lambda i, r: (0, i, r, 0))
    const2 = lambda shape: pl.BlockSpec(shape, lambda i, r: (0,) * len(shape))
    return pl.pallas_call(
        functools.partial(_lru_pre_kernel, cl=cl, lt=lt, tr=tr),
        out_shape=(out, out), grid=(b, lt // tr),
        in_specs=[pl.BlockSpec((1, tr, w), lambda i, r: (i, r, 0)), prev_spec, next_spec,
                  const2(conv_w.shape), const2((1, c)), const2((2, c, c)), const2((2, 1, c)),
                  const2((2, c, c)), const2((2, 1, c)), const2((2, 1, c))],
        out_specs=(ospec, ospec),
        compiler_params=_cparams("parallel", "parallel"), name="lru_pre",
    )(plr, plr, plr, conv_w, conv_b.reshape(1, c), _blockdiag_dense(wa).astype(BF16), ba.reshape(2, 1, c),
      _blockdiag_dense(wx).astype(BF16), bx.reshape(2, 1, c), lam.reshape(2, 1, c))


def _gelu_tanh(x):
    return 0.5 * x * (1.0 + jnp.tanh(math.sqrt(2.0 / math.pi) * (x + 0.044715 * (x * x * x))))


def _lru_scan_kernel(af_ref, bf_ref, ar_ref, br_ref, g_ref, o_ref, acc_ref, *, cl, lt):
    row = lax.broadcasted_iota(jnp.int32, (SUBLANES, LANES), 0)

    def group_scan(a, b, reverse):
        for s in (1, 2, 4):
            if reverse:
                keep = row < SUBLANES - s
                a_s = jnp.where(keep, pltpu.roll(a, SUBLANES - s, 0), 1.0)
                b_s = jnp.where(keep, pltpu.roll(b, SUBLANES - s, 0), 0.0)
            else:
                keep = row >= s
                a_s = jnp.where(keep, pltpu.roll(a, s, 0), 1.0)
                b_s = jnp.where(keep, pltpu.roll(b, s, 0), 0.0)
            b = a * b_s + b
            a = a * a_s
        return a, b

    def fwd_body(g, h):
        sl = pl.ds(pl.multiple_of(g * SUBLANES, SUBLANES), SUBLANES)
        a, b = group_scan(af_ref[0, 0, sl, :], bf_ref[0, 0, sl, :], False)
        hh = a * h + b
        acc_ref[sl, :] = hh
        return hh[SUBLANES - 1:SUBLANES, :]

    lax.fori_loop(0, lt // SUBLANES, fwd_body, jnp.zeros((1, LANES), F32))

    def rev_body(i, h, top):
        g = top - 1 - i
        sl = pl.ds(pl.multiple_of(g * SUBLANES, SUBLANES), SUBLANES)
        a, b = group_scan(ar_ref[0, 0, sl, :], br_ref[0, 0, sl, :], True)
        hh = a * h + b
        gate = g_ref[0, sl, :].astype(F32)
        o_ref[0, sl, :] = ((acc_ref[sl, :] + hh) * _gelu_tanh(gate)).astype(o_ref.dtype)
        return hh[0:1, :]

    h = lax.fori_loop(0, cl // SUBLANES, functools.partial(rev_body, top=cl // SUBLANES), jnp.zeros((1, LANES), F32))
    lax.fori_loop(0, (lt - cl) // SUBLANES, functools.partial(rev_body, top=lt // SUBLANES), h)


def _lru_scan(a, b_, plr, cl):
    _, b, lt, c = a.shape
    nl = c // LANES
    fwd = pl.BlockSpec((1, 1, lt, LANES), lambda i, j: (0, i, 0, j))
    rev = pl.BlockSpec((1, 1, lt, LANES), lambda i, j: (1, i, 0, j))
    return pl.pallas_call(
        functools.partial(_lru_scan_kernel, cl=cl, lt=lt),
        out_shape=jax.ShapeDtypeStruct((b, lt, c), BF16), grid=(b, nl),
        in_specs=[fwd, fwd, rev, rev, pl.BlockSpec((1, lt, LANES), lambda i, j: (i, 0, j))],
        out_specs=pl.BlockSpec((1, lt, LANES), lambda i, j: (i, 0, j)),
        scratch_shapes=[pltpu.VMEM((lt, LANES), F32)],
        compiler_params=_cparams("parallel", "parallel"), name="lru_scan",
    )(a, b_, a, b_, plr)


def _row_perm(tr, rev):
    t = lax.broadcasted_iota(jnp.int32, (tr, tr), 0)
    s = lax.broadcasted_iota(jnp.int32, (tr, tr), 1)
    return jnp.where(s == jnp.where(rev, tr - 1 - t, t), 1.0, 0.0).astype(BF16)


def _mirror_tile(r, rev, n_ctx_tiles, n_tiles):
    m = jnp.where(r < n_ctx_tiles, n_ctx_tiles - 1 - r, n_tiles - 1 + n_ctx_tiles - r)
    return jnp.where(rev, m, r)


def _chunk_masks(tr):
    t = lax.broadcasted_iota(jnp.int32, (tr, tr), 0)
    s = lax.broadcasted_iota(jnp.int32, (tr, tr), 1)
    same = (t // RWKV_CHUNK) == (s // RWKV_CHUNK)
    return same, same & (s <= t), same & (s < t)


def _rwkv_prep_kernel(p_ref, pv_ref, nx_ref, mu_ref, w0_ref, w2_ref, a0_ref, a2_ref, g2_ref, kk_ref, ka_ref,
                      rk_ref, ones_ref,
                      aq_ref, vp_ref, y0_ref, rt_ref, mrb_ref, bht_ref, gm_ref, pc_ref, g_ref, bonus_ref,
                      *, cl, lt, tr):
    c = RWKV_WIDTH
    hd = HEAD_DIM
    ch = RWKV_CHUNK
    nch = tr // ch
    rows = _row_ids(tr, pl.program_id(2))
    rev = pl.program_id(0) == 1
    p = _dot(_row_perm(tr, rev), p_ref[0])
    before = jnp.where(rev, nx_ref[0, :1, :], pv_ref[0, SUBLANES - 1:, :]).astype(F32)
    after = jnp.where(rev, pv_ref[0, SUBLANES - 1:, :], nx_ref[0, :1, :]).astype(F32)
    prev = _shift_rows(p, before, -1, rows, cl, lt)
    nxt = _shift_rows(p, after, 1, rows, cl, lt)
    xm = p + (prev - p) * mu_ref[0, 0:1, :] + (nxt - p) * mu_ref[0, 1:2, :]
    r, k, v = xm[:, :c], xm[:, c:2 * c], xm[:, 2 * c:3 * c]
    o = 3 * c
    w1 = xm[:, o:o + 2 * RWKV_DECAY_RANK]
    a1 = xm[:, o + 2 * RWKV_DECAY_RANK:o + 2 * RWKV_DECAY_RANK + 2 * RWKV_ICLR_RANK]
    g1 = xm[:, o + 2 * RWKV_DECAY_RANK + 2 * RWKV_ICLR_RANK:]
    wlog = -_softplus(-(w0_ref[0] + _dot(jnp.tanh(w1).astype(BF16), w2_ref[0]))) - 0.5
    ld = -jnp.exp(wlog)
    a = _sigmoid(a0_ref[0] + _dot(a1.astype(BF16), a2_ref[0]))
    g_ref[0, 0] = _dot(_sigmoid(g1).astype(BF16), g2_ref[...])
    kk = k * kk_ref[...]
    kk = kk * lax.rsqrt(_dot_exact_rhs(kk * kk, ones_ref[...]) + 1e-12)
    kd = k * (1.0 + (a - 1.0) * ka_ref[...])
    bonus_ref[0, 0] = _dot_exact_rhs(r * kd * rk_ref[...], ones_ref[...]) * v
    beta = kk * a

    _rwkv_chunk_stage(ld, kk, r, kd, beta, v, aq_ref, vp_ref, y0_ref, rt_ref, mrb_ref, bht_ref, gm_ref, pc_ref)


def _rwkv_chunk_stage(ld, kk, r, kd, beta, v, aq_ref, vp_ref, y0_ref, rt_ref, mrb_ref, bht_ref, gm_ref, pc_ref):
    tr = ld.shape[0]
    n = min(RWKV_STAGE_ROWS, tr)
    nparts = tr // n
    hd = HEAD_DIM
    ch = RWKV_CHUNK
    nch = n // ch
    same_t, incl_t, _ = _chunk_masks(tr)
    cum = _dot_exact_lhs(jnp.where(incl_t, 1.0, 0.0).astype(BF16), ld)
    tot = _dot_exact_lhs(jnp.where(same_t, 1.0, 0.0).astype(BF16), ld)
    _, incl, strict = _chunk_masks(n)
    alpha_t = kk * jnp.exp(cum - ld)
    r_t = r * jnp.exp(cum)
    e_neg = jnp.exp(-cum)
    k_t = kd * e_neg
    b_t = beta * e_neg
    e_rem = jnp.exp(tot - cum)
    k_hat_t = (kd * e_rem).T
    b_hat_t = (beta * e_rem).T
    pc_t = jnp.exp(tot).T
    t_i = lax.broadcasted_iota(jnp.int32, (n, n), 0)
    s_i = lax.broadcasted_iota(jnp.int32, (n, n), 1)
    eye_f = jnp.where(t_i == s_i, 1.0, 0.0)
    same_blk = []
    size = RWKV_INV_BASE
    while size <= ch:
        same_blk.append((t_i // size) == (s_i // size))
        size *= 2
    col_chunk = lax.broadcasted_iota(jnp.int32, (n, nch * hd), 1) // hd
    row_chunk = lax.broadcasted_iota(jnp.int32, (n, nch * hd), 0) // ch

    def diag_blocks(m):
        out = m[:, :ch]
        for j in range(1, nch):
            out = out + m[:, j * ch:(j + 1) * ch]
        return out

    chains = [(p, h) for p in range(nparts) for h in range(RWKV_HEADS)]

    def blk(t, p, h):
        return t[p * n:(p + 1) * n, h * hd:(h + 1) * hd]

    prods = [_dot_nt(jnp.concatenate([blk(alpha_t, p, h), blk(r_t, p, h)], axis=0).astype(BF16),
                     jnp.concatenate([blk(b_t, p, h), blk(k_t, p, h)], axis=0).astype(BF16))
             for p, h in chains]
    l_ab = [jnp.where(strict, pr[:n, :n], 0.0) for pr in prods]
    pw = [jnp.where(same_blk[0], l, 0.0) for l in l_ab]
    t_inv = [eye_f - l for l in pw]
    for _ in range(int(math.log2(RWKV_INV_BASE)) - 1):
        pw = [_dot_bf(q, q) for q in pw]
        t_inv = [t + _dot_bf(t, q) for t, q in zip(t_inv, pw)]
    for lvl in range(1, len(same_blk)):
        off = same_blk[lvl] & jnp.logical_not(same_blk[lvl - 1])
        half = [_dot_bf(t, jnp.where(off, l, 0.0)) for t, l in zip(t_inv, l_ab)]
        t_inv = [t - _dot_bf(hf, t) for t, hf in zip(t_inv, half)]
    vh = [blk(v, p, h).astype(BF16) for p, h in chains]
    lakv = [_dot(jnp.where(strict, pr[:n, n:], 0.0).astype(BF16), vv) for pr, vv in zip(prods, vh)]
    x = [_dot(t.astype(BF16), jnp.concatenate([blk(alpha_t, p, h), lv], axis=1).astype(BF16))
         for t, (p, h), lv in zip(t_inv, chains, lakv)]
    y0 = [_dot(jnp.where(incl, pr[n:, n:], 0.0).astype(BF16), vv) for pr, vv in zip(prods, vh)]
    mrb = [diag_blocks(jnp.where(incl, pr[n:, :n], 0.0)) for pr in prods]
    gms = [_dot(k_hat_t[h * hd:(h + 1) * hd, p * n:(p + 1) * n].astype(BF16),
                jnp.where(col_chunk == row_chunk, jnp.concatenate([blk(v, p, h)] * nch, axis=1), 0.0).astype(BF16))
           for p, h in chains]

    def assemble(parts):
        return jnp.concatenate([jnp.concatenate(parts[p * RWKV_HEADS:(p + 1) * RWKV_HEADS], axis=1)
                                for p in range(nparts)], axis=0)

    aq_ref[0, 0] = assemble([t[:, :hd] for t in x]).astype(aq_ref.dtype)
    vp_ref[0, 0] = assemble([t[:, hd:] for t in x])
    y0_ref[0, 0] = assemble(y0)
    rt_ref[0, 0] = r_t.astype(rt_ref.dtype)
    mrb_ref[0, 0] = assemble(mrb).astype(mrb_ref.dtype)
    for p in range(nparts):
        for j in range(nch):
            jj = p * nch + j
            cs = slice(jj * ch, (jj + 1) * ch)
            bht_ref[0, 0, jj] = jnp.concatenate([b_hat_t[h * hd:(h + 1) * hd, cs] for h in range(RWKV_HEADS)],
                                                axis=1).astype(bht_ref.dtype)
            gm_ref[0, 0, jj] = jnp.concatenate([gms[p * RWKV_HEADS + h][:, j * hd:(j + 1) * hd]
                                                for h in range(RWKV_HEADS)], axis=1)
            pc_ref[0, 0, jj] = jnp.concatenate([pc_t[h * hd:(h + 1) * hd, cs] for h in range(RWKV_HEADS)], axis=1)


def _rwkv_tile(lt, cl):
    tr = 256 if (lt % 256 == 0 and cl % 256 == 0) else 128
    assert lt % tr == 0 and cl % tr == 0
    return tr


def _rwkv_prep(prw, mud, w0, w2p, a0, a2p, g2, k_k, k_a, r_k, cl):
    b, lt, w = prw.shape
    c = RWKV_WIDTH
    tr = _rwkv_tile(lt, cl)
    nch = tr // RWKV_CHUNK
    per = tr // SUBLANES
    src = lambda d, r: _mirror_tile(r, d == 1, cl // tr, lt // tr)
    prev_spec = pl.BlockSpec((1, SUBLANES, w), lambda d, i, r: (i, jnp.maximum(src(d, r) * per - 1, 0), 0))
    next_spec = pl.BlockSpec((1, SUBLANES, w),
                             lambda d, i, r: (i, jnp.minimum((src(d, r) + 1) * per, lt // SUBLANES - 1), 0))
    tile = pl.BlockSpec((1, 1, tr, c), lambda d, i, r: (d, i, r, 0))
    per_dir = lambda shape: pl.BlockSpec((1, *shape), lambda d, i, r: (d,) + (0,) * len(shape))
    const = lambda shape: pl.BlockSpec(shape, lambda d, i, r: (0,) * len(shape))
    seq = lambda dt: jax.ShapeDtypeStruct((2, b, lt, c), dt)
    chunked = jax.ShapeDtypeStruct((2, b, lt // RWKV_CHUNK, HEAD_DIM, c), F32)
    chunk_spec = pl.BlockSpec((1, 1, nch, HEAD_DIM, c), lambda d, i, r: (d, i, r, 0, 0))
    return pl.pallas_call(
        functools.partial(_rwkv_prep_kernel, cl=cl, lt=lt, tr=tr),
        out_shape=(seq(BF16), seq(F32), seq(F32), seq(BF16), seq(BF16),
                   jax.ShapeDtypeStruct(chunked.shape, BF16), chunked, chunked, seq(F32), seq(F32)),
        grid=(2, b, lt // tr),
        in_specs=[pl.BlockSpec((1, tr, w), lambda d, i, r: (i, src(d, r), 0)), prev_spec, next_spec,
                  per_dir((2, w)), per_dir((1, c)), per_dir((2 * RWKV_DECAY_RANK, c)), per_dir((1, c)),
                  per_dir((2 * RWKV_ICLR_RANK, c)), const((RWKV_GATE_RANK, c)), const((1, c)), const((1, c)),
                  const((1, c)), const((c, c))],
        out_specs=(tile, tile, tile, tile, tile, chunk_spec, chunk_spec, chunk_spec, tile, tile),
        compiler_params=_cparams("parallel", "parallel", "parallel"), name="rwkv_prep",
    )(prw, prw, prw, mud, w0.reshape(2, 1, c), w2p, a0.reshape(2, 1, c), a2p, g2.astype(BF16), k_k.reshape(1, c),
      k_a.reshape(1, c), r_k.reshape(1, c), _head_ones(c))


def _rwkv_scan_kernel(aq_ref, vp_ref, y0_ref, rt_ref, mrb_ref, bht_ref, gm_ref, pc_ref, y_ref, s_ref, *, ns):
    @pl.when(pl.program_id(1) == 0)
    def _():
        s_ref[...] = jnp.zeros_like(s_ref)

    c = RWKV_WIDTH
    ch = RWKV_CHUNK
    same_head = (lax.broadcasted_iota(jnp.int32, (c, c), 0) // HEAD_DIM
                 == lax.broadcasted_iota(jnp.int32, (c, c), 1) // HEAD_DIM)

    def head_blockdiag(t):
        return jnp.where(same_head, jnp.concatenate([t] * RWKV_HEADS, axis=0), 0.0).astype(BF16)

    for i in range(ns):
        s0 = s_ref[i]
        r1 = _dot(jnp.concatenate([aq_ref[i, 0], rt_ref[i, 0]], axis=0), head_blockdiag(s0))
        u = r1[:ch] + vp_ref[i, 0]
        r2 = _dot(jnp.concatenate([mrb_ref[i, 0], bht_ref[i, 0, 0]], axis=0), head_blockdiag(u))
        y_ref[i, 0] = r1[ch:] + y0_ref[i, 0] - r2[:ch]
        s_ref[i] = pc_ref[i, 0, 0] * s0 + gm_ref[i, 0, 0] - r2[ch:]


def _rwkv_scan(aq, vp, y0, rt, mrb, bht, gm, pc):
    _, b, lt, c = aq.shape
    nstream = 2 * b
    ns = nstream
    ch = RWKV_CHUNK
    merge = lambda t: t.reshape(nstream, 1, *t.shape[2:])
    tile = pl.BlockSpec((ns, 1, ch, c), lambda s, j: (s, 0, j, 0))
    chunk_spec = pl.BlockSpec((ns, 1, 1, HEAD_DIM, c), lambda s, j: (s, 0, j, 0, 0))
    y = pl.pallas_call(
        functools.partial(_rwkv_scan_kernel, ns=ns),
        out_shape=jax.ShapeDtypeStruct((nstream, 1, lt, c), F32), grid=(nstream // ns, lt // ch),
        in_specs=[tile, tile, tile, tile, tile, chunk_spec, chunk_spec, chunk_spec],
        out_specs=tile,
        scratch_shapes=[pltpu.VMEM((ns, HEAD_DIM, c), F32)],
        compiler_params=_cparams("parallel", "arbitrary"), name="rwkv_scan",
    )(*(merge(t) for t in (aq, vp, y0, rt, mrb, bht, gm, pc)))
    return y.reshape(2, b, lt, c)


def _rwkv_readout_kernel(yf_ref, yr_ref, bf_ref, br_ref, g_ref, lw_ref, lb_ref, ones_ref, o_ref, *, tr):
    unflip = _row_perm(tr, True)
    y = yf_ref[0, 0] + _dot_exact_lhs(unflip, yr_ref[0, 0])
    bonus = bf_ref[0, 0] + _dot_exact_lhs(unflip, br_ref[0, 0])
    inv = 1.0 / HEAD_DIM
    mu = _dot_exact_rhs(y, ones_ref[...]) * inv
    yc = y - mu
    var = _dot_exact_rhs(yc * yc, ones_ref[...]) * inv
    yn = yc * lax.rsqrt(var + RWKV_GN_EPS) * lw_ref[...] + lb_ref[...]
    o_ref[0] = ((yn + bonus) * g_ref[0, 0]).astype(o_ref.dtype)


def _rwkv_readout(y, bonus, g, ln_w, ln_b, cl):
    _, b, lt, c = y.shape
    tr = _rwkv_tile(lt, cl)
    fwd = pl.BlockSpec((1, 1, tr, c), lambda i, r: (0, i, r, 0))
    rev = pl.BlockSpec((1, 1, tr, c), lambda i, r: (1, i, _mirror_tile(r, True, cl // tr, lt // tr), 0))
    row = pl.BlockSpec((1, c), lambda i, r: (0, 0))
    return pl.pallas_call(
        functools.partial(_rwkv_readout_kernel, tr=tr),
        out_shape=jax.ShapeDtypeStruct((b, lt, c), BF16), grid=(b, lt // tr),
        in_specs=[fwd, rev, fwd, rev, fwd, row, row, pl.BlockSpec((c, c), lambda i, r: (0, 0))],
        out_specs=pl.BlockSpec((1, tr, c), lambda i, r: (i, r, 0)),
        compiler_params=_cparams("parallel", "parallel"), name="rwkv_readout",
    )(y, y, bonus, bonus, g, ln_w.reshape(1, c), ln_b.reshape(1, c), _head_ones(c))


def _pad_rank_rows(w):
    z = jnp.zeros_like(w[0])
    return jnp.stack([jnp.concatenate([w[0], z], 0), jnp.concatenate([z, w[1]], 0)])


def _rwkv_mixer(prw, mu, w0, w2, a0, a2, g2, k_k, k_a, r_k, ln_w, ln_b, cl):
    mud = jnp.stack([mu, mu[::-1]])
    outs = _rwkv_prep(prw, mud, w0, _pad_rank_rows(w2).astype(BF16), a0, _pad_rank_rows(a2).astype(BF16), g2,
                      k_k, k_a, r_k, cl)
    *scan_in, g, bonus = outs
    y = _rwkv_scan(*scan_in)
    return _rwkv_readout(y, bonus, g, ln_w, ln_b, cl)


def _merge_kernel(ya_ref, yh_ref, yr_ref, yl_ref, gt_ref, x_ref, modx_ref, modc_ref,
                  wa_ref, wh_ref, wr_ref, wl_ref, wo_ref, o_ref, *, cl, tm, d):
    is_ctx = _row_ids(tm, pl.program_id(1)) < cl
    m = None
    for i, (y_ref, w_ref) in enumerate(((ya_ref, wa_ref), (yh_ref, wh_ref), (yr_ref, wr_ref), (yl_ref, wl_ref))):
        gate = _sigmoid(gt_ref[0, :, i * d:(i + 1) * d].astype(F32))
        term = gate * _dot(y_ref[0], w_ref[...])
        m = term if m is None else m + term
    g1 = _mod_rows(modx_ref, modc_ref, 2, is_ctx)
    o_ref[0] = x_ref[0] + g1 * _dot(m.astype(BF16), wo_ref[...])


def _merge(ys, gates, xc, mod, w_brs, w_out, cl):
    b, lt, d = xc.shape
    tm = _largest_tile(lt, 544, 16)
    row = lambda w: pl.BlockSpec((1, tm, w), lambda i, r: (i, r, 0))
    const = lambda a: pl.BlockSpec(a.shape, lambda i, r: (0, 0))
    ws = [w.astype(BF16) for w in w_brs] + [w_out.astype(BF16)]
    return pl.pallas_call(
        functools.partial(_merge_kernel, cl=cl, tm=tm, d=d),
        out_shape=jax.ShapeDtypeStruct((b, lt, d), F32), grid=(b, lt // tm),
        in_specs=[row(y.shape[-1]) for y in ys] + [row(N_BRANCH * d), row(d),
                  pl.BlockSpec((1, 6, d), lambda i, r: (i, 0, 0)), pl.BlockSpec((1, 6, d), lambda i, r: (b, 0, 0))]
                 + [const(w) for w in ws],
        out_specs=row(d), compiler_params=_cparams("parallel", "parallel"), name="merge",
    )(*ys, gates, xc, mod, mod, *ws)


def _route_kernel(x_ref, modx_ref, modc_ref, wr_ref, br_ref, h_ref, cmb_ref, *, cl, tm):
    is_ctx = _row_ids(tm, pl.program_id(1)) < cl
    h = _rms_modulate(x_ref[0], _mod_rows(modx_ref, modc_ref, 3, is_ctx), _mod_rows(modx_ref, modc_ref, 4, is_ctx))
    h_ref[0] = h.astype(h_ref.dtype)
    lg = jnp.dot(h, wr_ref[...], preferred_element_type=F32, precision=lax.Precision.HIGHEST) + br_ref[...]
    lane = lax.broadcasted_iota(jnp.int32, lg.shape, 1)
    lane_f = lane.astype(F32)
    neg = -jnp.inf
    big = 1e9

    def first_lane(cond):
        return jnp.min(jnp.where(cond, lane_f, big), axis=-1, keepdims=True)

    is_grp = (lane >= N_EXPERTS) & (lane < N_EXPERTS + N_GROUPS)
    gl = jnp.where(is_grp, lg, neg)
    gmax = jnp.max(gl, axis=-1, keepdims=True)
    ge = jnp.where(is_grp, jnp.exp(gl - gmax), 0.0)
    gp = ge / jnp.sum(ge, axis=-1, keepdims=True)
    g_val = jnp.max(gp, axis=-1, keepdims=True)
    g_idx = first_lane(is_grp & (gp == g_val)) - N_EXPERTS
    lo = g_idx * EXPERTS_PER_GROUP
    in_grp = (lane_f >= lo) & (lane_f < lo + EXPERTS_PER_GROUP)
    el = jnp.where(in_grp, lg, neg)
    emax = jnp.max(el, axis=-1, keepdims=True)
    ee = jnp.where(in_grp, jnp.exp(el - emax), 0.0)
    pe = ee / jnp.sum(ee, axis=-1, keepdims=True)
    v1 = jnp.max(jnp.where(in_grp, pe, -1.0), axis=-1, keepdims=True)
    i1 = first_lane(in_grp & (pe == v1))
    rest = in_grp & (lane_f != i1)
    v2 = jnp.max(jnp.where(rest, pe, -1.0), axis=-1, keepdims=True)
    i2 = first_lane(rest & (pe == v2))
    den = v1 + v2
    cmb_ref[0] = jnp.where(lane_f == i1, g_val * v1 / den, 0.0) + jnp.where(lane_f == i2, g_val * v2 / den, 0.0)


def _route(xc, mod, w_grp, b_grp, w_rt, b_rt, cl):
    b, lt, d = xc.shape
    tm = _largest_tile(lt, 544, 16)
    wr = jnp.zeros((d, LANES), F32).at[:, :N_EXPERTS].set(w_rt).at[:, N_EXPERTS:N_EXPERTS + N_GROUPS].set(w_grp)
    br = jnp.zeros((1, LANES), F32).at[0, :N_EXPERTS].set(b_rt).at[0, N_EXPERTS:N_EXPERTS + N_GROUPS].set(b_grp)
    return pl.pallas_call(
        functools.partial(_route_kernel, cl=cl, tm=tm),
        out_shape=(jax.ShapeDtypeStruct((b, lt, d), BF16), jax.ShapeDtypeStruct((b, lt, LANES), F32)),
        grid=(b, lt // tm),
        in_specs=[pl.BlockSpec((1, tm, d), lambda i, r: (i, r, 0)),
                  pl.BlockSpec((1, 6, d), lambda i, r: (i, 0, 0)), pl.BlockSpec((1, 6, d), lambda i, r: (b, 0, 0)),
                  pl.BlockSpec((d, LANES), lambda i, r: (0, 0)), pl.BlockSpec((1, LANES), lambda i, r: (0, 0))],
        out_specs=(pl.BlockSpec((1, tm, d), lambda i, r: (i, r, 0)), pl.BlockSpec((1, tm, LANES), lambda i, r: (i, r, 0))),
        compiler_params=_cparams("parallel", "parallel"), name="moe_route",
    )(xc, mod, mod, wr, br)


def _moe_kernel(h_ref, cmb_ref, x_ref, modx_ref, modc_ref, w1_ref, w3_ref, w2_ref, o_ref, acc_ref, *, cl, tm):
    e = pl.program_id(2)

    @pl.when(e == 0)
    def _():
        acc_ref[...] = jnp.zeros_like(acc_ref)

    h = h_ref[0]
    t = _silu(_dot(h, w1_ref[0])) * _dot(h, w3_ref[0])
    y = _dot(t.astype(BF16), w2_ref[0])
    lane = lax.broadcasted_iota(jnp.int32, (tm, LANES), 1)
    wcol = jnp.sum(jnp.where(lane == e, cmb_ref[0], 0.0), axis=-1, keepdims=True)
    acc_ref[...] += wcol * y

    @pl.when(e == pl.num_programs(2) - 1)
    def _():
        is_ctx = _row_ids(tm, pl.program_id(1)) < cl
        o_ref[0] = x_ref[0] + _mod_rows(modx_ref, modc_ref, 5, is_ctx) * acc_ref[...]


def _moe(h2, cmb, xc, mod, w1, w3, w2, cl):
    b, lt, d = xc.shape
    ne, _, hid = w1.shape
    tm = _largest_tile(lt, 1088, 16)
    return pl.pallas_call(
        functools.partial(_moe_kernel, cl=cl, tm=tm),
        out_shape=jax.ShapeDtypeStruct((b, lt, d), F32), grid=(b, lt // tm, ne),
        in_specs=[pl.BlockSpec((1, tm, d), lambda i, r, e: (i, r, 0)),
                  pl.BlockSpec((1, tm, LANES), lambda i, r, e: (i, r, 0)),
                  pl.BlockSpec((1, tm, d), lambda i, r, e: (i, r, 0)),
                  pl.BlockSpec((1, 6, d), lambda i, r, e: (i, 0, 0)),
                  pl.BlockSpec((1, 6, d), lambda i, r, e: (b, 0, 0)),
                  pl.BlockSpec((1, d, hid), lambda i, r, e: (e, 0, 0)),
                  pl.BlockSpec((1, d, hid), lambda i, r, e: (e, 0, 0)),
                  pl.BlockSpec((1, hid, d), lambda i, r, e: (e, 0, 0))],
        out_specs=pl.BlockSpec((1, tm, d), lambda i, r, e: (i, r, 0)),
        scratch_shapes=[pltpu.VMEM((tm, d), F32)],
        compiler_params=_cparams("parallel", "parallel", "arbitrary"), name="moe_experts",
    )(h2, cmb, xc, mod, mod, w1.astype(BF16), w3.astype(BF16), w2.astype(BF16))


def kernel(x, c, ctx, c_ctx, ada_w, ada_b, w_in, q_norm, k_norm, hy_conv_w, hy_conv_b, hy_f1, hy_fb1, hy_f2, hy_fb2, hy_f3, hy_skip, rw_mu, rw_w0, rw_w2, rw_a0, rw_a2, rw_g2, rw_k_k, rw_k_a, rw_r_k, rw_ln_w, rw_ln_b, lru_conv_w, lru_conv_b, lru_wa, lru_ba, lru_wx, lru_bx, lru_lambda, w_br_attn, w_br_hyena, w_br_rwkv, w_br_lru, w_out, moe_w_grp, moe_b_grp, moe_w_rt, moe_b_rt, moe_w1, moe_w3, moe_w2):
    b, l, d = x.shape
    cl = ctx.shape[1]
    depth = ada_w.shape[0]
    assert b < MOD_ROWS and cl % RWKV_CHUNK == 0 and l % RWKV_CHUNK == 0

    xc = jnp.concatenate([ctx, x], axis=1)
    cc = jnp.zeros((MOD_ROWS, d), F32).at[:b].set(c).at[b].set(c_ctx)
    mod_all = _ada_mod(cc, ada_w, ada_b).reshape(depth, MOD_ROWS, 6, d)

    cos2, sin2 = _rope_tables(l, cl)
    mats_x = _dft_mats(l)
    mats_c = _dft_mats(cl)
    qkv_w = ATTN_WIDTH + 2 * ATTN_KV_WIDTH
    col = np.cumsum([0, qkv_w, 3 * HYENA_WIDTH, RWKV_PROJ, 2 * LRU_WIDTH, N_BRANCH * d])

    for i in range(depth):
        need_ctx = i < depth - 1
        mod = mod_all[i]
        w_i = w_in[i].astype(BF16)
        pqkv, phy, prw, plr, gates = (_modmm(xc, mod, w_i[:, col[j]:col[j + 1]], cl) for j in range(5))

        qn, kt, vx = _attn_prep(pqkv, cos2, sin2, q_norm[i], k_norm[i])
        y_att = _attention(qn, kt, vx, cl)

        hv, hx1, hx2 = _hyena_pre(phy, hy_conv_w[i], hy_conv_b[i], cl)
        filt = (hy_f1[i], hy_fb1[i], hy_f2[i], hy_fb2[i], hy_f3[i])
        y_hx = _hyena_run(hv[:, cl:], hx1[:, cl:], hx2[:, cl:], _hyena_spectra(l, filt, mats_x), hy_skip[i], mats_x)
        if need_ctx:
            y_hc = _hyena_run(hv[:, :cl], hx1[:, :cl], hx2[:, :cl], _hyena_spectra(cl, filt, mats_c), hy_skip[i],
                              mats_c)
        else:
            y_hc = jnp.zeros((b, cl, HYENA_WIDTH), BF16)
        y_hy = jnp.concatenate([y_hc, y_hx], axis=1)

        y_rw = _rwkv_mixer(prw, rw_mu[i], rw_w0[i], rw_w2[i], rw_a0[i], rw_a2[i], rw_g2[i], rw_k_k[i], rw_k_a[i],
                           rw_r_k[i].reshape(-1), rw_ln_w[i], rw_ln_b[i], cl)

        la, lb = _lru_pre(plr, lru_conv_w[i], lru_conv_b[i], lru_wa[i], lru_ba[i], lru_wx[i], lru_bx[i],
                          lru_lambda[i], cl)
        y_lr = _lru_scan(la, lb, plr, cl)

        xc = _merge((y_att, y_hy, y_rw, y_lr), gates, xc, mod,
                    (w_br_attn[i], w_br_hyena[i], w_br_rwkv[i], w_br_lru[i]), w_out[i], cl)

        h2, cmb = _route(xc, mod, moe_w_grp[i], moe_b_grp[i], moe_w_rt[i], moe_b_rt[i], cl)
        xc = _moe(h2, cmb, xc, mod, moe_w1[i], moe_w3[i], moe_w2[i], cl)
    return xc[:, cl:]
```

```python
import functools
import math

import numpy as np
import jax
import jax.numpy as jnp
from jax import lax
from jax.experimental import pallas as pl
from jax.experimental.pallas import tpu as pltpu

F32 = jnp.float32
BF16 = jnp.bfloat16

HEAD_DIM = 64
GRID_W = 64
EPS = 1e-6
ATTN_HEADS = 8
ATTN_KV_HEADS = 2
ATTN_GROUP = ATTN_HEADS // ATTN_KV_HEADS
ATTN_Q_BLOCKS = 2
ATTN_WIDTH = ATTN_HEADS * HEAD_DIM
ATTN_KV_WIDTH = ATTN_KV_HEADS * HEAD_DIM
ROPE_THETA = 10000.0
HYENA_WIDTH = 256
HYENA_ORDER = 2
HYENA_BANDS = 16
HYENA_DECAY_TARGET = 1e-2
HYENA_FAST_DECAY = 0.3
HYENA_SLOW_DECAY = 1.5
RWKV_HEADS = 4
RWKV_WIDTH = RWKV_HEADS * HEAD_DIM
RWKV_DECAY_RANK = 64
RWKV_ICLR_RANK = 64
RWKV_GATE_RANK = 128
RWKV_GN_EPS = 64e-5
RWKV_PROJ = 3 * RWKV_WIDTH + 2 * RWKV_DECAY_RANK + 2 * RWKV_ICLR_RANK + RWKV_GATE_RANK
RWKV_CHUNK = 64
RWKV_INV_BASE = 4
RWKV_STAGE_ROWS = 128
LRU_WIDTH = 256
LRU_BLOCKS = 4
LRU_C = 8.0
LRU_GROUPS_PER_STEP = 4
N_BRANCH = 4
N_GROUPS = 4
EXPERTS_PER_GROUP = 4
N_EXPERTS = N_GROUPS * EXPERTS_PER_GROUP

V7X_VMEM_LIMIT_BYTES = 52 * 1024 * 1024
SUBLANES = 8
LANES = 128
MOD_ROWS = 16


def _cparams(*sem):
    return pltpu.CompilerParams(dimension_semantics=sem, vmem_limit_bytes=V7X_VMEM_LIMIT_BYTES)


def _dot(a, b):
    return jnp.dot(a, b, preferred_element_type=F32)


def _dot_nt(a, b):
    return lax.dot_general(a, b, (((1,), (1,)), ((), ())), preferred_element_type=F32)


def _split3(x):
    hi = x.astype(BF16)
    r1 = x - hi.astype(F32)
    mid = r1.astype(BF16)
    lo = (r1 - mid.astype(F32)).astype(BF16)
    return hi, mid, lo


def _dot_exact_lhs(m_bf16, x):
    hi, mid, lo = _split3(x)
    return _dot(m_bf16, hi) + _dot(m_bf16, mid) + _dot(m_bf16, lo)


def _dot_exact_rhs(x, m_bf16):
    hi, mid, lo = _split3(x)
    return _dot(hi, m_bf16) + _dot(mid, m_bf16) + _dot(lo, m_bf16)


def _dot_bf(a, b):
    return _dot(a.astype(BF16), b.astype(BF16))


def _sigmoid(x):
    return 1.0 / (1.0 + jnp.exp(-x))


def _softplus(x):
    return jnp.maximum(x, 0.0) + jnp.log(1.0 + jnp.exp(-jnp.abs(x)))


def _silu(x):
    return x * _sigmoid(x)


def _largest_tile(n, cap, mult):
    best = None
    for t in range(mult, min(n, cap) + 1, mult):
        if n % t == 0:
            best = t
    assert best is not None, (n, cap, mult)
    return best


def _head_ones(width):
    idx = np.arange(width) // HEAD_DIM
    return jnp.asarray((idx[:, None] == idx[None, :]).astype(np.float32), dtype=BF16)


def _row_ids(tile_rows, tile_idx):
    return tile_idx * tile_rows + lax.broadcasted_iota(jnp.int32, (tile_rows, 1), 0)


def _mod_rows(modx_ref, modc_ref, idx, is_ctx):
    return jnp.where(is_ctx, modc_ref[0, idx:idx + 1, :], modx_ref[0, idx:idx + 1, :])


def _rms_modulate(x, shift, scale):
    ms = jnp.mean(x * x, axis=-1, keepdims=True)
    return (x * lax.rsqrt(ms + EPS)) * (1.0 + scale) + shift


def _ada_kernel(c_ref, w_ref, b_ref, o_ref):
    s = _silu(c_ref[...])
    o_ref[0] = jnp.dot(s, w_ref[0], preferred_element_type=F32, precision=lax.Precision.HIGHEST) + b_ref[0]


def _ada_mod(cc, ada_w, ada_b):
    depth, d, n6 = ada_w.shape
    tn = _largest_tile(n6, 1024, LANES)
    return pl.pallas_call(
        _ada_kernel,
        out_shape=jax.ShapeDtypeStruct((depth, MOD_ROWS, n6), F32),
        grid=(depth, n6 // tn),
        in_specs=[pl.BlockSpec((MOD_ROWS, d), lambda i, j: (0, 0)),
                  pl.BlockSpec((1, d, tn), lambda i, j: (i, 0, j)),
                  pl.BlockSpec((1, 1, tn), lambda i, j: (i, 0, j))],
        out_specs=pl.BlockSpec((1, MOD_ROWS, tn), lambda i, j: (i, 0, j)),
        compiler_params=_cparams("parallel", "parallel"),
        name="ada_mod",
    )(cc, ada_w, ada_b.reshape(depth, 1, n6))


def _modnorm_kernel(x_ref, modx_ref, modc_ref, o_ref, *, cl, tm):
    is_ctx = _row_ids(tm, pl.program_id(1)) < cl
    h = _rms_modulate(x_ref[0], _mod_rows(modx_ref, modc_ref, 0, is_ctx), _mod_rows(modx_ref, modc_ref, 1, is_ctx))
    o_ref[0] = h.astype(o_ref.dtype)


def _modnorm(xc, mod, cl):
    b, lt, d = xc.shape
    tm = _largest_tile(lt, 1088, 16)
    return pl.pallas_call(
        functools.partial(_modnorm_kernel, cl=cl, tm=tm),
        out_shape=jax.ShapeDtypeStruct((b, lt, d), BF16), grid=(b, lt // tm),
        in_specs=[pl.BlockSpec((1, tm, d), lambda i, r: (i, r, 0)),
                  pl.BlockSpec((1, 6, d), lambda i, r: (i, 0, 0)),
                  pl.BlockSpec((1, 6, d), lambda i, r: (b, 0, 0))],
        out_specs=pl.BlockSpec((1, tm, d), lambda i, r: (i, r, 0)),
        compiler_params=_cparams("parallel", "parallel"), name="modnorm",
    )(xc, mod, mod)


def _proj_kernel(h_ref, w_ref, o_ref):
    o_ref[0] = _dot(h_ref[0], w_ref[...]).astype(o_ref.dtype)


def _proj(h, w):
    b, lt, d = h.shape
    n = w.shape[1]
    tm = _largest_tile(lt, 2176, 16)
    tn = n if n <= 1280 else _largest_tile(n, 1024, 2 * LANES)
    return pl.pallas_call(
        _proj_kernel, out_shape=jax.ShapeDtypeStruct((b, lt, n), BF16), grid=(b, lt // tm, n // tn),
        in_specs=[pl.BlockSpec((1, tm, d), lambda i, r, j: (i, r, 0)), pl.BlockSpec((d, tn), lambda i, r, j: (0, j))],
        out_specs=pl.BlockSpec((1, tm, tn), lambda i, r, j: (i, r, j)),
        compiler_params=_cparams("parallel", "parallel", "parallel"), name="proj",
    )(h, w)


def _rope_tables(l, cl):
    n_freq = HEAD_DIM // 4
    t = jnp.arange(l)
    freqs = ROPE_THETA ** (-jnp.arange(n_freq, dtype=F32) / n_freq)
    pos = jnp.stack([t // GRID_W, t % GRID_W], -1).astype(F32)
    ang = pos[..., None] * freqs
    cos64 = jnp.stack([jnp.cos(ang), jnp.cos(ang)], axis=2).reshape(l, HEAD_DIM)
    sin64 = jnp.stack([-jnp.sin(ang), jnp.sin(ang)], axis=2).reshape(l, HEAD_DIM)
    cos64 = jnp.concatenate([jnp.ones((cl, HEAD_DIM), F32), cos64], 0)
    sin64 = jnp.concatenate([jnp.zeros((cl, HEAD_DIM), F32), sin64], 0)
    return jnp.tile(cos64, (1, 2)), jnp.tile(sin64, (1, 2))


def _head_rms(t, ones_ref):
    ms = _dot_exact_rhs(t * t, ones_ref[...]) * (1.0 / HEAD_DIM)
    return t * lax.rsqrt(ms + EPS)


def _rope(t, cos, sin):
    w = t.shape[-1]
    lane = lax.broadcasted_iota(jnp.int32, t.shape, 1)
    q4 = HEAD_DIM // 4
    first_half = (lane % (2 * q4)) < q4
    partner = jnp.where(first_half, pltpu.roll(t, w - q4, 1), pltpu.roll(t, q4, 1))
    return t * cos + partner * sin


def _attn_prep_kernel(p_ref, cos_ref, sin_ref, qg_ref, kg_ref, oq_ref, ok_ref, q_ref, kt_ref, vx_ref):
    p = p_ref[0].astype(F32)
    v = p[:, ATTN_WIDTH + ATTN_KV_WIDTH:]
    low = lax.broadcasted_iota(jnp.int32, v.shape, 1) < HEAD_DIM
    vx_ref[0, 0] = jnp.where(low, v, 1.0).astype(vx_ref.dtype)
    vx_ref[0, 1] = jnp.where(low, pltpu.roll(v, HEAD_DIM, 1), 1.0).astype(vx_ref.dtype)
    cos2, sin2 = cos_ref[...], sin_ref[...]
    reps = ATTN_WIDTH // (2 * HEAD_DIM)
    cos_q = jnp.concatenate([cos2] * reps, axis=1)
    sin_q = jnp.concatenate([sin2] * reps, axis=1)
    q = _head_rms(p[:, :ATTN_WIDTH], oq_ref) * qg_ref[...]
    q_ref[0] = _rope(q, cos_q, sin_q).astype(q_ref.dtype)
    k = _head_rms(p[:, ATTN_WIDTH:ATTN_WIDTH + ATTN_KV_WIDTH], ok_ref) * kg_ref[...]
    kt_ref[0] = _rope(k, cos2, sin2).T.astype(kt_ref.dtype)


def _attn_prep(pqkv, cos2, sin2, q_gain, k_gain):
    b, lt, wtot = pqkv.shape
    tr = _largest_tile(lt, 2176, LANES)
    qg = jnp.tile(q_gain * (HEAD_DIM ** -0.5 * math.log2(math.e)), ATTN_HEADS).reshape(1, ATTN_WIDTH)
    kg = jnp.tile(k_gain, ATTN_KV_HEADS).reshape(1, ATTN_KV_WIDTH)
    return pl.pallas_call(
        _attn_prep_kernel,
        out_shape=(jax.ShapeDtypeStruct((b, lt, ATTN_WIDTH), BF16),
                   jax.ShapeDtypeStruct((b, ATTN_KV_WIDTH, lt), BF16),
                   jax.ShapeDtypeStruct((b, ATTN_KV_HEADS, lt, 2 * HEAD_DIM), BF16)),
        grid=(b, lt // tr),
        in_specs=[pl.BlockSpec((1, tr, wtot), lambda i, r: (i, r, 0)),
                  pl.BlockSpec((tr, 2 * HEAD_DIM), lambda i, r: (r, 0)),
                  pl.BlockSpec((tr, 2 * HEAD_DIM), lambda i, r: (r, 0)),
                  pl.BlockSpec((1, ATTN_WIDTH), lambda i, r: (0, 0)),
                  pl.BlockSpec((1, ATTN_KV_WIDTH), lambda i, r: (0, 0)),
                  pl.BlockSpec((ATTN_WIDTH, ATTN_WIDTH), lambda i, r: (0, 0)),
                  pl.BlockSpec((ATTN_KV_WIDTH, ATTN_KV_WIDTH), lambda i, r: (0, 0))],
        out_specs=(pl.BlockSpec((1, tr, ATTN_WIDTH), lambda i, r: (i, r, 0)),
                   pl.BlockSpec((1, ATTN_KV_WIDTH, tr), lambda i, r: (i, 0, r)),
                   pl.BlockSpec((1, ATTN_KV_HEADS, tr, 2 * HEAD_DIM), lambda i, r: (i, 0, r, 0))),
        compiler_params=_cparams("parallel", "parallel"),
        name="attn_prep",
    )(pqkv, cos2, sin2, qg, kg, _head_ones(ATTN_WIDTH), _head_ones(ATTN_KV_WIDTH))


def _attn_kernel(*refs, nq, nk):
    q_refs, (kt_ref, v_ref, o_ref) = refs[:nq], refs[nq:]
    outs = []
    for h in range(ATTN_HEADS):
        kv = h // ATTN_GROUP
        hs = slice(h * HEAD_DIM, (h + 1) * HEAD_DIM)
        qh = jnp.concatenate([q_ref[0, :, hs] for q_ref in q_refs], axis=0)
        s = _dot(qh, kt_ref[0, kv * HEAD_DIM:(kv + 1) * HEAD_DIM, :nk])
        m = jnp.max(s, axis=-1, keepdims=True)
        p = jnp.exp2(s - m)
        o = _dot(p.astype(BF16), v_ref[0, kv, :nk, :])
        outs.append(o[:, :HEAD_DIM] / o[:, HEAD_DIM:])
    o_ref[0] = jnp.concatenate(outs, axis=-1).astype(o_ref.dtype)


def _attention(qn, kt, vx, cl, rows, nk):
    b, lt, _ = qn.shape
    blk = 256 if (cl % 256 == 0 and lt % 256 == 0) else 128
    if rows == 'ctx':
        n_rows, first, nq = cl, 0, 1
    else:
        n_rows, first = lt - cl, cl // blk
        nq = ATTN_Q_BLOCKS if n_rows % (ATTN_Q_BLOCKS * blk) == 0 else 1
    tq = nq * blk
    q_specs = [pl.BlockSpec((1, blk, ATTN_WIDTH), lambda i, t, j=j: (i, first + nq * t + j, 0)) for j in range(nq)]
    return pl.pallas_call(
        functools.partial(_attn_kernel, nq=nq, nk=nk),
        out_shape=jax.ShapeDtypeStruct((b, n_rows, ATTN_WIDTH), BF16),
        grid=(b, n_rows // tq),
        in_specs=q_specs + [pl.BlockSpec((1, ATTN_KV_WIDTH, lt), lambda i, t: (i, 0, 0)),
                            pl.BlockSpec((1, ATTN_KV_HEADS, lt, 2 * HEAD_DIM), lambda i, t: (i, 0, 0, 0))],
        out_specs=pl.BlockSpec((1, tq, ATTN_WIDTH), lambda i, t: (i, t, 0)),
        compiler_params=_cparams("parallel", "parallel"),
        name="attention_" + rows,
    )(*([qn] * nq), kt, vx)


def _halo_specs(tr, lt, width, lead):
    per = tr // SUBLANES
    last = lt // SUBLANES - 1
    nlead = len(lead)

    def prev_map(*ids):
        return (*ids[:nlead], jnp.maximum(ids[nlead] * per - 1, 0), 0)

    def next_map(*ids):
        return (*ids[:nlead], jnp.minimum((ids[nlead] + 1) * per, last), 0)

    blk = (*lead, SUBLANES, width)
    return pl.BlockSpec(blk, prev_map), pl.BlockSpec(blk, next_map)


def _shift_rows(x, halo, offset, rows, cl, lt):
    tr = x.shape[0]
    local = lax.broadcasted_iota(jnp.int32, (tr, 1), 0)
    if offset < 0:
        y = pltpu.roll(x, -offset, 0)
        y = jnp.where(local == 0, halo, y)
        bad = (rows == 0) | (rows == cl)
    else:
        y = pltpu.roll(x, tr - offset, 0)
        for j in range(offset):
            y = jnp.where(local == tr - offset + j, halo[j:j + 1, :], y)
        bad = (rows >= lt - offset) | ((rows >= cl - offset) & (rows < cl))
    return jnp.where(bad, 0.0, y)


def _hyena_pre_kernel(p_ref, pv_ref, nx_ref, w_ref, b_ref, v_ref, x1_ref, x2_ref, *, cl, lt, tr):
    rows = _row_ids(tr, pl.program_id(1))
    p = p_ref[0].astype(F32)
    pm = _shift_rows(p, pv_ref[0, SUBLANES - 1:, :].astype(F32), -1, rows, cl, lt)
    pp = _shift_rows(p, nx_ref[0, :1, :].astype(F32), 1, rows, cl, lt)
    z = pm * w_ref[0:1, :] + p * w_ref[1:2, :] + pp * w_ref[2:3, :] + b_ref[...]
    c = HYENA_WIDTH
    v_ref[0] = z[:, :c].astype(v_ref.dtype)
    x1_ref[0] = z[:, c:2 * c].astype(x1_ref.dtype)
    x2_ref[0] = z[:, 2 * c:].astype(x2_ref.dtype)


def _hyena_pre(phy, conv_w, conv_b, cl):
    b, lt, w = phy.shape
    tr = _largest_tile(lt, 1088, 16)
    prev_spec, next_spec = _halo_specs(tr, lt, w, (1,))
    out = jax.ShapeDtypeStruct((b, lt, HYENA_WIDTH), BF16)
    ospec = pl.BlockSpec((1, tr, HYENA_WIDTH), lambda i, r: (i, r, 0))
    return pl.pallas_call(
        functools.partial(_hyena_pre_kernel, cl=cl, lt=lt, tr=tr),
        out_shape=(out, out, out),
        grid=(b, lt // tr),
        in_specs=[pl.BlockSpec((1, tr, w), lambda i, r: (i, r, 0)), prev_spec, next_spec,
                  pl.BlockSpec(conv_w.shape, lambda i, r: (0, 0)),
                  pl.BlockSpec((1, w), lambda i, r: (0, 0))],
        out_specs=(ospec, ospec, ospec),
        compiler_params=_cparams("parallel", "parallel"),
        name="hyena_pre",
    )(phy, phy, phy, conv_w, conv_b.reshape(1, w))


def _dft_mats(n):
    tm = _largest_tile(n, 256, 16)
    tn = _largest_tile(n, 1024, LANES)
    s = jnp.arange(n, dtype=jnp.int32)[None, :]

    def tables(k):
        ang = ((k[:, None] * s) % (2 * n)).astype(F32) * (math.pi / n)
        return jnp.cos(ang), jnp.sin(ang)

    cb, sb = tables(jnp.arange(0, n, tm, dtype=jnp.int32))
    co, so = tables(jnp.arange(tm, dtype=jnp.int32))
    base = pl.BlockSpec((1, 1, tn), lambda j, c: (j, 0, c))
    off = pl.BlockSpec((tm, tn), lambda j, c: (0, c))
    out = jax.ShapeDtypeStruct((n, n), BF16)
    ospec = pl.BlockSpec((tm, tn), lambda j, c: (j, c))
    return pl.pallas_call(
        functools.partial(_dft_mats_kernel, tm=tm, tn=tn), out_shape=(out, out, out), grid=(n // tm, n // tn),
        in_specs=[base, base, off, off], out_specs=(ospec, ospec, ospec),
        compiler_params=_cparams("parallel", "parallel"), name="dft_mats",
    )(cb.reshape(n // tm, 1, n), sb.reshape(n // tm, 1, n), co, so)


def _dft_mats_kernel(cb_ref, sb_ref, co_ref, so_ref, cm_ref, sm_ref, smt_ref, *, tm, tn):
    cb, sb, co, so = cb_ref[0], sb_ref[0], co_ref[...], so_ref[...]
    k = pl.program_id(0) * tm + lax.broadcasted_iota(jnp.int32, (tm, tn), 0)
    s = pl.program_id(1) * tn + lax.broadcasted_iota(jnp.int32, (tm, tn), 1)
    sn = sb * co + cb * so
    cm_ref[...] = (cb * co - sb * so).astype(cm_ref.dtype)
    sm_ref[...] = jnp.where(k == 0, (1 - 2 * (s % 2)).astype(F32), sn).astype(sm_ref.dtype)
    smt_ref[...] = jnp.where(s == 0, (1 - 2 * (k % 2)).astype(F32), sn).astype(smt_ref.dtype)


def _dft_raw_kernel(cm_ref, sm_ref, z_ref, zr_ref, zi_ref):
    zr_ref[...] = _dot(cm_ref[...], z_ref[...])
    zi_ref[...] = _dot(sm_ref[...], z_ref[...])


def _dft_raw(cm, sm, z):
    n, c = z.shape
    tm = _largest_tile(n, 256, 16)
    mat = pl.BlockSpec((tm, n), lambda j: (j, 0))
    out = jax.ShapeDtypeStruct((n, c), F32)
    ospec = pl.BlockSpec((tm, c), lambda j: (j, 0))
    return pl.pallas_call(
        _dft_raw_kernel, out_shape=(out, out), grid=(n // tm,),
        in_specs=[mat, mat, pl.BlockSpec((n, c), lambda j: (0, 0))],
        out_specs=(ospec, ospec),
        compiler_params=_cparams("parallel"), name="dft_raw",
    )(cm, sm, z)


def _dft_fwd_kernel(cm_ref, sm_ref, z_ref, hr_ref, hi_ref, yr_ref, yi_ref, *, bb, tm):
    is_row0 = _row_ids(tm, pl.program_id(1)) == 0
    hr, hi = hr_ref[...], hi_ref[...]
    for i in range(bb):
        z = z_ref[i]
        zr = _dot(cm_ref[...], z)
        zi = _dot(sm_ref[...], z)
        yr = jnp.where(is_row0, 0.5 * zr * hr, zr * hr + zi * hi)
        yi = jnp.where(is_row0, 0.5 * zi * hi, zi * hr - zr * hi)
        yr_ref[i] = yr.astype(yr_ref.dtype)
        yi_ref[i] = yi.astype(yi_ref.dtype)


def _dft_inv_kernel(cm_ref, smt_ref, yr_ref, yi_ref, z_ref, g_ref, skip_ref, o_ref, *, bb):
    for i in range(bb):
        y = _dot(cm_ref[...], yr_ref[i]) + _dot(smt_ref[...], yi_ref[i])
        y = y + z_ref[i].astype(F32) * skip_ref[...]
        o_ref[i] = (g_ref[i].astype(F32) * y).astype(o_ref.dtype)


def _longconv_gated(z, gate, hr, hi, skip, mats):
    cm, sm, smt = mats
    b, n, c = z.shape
    bb = 2 if b % 2 == 0 else 1
    tm = _largest_tile(n, 256, 16)
    mat = pl.BlockSpec((tm, n), lambda i, j: (j, 0))
    full = pl.BlockSpec((bb, n, c), lambda i, j: (i, 0, 0))
    tile = pl.BlockSpec((bb, tm, c), lambda i, j: (i, j, 0))
    filt = pl.BlockSpec((tm, c), lambda i, j: (j, 0))
    spec_shape = jax.ShapeDtypeStruct((b, n, c), BF16)
    yr, yi = pl.pallas_call(
        functools.partial(_dft_fwd_kernel, bb=bb, tm=tm),
        out_shape=(spec_shape, spec_shape), grid=(b // bb, n // tm),
        in_specs=[mat, mat, full, filt, filt], out_specs=(tile, tile),
        compiler_params=_cparams("parallel", "parallel"), name="dft_fwd",
    )(cm, sm, z, hr, hi)
    return pl.pallas_call(
        functools.partial(_dft_inv_kernel, bb=bb),
        out_shape=jax.ShapeDtypeStruct((b, n, c), BF16), grid=(b // bb, n // tm),
        in_specs=[mat, mat, full, full, tile, tile, pl.BlockSpec((1, c), lambda i, j: (0, 0))],
        out_specs=tile,
        compiler_params=_cparams("parallel", "parallel"), name="dft_inv",
    )(cm, smt, yr, yi, z, gate, skip.reshape(1, c))


def _hyena_filters(n, f1, fb1, f2, fb2, f3):
    t = jnp.arange(n, dtype=F32) / n
    bands = jnp.arange(1, HYENA_BANDS + 1, dtype=F32)
    ang = 2.0 * math.pi * t[:, None] * bands
    feat = jnp.concatenate([t[:, None], jnp.sin(ang), jnp.cos(ang)], axis=-1)
    hp = lax.Precision.HIGHEST
    h = jnp.sin(jnp.dot(feat, f1, precision=hp) + fb1)
    h = jnp.sin(jnp.dot(h, f2, precision=hp) + fb2)
    h = jnp.dot(h, f3, precision=hp).reshape(n, HYENA_ORDER, 2, HYENA_WIDTH)
    deltas = jnp.linspace(-math.log(HYENA_DECAY_TARGET) / HYENA_SLOW_DECAY,
                          -math.log(HYENA_DECAY_TARGET) / HYENA_FAST_DECAY, HYENA_WIDTH, dtype=F32)
    h = h * jnp.exp(-t[:, None] * deltas)[:, None, None, :]
    return h / jnp.sum(jnp.abs(h), axis=(0, 2), keepdims=True)


def _hyena_spectra(n, filt_params, mats):
    cm, sm, _ = mats
    h = _hyena_filters(n, *filt_params)
    hf = h[:, :, 0].reshape(n, HYENA_ORDER * HYENA_WIDTH)
    hb = h[:, :, 1].reshape(n, HYENA_ORDER * HYENA_WIDTH)
    hb = jnp.where(jnp.arange(n)[:, None] == 0, 0.0, hb)
    zr, zi = _dft_raw(cm, sm, jnp.concatenate([hf + hb, hb - hf], axis=1).astype(BF16))
    oc = HYENA_ORDER * HYENA_WIDTH
    hr = zr[:, :oc]
    hi = jnp.where(jnp.arange(n)[:, None] == 0, zi[:, :oc], zi[:, oc:])
    scale = 1.0 / n
    hr = (hr * scale).reshape(n, HYENA_ORDER, HYENA_WIDTH).transpose(1, 0, 2)
    hi = (hi * scale).reshape(n, HYENA_ORDER, HYENA_WIDTH).transpose(1, 0, 2)
    return hr, hi


def _hyena_run(v, x1, x2, spectra, skip, mats):
    hr, hi = spectra
    y = v
    for o, gate in enumerate((x1, x2)):
        y = _longconv_gated(y, gate, hr[o], hi[o], skip[o], mats)
    return y


def _blockdiag_dense(w):
    nb, blk = w.shape[1], w.shape[2]
    eye = jnp.eye(nb, dtype=w.dtype)
    return jnp.einsum('dncf,nm->dncmf', w, eye).reshape(w.shape[0], nb * blk, nb * blk)


def _lru_pre_kernel(p_ref, pv_ref, nx_ref, cw_ref, cb_ref, wa_ref, ba_ref, wx_ref, bx_ref, lam_ref,
                    a_ref, b_ref, *, cl, lt, tr):
    rows = _row_ids(tr, pl.program_id(1))
    c = LRU_WIDTH
    x = p_ref[0][:, c:].astype(F32)
    pv = pv_ref[0][SUBLANES - 1:, c:].astype(F32)
    nx = nx_ref[0][:, c:].astype(F32)
    xc = (_shift_rows(x, pv, -1, rows, cl, lt) * cw_ref[0:1, :] + x * cw_ref[1:2, :]
          + _shift_rows(x, nx[:1], 1, rows, cl, lt) * cw_ref[2:3, :]
          + _shift_rows(x, nx[:2], 2, rows, cl, lt) * cw_ref[3:4, :] + cb_ref[...])
    xcb = xc.astype(BF16)
    for d in range(2):
        r = _sigmoid(_dot(xcb, wa_ref[d]) + ba_ref[d])
        i = _sigmoid(_dot(xcb, wx_ref[d]) + bx_ref[d])
        log_a = -LRU_C * r * _softplus(-lam_ref[d])
        a_ref[d, 0] = jnp.exp(log_a)
        b_ref[d, 0] = jnp.sqrt(1.0 - jnp.exp(2.0 * log_a)) * (i * xc)


def _lru_pre(plr, conv_w, conv_b, wa, ba, wx, bx, lam, cl):
    b, lt, w = plr.shape
    c = LRU_WIDTH
    tr = _largest_tile(lt, 1088, 16)
    prev_spec, next_spec = _halo_specs(tr, lt, w, (1,))
    out = jax.ShapeDtypeStruct((2, b, lt, c), F32)
    ospec = pl.BlockSpec((2, 1, tr, c), lambda i, r: (0, i, r, 0))
    const2 = lambda shape: pl.BlockSpec(shape, lambda i, r: (0,) * len(shape))
    return pl.pallas_call(
        functools.partial(_lru_pre_kernel, cl=cl, lt=lt, tr=tr),
        out_shape=(out, out), grid=(b, lt // tr),
        in_specs=[pl.BlockSpec((1, tr, w), lambda i, r: (i, r, 0)), prev_spec, next_spec,
                  const2(conv_w.shape), const2((1, c)), const2((2, c, c)), const2((2, 1, c)),
                  const2((2, c, c)), const2((2, 1, c)), const2((2, 1, c))],
        out_specs=(ospec, ospec),
        compiler_params=_cparams("parallel", "parallel"), name="lru_pre",
    )(plr, plr, plr, conv_w, conv_b.reshape(1, c), _blockdiag_dense(wa).astype(BF16), ba.reshape(2, 1, c),
      _blockdiag_dense(wx).astype(BF16), bx.reshape(2, 1, c), lam.reshape(2, 1, c))


def _gelu_tanh(x):
    return 0.5 * x * (1.0 + jnp.tanh(math.sqrt(2.0 / math.pi) * (x + 0.044715 * (x * x * x))))


def _lru_scan_kernel(af_ref, bf_ref, ar_ref, br_ref, g_ref, o_ref, acc_ref, *, cl, lt):
    row = lax.broadcasted_iota(jnp.int32, (SUBLANES, LANES), 0)

    def group_scan(a, b, reverse):
        for s in (1, 2, 4):
            if reverse:
                keep = row < SUBLANES - s
                a_s = jnp.where(keep, pltpu.roll(a, SUBLANES - s, 0), 1.0)
                b_s = jnp.where(keep, pltpu.roll(b, SUBLANES - s, 0), 0.0)
            else:
                keep = row >= s
                a_s = jnp.where(keep, pltpu.roll(a, s, 0), 1.0)
                b_s = jnp.where(keep, pltpu.roll(b, s, 0), 0.0)
            b = a * b_s + b
            a = a * a_s
        return a, b

    ng = math.gcd(math.gcd(cl // SUBLANES, (lt - cl) // SUBLANES), LRU_GROUPS_PER_STEP)
    span = ng * SUBLANES

    def fwd_body(i, h):
        sl = pl.ds(pl.multiple_of(i * span, span), span)
        a_all, b_all = af_ref[0, 0, sl, :], bf_ref[0, 0, sl, :]
        scans = [group_scan(a_all[j * SUBLANES:(j + 1) * SUBLANES], b_all[j * SUBLANES:(j + 1) * SUBLANES], False)
                 for j in range(ng)]
        outs = []
        for a, b in scans:
            hh = a * h + b
            outs.append(hh)
            h = hh[SUBLANES - 1:SUBLANES, :]
        acc_ref[sl, :] = jnp.concatenate(outs, axis=0)
        return h

    lax.fori_loop(0, lt // span, fwd_body, jnp.zeros((1, LANES), F32))

    def rev_body(i, h, top):
        sl = pl.ds(pl.multiple_of((top - 1 - i) * span, span), span)
        a_all, b_all = ar_ref[0, 0, sl, :], br_ref[0, 0, sl, :]
        scans = [group_scan(a_all[j * SUBLANES:(j + 1) * SUBLANES], b_all[j * SUBLANES:(j + 1) * SUBLANES], True)
                 for j in range(ng)]
        outs = [None] * ng
        for j in reversed(range(ng)):
            a, b = scans[j]
            hh = a * h + b
            outs[j] = hh
            h = hh[0:1, :]
        gate = g_ref[0, sl, :].astype(F32)
        o_ref[0, sl, :] = ((acc_ref[sl, :] + jnp.concatenate(outs, axis=0)) * _gelu_tanh(gate)).astype(o_ref.dtype)
        return h

    h = lax.fori_loop(0, cl // span, functools.partial(rev_body, top=cl // span), jnp.zeros((1, LANES), F32))
    lax.fori_loop(0, (lt - cl) // span, functools.partial(rev_body, top=lt // span), h)


def _lru_scan(a, b_, plr, cl):
    _, b, lt, c = a.shape
    nl = c // LANES
    fwd = pl.BlockSpec((1, 1, lt, LANES), lambda i, j: (0, i, 0, j))
    rev = pl.BlockSpec((1, 1, lt, LANES), lambda i, j: (1, i, 0, j))
    return pl.pallas_call(
        functools.partial(_lru_scan_kernel, cl=cl, lt=lt),
        out_shape=jax.ShapeDtypeStruct((b, lt, c), BF16), grid=(b, nl),
        in_specs=[fwd, fwd, rev, rev, pl.BlockSpec((1, lt, LANES), lambda i, j: (i, 0, j))],
        out_specs=pl.BlockSpec((1, lt, LANES), lambda i, j: (i, 0, j)),
        scratch_shapes=[pltpu.VMEM((lt, LANES), F32)],
        compiler_params=_cparams("parallel", "parallel"), name="lru_scan",
    )(a, b_, a, b_, plr)


def _row_perm(tr, rev):
    t = lax.broadcasted_iota(jnp.int32, (tr, tr), 0)
    s = lax.broadcasted_iota(jnp.int32, (tr, tr), 1)
    return jnp.where(s == jnp.where(rev, tr - 1 - t, t), 1.0, 0.0).astype(BF16)


def _mirror_tile(r, rev, n_ctx_tiles, n_tiles):
    m = jnp.where(r < n_ctx_tiles, n_ctx_tiles - 1 - r, n_tiles - 1 + n_ctx_tiles - r)
    return jnp.where(rev, m, r)


def _chunk_masks(tr):
    t = lax.broadcasted_iota(jnp.int32, (tr, tr), 0)
    s = lax.broadcasted_iota(jnp.int32, (tr, tr), 1)
    same = (t // RWKV_CHUNK) == (s // RWKV_CHUNK)
    return same, same & (s <= t), same & (s < t)


def _rwkv_prep_kernel(p_ref, pv_ref, nx_ref, mu_ref, w0_ref, w2_ref, a0_ref, a2_ref, g2_ref, kk_ref, ka_ref,
                      rk_ref, ones_ref,
                      aq_ref, vp_ref, y0_ref, rt_ref, mrb_ref, bht_ref, gm_ref, pc_ref, g_ref, bonus_ref,
                      *, cl, lt, tr):
    c = RWKV_WIDTH
    rows = _row_ids(tr, pl.program_id(2))
    rev = pl.program_id(0) == 1
    p = _dot(_row_perm(tr, rev), p_ref[0])
    before = jnp.where(rev, nx_ref[0, :1, :], pv_ref[0, SUBLANES - 1:, :]).astype(F32)
    after = jnp.where(rev, pv_ref[0, SUBLANES - 1:, :], nx_ref[0, :1, :]).astype(F32)
    prev = _shift_rows(p, before, -1, rows, cl, lt)
    nxt = _shift_rows(p, after, 1, rows, cl, lt)
    xm = p + (prev - p) * mu_ref[0, 0:1, :] + (nxt - p) * mu_ref[0, 1:2, :]
    r, k, v = xm[:, :c], xm[:, c:2 * c], xm[:, 2 * c:3 * c]
    o = 3 * c
    w1 = xm[:, o:o + 2 * RWKV_DECAY_RANK]
    a1 = xm[:, o + 2 * RWKV_DECAY_RANK:o + 2 * RWKV_DECAY_RANK + 2 * RWKV_ICLR_RANK]
    g1 = xm[:, o + 2 * RWKV_DECAY_RANK + 2 * RWKV_ICLR_RANK:]
    wlog = -_softplus(-(w0_ref[0] + _dot(jnp.tanh(w1).astype(BF16), w2_ref[0]))) - 0.5
    ld = -jnp.exp(wlog)
    a = _sigmoid(a0_ref[0] + _dot(a1.astype(BF16), a2_ref[0]))
    g_ref[0, 0] = _dot(_sigmoid(g1).astype(BF16), g2_ref[...])
    kk = k * kk_ref[...]
    kk = kk * lax.rsqrt(_dot_exact_rhs(kk * kk, ones_ref[...]) + 1e-12)
    kd = k * (1.0 + (a - 1.0) * ka_ref[...])
    bonus_ref[0, 0] = _dot_exact_rhs(r * kd * rk_ref[...], ones_ref[...]) * v
    beta = kk * a

    _rwkv_chunk_stage(ld, kk, r, kd, beta, v, aq_ref, vp_ref, y0_ref, rt_ref, mrb_ref, bht_ref, gm_ref, pc_ref)


def _rwkv_chunk_stage(ld, kk, r, kd, beta, v, aq_ref, vp_ref, y0_ref, rt_ref, mrb_ref, bht_ref, gm_ref, pc_ref):
    tr = ld.shape[0]
    n = min(RWKV_STAGE_ROWS, tr)
    nparts = tr // n
    hd = HEAD_DIM
    ch = RWKV_CHUNK
    nch = n // ch
    same_t, incl_t, _ = _chunk_masks(tr)
    cum = _dot_exact_lhs(jnp.where(incl_t, 1.0, 0.0).astype(BF16), ld)
    tot = _dot_exact_lhs(jnp.where(same_t, 1.0, 0.0).astype(BF16), ld)
    _, incl, strict = _chunk_masks(n)
    alpha_t = kk * jnp.exp(cum - ld)
    r_t = r * jnp.exp(cum)
    e_neg = jnp.exp(-cum)
    k_t = kd * e_neg
    b_t = beta * e_neg
    e_rem = jnp.exp(tot - cum)
    k_hat_t = (kd * e_rem).T
    b_hat_t = (beta * e_rem).T
    pc_t = jnp.exp(tot).T
    t_i = lax.broadcasted_iota(jnp.int32, (n, n), 0)
    s_i = lax.broadcasted_iota(jnp.int32, (n, n), 1)
    eye_f = jnp.where(t_i == s_i, 1.0, 0.0)
    same_blk = []
    size = RWKV_INV_BASE
    while size <= ch:
        same_blk.append((t_i // size) == (s_i // size))
        size *= 2
    col_chunk = lax.broadcasted_iota(jnp.int32, (n, nch * hd), 1) // hd
    row_chunk = lax.broadcasted_iota(jnp.int32, (n, nch * hd), 0) // ch

    def diag_blocks(m):
        out = m[:, :ch]
        for j in range(1, nch):
            out = out + m[:, j * ch:(j + 1) * ch]
        return out

    chains = [(p, h) for p in range(nparts) for h in range(RWKV_HEADS)]

    def blk(t, p, h):
        return t[p * n:(p + 1) * n, h * hd:(h + 1) * hd]

    prods = [_dot_nt(jnp.concatenate([blk(alpha_t, p, h), blk(r_t, p, h)], axis=0).astype(BF16),
                     jnp.concatenate([blk(b_t, p, h), blk(k_t, p, h)], axis=0).astype(BF16))
             for p, h in chains]
    l_ab = [jnp.where(strict, pr[:n, :n], 0.0) for pr in prods]
    pw = [jnp.where(same_blk[0], l, 0.0) for l in l_ab]
    t_inv = [eye_f - l for l in pw]
    for _ in range(int(math.log2(RWKV_INV_BASE)) - 1):
        pw = [_dot_bf(q, q) for q in pw]
        t_inv = [t + _dot_bf(t, q) for t, q in zip(t_inv, pw)]
    for lvl in range(1, len(same_blk)):
        off = same_blk[lvl] & jnp.logical_not(same_blk[lvl - 1])
        half = [_dot_bf(t, jnp.where(off, l, 0.0)) for t, l in zip(t_inv, l_ab)]
        t_inv = [t - _dot_bf(hf, t) for t, hf in zip(t_inv, half)]
    vh = [blk(v, p, h).astype(BF16) for p, h in chains]
    lakv = [_dot(jnp.where(strict, pr[:n, n:], 0.0).astype(BF16), vv) for pr, vv in zip(prods, vh)]
    x = [_dot(t.astype(BF16), jnp.concatenate([blk(alpha_t, p, h), lv], axis=1).astype(BF16))
         for t, (p, h), lv in zip(t_inv, chains, lakv)]
    y0 = [_dot(jnp.where(incl, pr[n:, n:], 0.0).astype(BF16), vv) for pr, vv in zip(prods, vh)]
    mrb = [diag_blocks(jnp.where(incl, pr[n:, :n], 0.0)) for pr in prods]
    gms = [_dot(k_hat_t[h * hd:(h + 1) * hd, p * n:(p + 1) * n].astype(BF16),
                jnp.where(col_chunk == row_chunk, jnp.concatenate([blk(v, p, h)] * nch, axis=1), 0.0).astype(BF16))
           for p, h in chains]

    def assemble(parts):
        return jnp.concatenate([jnp.concatenate(parts[p * RWKV_HEADS:(p + 1) * RWKV_HEADS], axis=1)
                                for p in range(nparts)], axis=0)

    aq_ref[0, 0] = assemble([t[:, :hd] for t in x]).astype(aq_ref.dtype)
    vp_ref[0, 0] = assemble([t[:, hd:] for t in x])
    y0_ref[0, 0] = assemble(y0)
    rt_ref[0, 0] = r_t.astype(rt_ref.dtype)
    mrb_ref[0, 0] = assemble(mrb).astype(mrb_ref.dtype)
    for p in range(nparts):
        for j in range(nch):
            jj = p * nch + j
            cs = slice(jj * ch, (jj + 1) * ch)
            bht_ref[0, 0, jj] = jnp.concatenate([b_hat_t[h * hd:(h + 1) * hd, cs] for h in range(RWKV_HEADS)],
                                                axis=1).astype(bht_ref.dtype)
            gm_ref[0, 0, jj] = jnp.concatenate([gms[p * RWKV_HEADS + h][:, j * hd:(j + 1) * hd]
                                                for h in range(RWKV_HEADS)], axis=1)
            pc_ref[0, 0, jj] = jnp.concatenate([pc_t[h * hd:(h + 1) * hd, cs] for h in range(RWKV_HEADS)], axis=1)


def _rwkv_tile(lt, cl):
    tr = 256 if (lt % 256 == 0 and cl % 256 == 0) else 128
    assert lt % tr == 0 and cl % tr == 0
    return tr


def _rwkv_prep(prw, mud, w0, w2p, a0, a2p, g2, k_k, k_a, r_k, cl):
    b, lt, w = prw.shape
    c = RWKV_WIDTH
    tr = _rwkv_tile(lt, cl)
    nch = tr // RWKV_CHUNK
    per = tr // SUBLANES
    src = lambda d, r: _mirror_tile(r, d == 1, cl // tr, lt // tr)
    prev_spec = pl.BlockSpec((1, SUBLANES, w), lambda d, i, r: (i, jnp.maximum(src(d, r) * per - 1, 0), 0))
    next_spec = pl.BlockSpec((1, SUBLANES, w),
                             lambda d, i, r: (i, jnp.minimum((src(d, r) + 1) * per, lt // SUBLANES - 1), 0))
    tile = pl.BlockSpec((1, 1, tr, c), lambda d, i, r: (d, i, r, 0))
    per_dir = lambda shape: pl.BlockSpec((1, *shape), lambda d, i, r: (d,) + (0,) * len(shape))
    const = lambda shape: pl.BlockSpec(shape, lambda d, i, r: (0,) * len(shape))
    seq = lambda dt: jax.ShapeDtypeStruct((2, b, lt, c), dt)
    chunked = jax.ShapeDtypeStruct((2, b, lt // RWKV_CHUNK, HEAD_DIM, c), F32)
    chunk_spec = pl.BlockSpec((1, 1, nch, HEAD_DIM, c), lambda d, i, r: (d, i, r, 0, 0))
    return pl.pallas_call(
        functools.partial(_rwkv_prep_kernel, cl=cl, lt=lt, tr=tr),
        out_shape=(seq(BF16), seq(F32), seq(F32), seq(BF16), seq(BF16),
                   jax.ShapeDtypeStruct(chunked.shape, BF16), chunked, chunked, seq(F32), seq(F32)),
        grid=(2, b, lt // tr),
        in_specs=[pl.BlockSpec((1, tr, w), lambda d, i, r: (i, src(d, r), 0)), prev_spec, next_spec,
                  per_dir((2, w)), per_dir((1, c)), per_dir((2 * RWKV_DECAY_RANK, c)), per_dir((1, c)),
                  per_dir((2 * RWKV_ICLR_RANK, c)), const((RWKV_GATE_RANK, c)), const((1, c)), const((1, c)),
                  const((1, c)), const((c, c))],
        out_specs=(tile, tile, tile, tile, tile, chunk_spec, chunk_spec, chunk_spec, tile, tile),
        compiler_params=_cparams("parallel", "parallel", "parallel"), name="rwkv_prep",
    )(prw, prw, prw, mud, w0.reshape(2, 1, c), w2p, a0.reshape(2, 1, c), a2p, g2.astype(BF16), k_k.reshape(1, c),
      k_a.reshape(1, c), r_k.reshape(1, c), _head_ones(c))


def _rwkv_scan_kernel(aq_ref, vp_ref, y0_ref, rt_ref, mrb_ref, bht_ref, gm_ref, pc_ref, y_ref, s_ref, *, ns):
    @pl.when(pl.program_id(1) == 0)
    def _():
        s_ref[...] = jnp.zeros_like(s_ref)

    c = RWKV_WIDTH
    ch = RWKV_CHUNK
    same_head = (lax.broadcasted_iota(jnp.int32, (c, c), 0) // HEAD_DIM
                 == lax.broadcasted_iota(jnp.int32, (c, c), 1) // HEAD_DIM)

    def head_blockdiag(t):
        return jnp.where(same_head, jnp.concatenate([t] * RWKV_HEADS, axis=0), 0.0).astype(BF16)

    for i in range(ns):
        s0 = s_ref[i]
        r1 = _dot(jnp.concatenate([aq_ref[i, 0], rt_ref[i, 0]], axis=0), head_blockdiag(s0))
        u = r1[:ch] + vp_ref[i, 0]
        r2 = _dot(jnp.concatenate([mrb_ref[i, 0], bht_ref[i, 0, 0]], axis=0), head_blockdiag(u))
        y_ref[i, 0] = r1[ch:] + y0_ref[i, 0] - r2[:ch]
        s_ref[i] = pc_ref[i, 0, 0] * s0 + gm_ref[i, 0, 0] - r2[ch:]


def _rwkv_scan(aq, vp, y0, rt, mrb, bht, gm, pc):
    _, b, lt, c = aq.shape
    nstream = 2 * b
    ns = nstream
    ch = RWKV_CHUNK
    merge = lambda t: t.reshape(nstream, 1, *t.shape[2:])
    tile = pl.BlockSpec((ns, 1, ch, c), lambda s, j: (s, 0, j, 0))
    chunk_spec = pl.BlockSpec((ns, 1, 1, HEAD_DIM, c), lambda s, j: (s, 0, j, 0, 0))
    y = pl.pallas_call(
        functools.partial(_rwkv_scan_kernel, ns=ns),
        out_shape=jax.ShapeDtypeStruct((nstream, 1, lt, c), F32), grid=(nstream // ns, lt // ch),
        in_specs=[tile, tile, tile, tile, tile, chunk_spec, chunk_spec, chunk_spec],
        out_specs=tile,
        scratch_shapes=[pltpu.VMEM((ns, HEAD_DIM, c), F32)],
        compiler_params=_cparams("parallel", "arbitrary"), name="rwkv_scan",
    )(*(merge(t) for t in (aq, vp, y0, rt, mrb, bht, gm, pc)))
    return y.reshape(2, b, lt, c)


def _rwkv_readout_kernel(yf_ref, yr_ref, bf_ref, br_ref, g_ref, lw_ref, lb_ref, ones_ref, o_ref, *, tr):
    unflip = _row_perm(tr, True)
    y = yf_ref[0, 0] + _dot_exact_lhs(unflip, yr_ref[0, 0])
    bonus = bf_ref[0, 0] + _dot_exact_lhs(unflip, br_ref[0, 0])
    inv = 1.0 / HEAD_DIM
    mu = _dot_exact_rhs(y, ones_ref[...]) * inv
    yc = y - mu
    var = _dot_exact_rhs(yc * yc, ones_ref[...]) * inv
    yn = yc * lax.rsqrt(var + RWKV_GN_EPS) * lw_ref[...] + lb_ref[...]
    o_ref[0] = ((yn + bonus) * g_ref[0, 0]).astype(o_ref.dtype)


def _rwkv_readout(y, bonus, g, ln_w, ln_b, cl):
    _, b, lt, c = y.shape
    tr = _rwkv_tile(lt, cl)
    fwd = pl.BlockSpec((1, 1, tr, c), lambda i, r: (0, i, r, 0))
    rev = pl.BlockSpec((1, 1, tr, c), lambda i, r: (1, i, _mirror_tile(r, True, cl // tr, lt // tr), 0))
    row = pl.BlockSpec((1, c), lambda i, r: (0, 0))
    return pl.pallas_call(
        functools.partial(_rwkv_readout_kernel, tr=tr),
        out_shape=jax.ShapeDtypeStruct((b, lt, c), BF16), grid=(b, lt // tr),
        in_specs=[fwd, rev, fwd, rev, fwd, row, row, pl.BlockSpec((c, c), lambda i, r: (0, 0))],
        out_specs=pl.BlockSpec((1, tr, c), lambda i, r: (i, r, 0)),
        compiler_params=_cparams("parallel", "parallel"), name="rwkv_readout",
    )(y, y, bonus, bonus, g, ln_w.reshape(1, c), ln_b.reshape(1, c), _head_ones(c))


def _pad_rank_rows(w):
    z = jnp.zeros_like(w[0])
    return jnp.stack([jnp.concatenate([w[0], z], 0), jnp.concatenate([z, w[1]], 0)])


def _rwkv_mixer(prw, mu, w0, w2, a0, a2, g2, k_k, k_a, r_k, ln_w, ln_b, cl):
    mud = jnp.stack([mu, mu[::-1]])
    outs = _rwkv_prep(prw, mud, w0, _pad_rank_rows(w2).astype(BF16), a0, _pad_rank_rows(a2).astype(BF16), g2,
                      k_k, k_a, r_k, cl)
    *scan_in, g, bonus = outs
    y = _rwkv_scan(*scan_in)
    return _rwkv_readout(y, bonus, g, ln_w, ln_b, cl)


def _merge_kernel(ya_ref, yh_ref, yr_ref, yl_ref, gt_ref, x_ref, modx_ref, modc_ref,
                  wa_ref, wh_ref, wr_ref, wl_ref, wo_ref, o_ref, *, cl, tm, d):
    is_ctx = _row_ids(tm, pl.program_id(1)) < cl
    m = None
    for i, (y_ref, w_ref) in enumerate(((ya_ref, wa_ref), (yh_ref, wh_ref), (yr_ref, wr_ref), (yl_ref, wl_ref))):
        gate = _sigmoid(gt_ref[0, :, i * d:(i + 1) * d].astype(F32))
        term = gate * _dot(y_ref[0], w_ref[...])
        m = term if m is None else m + term
    g1 = _mod_rows(modx_ref, modc_ref, 2, is_ctx)
    o_ref[0] = x_ref[0] + g1 * _dot(m.astype(BF16), wo_ref[...])


def _merge(ys, gates, xc, mod, w_brs, w_out, cl):
    b, lt, d = xc.shape
    tm = _largest_tile(lt, 544, 16)
    row = lambda w: pl.BlockSpec((1, tm, w), lambda i, r: (i, r, 0))
    const = lambda a: pl.BlockSpec(a.shape, lambda i, r: (0, 0))
    ws = [w.astype(BF16) for w in w_brs] + [w_out.astype(BF16)]
    return pl.pallas_call(
        functools.partial(_merge_kernel, cl=cl, tm=tm, d=d),
        out_shape=jax.ShapeDtypeStruct((b, lt, d), F32), grid=(b, lt // tm),
        in_specs=[row(y.shape[-1]) for y in ys] + [row(N_BRANCH * d), row(d),
                  pl.BlockSpec((1, 6, d), lambda i, r: (i, 0, 0)), pl.BlockSpec((1, 6, d), lambda i, r: (b, 0, 0))]
                 + [const(w) for w in ws],
        out_specs=row(d), compiler_params=_cparams("parallel", "parallel"), name="merge",
    )(*ys, gates, xc, mod, mod, *ws)


def _route_kernel(x_ref, modx_ref, modc_ref, wr_ref, br_ref, h_ref, cmb_ref, *, cl, tm):
    is_ctx = _row_ids(tm, pl.program_id(1)) < cl
    h = _rms_modulate(x_ref[0], _mod_rows(modx_ref, modc_ref, 3, is_ctx), _mod_rows(modx_ref, modc_ref, 4, is_ctx))
    h_ref[0] = h.astype(h_ref.dtype)
    lg = jnp.dot(h, wr_ref[...], preferred_element_type=F32, precision=lax.Precision.HIGHEST) + br_ref[...]
    lane = lax.broadcasted_iota(jnp.int32, lg.shape, 1)
    lane_f = lane.astype(F32)
    neg = -jnp.inf
    big = 1e9

    def first_lane(cond):
        return jnp.min(jnp.where(cond, lane_f, big), axis=-1, keepdims=True)

    is_grp = (lane >= N_EXPERTS) & (lane < N_EXPERTS + N_GROUPS)
    gl = jnp.where(is_grp, lg, neg)
    gmax = jnp.max(gl, axis=-1, keepdims=True)
    ge = jnp.where(is_grp, jnp.exp(gl - gmax), 0.0)
    gp = ge / jnp.sum(ge, axis=-1, keepdims=True)
    g_val = jnp.max(gp, axis=-1, keepdims=True)
    g_idx = first_lane(is_grp & (gp == g_val)) - N_EXPERTS
    lo = g_idx * EXPERTS_PER_GROUP
    in_grp = (lane_f >= lo) & (lane_f < lo + EXPERTS_PER_GROUP)
    el = jnp.where(in_grp, lg, neg)
    emax = jnp.max(el, axis=-1, keepdims=True)
    ee = jnp.where(in_grp, jnp.exp(el - emax), 0.0)
    pe = ee / jnp.sum(ee, axis=-1, keepdims=True)
    v1 = jnp.max(jnp.where(in_grp, pe, -1.0), axis=-1, keepdims=True)
    i1 = first_lane(in_grp & (pe == v1))
    rest = in_grp & (lane_f != i1)
    v2 = jnp.max(jnp.where(rest, pe, -1.0), axis=-1, keepdims=True)
    i2 = first_lane(rest & (pe == v2))
    den = v1 + v2
    cmb_ref[0] = jnp.where(lane_f == i1, g_val * v1 / den, 0.0) + jnp.where(lane_f == i2, g_val * v2 / den, 0.0)


def _route(xc, mod, w_grp, b_grp, w_rt, b_rt, cl):
    b, lt, d = xc.shape
    tm = _largest_tile(lt, 544, 16)
    wr = jnp.zeros((d, LANES), F32).at[:, :N_EXPERTS].set(w_rt).at[:, N_EXPERTS:N_EXPERTS + N_GROUPS].set(w_grp)
    br = jnp.zeros((1, LANES), F32).at[0, :N_EXPERTS].set(b_rt).at[0, N_EXPERTS:N_EXPERTS + N_GROUPS].set(b_grp)
    return pl.pallas_call(
        functools.partial(_route_kernel, cl=cl, tm=tm),
        out_shape=(jax.ShapeDtypeStruct((b, lt, d), BF16), jax.ShapeDtypeStruct((b, lt, LANES), F32)),
        grid=(b, lt // tm),
        in_specs=[pl.BlockSpec((1, tm, d), lambda i, r: (i, r, 0)),
                  pl.BlockSpec((1, 6, d), lambda i, r: (i, 0, 0)), pl.BlockSpec((1, 6, d), lambda i, r: (b, 0, 0)),
                  pl.BlockSpec((d, LANES), lambda i, r: (0, 0)), pl.BlockSpec((1, LANES), lambda i, r: (0, 0))],
        out_specs=(pl.BlockSpec((1, tm, d), lambda i, r: (i, r, 0)), pl.BlockSpec((1, tm, LANES), lambda i, r: (i, r, 0))),
        compiler_params=_cparams("parallel", "parallel"), name="moe_route",
    )(xc, mod, mod, wr, br)


def _moe_kernel(h_ref, cmb_ref, x_ref, modx_ref, modc_ref, w1_ref, w3_ref, w2_ref, o_ref, acc_ref, *, cl, tm):
    e = pl.program_id(2)

    @pl.when(e == 0)
    def _():
        acc_ref[...] = jnp.zeros_like(acc_ref)

    h = h_ref[0]
    t = _silu(_dot(h, w1_ref[0])) * _dot(h, w3_ref[0])
    y = _dot(t.astype(BF16), w2_ref[0])
    lane = lax.broadcasted_iota(jnp.int32, (tm, LANES), 1)
    wcol = jnp.sum(jnp.where(lane == e, cmb_ref[0], 0.0), axis=-1, keepdims=True)
    acc_ref[...] += wcol * y

    @pl.when(e == pl.num_programs(2) - 1)
    def _():
        is_ctx = _row_ids(tm, pl.program_id(1)) < cl
        o_ref[0] = x_ref[0] + _mod_rows(modx_ref, modc_ref, 5, is_ctx) * acc_ref[...]


def _moe(h2, cmb, xc, mod, w1, w3, w2, cl):
    b, lt, d = xc.shape
    ne, _, hid = w1.shape
    tm = _largest_tile(lt, 1088, 16)
    return pl.pallas_call(
        functools.partial(_moe_kernel, cl=cl, tm=tm),
        out_shape=jax.ShapeDtypeStruct((b, lt, d), F32), grid=(b, lt // tm, ne),
        in_specs=[pl.BlockSpec((1, tm, d), lambda i, r, e: (i, r, 0)),
                  pl.BlockSpec((1, tm, LANES), lambda i, r, e: (i, r, 0)),
                  pl.BlockSpec((1, tm, d), lambda i, r, e: (i, r, 0)),
                  pl.BlockSpec((1, 6, d), lambda i, r, e: (i, 0, 0)),
                  pl.BlockSpec((1, 6, d), lambda i, r, e: (b, 0, 0)),
                  pl.BlockSpec((1, d, hid), lambda i, r, e: (e, 0, 0)),
                  pl.BlockSpec((1, d, hid), lambda i, r, e: (e, 0, 0)),
                  pl.BlockSpec((1, hid, d), lambda i, r, e: (e, 0, 0))],
        out_specs=pl.BlockSpec((1, tm, d), lambda i, r, e: (i, r, 0)),
        scratch_shapes=[pltpu.VMEM((tm, d), F32)],
        compiler_params=_cparams("parallel", "parallel", "arbitrary"), name="moe_experts",
    )(h2, cmb, xc, mod, mod, w1.astype(BF16), w3.astype(BF16), w2.astype(BF16))


def kernel(x, c, ctx, c_ctx, ada_w, ada_b, w_in, q_norm, k_norm, hy_conv_w, hy_conv_b, hy_f1, hy_fb1, hy_f2, hy_fb2, hy_f3, hy_skip, rw_mu, rw_w0, rw_w2, rw_a0, rw_a2, rw_g2, rw_k_k, rw_k_a, rw_r_k, rw_ln_w, rw_ln_b, lru_conv_w, lru_conv_b, lru_wa, lru_ba, lru_wx, lru_bx, lru_lambda, w_br_attn, w_br_hyena, w_br_rwkv, w_br_lru, w_out, moe_w_grp, moe_b_grp, moe_w_rt, moe_b_rt, moe_w1, moe_w3, moe_w2):
    b, l, d = x.shape
    cl = ctx.shape[1]
    depth = ada_w.shape[0]
    assert b < MOD_ROWS and cl % RWKV_CHUNK == 0 and l % RWKV_CHUNK == 0

    xc = jnp.concatenate([ctx, x], axis=1)
    cc = jnp.zeros((MOD_ROWS, d), F32).at[:b].set(c).at[b].set(c_ctx)
    mod_all = _ada_mod(cc, ada_w, ada_b).reshape(depth, MOD_ROWS, 6, d)

    cos2, sin2 = _rope_tables(l, cl)
    mats_x = _dft_mats(l)
    mats_c = _dft_mats(cl)
    qkv_w = ATTN_WIDTH + 2 * ATTN_KV_WIDTH
    col = np.cumsum([0, qkv_w, 3 * HYENA_WIDTH, RWKV_PROJ, 2 * LRU_WIDTH, N_BRANCH * d])

    for i in range(depth):
        need_ctx = i < depth - 1
        mod = mod_all[i]
        w_i = w_in[i].astype(BF16)
        h1 = _modnorm(xc, mod, cl)
        pqkv, phy, prw, plr, gates = (_proj(h1, w_i[:, col[j]:col[j + 1]]) for j in range(5))

        qn, kt, vx = _attn_prep(pqkv, cos2, sin2, q_norm[i], k_norm[i])
        y_att_x = _attention(qn, kt, vx, cl, 'x', cl + l)
        if need_ctx:
            y_att_c = _attention(qn, kt, vx, cl, 'ctx', cl)
        else:
            y_att_c = jnp.zeros((b, cl, ATTN_WIDTH), BF16)
        y_att = jnp.concatenate([y_att_c, y_att_x], axis=1)

        hv, hx1, hx2 = _hyena_pre(phy, hy_conv_w[i], hy_conv_b[i], cl)
        filt = (hy_f1[i], hy_fb1[i], hy_f2[i], hy_fb2[i], hy_f3[i])
        y_hx = _hyena_run(hv[:, cl:], hx1[:, cl:], hx2[:, cl:], _hyena_spectra(l, filt, mats_x), hy_skip[i], mats_x)
        if need_ctx:
            y_hc = _hyena_run(hv[:, :cl], hx1[:, :cl], hx2[:, :cl], _hyena_spectra(cl, filt, mats_c), hy_skip[i],
                              mats_c)
        else:
            y_hc = jnp.zeros((b, cl, HYENA_WIDTH), BF16)
        y_hy = jnp.concatenate([y_hc, y_hx], axis=1)

        y_rw = _rwkv_mixer(prw, rw_mu[i], rw_w0[i], rw_w2[i], rw_a0[i], rw_a2[i], rw_g2[i], rw_k_k[i], rw_k_a[i],
                           rw_r_k[i].reshape(-1), rw_ln_w[i], rw_ln_b[i], cl)

        la, lb = _lru_pre(plr, lru_conv_w[i], lru_conv_b[i], lru_wa[i], lru_ba[i], lru_wx[i], lru_bx[i],
                          lru_lambda[i], cl)
        y_lr = _lru_scan(la, lb, plr, cl)

        xc = _merge((y_att, y_hy, y_rw, y_lr), gates, xc, mod,
                    (w_br_attn[i], w_br_hyena[i], w_br_rwkv[i], w_br_lru[i]), w_out[i], cl)

        h2, cmb = _route(xc, mod, moe_w_grp[i], moe_b_grp[i], moe_w_rt[i], moe_b_rt[i], cl)
        xc = _moe(h2, cmb, xc, mod, moe_w1[i], moe_w3[i], moe_w2[i], cl)
    return xc[:, cl:]
```

```python
import functools
import math

import numpy as np
import jax
import jax.numpy as jnp
from jax import lax
from jax.experimental import pallas as pl
from jax.experimental.pallas import tpu as pltpu

F32 = jnp.float32
BF16 = jnp.bfloat16

HEAD_DIM = 64
GRID_W = 64
EPS = 1e-6
ATTN_HEADS = 8
ATTN_KV_HEADS = 2
ATTN_GROUP = ATTN_HEADS // ATTN_KV_HEADS
ATTN_Q_BLOCKS = 2
ATTN_WIDTH = ATTN_HEADS * HEAD_DIM
ATTN_KV_WIDTH = ATTN_KV_HEADS * HEAD_DIM
ROPE_THETA = 10000.0
HYENA_WIDTH = 256
HYENA_ORDER = 2
HYENA_BANDS = 16
HYENA_DECAY_TARGET = 1e-2
HYENA_FAST_DECAY = 0.3
HYENA_SLOW_DECAY = 1.5
RWKV_HEADS = 4
RWKV_WIDTH = RWKV_HEADS * HEAD_DIM
RWKV_DECAY_RANK = 64
RWKV_ICLR_RANK = 64
RWKV_GATE_RANK = 128
RWKV_GN_EPS = 64e-5
RWKV_PROJ = 3 * RWKV_WIDTH + 2 * RWKV_DECAY_RANK + 2 * RWKV_ICLR_RANK + RWKV_GATE_RANK
RWKV_CHUNK = 64
RWKV_INV_BASE = 4
RWKV_STAGE_ROWS = 128
LRU_WIDTH = 256
LRU_BLOCKS = 4
LRU_C = 8.0
LRU_GROUPS_PER_STEP = 4
N_BRANCH = 4
N_GROUPS = 4
EXPERTS_PER_GROUP = 4
N_EXPERTS = N_GROUPS * EXPERTS_PER_GROUP
MOE_BLOCK_ROWS = 512

V7X_VMEM_LIMIT_BYTES = 52 * 1024 * 1024
SUBLANES = 8
LANES = 128
MOD_ROWS = 16


def _cparams(*sem):
    return pltpu.CompilerParams(dimension_semantics=sem, vmem_limit_bytes=V7X_VMEM_LIMIT_BYTES)


def _dot(a, b):
    return jnp.dot(a, b, preferred_element_type=F32)


def _dot_nt(a, b):
    return lax.dot_general(a, b, (((1,), (1,)), ((), ())), preferred_element_type=F32)


def _split3(x):
    hi = x.astype(BF16)
    r1 = x - hi.astype(F32)
    mid = r1.astype(BF16)
    lo = (r1 - mid.astype(F32)).astype(BF16)
    return hi, mid, lo


def _dot_exact_lhs(m_bf16, x):
    hi, mid, lo = _split3(x)
    return _dot(m_bf16, hi) + _dot(m_bf16, mid) + _dot(m_bf16, lo)


def _dot_exact_rhs(x, m_bf16):
    hi, mid, lo = _split3(x)
    return _dot(hi, m_bf16) + _dot(mid, m_bf16) + _dot(lo, m_bf16)


def _dot_bf(a, b):
    return _dot(a.astype(BF16), b.astype(BF16))


def _sigmoid(x):
    return 1.0 / (1.0 + jnp.exp(-x))


def _softplus(x):
    return jnp.maximum(x, 0.0) + jnp.log(1.0 + jnp.exp(-jnp.abs(x)))


def _silu(x):
    return x * _sigmoid(x)


def _largest_tile(n, cap, mult):
    best = None
    for t in range(mult, min(n, cap) + 1, mult):
        if n % t == 0:
            best = t
    assert best is not None, (n, cap, mult)
    return best


def _head_ones(width):
    idx = np.arange(width) // HEAD_DIM
    return jnp.asarray((idx[:, None] == idx[None, :]).astype(np.float32), dtype=BF16)


def _row_ids(tile_rows, tile_idx):
    return tile_idx * tile_rows + lax.broadcasted_iota(jnp.int32, (tile_rows, 1), 0)


def _mod_rows(modx_ref, modc_ref, idx, is_ctx):
    return jnp.where(is_ctx, modc_ref[0, idx:idx + 1, :], modx_ref[0, idx:idx + 1, :])


def _rms_modulate(x, shift, scale):
    ms = jnp.mean(x * x, axis=-1, keepdims=True)
    return (x * lax.rsqrt(ms + EPS)) * (1.0 + scale) + shift


def _ada_kernel(c_ref, w_ref, b_ref, o_ref):
    s = _silu(c_ref[...])
    o_ref[0] = jnp.dot(s, w_ref[0], preferred_element_type=F32, precision=lax.Precision.HIGHEST) + b_ref[0]


def _ada_mod(cc, ada_w, ada_b):
    depth, d, n6 = ada_w.shape
    tn = _largest_tile(n6, 1024, LANES)
    return pl.pallas_call(
        _ada_kernel,
        out_shape=jax.ShapeDtypeStruct((depth, MOD_ROWS, n6), F32),
        grid=(depth, n6 // tn),
        in_specs=[pl.BlockSpec((MOD_ROWS, d), lambda i, j: (0, 0)),
                  pl.BlockSpec((1, d, tn), lambda i, j: (i, 0, j)),
                  pl.BlockSpec((1, 1, tn), lambda i, j: (i, 0, j))],
        out_specs=pl.BlockSpec((1, MOD_ROWS, tn), lambda i, j: (i, 0, j)),
        compiler_params=_cparams("parallel", "parallel"),
        name="ada_mod",
    )(cc, ada_w, ada_b.reshape(depth, 1, n6))


def _modnorm_kernel(x_ref, modx_ref, modc_ref, o_ref, *, cl, tm):
    is_ctx = _row_ids(tm, pl.program_id(1)) < cl
    h = _rms_modulate(x_ref[0], _mod_rows(modx_ref, modc_ref, 0, is_ctx), _mod_rows(modx_ref, modc_ref, 1, is_ctx))
    o_ref[0] = h.astype(o_ref.dtype)


def _modnorm(xc, mod, cl):
    b, lt, d = xc.shape
    tm = _largest_tile(lt, 1088, 16)
    return pl.pallas_call(
        functools.partial(_modnorm_kernel, cl=cl, tm=tm),
        out_shape=jax.ShapeDtypeStruct((b, lt, d), BF16), grid=(b, lt // tm),
        in_specs=[pl.BlockSpec((1, tm, d), lambda i, r: (i, r, 0)),
                  pl.BlockSpec((1, 6, d), lambda i, r: (i, 0, 0)),
                  pl.BlockSpec((1, 6, d), lambda i, r: (b, 0, 0))],
        out_specs=pl.BlockSpec((1, tm, d), lambda i, r: (i, r, 0)),
        compiler_params=_cparams("parallel", "parallel"), name="modnorm",
    )(xc, mod, mod)


def _proj_kernel(h_ref, w_ref, o_ref):
    o_ref[0] = _dot(h_ref[0], w_ref[...]).astype(o_ref.dtype)


def _proj(h, w):
    b, lt, d = h.shape
    n = w.shape[1]
    tm = _largest_tile(lt, 2176, 16)
    tn = n if n <= 1280 else _largest_tile(n, 1024, 2 * LANES)
    return pl.pallas_call(
        _proj_kernel, out_shape=jax.ShapeDtypeStruct((b, lt, n), BF16), grid=(b, lt // tm, n // tn),
        in_specs=[pl.BlockSpec((1, tm, d), lambda i, r, j: (i, r, 0)), pl.BlockSpec((d, tn), lambda i, r, j: (0, j))],
        out_specs=pl.BlockSpec((1, tm, tn), lambda i, r, j: (i, r, j)),
        compiler_params=_cparams("parallel", "parallel", "parallel"), name="proj",
    )(h, w)


def _rope_tables(l, cl):
    n_freq = HEAD_DIM // 4
    t = jnp.arange(l)
    freqs = ROPE_THETA ** (-jnp.arange(n_freq, dtype=F32) / n_freq)
    pos = jnp.stack([t // GRID_W, t % GRID_W], -1).astype(F32)
    ang = pos[..., None] * freqs
    cos64 = jnp.stack([jnp.cos(ang), jnp.cos(ang)], axis=2).reshape(l, HEAD_DIM)
    sin64 = jnp.stack([-jnp.sin(ang), jnp.sin(ang)], axis=2).reshape(l, HEAD_DIM)
    cos64 = jnp.concatenate([jnp.ones((cl, HEAD_DIM), F32), cos64], 0)
    sin64 = jnp.concatenate([jnp.zeros((cl, HEAD_DIM), F32), sin64], 0)
    return jnp.tile(cos64, (1, 2)), jnp.tile(sin64, (1, 2))


def _head_rms(t, ones_ref):
    ms = _dot_exact_rhs(t * t, ones_ref[...]) * (1.0 / HEAD_DIM)
    return t * lax.rsqrt(ms + EPS)


def _rope(t, cos, sin):
    w = t.shape[-1]
    lane = lax.broadcasted_iota(jnp.int32, t.shape, 1)
    q4 = HEAD_DIM // 4
    first_half = (lane % (2 * q4)) < q4
    partner = jnp.where(first_half, pltpu.roll(t, w - q4, 1), pltpu.roll(t, q4, 1))
    return t * cos + partner * sin


def _attn_prep_kernel(p_ref, cos_ref, sin_ref, qg_ref, kg_ref, oq_ref, ok_ref, q_ref, kt_ref, vx_ref):
    p = p_ref[0].astype(F32)
    v = p[:, ATTN_WIDTH + ATTN_KV_WIDTH:]
    low = lax.broadcasted_iota(jnp.int32, v.shape, 1) < HEAD_DIM
    vx_ref[0, 0] = jnp.where(low, v, 1.0).astype(vx_ref.dtype)
    vx_ref[0, 1] = jnp.where(low, pltpu.roll(v, HEAD_DIM, 1), 1.0).astype(vx_ref.dtype)
    cos2, sin2 = cos_ref[...], sin_ref[...]
    reps = ATTN_WIDTH // (2 * HEAD_DIM)
    cos_q = jnp.concatenate([cos2] * reps, axis=1)
    sin_q = jnp.concatenate([sin2] * reps, axis=1)
    q = _head_rms(p[:, :ATTN_WIDTH], oq_ref) * qg_ref[...]
    q_ref[0] = _rope(q, cos_q, sin_q).astype(q_ref.dtype)
    k = _head_rms(p[:, ATTN_WIDTH:ATTN_WIDTH + ATTN_KV_WIDTH], ok_ref) * kg_ref[...]
    kt_ref[0] = _rope(k, cos2, sin2).T.astype(kt_ref.dtype)


def _attn_prep(pqkv, cos2, sin2, q_gain, k_gain):
    b, lt, wtot = pqkv.shape
    tr = _largest_tile(lt, 2176, LANES)
    qg = jnp.tile(q_gain * (HEAD_DIM ** -0.5 * math.log2(math.e)), ATTN_HEADS).reshape(1, ATTN_WIDTH)
    kg = jnp.tile(k_gain, ATTN_KV_HEADS).reshape(1, ATTN_KV_WIDTH)
    return pl.pallas_call(
        _attn_prep_kernel,
        out_shape=(jax.ShapeDtypeStruct((b, lt, ATTN_WIDTH), BF16),
                   jax.ShapeDtypeStruct((b, ATTN_KV_WIDTH, lt), BF16),
                   jax.ShapeDtypeStruct((b, ATTN_KV_HEADS, lt, 2 * HEAD_DIM), BF16)),
        grid=(b, lt // tr),
        in_specs=[pl.BlockSpec((1, tr, wtot), lambda i, r: (i, r, 0)),
                  pl.BlockSpec((tr, 2 * HEAD_DIM), lambda i, r: (r, 0)),
                  pl.BlockSpec((tr, 2 * HEAD_DIM), lambda i, r: (r, 0)),
                  pl.BlockSpec((1, ATTN_WIDTH), lambda i, r: (0, 0)),
                  pl.BlockSpec((1, ATTN_KV_WIDTH), lambda i, r: (0, 0)),
                  pl.BlockSpec((ATTN_WIDTH, ATTN_WIDTH), lambda i, r: (0, 0)),
                  pl.BlockSpec((ATTN_KV_WIDTH, ATTN_KV_WIDTH), lambda i, r: (0, 0))],
        out_specs=(pl.BlockSpec((1, tr, ATTN_WIDTH), lambda i, r: (i, r, 0)),
                   pl.BlockSpec((1, ATTN_KV_WIDTH, tr), lambda i, r: (i, 0, r)),
                   pl.BlockSpec((1, ATTN_KV_HEADS, tr, 2 * HEAD_DIM), lambda i, r: (i, 0, r, 0))),
        compiler_params=_cparams("parallel", "parallel"),
        name="attn_prep",
    )(pqkv, cos2, sin2, qg, kg, _head_ones(ATTN_WIDTH), _head_ones(ATTN_KV_WIDTH))


def _attn_kernel(*refs, nq, nk):
    q_refs, (kt_ref, v_ref, o_ref) = refs[:nq], refs[nq:]
    outs = []
    for h in range(ATTN_HEADS):
        kv = h // ATTN_GROUP
        hs = slice(h * HEAD_DIM, (h + 1) * HEAD_DIM)
        qh = jnp.concatenate([q_ref[0, :, hs] for q_ref in q_refs], axis=0)
        s = _dot(qh, kt_ref[0, kv * HEAD_DIM:(kv + 1) * HEAD_DIM, :nk])
        m = jnp.max(s, axis=-1, keepdims=True)
        p = jnp.exp2(s - m)
        o = _dot(p.astype(BF16), v_ref[0, kv, :nk, :])
        outs.append(o[:, :HEAD_DIM] / o[:, HEAD_DIM:])
    o_ref[0] = jnp.concatenate(outs, axis=-1).astype(o_ref.dtype)


def _attention(qn, kt, vx, cl, rows, nk):
    b, lt, _ = qn.shape
    blk = 256 if (cl % 256 == 0 and lt % 256 == 0) else 128
    if rows == 'ctx':
        n_rows, first, nq = cl, 0, 1
    else:
        n_rows, first = lt - cl, cl // blk
        nq = ATTN_Q_BLOCKS if n_rows % (ATTN_Q_BLOCKS * blk) == 0 else 1
    tq = nq * blk
    q_specs = [pl.BlockSpec((1, blk, ATTN_WIDTH), lambda i, t, j=j: (i, first + nq * t + j, 0)) for j in range(nq)]
    return pl.pallas_call(
        functools.partial(_attn_kernel, nq=nq, nk=nk),
        out_shape=jax.ShapeDtypeStruct((b, n_rows, ATTN_WIDTH), BF16),
        grid=(b, n_rows // tq),
        in_specs=q_specs + [pl.BlockSpec((1, ATTN_KV_WIDTH, lt), lambda i, t: (i, 0, 0)),
                            pl.BlockSpec((1, ATTN_KV_HEADS, lt, 2 * HEAD_DIM), lambda i, t: (i, 0, 0, 0))],
        out_specs=pl.BlockSpec((1, tq, ATTN_WIDTH), lambda i, t: (i, t, 0)),
        compiler_params=_cparams("parallel", "parallel"),
        name="attention_" + rows,
    )(*([qn] * nq), kt, vx)


def _halo_specs(tr, lt, width, lead):
    per = tr // SUBLANES
    last = lt // SUBLANES - 1
    nlead = len(lead)

    def prev_map(*ids):
        return (*ids[:nlead], jnp.maximum(ids[nlead] * per - 1, 0), 0)

    def next_map(*ids):
        return (*ids[:nlead], jnp.minimum((ids[nlead] + 1) * per, last), 0)

    blk = (*lead, SUBLANES, width)
    return pl.BlockSpec(blk, prev_map), pl.BlockSpec(blk, next_map)


def _shift_rows(x, halo, offset, rows, cl, lt):
    tr = x.shape[0]
    local = lax.broadcasted_iota(jnp.int32, (tr, 1), 0)
    if offset < 0:
        y = pltpu.roll(x, -offset, 0)
        y = jnp.where(local == 0, halo, y)
        bad = (rows == 0) | (rows == cl)
    else:
        y = pltpu.roll(x, tr - offset, 0)
        for j in range(offset):
            y = jnp.where(local == tr - offset + j, halo[j:j + 1, :], y)
        bad = (rows >= lt - offset) | ((rows >= cl - offset) & (rows < cl))
    return jnp.where(bad, 0.0, y)


def _hyena_pre_kernel(p_ref, pv_ref, nx_ref, w_ref, b_ref, v_ref, x1_ref, x2_ref, *, cl, lt, tr):
    rows = _row_ids(tr, pl.program_id(1))
    p = p_ref[0].astype(F32)
    pm = _shift_rows(p, pv_ref[0, SUBLANES - 1:, :].astype(F32), -1, rows, cl, lt)
    pp = _shift_rows(p, nx_ref[0, :1, :].astype(F32), 1, rows, cl, lt)
    z = pm * w_ref[0:1, :] + p * w_ref[1:2, :] + pp * w_ref[2:3, :] + b_ref[...]
    c = HYENA_WIDTH
    v_ref[0] = z[:, :c].astype(v_ref.dtype)
    x1_ref[0] = z[:, c:2 * c].astype(x1_ref.dtype)
    x2_ref[0] = z[:, 2 * c:].astype(x2_ref.dtype)


def _hyena_pre(phy, conv_w, conv_b, cl):
    b, lt, w = phy.shape
    tr = _largest_tile(lt, 1088, 16)
    prev_spec, next_spec = _halo_specs(tr, lt, w, (1,))
    out = jax.ShapeDtypeStruct((b, lt, HYENA_WIDTH), BF16)
    ospec = pl.BlockSpec((1, tr, HYENA_WIDTH), lambda i, r: (i, r, 0))
    return pl.pallas_call(
        functools.partial(_hyena_pre_kernel, cl=cl, lt=lt, tr=tr),
        out_shape=(out, out, out),
        grid=(b, lt // tr),
        in_specs=[pl.BlockSpec((1, tr, w), lambda i, r: (i, r, 0)), prev_spec, next_spec,
                  pl.BlockSpec(conv_w.shape, lambda i, r: (0, 0)),
                  pl.BlockSpec((1, w), lambda i, r: (0, 0))],
        out_specs=(ospec, ospec, ospec),
        compiler_params=_cparams("parallel", "parallel"),
        name="hyena_pre",
    )(phy, phy, phy, conv_w, conv_b.reshape(1, w))


def _dft_mats(n):
    tm = _largest_tile(n, 256, 16)
    tn = _largest_tile(n, 1024, LANES)
    s = jnp.arange(n, dtype=jnp.int32)[None, :]

    def tables(k):
        ang = ((k[:, None] * s) % (2 * n)).astype(F32) * (math.pi / n)
        return jnp.cos(ang), jnp.sin(ang)

    cb, sb = tables(jnp.arange(0, n, tm, dtype=jnp.int32))
    co, so = tables(jnp.arange(tm, dtype=jnp.int32))
    base = pl.BlockSpec((1, 1, tn), lambda j, c: (j, 0, c))
    off = pl.BlockSpec((tm, tn), lambda j, c: (0, c))
    out = jax.ShapeDtypeStruct((n, n), BF16)
    ospec = pl.BlockSpec((tm, tn), lambda j, c: (j, c))
    return pl.pallas_call(
        functools.partial(_dft_mats_kernel, tm=tm, tn=tn), out_shape=(out, out, out), grid=(n // tm, n // tn),
        in_specs=[base, base, off, off], out_specs=(ospec, ospec, ospec),
        compiler_params=_cparams("parallel", "parallel"), name="dft_mats",
    )(cb.reshape(n // tm, 1, n), sb.reshape(n // tm, 1, n), co, so)


def _dft_mats_kernel(cb_ref, sb_ref, co_ref, so_ref, cm_ref, sm_ref, smt_ref, *, tm, tn):
    cb, sb, co, so = cb_ref[0], sb_ref[0], co_ref[...], so_ref[...]
    k = pl.program_id(0) * tm + lax.broadcasted_iota(jnp.int32, (tm, tn), 0)
    s = pl.program_id(1) * tn + lax.broadcasted_iota(jnp.int32, (tm, tn), 1)
    sn = sb * co + cb * so
    cm_ref[...] = (cb * co - sb * so).astype(cm_ref.dtype)
    sm_ref[...] = jnp.where(k == 0, (1 - 2 * (s % 2)).astype(F32), sn).astype(sm_ref.dtype)
    smt_ref[...] = jnp.where(s == 0, (1 - 2 * (k % 2)).astype(F32), sn).astype(smt_ref.dtype)


def _dft_raw_kernel(cm_ref, sm_ref, z_ref, zr_ref, zi_ref):
    zr_ref[...] = _dot(cm_ref[...], z_ref[...])
    zi_ref[...] = _dot(sm_ref[...], z_ref[...])


def _dft_raw(cm, sm, z):
    n, c = z.shape
    tm = _largest_tile(n, 256, 16)
    mat = pl.BlockSpec((tm, n), lambda j: (j, 0))
    out = jax.ShapeDtypeStruct((n, c), F32)
    ospec = pl.BlockSpec((tm, c), lambda j: (j, 0))
    return pl.pallas_call(
        _dft_raw_kernel, out_shape=(out, out), grid=(n // tm,),
        in_specs=[mat, mat, pl.BlockSpec((n, c), lambda j: (0, 0))],
        out_specs=(ospec, ospec),
        compiler_params=_cparams("parallel"), name="dft_raw",
    )(cm, sm, z)


def _dft_fwd_kernel(cm_ref, sm_ref, z_ref, hr_ref, hi_ref, yr_ref, yi_ref, *, bb, tm):
    is_row0 = _row_ids(tm, pl.program_id(1)) == 0
    hr, hi = hr_ref[...], hi_ref[...]
    for i in range(bb):
        z = z_ref[i]
        zr = _dot(cm_ref[...], z)
        zi = _dot(sm_ref[...], z)
        yr = jnp.where(is_row0, 0.5 * zr * hr, zr * hr + zi * hi)
        yi = jnp.where(is_row0, 0.5 * zi * hi, zi * hr - zr * hi)
        yr_ref[i] = yr.astype(yr_ref.dtype)
        yi_ref[i] = yi.astype(yi_ref.dtype)


def _dft_inv_kernel(cm_ref, smt_ref, yr_ref, yi_ref, z_ref, g_ref, skip_ref, o_ref, *, bb):
    for i in range(bb):
        y = _dot(cm_ref[...], yr_ref[i]) + _dot(smt_ref[...], yi_ref[i])
        y = y + z_ref[i].astype(F32) * skip_ref[...]
        o_ref[i] = (g_ref[i].astype(F32) * y).astype(o_ref.dtype)


def _longconv_gated(z, gate, hr, hi, skip, mats):
    cm, sm, smt = mats
    b, n, c = z.shape
    bb = 2 if b % 2 == 0 else 1
    tm = _largest_tile(n, 256, 16)
    mat = pl.BlockSpec((tm, n), lambda i, j: (j, 0))
    full = pl.BlockSpec((bb, n, c), lambda i, j: (i, 0, 0))
    tile = pl.BlockSpec((bb, tm, c), lambda i, j: (i, j, 0))
    filt = pl.BlockSpec((tm, c), lambda i, j: (j, 0))
    spec_shape = jax.ShapeDtypeStruct((b, n, c), BF16)
    yr, yi = pl.pallas_call(
        functools.partial(_dft_fwd_kernel, bb=bb, tm=tm),
        out_shape=(spec_shape, spec_shape), grid=(b // bb, n // tm),
        in_specs=[mat, mat, full, filt, filt], out_specs=(tile, tile),
        compiler_params=_cparams("parallel", "parallel"), name="dft_fwd",
    )(cm, sm, z, hr, hi)
    return pl.pallas_call(
        functools.partial(_dft_inv_kernel, bb=bb),
        out_shape=jax.ShapeDtypeStruct((b, n, c), BF16), grid=(b // bb, n // tm),
        in_specs=[mat, mat, full, full, tile, tile, pl.BlockSpec((1, c), lambda i, j: (0, 0))],
        out_specs=tile,
        compiler_params=_cparams("parallel", "parallel"), name="dft_inv",
    )(cm, smt, yr, yi, z, gate, skip.reshape(1, c))


def _hyena_filters(n, f1, fb1, f2, fb2, f3):
    t = jnp.arange(n, dtype=F32) / n
    bands = jnp.arange(1, HYENA_BANDS + 1, dtype=F32)
    ang = 2.0 * math.pi * t[:, None] * bands
    feat = jnp.concatenate([t[:, None], jnp.sin(ang), jnp.cos(ang)], axis=-1)
    hp = lax.Precision.HIGHEST
    h = jnp.sin(jnp.dot(feat, f1, precision=hp) + fb1)
    h = jnp.sin(jnp.dot(h, f2, precision=hp) + fb2)
    h = jnp.dot(h, f3, precision=hp).reshape(n, HYENA_ORDER, 2, HYENA_WIDTH)
    deltas = jnp.linspace(-math.log(HYENA_DECAY_TARGET) / HYENA_SLOW_DECAY,
                          -math.log(HYENA_DECAY_TARGET) / HYENA_FAST_DECAY, HYENA_WIDTH, dtype=F32)
    h = h * jnp.exp(-t[:, None] * deltas)[:, None, None, :]
    return h / jnp.sum(jnp.abs(h), axis=(0, 2), keepdims=True)


def _hyena_spectra(n, filt_params, mats):
    cm, sm, _ = mats
    h = _hyena_filters(n, *filt_params)
    hf = h[:, :, 0].reshape(n, HYENA_ORDER * HYENA_WIDTH)
    hb = h[:, :, 1].reshape(n, HYENA_ORDER * HYENA_WIDTH)
    hb = jnp.where(jnp.arange(n)[:, None] == 0, 0.0, hb)
    zr, zi = _dft_raw(cm, sm, jnp.concatenate([hf + hb, hb - hf], axis=1).astype(BF16))
    oc = HYENA_ORDER * HYENA_WIDTH
    hr = zr[:, :oc]
    hi = jnp.where(jnp.arange(n)[:, None] == 0, zi[:, :oc], zi[:, oc:])
    scale = 1.0 / n
    hr = (hr * scale).reshape(n, HYENA_ORDER, HYENA_WIDTH).transpose(1, 0, 2)
    hi = (hi * scale).reshape(n, HYENA_ORDER, HYENA_WIDTH).transpose(1, 0, 2)
    return hr, hi


def _hyena_run(v, x1, x2, spectra, skip, mats):
    hr, hi = spectra
    y = v
    for o, gate in enumerate((x1, x2)):
        y = _longconv_gated(y, gate, hr[o], hi[o], skip[o], mats)
    return y


def _blockdiag_dense(w):
    nb, blk = w.shape[1], w.shape[2]
    eye = jnp.eye(nb, dtype=w.dtype)
    return jnp.einsum('dncf,nm->dncmf', w, eye).reshape(w.shape[0], nb * blk, nb * blk)


def _lru_pre_kernel(p_ref, pv_ref, nx_ref, cw_ref, cb_ref, wa_ref, ba_ref, wx_ref, bx_ref, lam_ref,
                    a_ref, b_ref, *, cl, lt, tr):
    rows = _row_ids(tr, pl.program_id(1))
    c = LRU_WIDTH
    x = p_ref[0][:, c:].astype(F32)
    pv = pv_ref[0][SUBLANES - 1:, c:].astype(F32)
    nx = nx_ref[0][:, c:].astype(F32)
    xc = (_shift_rows(x, pv, -1, rows, cl, lt) * cw_ref[0:1, :] + x * cw_ref[1:2, :]
          + _shift_rows(x, nx[:1], 1, rows, cl, lt) * cw_ref[2:3, :]
          + _shift_rows(x, nx[:2], 2, rows, cl, lt) * cw_ref[3:4, :] + cb_ref[...])
    xcb = xc.astype(BF16)
    for d in range(2):
        r = _sigmoid(_dot(xcb, wa_ref[d]) + ba_ref[d])
        i = _sigmoid(_dot(xcb, wx_ref[d]) + bx_ref[d])
        log_a = -LRU_C * r * _softplus(-lam_ref[d])
        a_ref[d, 0] = jnp.exp(log_a)
        b_ref[d, 0] = jnp.sqrt(1.0 - jnp.exp(2.0 * log_a)) * (i * xc)


def _lru_pre(plr, conv_w, conv_b, wa, ba, wx, bx, lam, cl):
    b, lt, w = plr.shape
    c = LRU_WIDTH
    tr = _largest_tile(lt, 1088, 16)
    prev_spec, next_spec = _halo_specs(tr, lt, w, (1,))
    out = jax.ShapeDtypeStruct((2, b, lt, c), F32)
    ospec = pl.BlockSpec((2, 1, tr, c), lambda i, r: (0, i, r, 0))
    const2 = lambda shape: pl.BlockSpec(shape, lambda i, r: (0,) * len(shape))
    return pl.pallas_call(
        functools.partial(_lru_pre_kernel, cl=cl, lt=lt, tr=tr),
        out_shape=(out, out), grid=(b, lt // tr),
        in_specs=[pl.BlockSpec((1, tr, w), lambda i, r: (i, r, 0)), prev_spec, next_spec,
                  const2(conv_w.shape), const2((1, c)), const2((2, c, c)), const2((2, 1, c)),
                  const2((2, c, c)), const2((2, 1, c)), const2((2, 1, c))],
        out_specs=(ospec, ospec),
        compiler_params=_cparams("parallel", "parallel"), name="lru_pre",
    )(plr, plr, plr, conv_w, conv_b.reshape(1, c), _blockdiag_dense(wa).astype(BF16), ba.reshape(2, 1, c),
      _blockdiag_dense(wx).astype(BF16), bx.reshape(2, 1, c), lam.reshape(2, 1, c))


def _gelu_tanh(x):
    return 0.5 * x * (1.0 + jnp.tanh(math.sqrt(2.0 / math.pi) * (x + 0.044715 * (x * x * x))))


def _lru_scan_kernel(af_ref, bf_ref, ar_ref, br_ref, g_ref, o_ref, acc_ref, *, cl, lt):
    row = lax.broadcasted_iota(jnp.int32, (SUBLANES, LANES), 0)

    def group_scan(a, b, reverse):
        for s in (1, 2, 4):
            if reverse:
                keep = row < SUBLANES - s
                a_s = jnp.where(keep, pltpu.roll(a, SUBLANES - s, 0), 1.0)
                b_s = jnp.where(keep, pltpu.roll(b, SUBLANES - s, 0), 0.0)
            else:
                keep = row >= s
                a_s = jnp.where(keep, pltpu.roll(a, s, 0), 1.0)
                b_s = jnp.where(keep, pltpu.roll(b, s, 0), 0.0)
            b = a * b_s + b
            a = a * a_s
        return a, b

    ng = math.gcd(math.gcd(cl // SUBLANES, (lt - cl) // SUBLANES), LRU_GROUPS_PER_STEP)
    span = ng * SUBLANES

    def fwd_body(i, h):
        sl = pl.ds(pl.multiple_of(i * span, span), span)
        a_all, b_all = af_ref[0, 0, sl, :], bf_ref[0, 0, sl, :]
        scans = [group_scan(a_all[j * SUBLANES:(j + 1) * SUBLANES], b_all[j * SUBLANES:(j + 1) * SUBLANES], False)
                 for j in range(ng)]
        outs = []
        for a, b in scans:
            hh = a * h + b
            outs.append(hh)
            h = hh[SUBLANES - 1:SUBLANES, :]
        acc_ref[sl, :] = jnp.concatenate(outs, axis=0)
        return h

    lax.fori_loop(0, lt // span, fwd_body, jnp.zeros((1, LANES), F32))

    def rev_body(i, h, top):
        sl = pl.ds(pl.multiple_of((top - 1 - i) * span, span), span)
        a_all, b_all = ar_ref[0, 0, sl, :], br_ref[0, 0, sl, :]
        scans = [group_scan(a_all[j * SUBLANES:(j + 1) * SUBLANES], b_all[j * SUBLANES:(j + 1) * SUBLANES], True)
                 for j in range(ng)]
        outs = [None] * ng
        for j in reversed(range(ng)):
            a, b = scans[j]
            hh = a * h + b
            outs[j] = hh
            h = hh[0:1, :]
        gate = g_ref[0, sl, :].astype(F32)
        o_ref[0, sl, :] = ((acc_ref[sl, :] + jnp.concatenate(outs, axis=0)) * _gelu_tanh(gate)).astype(o_ref.dtype)
        return h

    h = lax.fori_loop(0, cl // span, functools.partial(rev_body, top=cl // span), jnp.zeros((1, LANES), F32))
    lax.fori_loop(0, (lt - cl) // span, functools.partial(rev_body, top=lt // span), h)


def _lru_scan(a, b_, plr, cl):
    _, b, lt, c = a.shape
    nl = c // LANES
    fwd = pl.BlockSpec((1, 1, lt, LANES), lambda i, j: (0, i, 0, j))
    rev = pl.BlockSpec((1, 1, lt, LANES), lambda i, j: (1, i, 0, j))
    return pl.pallas_call(
        functools.partial(_lru_scan_kernel, cl=cl, lt=lt),
        out_shape=jax.ShapeDtypeStruct((b, lt, c), BF16), grid=(b, nl),
        in_specs=[fwd, fwd, rev, rev, pl.BlockSpec((1, lt, LANES), lambda i, j: (i, 0, j))],
        out_specs=pl.BlockSpec((1, lt, LANES), lambda i, j: (i, 0, j)),
        scratch_shapes=[pltpu.VMEM((lt, LANES), F32)],
        compiler_params=_cparams("parallel", "parallel"), name="lru_scan",
    )(a, b_, a, b_, plr)


def _row_perm(tr, rev):
    t = lax.broadcasted_iota(jnp.int32, (tr, tr), 0)
    s = lax.broadcasted_iota(jnp.int32, (tr, tr), 1)
    return jnp.where(s == jnp.where(rev, tr - 1 - t, t), 1.0, 0.0).astype(BF16)


def _mirror_tile(r, rev, n_ctx_tiles, n_tiles):
    m = jnp.where(r < n_ctx_tiles, n_ctx_tiles - 1 - r, n_tiles - 1 + n_ctx_tiles - r)
    return jnp.where(rev, m, r)


def _chunk_masks(tr):
    t = lax.broadcasted_iota(jnp.int32, (tr, tr), 0)
    s = lax.broadcasted_iota(jnp.int32, (tr, tr), 1)
    same = (t // RWKV_CHUNK) == (s // RWKV_CHUNK)
    return same, same & (s <= t), same & (s < t)


def _rwkv_prep_kernel(p_ref, pv_ref, nx_ref, mu_ref, w0_ref, w2_ref, a0_ref, a2_ref, g2_ref, kk_ref, ka_ref,
                      rk_ref, ones_ref,
                      aq_ref, vp_ref, y0_ref, rt_ref, mrb_ref, bht_ref, gm_ref, pc_ref, g_ref, bonus_ref,
                      *, cl, lt, tr):
    c = RWKV_WIDTH
    rows = _row_ids(tr, pl.program_id(2))
    rev = pl.program_id(0) == 1
    p = _dot(_row_perm(tr, rev), p_ref[0])
    before = jnp.where(rev, nx_ref[0, :1, :], pv_ref[0, SUBLANES - 1:, :]).astype(F32)
    after = jnp.where(rev, pv_ref[0, SUBLANES - 1:, :], nx_ref[0, :1, :]).astype(F32)
    prev = _shift_rows(p, before, -1, rows, cl, lt)
    nxt = _shift_rows(p, after, 1, rows, cl, lt)
    xm = p + (prev - p) * mu_ref[0, 0:1, :] + (nxt - p) * mu_ref[0, 1:2, :]
    r, k, v = xm[:, :c], xm[:, c:2 * c], xm[:, 2 * c:3 * c]
    o = 3 * c
    w1 = xm[:, o:o + 2 * RWKV_DECAY_RANK]
    a1 = xm[:, o + 2 * RWKV_DECAY_RANK:o + 2 * RWKV_DECAY_RANK + 2 * RWKV_ICLR_RANK]
    g1 = xm[:, o + 2 * RWKV_DECAY_RANK + 2 * RWKV_ICLR_RANK:]
    wlog = -_softplus(-(w0_ref[0] + _dot(jnp.tanh(w1).astype(BF16), w2_ref[0]))) - 0.5
    ld = -jnp.exp(wlog)
    a = _sigmoid(a0_ref[0] + _dot(a1.astype(BF16), a2_ref[0]))
    g_ref[0, 0] = _dot(_sigmoid(g1).astype(BF16), g2_ref[...])
    kk = k * kk_ref[...]
    kk = kk * lax.rsqrt(_dot_exact_rhs(kk * kk, ones_ref[...]) + 1e-12)
    kd = k * (1.0 + (a - 1.0) * ka_ref[...])
    bonus_ref[0, 0] = _dot_exact_rhs(r * kd * rk_ref[...], ones_ref[...]) * v
    beta = kk * a

    _rwkv_chunk_stage(ld, kk, r, kd, beta, v, aq_ref, vp_ref, y0_ref, rt_ref, mrb_ref, bht_ref, gm_ref, pc_ref)


def _rwkv_chunk_stage(ld, kk, r, kd, beta, v, aq_ref, vp_ref, y0_ref, rt_ref, mrb_ref, bht_ref, gm_ref, pc_ref):
    tr = ld.shape[0]
    n = min(RWKV_STAGE_ROWS, tr)
    nparts = tr // n
    hd = HEAD_DIM
    ch = RWKV_CHUNK
    nch = n // ch
    same_t, incl_t, _ = _chunk_masks(tr)
    cum = _dot_exact_lhs(jnp.where(incl_t, 1.0, 0.0).astype(BF16), ld)
    tot = _dot_exact_lhs(jnp.where(same_t, 1.0, 0.0).astype(BF16), ld)
    _, incl, strict = _chunk_masks(n)
    alpha_t = kk * jnp.exp(cum - ld)
    r_t = r * jnp.exp(cum)
    e_neg = jnp.exp(-cum)
    k_t = kd * e_neg
    b_t = beta * e_neg
    e_rem = jnp.exp(tot - cum)
    k_hat_t = (kd * e_rem).T
    b_hat_t = (beta * e_rem).T
    pc_t = jnp.exp(tot).T
    t_i = lax.broadcasted_iota(jnp.int32, (n, n), 0)
    s_i = lax.broadcasted_iota(jnp.int32, (n, n), 1)
    eye_f = jnp.where(t_i == s_i, 1.0, 0.0)
    same_blk = []
    size = RWKV_INV_BASE
    while size <= ch:
        same_blk.append((t_i // size) == (s_i // size))
        size *= 2
    col_chunk = lax.broadcasted_iota(jnp.int32, (n, nch * hd), 1) // hd
    row_chunk = lax.broadcasted_iota(jnp.int32, (n, nch * hd), 0) // ch

    def diag_blocks(m):
        out = m[:, :ch]
        for j in range(1, nch):
            out = out + m[:, j * ch:(j + 1) * ch]
        return out

    chains = [(p, h) for p in range(nparts) for h in range(RWKV_HEADS)]

    def blk(t, p, h):
        return t[p * n:(p + 1) * n, h * hd:(h + 1) * hd]

    prods = [_dot_nt(jnp.concatenate([blk(alpha_t, p, h), blk(r_t, p, h)], axis=0).astype(BF16),
                     jnp.concatenate([blk(b_t, p, h), blk(k_t, p, h)], axis=0).astype(BF16))
             for p, h in chains]
    l_ab = [jnp.where(strict, pr[:n, :n], 0.0) for pr in prods]
    pw = [jnp.where(same_blk[0], l, 0.0) for l in l_ab]
    t_inv = [eye_f - l for l in pw]
    for _ in range(int(math.log2(RWKV_INV_BASE)) - 1):
        pw = [_dot_bf(q, q) for q in pw]
        t_inv = [t + _dot_bf(t, q) for t, q in zip(t_inv, pw)]
    for lvl in range(1, len(same_blk)):
        off = same_blk[lvl] & jnp.logical_not(same_blk[lvl - 1])
        half = [_dot_bf(t, jnp.where(off, l, 0.0)) for t, l in zip(t_inv, l_ab)]
        t_inv = [t - _dot_bf(hf, t) for t, hf in zip(t_inv, half)]
    vh = [blk(v, p, h).astype(BF16) for p, h in chains]
    lakv = [_dot(jnp.where(strict, pr[:n, n:], 0.0).astype(BF16), vv) for pr, vv in zip(prods, vh)]
    x = [_dot(t.astype(BF16), jnp.concatenate([blk(alpha_t, p, h), lv], axis=1).astype(BF16))
         for t, (p, h), lv in zip(t_inv, chains, lakv)]
    y0 = [_dot(jnp.where(incl, pr[n:, n:], 0.0).astype(BF16), vv) for pr, vv in zip(prods, vh)]
    mrb = [diag_blocks(jnp.where(incl, pr[n:, :n], 0.0)) for pr in prods]
    gms = [_dot(k_hat_t[h * hd:(h + 1) * hd, p * n:(p + 1) * n].astype(BF16),
                jnp.where(col_chunk == row_chunk, jnp.concatenate([blk(v, p, h)] * nch, axis=1), 0.0).astype(BF16))
           for p, h in chains]

    def assemble(parts):
        return jnp.concatenate([jnp.concatenate(parts[p * RWKV_HEADS:(p + 1) * RWKV_HEADS], axis=1)
                                for p in range(nparts)], axis=0)

    aq_ref[0, 0] = assemble([t[:, :hd] for t in x]).astype(aq_ref.dtype)
    vp_ref[0, 0] = assemble([t[:, hd:] for t in x])
    y0_ref[0, 0] = assemble(y0)
    rt_ref[0, 0] = r_t.astype(rt_ref.dtype)
    mrb_ref[0, 0] = assemble(mrb).astype(mrb_ref.dtype)
    for p in range(nparts):
        for j in range(nch):
            jj = p * nch + j
            cs = slice(jj * ch, (jj + 1) * ch)
            bht_ref[0, 0, jj] = jnp.concatenate([b_hat_t[h * hd:(h + 1) * hd, cs] for h in range(RWKV_HEADS)],
                                                axis=1).astype(bht_ref.dtype)
            gm_ref[0, 0, jj] = jnp.concatenate([gms[p * RWKV_HEADS + h][:, j * hd:(j + 1) * hd]
                                                for h in range(RWKV_HEADS)], axis=1)
            pc_ref[0, 0, jj] = jnp.concatenate([pc_t[h * hd:(h + 1) * hd, cs] for h in range(RWKV_HEADS)], axis=1)


def _rwkv_tile(lt, cl):
    tr = 256 if (lt % 256 == 0 and cl % 256 == 0) else 128
    assert lt % tr == 0 and cl % tr == 0
    return tr


def _rwkv_prep(prw, mud, w0, w2p, a0, a2p, g2, k_k, k_a, r_k, cl):
    b, lt, w = prw.shape
    c = RWKV_WIDTH
    tr = _rwkv_tile(lt, cl)
    nch = tr // RWKV_CHUNK
    per = tr // SUBLANES
    src = lambda d, r: _mirror_tile(r, d == 1, cl // tr, lt // tr)
    prev_spec = pl.BlockSpec((1, SUBLANES, w), lambda d, i, r: (i, jnp.maximum(src(d, r) * per - 1, 0), 0))
    next_spec = pl.BlockSpec((1, SUBLANES, w),
                             lambda d, i, r: (i, jnp.minimum((src(d, r) + 1) * per, lt // SUBLANES - 1), 0))
    tile = pl.BlockSpec((1, 1, tr, c), lambda d, i, r: (d, i, r, 0))
    per_dir = lambda shape: pl.BlockSpec((1, *shape), lambda d, i, r: (d,) + (0,) * len(shape))
    const = lambda shape: pl.BlockSpec(shape, lambda d, i, r: (0,) * len(shape))
    seq = lambda dt: jax.ShapeDtypeStruct((2, b, lt, c), dt)
    chunked = jax.ShapeDtypeStruct((2, b, lt // RWKV_CHUNK, HEAD_DIM, c), F32)
    chunk_spec = pl.BlockSpec((1, 1, nch, HEAD_DIM, c), lambda d, i, r: (d, i, r, 0, 0))
    return pl.pallas_call(
        functools.partial(_rwkv_prep_kernel, cl=cl, lt=lt, tr=tr),
        out_shape=(seq(BF16), seq(F32), seq(F32), seq(BF16), seq(BF16),
                   jax.ShapeDtypeStruct(chunked.shape, BF16), chunked, chunked, seq(F32), seq(F32)),
        grid=(2, b, lt // tr),
        in_specs=[pl.BlockSpec((1, tr, w), lambda d, i, r: (i, src(d, r), 0)), prev_spec, next_spec,
                  per_dir((2, w)), per_dir((1, c)), per_dir((2 * RWKV_DECAY_RANK, c)), per_dir((1, c)),
                  per_dir((2 * RWKV_ICLR_RANK, c)), const((RWKV_GATE_RANK, c)), const((1, c)), const((1, c)),
                  const((1, c)), const((c, c))],
        out_specs=(tile, tile, tile, tile, tile, chunk_spec, chunk_spec, chunk_spec, tile, tile),
        compiler_params=_cparams("parallel", "parallel", "parallel"), name="rwkv_prep",
    )(prw, prw, prw, mud, w0.reshape(2, 1, c), w2p, a0.reshape(2, 1, c), a2p, g2.astype(BF16), k_k.reshape(1, c),
      k_a.reshape(1, c), r_k.reshape(1, c), _head_ones(c))


def _rwkv_scan_kernel(aq_ref, vp_ref, y0_ref, rt_ref, mrb_ref, bht_ref, gm_ref, pc_ref, y_ref, s_ref, *, ns):
    @pl.when(pl.program_id(1) == 0)
    def _():
        s_ref[...] = jnp.zeros_like(s_ref)

    c = RWKV_WIDTH
    ch = RWKV_CHUNK
    same_head = (lax.broadcasted_iota(jnp.int32, (c, c), 0) // HEAD_DIM
                 == lax.broadcasted_iota(jnp.int32, (c, c), 1) // HEAD_DIM)

    def head_blockdiag(t):
        return jnp.where(same_head, jnp.concatenate([t] * RWKV_HEADS, axis=0), 0.0).astype(BF16)

    for i in range(ns):
        s0 = s_ref[i]
        r1 = _dot(jnp.concatenate([aq_ref[i, 0], rt_ref[i, 0]], axis=0), head_blockdiag(s0))
        u = r1[:ch] + vp_ref[i, 0]
        r2 = _dot(jnp.concatenate([mrb_ref[i, 0], bht_ref[i, 0, 0]], axis=0), head_blockdiag(u))
        y_ref[i, 0] = r1[ch:] + y0_ref[i, 0] - r2[:ch]
        s_ref[i] = pc_ref[i, 0, 0] * s0 + gm_ref[i, 0, 0] - r2[ch:]


def _rwkv_scan(aq, vp, y0, rt, mrb, bht, gm, pc):
    _, b, lt, c = aq.shape
    nstream = 2 * b
    ns = nstream
    ch = RWKV_CHUNK
    merge = lambda t: t.reshape(nstream, 1, *t.shape[2:])
    tile = pl.BlockSpec((ns, 1, ch, c), lambda s, j: (s, 0, j, 0))
    chunk_spec = pl.BlockSpec((ns, 1, 1, HEAD_DIM, c), lambda s, j: (s, 0, j, 0, 0))
    y = pl.pallas_call(
        functools.partial(_rwkv_scan_kernel, ns=ns),
        out_shape=jax.ShapeDtypeStruct((nstream, 1, lt, c), F32), grid=(nstream // ns, lt // ch),
        in_specs=[tile, tile, tile, tile, tile, chunk_spec, chunk_spec, chunk_spec],
        out_specs=tile,
        scratch_shapes=[pltpu.VMEM((ns, HEAD_DIM, c), F32)],
        compiler_params=_cparams("parallel", "arbitrary"), name="rwkv_scan",
    )(*(merge(t) for t in (aq, vp, y0, rt, mrb, bht, gm, pc)))
    return y.reshape(2, b, lt, c)


def _rwkv_readout_kernel(yf_ref, yr_ref, bf_ref, br_ref, g_ref, lw_ref, lb_ref, ones_ref, o_ref, *, tr):
    unflip = _row_perm(tr, True)
    y = yf_ref[0, 0] + _dot_exact_lhs(unflip, yr_ref[0, 0])
    bonus = bf_ref[0, 0] + _dot_exact_lhs(unflip, br_ref[0, 0])
    inv = 1.0 / HEAD_DIM
    mu = _dot_exact_rhs(y, ones_ref[...]) * inv
    yc = y - mu
    var = _dot_exact_rhs(yc * yc, ones_ref[...]) * inv
    yn = yc * lax.rsqrt(var + RWKV_GN_EPS) * lw_ref[...] + lb_ref[...]
    o_ref[0] = ((yn + bonus) * g_ref[0, 0]).astype(o_ref.dtype)


def _rwkv_readout(y, bonus, g, ln_w, ln_b, cl):
    _, b, lt, c = y.shape
    tr = _rwkv_tile(lt, cl)
    fwd = pl.BlockSpec((1, 1, tr, c), lambda i, r: (0, i, r, 0))
    rev = pl.BlockSpec((1, 1, tr, c), lambda i, r: (1, i, _mirror_tile(r, True, cl // tr, lt // tr), 0))
    row = pl.BlockSpec((1, c), lambda i, r: (0, 0))
    return pl.pallas_call(
        functools.partial(_rwkv_readout_kernel, tr=tr),
        out_shape=jax.ShapeDtypeStruct((b, lt, c), BF16), grid=(b, lt // tr),
        in_specs=[fwd, rev, fwd, rev, fwd, row, row, pl.BlockSpec((c, c), lambda i, r: (0, 0))],
        out_specs=pl.BlockSpec((1, tr, c), lambda i, r: (i, r, 0)),
        compiler_params=_cparams("parallel", "parallel"), name="rwkv_readout",
    )(y, y, bonus, bonus, g, ln_w.reshape(1, c), ln_b.reshape(1, c), _head_ones(c))


def _pad_rank_rows(w):
    z = jnp.zeros_like(w[0])
    return jnp.stack([jnp.concatenate([w[0], z], 0), jnp.concatenate([z, w[1]], 0)])


def _rwkv_mixer(prw, mu, w0, w2, a0, a2, g2, k_k, k_a, r_k, ln_w, ln_b, cl):
    mud = jnp.stack([mu, mu[::-1]])
    outs = _rwkv_prep(prw, mud, w0, _pad_rank_rows(w2).astype(BF16), a0, _pad_rank_rows(a2).astype(BF16), g2,
                      k_k, k_a, r_k, cl)
    *scan_in, g, bonus = outs
    y = _rwkv_scan(*scan_in)
    return _rwkv_readout(y, bonus, g, ln_w, ln_b, cl)


def _merge_kernel(ya_ref, yh_ref, yr_ref, yl_ref, gt_ref, x_ref, modx_ref, modc_ref,
                  wa_ref, wh_ref, wr_ref, wl_ref, wo_ref, o_ref, *, cl, tm, d):
    is_ctx = _row_ids(tm, pl.program_id(1)) < cl
    m = None
    for i, (y_ref, w_ref) in enumerate(((ya_ref, wa_ref), (yh_ref, wh_ref), (yr_ref, wr_ref), (yl_ref, wl_ref))):
        gate = _sigmoid(gt_ref[0, :, i * d:(i + 1) * d].astype(F32))
        term = gate * _dot(y_ref[0], w_ref[...])
        m = term if m is None else m + term
    g1 = _mod_rows(modx_ref, modc_ref, 2, is_ctx)
    o_ref[0] = x_ref[0] + g1 * _dot(m.astype(BF16), wo_ref[...])


def _merge(ys, gates, xc, mod, w_brs, w_out, cl):
    b, lt, d = xc.shape
    tm = _largest_tile(lt, 544, 16)
    row = lambda w: pl.BlockSpec((1, tm, w), lambda i, r: (i, r, 0))
    const = lambda a: pl.BlockSpec(a.shape, lambda i, r: (0, 0))
    ws = [w.astype(BF16) for w in w_brs] + [w_out.astype(BF16)]
    return pl.pallas_call(
        functools.partial(_merge_kernel, cl=cl, tm=tm, d=d),
        out_shape=jax.ShapeDtypeStruct((b, lt, d), F32), grid=(b, lt // tm),
        in_specs=[row(y.shape[-1]) for y in ys] + [row(N_BRANCH * d), row(d),
                  pl.BlockSpec((1, 6, d), lambda i, r: (i, 0, 0)), pl.BlockSpec((1, 6, d), lambda i, r: (b, 0, 0))]
                 + [const(w) for w in ws],
        out_specs=row(d), compiler_params=_cparams("parallel", "parallel"), name="merge",
    )(*ys, gates, xc, mod, mod, *ws)


def _route_kernel(x_ref, modx_ref, modc_ref, wr_ref, br_ref, h_ref, cmb_ref, *, cl, tm):
    is_ctx = _row_ids(tm, pl.program_id(1)) < cl
    h = _rms_modulate(x_ref[0], _mod_rows(modx_ref, modc_ref, 3, is_ctx), _mod_rows(modx_ref, modc_ref, 4, is_ctx))
    h_ref[0] = h.astype(h_ref.dtype)
    lg = jnp.dot(h, wr_ref[...], preferred_element_type=F32, precision=lax.Precision.HIGHEST) + br_ref[...]
    lane = lax.broadcasted_iota(jnp.int32, lg.shape, 1)
    lane_f = lane.astype(F32)
    neg = -jnp.inf
    big = 1e9

    def first_lane(cond):
        return jnp.min(jnp.where(cond, lane_f, big), axis=-1, keepdims=True)

    is_grp = (lane >= N_EXPERTS) & (lane < N_EXPERTS + N_GROUPS)
    gl = jnp.where(is_grp, lg, neg)
    gmax = jnp.max(gl, axis=-1, keepdims=True)
    ge = jnp.where(is_grp, jnp.exp(gl - gmax), 0.0)
    gp = ge / jnp.sum(ge, axis=-1, keepdims=True)
    g_val = jnp.max(gp, axis=-1, keepdims=True)
    g_idx = first_lane(is_grp & (gp == g_val)) - N_EXPERTS
    lo = g_idx * EXPERTS_PER_GROUP
    in_grp = (lane_f >= lo) & (lane_f < lo + EXPERTS_PER_GROUP)
    el = jnp.where(in_grp, lg, neg)
    emax = jnp.max(el, axis=-1, keepdims=True)
    ee = jnp.where(in_grp, jnp.exp(el - emax), 0.0)
    pe = ee / jnp.sum(ee, axis=-1, keepdims=True)
    v1 = jnp.max(jnp.where(in_grp, pe, -1.0), axis=-1, keepdims=True)
    i1 = first_lane(in_grp & (pe == v1))
    rest = in_grp & (lane_f != i1)
    v2 = jnp.max(jnp.where(rest, pe, -1.0), axis=-1, keepdims=True)
    i2 = first_lane(rest & (pe == v2))
    den = v1 + v2
    cmb_ref[0] = (jnp.where(lane_f == i1, g_val * v1 / den, 0.0) + jnp.where(lane_f == i2, g_val * v2 / den, 0.0)
                  + jnp.where(lane == N_EXPERTS, g_idx, 0.0))


def _route(xc, mod, w_grp, b_grp, w_rt, b_rt, cl):
    b, lt, d = xc.shape
    tm = _largest_tile(lt, 544, 16)
    wr = jnp.zeros((d, LANES), F32).at[:, :N_EXPERTS].set(w_rt).at[:, N_EXPERTS:N_EXPERTS + N_GROUPS].set(w_grp)
    br = jnp.zeros((1, LANES), F32).at[0, :N_EXPERTS].set(b_rt).at[0, N_EXPERTS:N_EXPERTS + N_GROUPS].set(b_grp)
    return pl.pallas_call(
        functools.partial(_route_kernel, cl=cl, tm=tm),
        out_shape=(jax.ShapeDtypeStruct((b, lt, d), BF16), jax.ShapeDtypeStruct((b, lt, LANES), F32)),
        grid=(b, lt // tm),
        in_specs=[pl.BlockSpec((1, tm, d), lambda i, r: (i, r, 0)),
                  pl.BlockSpec((1, 6, d), lambda i, r: (i, 0, 0)), pl.BlockSpec((1, 6, d), lambda i, r: (b, 0, 0)),
                  pl.BlockSpec((d, LANES), lambda i, r: (0, 0)), pl.BlockSpec((1, LANES), lambda i, r: (0, 0))],
        out_specs=(pl.BlockSpec((1, tm, d), lambda i, r: (i, r, 0)), pl.BlockSpec((1, tm, LANES), lambda i, r: (i, r, 0))),
        compiler_params=_cparams("parallel", "parallel"), name="moe_route",
    )(xc, mod, mod, wr, br)


def _moe_kernel(grp_ref, nvalid_ref, h_ref, cmb_ref, w1_ref, w3_ref, w2_ref, o_ref, acc_ref, *, bm):
    j = pl.program_id(0)
    e = pl.program_id(1)

    @pl.when(e == 0)
    def _():
        acc_ref[...] = jnp.zeros_like(acc_ref)

    @pl.when(j < nvalid_ref[0])
    def _():
        h = h_ref[...]
        t = _silu(_dot(h, w1_ref[0])) * _dot(h, w3_ref[0])
        y = _dot(t.astype(BF16), w2_ref[0])
        lane = lax.broadcasted_iota(jnp.int32, (bm, LANES), 1)
        expert = grp_ref[j] * EXPERTS_PER_GROUP + e
        wcol = jnp.sum(jnp.where(lane == expert, cmb_ref[...], 0.0), axis=-1, keepdims=True)
        acc_ref[...] += wcol * y

    @pl.when(e == pl.num_programs(1) - 1)
    def _():
        o_ref[...] = acc_ref[...].astype(o_ref.dtype)


def _moe_grouped(hs, ws, blk_group, nvalid, w1, w3, w2):
    s_rows, d = hs.shape
    hid = w1.shape[2]
    bm = MOE_BLOCK_ROWS
    wspec = lambda shape: pl.BlockSpec(shape, lambda j, e, grp, nv: (grp[j] * EXPERTS_PER_GROUP + e, 0, 0))
    return pl.pallas_call(
        functools.partial(_moe_kernel, bm=bm),
        out_shape=jax.ShapeDtypeStruct((s_rows, d), BF16),
        grid_spec=pltpu.PrefetchScalarGridSpec(
            num_scalar_prefetch=2, grid=(s_rows // bm, EXPERTS_PER_GROUP),
            in_specs=[pl.BlockSpec((bm, d), lambda j, e, grp, nv: (j, 0)),
                      pl.BlockSpec((bm, LANES), lambda j, e, grp, nv: (j, 0)),
                      wspec((1, d, hid)), wspec((1, d, hid)), wspec((1, hid, d))],
            out_specs=pl.BlockSpec((bm, d), lambda j, e, grp, nv: (j, 0)),
            scratch_shapes=[pltpu.VMEM((bm, d), F32)]),
        compiler_params=_cparams("parallel", "arbitrary"), name="moe_experts",
    )(blk_group, nvalid, hs, ws, w1.astype(BF16), w3.astype(BF16), w2.astype(BF16))


def _moe_finish_kernel(x_ref, y_ref, modx_ref, modc_ref, o_ref, *, cl, tm):
    is_ctx = _row_ids(tm, pl.program_id(1)) < cl
    o_ref[0] = x_ref[0] + _mod_rows(modx_ref, modc_ref, 5, is_ctx) * y_ref[0].astype(F32)


def _moe_finish(xc, y, mod, cl):
    b, lt, d = xc.shape
    tm = _largest_tile(lt, 1088, 16)
    tile = pl.BlockSpec((1, tm, d), lambda i, r: (i, r, 0))
    return pl.pallas_call(
        functools.partial(_moe_finish_kernel, cl=cl, tm=tm),
        out_shape=jax.ShapeDtypeStruct((b, lt, d), F32), grid=(b, lt // tm),
        in_specs=[tile, tile, pl.BlockSpec((1, 6, d), lambda i, r: (i, 0, 0)),
                  pl.BlockSpec((1, 6, d), lambda i, r: (b, 0, 0))],
        out_specs=tile, compiler_params=_cparams("parallel", "parallel"), name="moe_finish",
    )(xc, y, mod, mod)


def _moe(h2, cmb, xc, mod, w1, w3, w2, cl):
    b, lt, d = xc.shape
    t = b * lt
    bm = MOE_BLOCK_ROWS
    s_rows = -(-t // bm) * bm + N_GROUPS * bm
    cmb2 = cmb.reshape(t, LANES)
    gid = cmb2[:, N_EXPERTS].astype(jnp.int32)
    onehot = (gid[:, None] == jnp.arange(N_GROUPS, dtype=jnp.int32)).astype(jnp.int32)
    csum = jnp.cumsum(onehot, axis=0)
    rank = jnp.sum(onehot * csum, axis=1) - 1
    padded = -(-csum[-1] // bm) * bm
    ends = jnp.cumsum(padded)
    pos = (ends - padded)[gid] + rank
    slot_token = jnp.zeros((s_rows,), jnp.int32).at[pos].set(jnp.arange(t, dtype=jnp.int32))
    starts = jnp.arange(s_rows // bm, dtype=jnp.int32) * bm
    blk_group = jnp.minimum(jnp.sum((starts[:, None] >= ends[None, :]).astype(jnp.int32), axis=1), N_GROUPS - 1)
    nvalid = (ends[-1] // bm).reshape(1).astype(jnp.int32)
    hs = jnp.take(h2.reshape(t, d), slot_token, axis=0)
    ws = jnp.take(cmb2, slot_token, axis=0)
    ys = _moe_grouped(hs, ws, blk_group.astype(jnp.int32), nvalid, w1, w3, w2)
    y = jnp.take(ys, pos, axis=0).reshape(b, lt, d)
    return _moe_finish(xc, y, mod, cl)


def kernel(x, c, ctx, c_ctx, ada_w, ada_b, w_in, q_norm, k_norm, hy_conv_w, hy_conv_b, hy_f1, hy_fb1, hy_f2, hy_fb2, hy_f3, hy_skip, rw_mu, rw_w0, rw_w2, rw_a0, rw_a2, rw_g2, rw_k_k, rw_k_a, rw_r_k, rw_ln_w, rw_ln_b, lru_conv_w, lru_conv_b, lru_wa, lru_ba, lru_wx, lru_bx, lru_lambda, w_br_attn, w_br_hyena, w_br_rwkv, w_br_lru, w_out, moe_w_grp, moe_b_grp, moe_w_rt, moe_b_rt, moe_w1, moe_w3, moe_w2):
    b, l, d = x.shape
    cl = ctx.shape[1]
    depth = ada_w.shape[0]
    assert b < MOD_ROWS and cl % RWKV_CHUNK == 0 and l % RWKV_CHUNK == 0

    xc = jnp.concatenate([ctx, x], axis=1)
    cc = jnp.zeros((MOD_ROWS, d), F32).at[:b].set(c).at[b].set(c_ctx)
    mod_all = _ada_mod(cc, ada_w, ada_b).reshape(depth, MOD_ROWS, 6, d)

    cos2, sin2 = _rope_tables(l, cl)
    mats_x = _dft_mats(l)
    mats_c = _dft_mats(cl)
    qkv_w = ATTN_WIDTH + 2 * ATTN_KV_WIDTH
    col = np.cumsum([0, qkv_w, 3 * HYENA_WIDTH, RWKV_PROJ, 2 * LRU_WIDTH, N_BRANCH * d])

    for i in range(depth):
        need_ctx = i < depth - 1
        mod = mod_all[i]
        w_i = w_in[i].astype(BF16)
        h1 = _modnorm(xc, mod, cl)
        pqkv, phy, prw, plr, gates = (_proj(h1, w_i[:, col[j]:col[j + 1]]) for j in range(5))

        qn, kt, vx = _attn_prep(pqkv, cos2, sin2, q_norm[i], k_norm[i])
        y_att_x = _attention(qn, kt, vx, cl, 'x', cl + l)
        if need_ctx:
            y_att_c = _attention(qn, kt, vx, cl, 'ctx', cl)
        else:
            y_att_c = jnp.zeros((b, cl, ATTN_WIDTH), BF16)
        y_att = jnp.concatenate([y_att_c, y_att_x], axis=1)

        hv, hx1, hx2 = _hyena_pre(phy, hy_conv_w[i], hy_conv_b[i], cl)
        filt = (hy_f1[i], hy_fb1[i], hy_f2[i], hy_fb2[i], hy_f3[i])
        y_hx = _hyena_run(hv[:, cl:], hx1[:, cl:], hx2[:, cl:], _hyena_spectra(l, filt, mats_x), hy_skip[i], mats_x)
        if need_ctx:
            y_hc = _hyena_run(hv[:, :cl], hx1[:, :cl], hx2[:, :cl], _hyena_spectra(cl, filt, mats_c), hy_skip[i],
                              mats_c)
        else:
            y_hc = jnp.zeros((b, cl, HYENA_WIDTH), BF16)
        y_hy = jnp.concatenate([y_hc, y_hx], axis=1)

        y_rw = _rwkv_mixer(prw, rw_mu[i], rw_w0[i], rw_w2[i], rw_a0[i], rw_a2[i], rw_g2[i], rw_k_k[i], rw_k_a[i],
                           rw_r_k[i].reshape(-1), rw_ln_w[i], rw_ln_b[i], cl)

        la, lb = _lru_pre(plr, lru_conv_w[i], lru_conv_b[i], lru_wa[i], lru_ba[i], lru_wx[i], lru_bx[i],
                          lru_lambda[i], cl)
        y_lr = _lru_scan(la, lb, plr, cl)

        xc = _merge((y_att, y_hy, y_rw, y_lr), gates, xc, mod,
                    (w_br_attn[i], w_br_hyena[i], w_br_rwkv[i], w_br_lru[i]), w_out[i], cl)

        h2, cmb = _route(xc, mod, moe_w_grp[i], moe_b_grp[i], moe_w_rt[i], moe_b_rt[i], cl)
        xc = _moe(h2, cmb, xc, mod, moe_w1[i], moe_w3[i], moe_w2[i], cl)
    return xc[:, cl:]
```

```python
import functools
import math

import numpy as np
import jax
import jax.numpy as jnp
from jax import lax
from jax.experimental import pallas as pl
from jax.experimental.pallas import tpu as pltpu

F32 = jnp.float32
BF16 = jnp.bfloat16

HEAD_DIM = 64
GRID_W = 64
EPS = 1e-6
ATTN_HEADS = 8
ATTN_KV_HEADS = 2
ATTN_GROUP = ATTN_HEADS // ATTN_KV_HEADS
ATTN_Q_BLOCKS = 2
ATTN_WIDTH = ATTN_HEADS * HEAD_DIM
ATTN_KV_WIDTH = ATTN_KV_HEADS * HEAD_DIM
ROPE_THETA = 10000.0
HYENA_WIDTH = 256
HYENA_ORDER = 2
HYENA_BANDS = 16
HYENA_DECAY_TARGET = 1e-2
HYENA_FAST_DECAY = 0.3
HYENA_SLOW_DECAY = 1.5
RWKV_HEADS = 4
RWKV_WIDTH = RWKV_HEADS * HEAD_DIM
RWKV_DECAY_RANK = 64
RWKV_ICLR_RANK = 64
RWKV_GATE_RANK = 128
RWKV_GN_EPS = 64e-5
RWKV_PROJ = 3 * RWKV_WIDTH + 2 * RWKV_DECAY_RANK + 2 * RWKV_ICLR_RANK + RWKV_GATE_RANK
RWKV_CHUNK = 64
RWKV_INV_BASE = 4
RWKV_STAGE_ROWS = 128
LRU_WIDTH = 256
LRU_BLOCKS = 4
LRU_C = 8.0
LRU_GROUPS_PER_STEP = 4
N_BRANCH = 4
N_GROUPS = 4
EXPERTS_PER_GROUP = 4
N_EXPERTS = N_GROUPS * EXPERTS_PER_GROUP
MOE_BLOCK_ROWS = 512

V7X_VMEM_LIMIT_BYTES = 52 * 1024 * 1024
SUBLANES = 8
LANES = 128
MOD_ROWS = 16


def _cparams(*sem):
    return pltpu.CompilerParams(dimension_semantics=sem, vmem_limit_bytes=V7X_VMEM_LIMIT_BYTES)


def _dot(a, b):
    return jnp.dot(a, b, preferred_element_type=F32)


def _dot_nt(a, b):
    return lax.dot_general(a, b, (((1,), (1,)), ((), ())), preferred_element_type=F32)


def _split3(x):
    hi = x.astype(BF16)
    r1 = x - hi.astype(F32)
    mid = r1.astype(BF16)
    lo = (r1 - mid.astype(F32)).astype(BF16)
    return hi, mid, lo


def _dot_exact_lhs(m_bf16, x):
    hi, mid, lo = _split3(x)
    return _dot(m_bf16, hi) + _dot(m_bf16, mid) + _dot(m_bf16, lo)


def _dot_exact_rhs(x, m_bf16):
    hi, mid, lo = _split3(x)
    return _dot(hi, m_bf16) + _dot(mid, m_bf16) + _dot(lo, m_bf16)


def _dot_bf(a, b):
    return _dot(a.astype(BF16), b.astype(BF16))


def _sigmoid(x):
    return 1.0 / (1.0 + jnp.exp(-x))


def _softplus(x):
    return jnp.maximum(x, 0.0) + jnp.log(1.0 + jnp.exp(-jnp.abs(x)))


def _silu(x):
    return x * _sigmoid(x)


def _largest_tile(n, cap, mult):
    best = None
    for t in range(mult, min(n, cap) + 1, mult):
        if n % t == 0:
            best = t
    assert best is not None, (n, cap, mult)
    return best


def _head_ones(width):
    idx = np.arange(width) // HEAD_DIM
    return jnp.asarray((idx[:, None] == idx[None, :]).astype(np.float32), dtype=BF16)


def _row_ids(tile_rows, tile_idx):
    return tile_idx * tile_rows + lax.broadcasted_iota(jnp.int32, (tile_rows, 1), 0)


def _mod_rows(modx_ref, modc_ref, idx, is_ctx):
    return jnp.where(is_ctx, modc_ref[0, idx:idx + 1, :], modx_ref[0, idx:idx + 1, :])


def _rms_modulate(x, shift, scale):
    ms = jnp.mean(x * x, axis=-1, keepdims=True)
    return (x * lax.rsqrt(ms + EPS)) * (1.0 + scale) + shift


def _ada_kernel(c_ref, w_ref, b_ref, o_ref):
    s = _silu(c_ref[...])
    o_ref[0] = jnp.dot(s, w_ref[0], preferred_element_type=F32, precision=lax.Precision.HIGHEST) + b_ref[0]


def _ada_mod(cc, ada_w, ada_b):
    depth, d, n6 = ada_w.shape
    tn = _largest_tile(n6, 1024, LANES)
    return pl.pallas_call(
        _ada_kernel,
        out_shape=jax.ShapeDtypeStruct((depth, MOD_ROWS, n6), F32),
        grid=(depth, n6 // tn),
        in_specs=[pl.BlockSpec((MOD_ROWS, d), lambda i, j: (0, 0)),
                  pl.BlockSpec((1, d, tn), lambda i, j: (i, 0, j)),
                  pl.BlockSpec((1, 1, tn), lambda i, j: (i, 0, j))],
        out_specs=pl.BlockSpec((1, MOD_ROWS, tn), lambda i, j: (i, 0, j)),
        compiler_params=_cparams("parallel", "parallel"),
        name="ada_mod",
    )(cc, ada_w, ada_b.reshape(depth, 1, n6))


def _modnorm_kernel(x_ref, modx_ref, modc_ref, o_ref, *, cl, tm):
    is_ctx = _row_ids(tm, pl.program_id(1)) < cl
    h = _rms_modulate(x_ref[0], _mod_rows(modx_ref, modc_ref, 0, is_ctx), _mod_rows(modx_ref, modc_ref, 1, is_ctx))
    o_ref[0] = h.astype(o_ref.dtype)


def _modnorm(xc, mod, cl):
    b, lt, d = xc.shape
    tm = _largest_tile(lt, 1088, 16)
    return pl.pallas_call(
        functools.partial(_modnorm_kernel, cl=cl, tm=tm),
        out_shape=jax.ShapeDtypeStruct((b, lt, d), BF16), grid=(b, lt // tm),
        in_specs=[pl.BlockSpec((1, tm, d), lambda i, r: (i, r, 0)),
                  pl.BlockSpec((1, 6, d), lambda i, r: (i, 0, 0)),
                  pl.BlockSpec((1, 6, d), lambda i, r: (b, 0, 0))],
        out_specs=pl.BlockSpec((1, tm, d), lambda i, r: (i, r, 0)),
        compiler_params=_cparams("parallel", "parallel"), name="modnorm",
    )(xc, mod, mod)


def _proj_kernel(h_ref, w_ref, o_ref):
    o_ref[0] = _dot(h_ref[0], w_ref[...]).astype(o_ref.dtype)


def _proj(h, w):
    b, lt, d = h.shape
    n = w.shape[1]
    tm = _largest_tile(lt, 2176, 16)
    tn = n if n <= 1280 else _largest_tile(n, 1024, 2 * LANES)
    return pl.pallas_call(
        _proj_kernel, out_shape=jax.ShapeDtypeStruct((b, lt, n), BF16), grid=(b, lt // tm, n // tn),
        in_specs=[pl.BlockSpec((1, tm, d), lambda i, r, j: (i, r, 0)), pl.BlockSpec((d, tn), lambda i, r, j: (0, j))],
        out_specs=pl.BlockSpec((1, tm, tn), lambda i, r, j: (i, r, j)),
        compiler_params=_cparams("parallel", "parallel", "parallel"), name="proj",
    )(h, w)


def _rope_tables(l, cl):
    n_freq = HEAD_DIM // 4
    t = jnp.arange(l)
    freqs = ROPE_THETA ** (-jnp.arange(n_freq, dtype=F32) / n_freq)
    pos = jnp.stack([t // GRID_W, t % GRID_W], -1).astype(F32)
    ang = pos[..., None] * freqs
    cos64 = jnp.stack([jnp.cos(ang), jnp.cos(ang)], axis=2).reshape(l, HEAD_DIM)
    sin64 = jnp.stack([-jnp.sin(ang), jnp.sin(ang)], axis=2).reshape(l, HEAD_DIM)
    cos64 = jnp.concatenate([jnp.ones((cl, HEAD_DIM), F32), cos64], 0)
    sin64 = jnp.concatenate([jnp.zeros((cl, HEAD_DIM), F32), sin64], 0)
    return jnp.tile(cos64, (1, 2)), jnp.tile(sin64, (1, 2))


def _head_rms(t, ones_ref):
    ms = _dot_exact_rhs(t * t, ones_ref[...]) * (1.0 / HEAD_DIM)
    return t * lax.rsqrt(ms + EPS)


def _rope(t, cos, sin):
    w = t.shape[-1]
    lane = lax.broadcasted_iota(jnp.int32, t.shape, 1)
    q4 = HEAD_DIM // 4
    first_half = (lane % (2 * q4)) < q4
    partner = jnp.where(first_half, pltpu.roll(t, w - q4, 1), pltpu.roll(t, q4, 1))
    return t * cos + partner * sin


def _attn_prep_kernel(p_ref, cos_ref, sin_ref, qg_ref, kg_ref, oq_ref, ok_ref, q_ref, kt_ref, vx_ref):
    p = p_ref[0].astype(F32)
    v = p[:, ATTN_WIDTH + ATTN_KV_WIDTH:]
    low = lax.broadcasted_iota(jnp.int32, v.shape, 1) < HEAD_DIM
    vx_ref[0, 0] = jnp.where(low, v, 1.0).astype(vx_ref.dtype)
    vx_ref[0, 1] = jnp.where(low, pltpu.roll(v, HEAD_DIM, 1), 1.0).astype(vx_ref.dtype)
    cos2, sin2 = cos_ref[...], sin_ref[...]
    reps = ATTN_WIDTH // (2 * HEAD_DIM)
    cos_q = jnp.concatenate([cos2] * reps, axis=1)
    sin_q = jnp.concatenate([sin2] * reps, axis=1)
    q = _head_rms(p[:, :ATTN_WIDTH], oq_ref) * qg_ref[...]
    q_ref[0] = _rope(q, cos_q, sin_q).astype(q_ref.dtype)
    k = _head_rms(p[:, ATTN_WIDTH:ATTN_WIDTH + ATTN_KV_WIDTH], ok_ref) * kg_ref[...]
    kt_ref[0] = _rope(k, cos2, sin2).T.astype(kt_ref.dtype)


def _attn_prep(pqkv, cos2, sin2, q_gain, k_gain):
    b, lt, wtot = pqkv.shape
    tr = _largest_tile(lt, 2176, LANES)
    qg = jnp.tile(q_gain * (HEAD_DIM ** -0.5 * math.log2(math.e)), ATTN_HEADS).reshape(1, ATTN_WIDTH)
    kg = jnp.tile(k_gain, ATTN_KV_HEADS).reshape(1, ATTN_KV_WIDTH)
    return pl.pallas_call(
        _attn_prep_kernel,
        out_shape=(jax.ShapeDtypeStruct((b, lt, ATTN_WIDTH), BF16),
                   jax.ShapeDtypeStruct((b, ATTN_KV_WIDTH, lt), BF16),
                   jax.ShapeDtypeStruct((b, ATTN_KV_HEADS, lt, 2 * HEAD_DIM), BF16)),
        grid=(b, lt // tr),
        in_specs=[pl.BlockSpec((1, tr, wtot), lambda i, r: (i, r, 0)),
                  pl.BlockSpec((tr, 2 * HEAD_DIM), lambda i, r: (r, 0)),
                  pl.BlockSpec((tr, 2 * HEAD_DIM), lambda i, r: (r, 0)),
                  pl.BlockSpec((1, ATTN_WIDTH), lambda i, r: (0, 0)),
                  pl.BlockSpec((1, ATTN_KV_WIDTH), lambda i, r: (0, 0)),
                  pl.BlockSpec((ATTN_WIDTH, ATTN_WIDTH), lambda i, r: (0, 0)),
                  pl.BlockSpec((ATTN_KV_WIDTH, ATTN_KV_WIDTH), lambda i, r: (0, 0))],
        out_specs=(pl.BlockSpec((1, tr, ATTN_WIDTH), lambda i, r: (i, r, 0)),
                   pl.BlockSpec((1, ATTN_KV_WIDTH, tr), lambda i, r: (i, 0, r)),
                   pl.BlockSpec((1, ATTN_KV_HEADS, tr, 2 * HEAD_DIM), lambda i, r: (i, 0, r, 0))),
        compiler_params=_cparams("parallel", "parallel"),
        name="attn_prep",
    )(pqkv, cos2, sin2, qg, kg, _head_ones(ATTN_WIDTH), _head_ones(ATTN_KV_WIDTH))


def _attn_kernel(*refs, nq, nk):
    q_refs, (kt_ref, v_ref, o_ref) = refs[:nq], refs[nq:]
    outs = []
    for h in range(ATTN_HEADS):
        kv = h // ATTN_GROUP
        hs = slice(h * HEAD_DIM, (h + 1) * HEAD_DIM)
        qh = jnp.concatenate([q_ref[0, :, hs] for q_ref in q_refs], axis=0)
        s = _dot(qh, kt_ref[0, kv * HEAD_DIM:(kv + 1) * HEAD_DIM, :nk])
        m = jnp.max(s, axis=-1, keepdims=True)
        p = jnp.exp2(s - m)
        o = _dot(p.astype(BF16), v_ref[0, kv, :nk, :])
        outs.append(o[:, :HEAD_DIM] / o[:, HEAD_DIM:])
    o_ref[0] = jnp.concatenate(outs, axis=-1).astype(o_ref.dtype)


def _attention(qn, kt, vx, cl, rows, nk):
    b, lt, _ = qn.shape
    blk = 256 if (cl % 256 == 0 and lt % 256 == 0) else 128
    if rows == 'ctx':
        n_rows, first, nq = cl, 0, 1
    else:
        n_rows, first = lt - cl, cl // blk
        nq = ATTN_Q_BLOCKS if n_rows % (ATTN_Q_BLOCKS * blk) == 0 else 1
    tq = nq * blk
    q_specs = [pl.BlockSpec((1, blk, ATTN_WIDTH), lambda i, t, j=j: (i, first + nq * t + j, 0)) for j in range(nq)]
    return pl.pallas_call(
        functools.partial(_attn_kernel, nq=nq, nk=nk),
        out_shape=jax.ShapeDtypeStruct((b, n_rows, ATTN_WIDTH), BF16),
        grid=(b, n_rows // tq),
        in_specs=q_specs + [pl.BlockSpec((1, ATTN_KV_WIDTH, lt), lambda i, t: (i, 0, 0)),
                            pl.BlockSpec((1, ATTN_KV_HEADS, lt, 2 * HEAD_DIM), lambda i, t: (i, 0, 0, 0))],
        out_specs=pl.BlockSpec((1, tq, ATTN_WIDTH), lambda i, t: (i, t, 0)),
        compiler_params=_cparams("parallel", "parallel"),
        name="attention_" + rows,
    )(*([qn] * nq), kt, vx)


def _halo_specs(tr, lt, width, lead):
    per = tr // SUBLANES
    last = lt // SUBLANES - 1
    nlead = len(lead)

    def prev_map(*ids):
        return (*ids[:nlead], jnp.maximum(ids[nlead] * per - 1, 0), 0)

    def next_map(*ids):
        return (*ids[:nlead], jnp.minimum((ids[nlead] + 1) * per, last), 0)

    blk = (*lead, SUBLANES, width)
    return pl.BlockSpec(blk, prev_map), pl.BlockSpec(blk, next_map)


def _shift_rows(x, halo, offset, rows, cl, lt):
    tr = x.shape[0]
    local = lax.broadcasted_iota(jnp.int32, (tr, 1), 0)
    if offset < 0:
        y = pltpu.roll(x, -offset, 0)
        y = jnp.where(local == 0, halo, y)
        bad = (rows == 0) | (rows == cl)
    else:
        y = pltpu.roll(x, tr - offset, 0)
        for j in range(offset):
            y = jnp.where(local == tr - offset + j, halo[j:j + 1, :], y)
        bad = (rows >= lt - offset) | ((rows >= cl - offset) & (rows < cl))
    return jnp.where(bad, 0.0, y)


def _hyena_pre_kernel(p_ref, pv_ref, nx_ref, w_ref, b_ref, v_ref, x1_ref, x2_ref, *, cl, lt, tr):
    rows = _row_ids(tr, pl.program_id(1))
    p = p_ref[0].astype(F32)
    pm = _shift_rows(p, pv_ref[0, SUBLANES - 1:, :].astype(F32), -1, rows, cl, lt)
    pp = _shift_rows(p, nx_ref[0, :1, :].astype(F32), 1, rows, cl, lt)
    z = pm * w_ref[0:1, :] + p * w_ref[1:2, :] + pp * w_ref[2:3, :] + b_ref[...]
    c = HYENA_WIDTH
    v_ref[0] = z[:, :c].astype(v_ref.dtype)
    x1_ref[0] = z[:, c:2 * c].astype(x1_ref.dtype)
    x2_ref[0] = z[:, 2 * c:].astype(x2_ref.dtype)


def _hyena_pre(phy, conv_w, conv_b, cl):
    b, lt, w = phy.shape
    tr = _largest_tile(lt, 1088, 16)
    prev_spec, next_spec = _halo_specs(tr, lt, w, (1,))
    out = jax.ShapeDtypeStruct((b, lt, HYENA_WIDTH), BF16)
    ospec = pl.BlockSpec((1, tr, HYENA_WIDTH), lambda i, r: (i, r, 0))
    return pl.pallas_call(
        functools.partial(_hyena_pre_kernel, cl=cl, lt=lt, tr=tr),
        out_shape=(out, out, out),
        grid=(b, lt // tr),
        in_specs=[pl.BlockSpec((1, tr, w), lambda i, r: (i, r, 0)), prev_spec, next_spec,
                  pl.BlockSpec(conv_w.shape, lambda i, r: (0, 0)),
                  pl.BlockSpec((1, w), lambda i, r: (0, 0))],
        out_specs=(ospec, ospec, ospec),
        compiler_params=_cparams("parallel", "parallel"),
        name="hyena_pre",
    )(phy, phy, phy, conv_w, conv_b.reshape(1, w))


def _dft_mats(n):
    tm = _largest_tile(n, 256, 16)
    tn = _largest_tile(n, 1024, LANES)
    s = jnp.arange(n, dtype=jnp.int32)[None, :]

    def tables(k):
        ang = ((k[:, None] * s) % (2 * n)).astype(F32) * (math.pi / n)
        return jnp.cos(ang), jnp.sin(ang)

    cb, sb = tables(jnp.arange(0, n, tm, dtype=jnp.int32))
    co, so = tables(jnp.arange(tm, dtype=jnp.int32))
    base = pl.BlockSpec((1, 1, tn), lambda j, c: (j, 0, c))
    off = pl.BlockSpec((tm, tn), lambda j, c: (0, c))
    out = jax.ShapeDtypeStruct((n, n), BF16)
    ospec = pl.BlockSpec((tm, tn), lambda j, c: (j, c))
    return pl.pallas_call(
        functools.partial(_dft_mats_kernel, tm=tm, tn=tn), out_shape=(out, out, out), grid=(n // tm, n // tn),
        in_specs=[base, base, off, off], out_specs=(ospec, ospec, ospec),
        compiler_params=_cparams("parallel", "parallel"), name="dft_mats",
    )(cb.reshape(n // tm, 1, n), sb.reshape(n // tm, 1, n), co, so)


def _dft_mats_kernel(cb_ref, sb_ref, co_ref, so_ref, cm_ref, sm_ref, smt_ref, *, tm, tn):
    cb, sb, co, so = cb_ref[0], sb_ref[0], co_ref[...], so_ref[...]
    k = pl.program_id(0) * tm + lax.broadcasted_iota(jnp.int32, (tm, tn), 0)
    s = pl.program_id(1) * tn + lax.broadcasted_iota(jnp.int32, (tm, tn), 1)
    sn = sb * co + cb * so
    cm_ref[...] = (cb * co - sb * so).astype(cm_ref.dtype)
    sm_ref[...] = jnp.where(k == 0, (1 - 2 * (s % 2)).astype(F32), sn).astype(sm_ref.dtype)
    smt_ref[...] = jnp.where(s == 0, (1 - 2 * (k % 2)).astype(F32), sn).astype(smt_ref.dtype)


def _dft_raw_kernel(cm_ref, sm_ref, z_ref, zr_ref, zi_ref):
    zr_ref[...] = _dot(cm_ref[...], z_ref[...])
    zi_ref[...] = _dot(sm_ref[...], z_ref[...])


def _dft_raw(cm, sm, z):
    n, c = z.shape
    tm = _largest_tile(n, 256, 16)
    mat = pl.BlockSpec((tm, n), lambda j: (j, 0))
    out = jax.ShapeDtypeStruct((n, c), F32)
    ospec = pl.BlockSpec((tm, c), lambda j: (j, 0))
    return pl.pallas_call(
        _dft_raw_kernel, out_shape=(out, out), grid=(n // tm,),
        in_specs=[mat, mat, pl.BlockSpec((n, c), lambda j: (0, 0))],
        out_specs=(ospec, ospec),
        compiler_params=_cparams("parallel"), name="dft_raw",
    )(cm, sm, z)


def _dft_fwd_kernel(cm_ref, sm_ref, z_ref, hr_ref, hi_ref, yr_ref, yi_ref, *, bb, tm):
    is_row0 = _row_ids(tm, pl.program_id(1)) == 0
    hr, hi = hr_ref[...], hi_ref[...]
    for i in range(bb):
        z = z_ref[i]
        zr = _dot(cm_ref[...], z)
        zi = _dot(sm_ref[...], z)
        yr = jnp.where(is_row0, 0.5 * zr * hr, zr * hr + zi * hi)
        yi = jnp.where(is_row0, 0.5 * zi * hi, zi * hr - zr * hi)
        yr_ref[i] = yr.astype(yr_ref.dtype)
        yi_ref[i] = yi.astype(yi_ref.dtype)


def _dft_inv_kernel(cm_ref, smt_ref, yr_ref, yi_ref, z_ref, g_ref, skip_ref, o_ref, *, bb):
    for i in range(bb):
        y = _dot(cm_ref[...], yr_ref[i]) + _dot(smt_ref[...], yi_ref[i])
        y = y + z_ref[i].astype(F32) * skip_ref[...]
        o_ref[i] = (g_ref[i].astype(F32) * y).astype(o_ref.dtype)


def _longconv_gated(z, gate, hr, hi, skip, mats):
    cm, sm, smt = mats
    b, n, c = z.shape
    bb = 2 if b % 2 == 0 else 1
    tm = _largest_tile(n, 256, 16)
    mat = pl.BlockSpec((tm, n), lambda i, j: (j, 0))
    full = pl.BlockSpec((bb, n, c), lambda i, j: (i, 0, 0))
    tile = pl.BlockSpec((bb, tm, c), lambda i, j: (i, j, 0))
    filt = pl.BlockSpec((tm, c), lambda i, j: (j, 0))
    spec_shape = jax.ShapeDtypeStruct((b, n, c), BF16)
    yr, yi = pl.pallas_call(
        functools.partial(_dft_fwd_kernel, bb=bb, tm=tm),
        out_shape=(spec_shape, spec_shape), grid=(b // bb, n // tm),
        in_specs=[mat, mat, full, filt, filt], out_specs=(tile, tile),
        compiler_params=_cparams("parallel", "parallel"), name="dft_fwd",
    )(cm, sm, z, hr, hi)
    return pl.pallas_call(
        functools.partial(_dft_inv_kernel, bb=bb),
        out_shape=jax.ShapeDtypeStruct((b, n, c), BF16), grid=(b // bb, n // tm),
        in_specs=[mat, mat, full, full, tile, tile, pl.BlockSpec((1, c), lambda i, j: (0, 0))],
        out_specs=tile,
        compiler_params=_cparams("parallel", "parallel"), name="dft_inv",
    )(cm, smt, yr, yi, z, gate, skip.reshape(1, c))


def _hyena_filters(n, f1, fb1, f2, fb2, f3):
    t = jnp.arange(n, dtype=F32) / n
    bands = jnp.arange(1, HYENA_BANDS + 1, dtype=F32)
    ang = 2.0 * math.pi * t[:, None] * bands
    feat = jnp.concatenate([t[:, None], jnp.sin(ang), jnp.cos(ang)], axis=-1)
    hp = lax.Precision.HIGHEST
    h = jnp.sin(jnp.dot(feat, f1, precision=hp) + fb1)
    h = jnp.sin(jnp.dot(h, f2, precision=hp) + fb2)
    h = jnp.dot(h, f3, precision=hp).reshape(n, HYENA_ORDER, 2, HYENA_WIDTH)
    deltas = jnp.linspace(-math.log(HYENA_DECAY_TARGET) / HYENA_SLOW_DECAY,
                          -math.log(HYENA_DECAY_TARGET) / HYENA_FAST_DECAY, HYENA_WIDTH, dtype=F32)
    h = h * jnp.exp(-t[:, None] * deltas)[:, None, None, :]
    return h / jnp.sum(jnp.abs(h), axis=(0, 2), keepdims=True)


def _hyena_spectra(n, filt_params, mats):
    cm, sm, _ = mats
    h = _hyena_filters(n, *filt_params)
    hf = h[:, :, 0].reshape(n, HYENA_ORDER * HYENA_WIDTH)
    hb = h[:, :, 1].reshape(n, HYENA_ORDER * HYENA_WIDTH)
    hb = jnp.where(jnp.arange(n)[:, None] == 0, 0.0, hb)
    zr, zi = _dft_raw(cm, sm, jnp.concatenate([hf + hb, hb - hf], axis=1).astype(BF16))
    oc = HYENA_ORDER * HYENA_WIDTH
    hr = zr[:, :oc]
    hi = jnp.where(jnp.arange(n)[:, None] == 0, zi[:, :oc], zi[:, oc:])
    scale = 1.0 / n
    hr = (hr * scale).reshape(n, HYENA_ORDER, HYENA_WIDTH).transpose(1, 0, 2)
    hi = (hi * scale).reshape(n, HYENA_ORDER, HYENA_WIDTH).transpose(1, 0, 2)
    return hr, hi


def _hyena_run(v, x1, x2, spectra, skip, mats):
    hr, hi = spectra
    y = v
    for o, gate in enumerate((x1, x2)):
        y = _longconv_gated(y, gate, hr[o], hi[o], skip[o], mats)
    return y


def _blockdiag_dense(w):
    nb, blk = w.shape[1], w.shape[2]
    eye = jnp.eye(nb, dtype=w.dtype)
    return jnp.einsum('dncf,nm->dncmf', w, eye).reshape(w.shape[0], nb * blk, nb * blk)


def _lru_pre_kernel(p_ref, pv_ref, nx_ref, cw_ref, cb_ref, wa_ref, ba_ref, wx_ref, bx_ref, lam_ref,
                    a_ref, b_ref, *, cl, lt, tr):
    rows = _row_ids(tr, pl.program_id(1))
    c = LRU_WIDTH
    x = p_ref[0][:, c:].astype(F32)
    pv = pv_ref[0][SUBLANES - 1:, c:].astype(F32)
    nx = nx_ref[0][:, c:].astype(F32)
    xc = (_shift_rows(x, pv, -1, rows, cl, lt) * cw_ref[0:1, :] + x * cw_ref[1:2, :]
          + _shift_rows(x, nx[:1], 1, rows, cl, lt) * cw_ref[2:3, :]
          + _shift_rows(x, nx[:2], 2, rows, cl, lt) * cw_ref[3:4, :] + cb_ref[...])
    xcb = xc.astype(BF16)
    for d in range(2):
        r = _sigmoid(_dot(xcb, wa_ref[d]) + ba_ref[d])
        i = _sigmoid(_dot(xcb, wx_ref[d]) + bx_ref[d])
        log_a = -LRU_C * r * _softplus(-lam_ref[d])
        a_ref[d, 0] = jnp.exp(log_a)
        b_ref[d, 0] = jnp.sqrt(1.0 - jnp.exp(2.0 * log_a)) * (i * xc)


def _lru_pre(plr, conv_w, conv_b, wa, ba, wx, bx, lam, cl):
    b, lt, w = plr.shape
    c = LRU_WIDTH
    tr = _largest_tile(lt, 1088, 16)
    prev_spec, next_spec = _halo_specs(tr, lt, w, (1,))
    out = jax.ShapeDtypeStruct((2, b, lt, c), F32)
    ospec = pl.BlockSpec((2, 1, tr, c), lambda i, r: (0, i, r, 0))
    const2 = lambda shape: pl.BlockSpec(shape, lambda i, r: (0,) * len(shape))
    return pl.pallas_call(
        functools.partial(_lru_pre_kernel, cl=cl, lt=lt, tr=tr),
        out_shape=(out, out), grid=(b, lt // tr),
        in_specs=[pl.BlockSpec((1, tr, w), lambda i, r: (i, r, 0)), prev_spec, next_spec,
                  const2(conv_w.shape), const2((1, c)), const2((2, c, c)), const2((2, 1, c)),
                  const2((2, c, c)), const2((2, 1, c)), const2((2, 1, c))],
        out_specs=(ospec, ospec),
        compiler_params=_cparams("parallel", "parallel"), name="lru_pre",
    )(plr, plr, plr, conv_w, conv_b.reshape(1, c), _blockdiag_dense(wa).astype(BF16), ba.reshape(2, 1, c),
      _blockdiag_dense(wx).astype(BF16), bx.reshape(2, 1, c), lam.reshape(2, 1, c))


def _gelu_tanh(x):
    return 0.5 * x * (1.0 + jnp.tanh(math.sqrt(2.0 / math.pi) * (x + 0.044715 * (x * x * x))))


def _lru_scan_kernel(af_ref, bf_ref, ar_ref, br_ref, g_ref, o_ref, acc_ref, *, cl, lt):
    row = lax.broadcasted_iota(jnp.int32, (SUBLANES, LANES), 0)

    def group_scan(a, b, reverse):
        for s in (1, 2, 4):
            if reverse:
                keep = row < SUBLANES - s
                a_s = jnp.where(keep, pltpu.roll(a, SUBLANES - s, 0), 1.0)
                b_s = jnp.where(keep, pltpu.roll(b, SUBLANES - s, 0), 0.0)
            else:
                keep = row >= s
                a_s = jnp.where(keep, pltpu.roll(a, s, 0), 1.0)
                b_s = jnp.where(keep, pltpu.roll(b, s, 0), 0.0)
            b = a * b_s + b
            a = a * a_s
        return a, b

    ng = math.gcd(math.gcd(cl // SUBLANES, (lt - cl) // SUBLANES), LRU_GROUPS_PER_STEP)
    span = ng * SUBLANES

    def fwd_body(i, h):
        sl = pl.ds(pl.multiple_of(i * span, span), span)
        a_all, b_all = af_ref[0, 0, sl, :], bf_ref[0, 0, sl, :]
        scans = [group_scan(a_all[j * SUBLANES:(j + 1) * SUBLANES], b_all[j * SUBLANES:(j + 1) * SUBLANES], False)
                 for j in range(ng)]
        outs = []
        for a, b in scans:
            hh = a * h + b
            outs.append(hh)
            h = hh[SUBLANES - 1:SUBLANES, :]
        acc_ref[sl, :] = jnp.concatenate(outs, axis=0)
        return h

    lax.fori_loop(0, lt // span, fwd_body, jnp.zeros((1, LANES), F32))

    def rev_body(i, h, top):
        sl = pl.ds(pl.multiple_of((top - 1 - i) * span, span), span)
        a_all, b_all = ar_ref[0, 0, sl, :], br_ref[0, 0, sl, :]
        scans = [group_scan(a_all[j * SUBLANES:(j + 1) * SUBLANES], b_all[j * SUBLANES:(j + 1) * SUBLANES], True)
                 for j in range(ng)]
        outs = [None] * ng
        for j in reversed(range(ng)):
            a, b = scans[j]
            hh = a * h + b
            outs[j] = hh
            h = hh[0:1, :]
        gate = g_ref[0, sl, :].astype(F32)
        o_ref[0, sl, :] = ((acc_ref[sl, :] + jnp.concatenate(outs, axis=0)) * _gelu_tanh(gate)).astype(o_ref.dtype)
        return h

    h = lax.fori_loop(0, cl // span, functools.partial(rev_body, top=cl // span), jnp.zeros((1, LANES), F32))
    lax.fori_loop(0, (lt - cl) // span, functools.partial(rev_body, top=lt // span), h)


def _lru_scan(a, b_, plr, cl):
    _, b, lt, c = a.shape
    nl = c // LANES
    fwd = pl.BlockSpec((1, 1, lt, LANES), lambda i, j: (0, i, 0, j))
    rev = pl.BlockSpec((1, 1, lt, LANES), lambda i, j: (1, i, 0, j))
    return pl.pallas_call(
        functools.partial(_lru_scan_kernel, cl=cl, lt=lt),
        out_shape=jax.ShapeDtypeStruct((b, lt, c), BF16), grid=(b, nl),
        in_specs=[fwd, fwd, rev, rev, pl.BlockSpec((1, lt, LANES), lambda i, j: (i, 0, j))],
        out_specs=pl.BlockSpec((1, lt, LANES), lambda i, j: (i, 0, j)),
        scratch_shapes=[pltpu.VMEM((lt, LANES), F32)],
        compiler_params=_cparams("parallel", "parallel"), name="lru_scan",
    )(a, b_, a, b_, plr)


def _row_perm(tr, rev):
    t = lax.broadcasted_iota(jnp.int32, (tr, tr), 0)
    s = lax.broadcasted_iota(jnp.int32, (tr, tr), 1)
    return jnp.where(s == jnp.where(rev, tr - 1 - t, t), 1.0, 0.0).astype(BF16)


def _mirror_tile(r, rev, n_ctx_tiles, n_tiles):
    m = jnp.where(r < n_ctx_tiles, n_ctx_tiles - 1 - r, n_tiles - 1 + n_ctx_tiles - r)
    return jnp.where(rev, m, r)


def _chunk_masks(tr):
    t = lax.broadcasted_iota(jnp.int32, (tr, tr), 0)
    s = lax.broadcasted_iota(jnp.int32, (tr, tr), 1)
    same = (t // RWKV_CHUNK) == (s // RWKV_CHUNK)
    return same, same & (s <= t), same & (s < t)


def _rwkv_prep_kernel(p_ref, pv_ref, nx_ref, mu_ref, w0_ref, w2_ref, a0_ref, a2_ref, g2_ref, kk_ref, ka_ref,
                      rk_ref, ones_ref,
                      aq_ref, vp_ref, y0_ref, rt_ref, mrb_ref, bht_ref, gm_ref, pc_ref, g_ref, bonus_ref,
                      *, cl, lt, tr):
    c = RWKV_WIDTH
    rows = _row_ids(tr, pl.program_id(2))
    rev = pl.program_id(0) == 1
    p = _dot(_row_perm(tr, rev), p_ref[0])
    before = jnp.where(rev, nx_ref[0, :1, :], pv_ref[0, SUBLANES - 1:, :]).astype(F32)
    after = jnp.where(rev, pv_ref[0, SUBLANES - 1:, :], nx_ref[0, :1, :]).astype(F32)
    prev = _shift_rows(p, before, -1, rows, cl, lt)
    nxt = _shift_rows(p, after, 1, rows, cl, lt)
    xm = p + (prev - p) * mu_ref[0, 0:1, :] + (nxt - p) * mu_ref[0, 1:2, :]
    r, k, v = xm[:, :c], xm[:, c:2 * c], xm[:, 2 * c:3 * c]
    o = 3 * c
    w1 = xm[:, o:o + 2 * RWKV_DECAY_RANK]
    a1 = xm[:, o + 2 * RWKV_DECAY_RANK:o + 2 * RWKV_DECAY_RANK + 2 * RWKV_ICLR_RANK]
    g1 = xm[:, o + 2 * RWKV_DECAY_RANK + 2 * RWKV_ICLR_RANK:]
    wlog = -_softplus(-(w0_ref[0] + _dot(jnp.tanh(w1).astype(BF16), w2_ref[0]))) - 0.5
    ld = -jnp.exp(wlog)
    a = _sigmoid(a0_ref[0] + _dot(a1.astype(BF16), a2_ref[0]))
    g_ref[0, 0] = _dot(_sigmoid(g1).astype(BF16), g2_ref[...])
    kk = k * kk_ref[...]
    kk = kk * lax.rsqrt(_dot_exact_rhs(kk * kk, ones_ref[...]) + 1e-12)
    kd = k * (1.0 + (a - 1.0) * ka_ref[...])
    bonus_ref[0, 0] = _dot_exact_rhs(r * kd * rk_ref[...], ones_ref[...]) * v
    beta = kk * a

    _rwkv_chunk_stage(ld, kk, r, kd, beta, v, aq_ref, vp_ref, y0_ref, rt_ref, mrb_ref, bht_ref, gm_ref, pc_ref)


def _rwkv_chunk_stage(ld, kk, r, kd, beta, v, aq_ref, vp_ref, y0_ref, rt_ref, mrb_ref, bht_ref, gm_ref, pc_ref):
    tr = ld.shape[0]
    n = min(RWKV_STAGE_ROWS, tr)
    nparts = tr // n
    hd = HEAD_DIM
    ch = RWKV_CHUNK
    nch = n // ch
    same_t, incl_t, _ = _chunk_masks(tr)
    cum = _dot_exact_lhs(jnp.where(incl_t, 1.0, 0.0).astype(BF16), ld)
    tot = _dot_exact_lhs(jnp.where(same_t, 1.0, 0.0).astype(BF16), ld)
    _, incl, strict = _chunk_masks(n)
    alpha_t = kk * jnp.exp(cum - ld)
    r_t = r * jnp.exp(cum)
    e_neg = jnp.exp(-cum)
    k_t = kd * e_neg
    b_t = beta * e_neg
    e_rem = jnp.exp(tot - cum)
    k_hat_t = (kd * e_rem).T
    b_hat_t = (beta * e_rem).T
    pc_t = jnp.exp(tot).T
    t_i = lax.broadcasted_iota(jnp.int32, (n, n), 0)
    s_i = lax.broadcasted_iota(jnp.int32, (n, n), 1)
    eye_f = jnp.where(t_i == s_i, 1.0, 0.0)
    same_blk = []
    size = RWKV_INV_BASE
    while size <= ch:
        same_blk.append((t_i // size) == (s_i // size))
        size *= 2
    col_chunk = lax.broadcasted_iota(jnp.int32, (n, nch * hd), 1) // hd
    row_chunk = lax.broadcasted_iota(jnp.int32, (n, nch * hd), 0) // ch

    def diag_blocks(m):
        out = m[:, :ch]
        for j in range(1, nch):
            out = out + m[:, j * ch:(j + 1) * ch]
        return out

    chains = [(p, h) for p in range(nparts) for h in range(RWKV_HEADS)]

    def blk(t, p, h):
        return t[p * n:(p + 1) * n, h * hd:(h + 1) * hd]

    prods = [_dot_nt(jnp.concatenate([blk(alpha_t, p, h), blk(r_t, p, h)], axis=0).astype(BF16),
                     jnp.concatenate([blk(b_t, p, h), blk(k_t, p, h)], axis=0).astype(BF16))
             for p, h in chains]
    l_ab = [jnp.where(strict, pr[:n, :n], 0.0) for pr in prods]
    pw = [jnp.where(same_blk[0], l, 0.0) for l in l_ab]
    t_inv = [eye_f - l for l in pw]
    for _ in range(int(math.log2(RWKV_INV_BASE)) - 1):
        pw = [_dot_bf(q, q) for q in pw]
        t_inv = [t + _dot_bf(t, q) for t, q in zip(t_inv, pw)]
    for lvl in range(1, len(same_blk)):
        off = same_blk[lvl] & jnp.logical_not(same_blk[lvl - 1])
        half = [_dot_bf(t, jnp.where(off, l, 0.0)) for t, l in zip(t_inv, l_ab)]
        t_inv = [t - _dot_bf(hf, t) for t, hf in zip(t_inv, half)]
    vh = [blk(v, p, h).astype(BF16) for p, h in chains]
    lakv = [_dot(jnp.where(strict, pr[:n, n:], 0.0).astype(BF16), vv) for pr, vv in zip(prods, vh)]
    x = [_dot(t.astype(BF16), jnp.concatenate([blk(alpha_t, p, h), lv], axis=1).astype(BF16))
         for t, (p, h), lv in zip(t_inv, chains, lakv)]
    y0 = [_dot(jnp.where(incl, pr[n:, n:], 0.0).astype(BF16), vv) for pr, vv in zip(prods, vh)]
    mrb = [diag_blocks(jnp.where(incl, pr[n:, :n], 0.0)) for pr in prods]
    gms = [_dot(k_hat_t[h * hd:(h + 1) * hd, p * n:(p + 1) * n].astype(BF16),
                jnp.where(col_chunk == row_chunk, jnp.concatenate([blk(v, p, h)] * nch, axis=1), 0.0).astype(BF16))
           for p, h in chains]

    def assemble(parts):
        return jnp.concatenate([jnp.concatenate(parts[p * RWKV_HEADS:(p + 1) * RWKV_HEADS], axis=1)
                                for p in range(nparts)], axis=0)

    aq_ref[0, 0] = assemble([t[:, :hd] for t in x]).astype(aq_ref.dtype)
    vp_ref[0, 0] = assemble([t[:, hd:] for t in x])
    y0_ref[0, 0] = assemble(y0)
    rt_ref[0, 0] = r_t.astype(rt_ref.dtype)
    mrb_ref[0, 0] = assemble(mrb).astype(mrb_ref.dtype)
    for p in range(nparts):
        for j in range(nch):
            jj = p * nch + j
            cs = slice(jj * ch, (jj + 1) * ch)
            bht_ref[0, 0, jj] = jnp.concatenate([b_hat_t[h * hd:(h + 1) * hd, cs] for h in range(RWKV_HEADS)],
                                                axis=1).astype(bht_ref.dtype)
            gm_ref[0, 0, jj] = jnp.concatenate([gms[p * RWKV_HEADS + h][:, j * hd:(j + 1) * hd]
                                                for h in range(RWKV_HEADS)], axis=1)
            pc_ref[0, 0, jj] = jnp.concatenate([pc_t[h * hd:(h + 1) * hd, cs] for h in range(RWKV_HEADS)], axis=1)


def _rwkv_tile(lt, cl):
    tr = 256 if (lt % 256 == 0 and cl % 256 == 0) else 128
    assert lt % tr == 0 and cl % tr == 0
    return tr


def _rwkv_prep(prw, mud, w0, w2p, a0, a2p, g2, k_k, k_a, r_k, cl):
    b, lt, w = prw.shape
    c = RWKV_WIDTH
    tr = _rwkv_tile(lt, cl)
    nch = tr // RWKV_CHUNK
    per = tr // SUBLANES
    src = lambda d, r: _mirror_tile(r, d == 1, cl // tr, lt // tr)
    prev_spec = pl.BlockSpec((1, SUBLANES, w), lambda d, i, r: (i, jnp.maximum(src(d, r) * per - 1, 0), 0))
    next_spec = pl.BlockSpec((1, SUBLANES, w),
                             lambda d, i, r: (i, jnp.minimum((src(d, r) + 1) * per, lt // SUBLANES - 1), 0))
    tile = pl.BlockSpec((1, 1, tr, c), lambda d, i, r: (d, i, r, 0))
    per_dir = lambda shape: pl.BlockSpec((1, *shape), lambda d, i, r: (d,) + (0,) * len(shape))
    const = lambda shape: pl.BlockSpec(shape, lambda d, i, r: (0,) * len(shape))
    seq = lambda dt: jax.ShapeDtypeStruct((2, b, lt, c), dt)
    chunked = jax.ShapeDtypeStruct((2, b, lt // RWKV_CHUNK, HEAD_DIM, c), F32)
    chunk_spec = pl.BlockSpec((1, 1, nch, HEAD_DIM, c), lambda d, i, r: (d, i, r, 0, 0))
    return pl.pallas_call(
        functools.partial(_rwkv_prep_kernel, cl=cl, lt=lt, tr=tr),
        out_shape=(seq(BF16), seq(F32), seq(F32), seq(BF16), seq(BF16),
                   jax.ShapeDtypeStruct(chunked.shape, BF16), chunked, chunked, seq(F32), seq(F32)),
        grid=(2, b, lt // tr),
        in_specs=[pl.BlockSpec((1, tr, w), lambda d, i, r: (i, src(d, r), 0)), prev_spec, next_spec,
                  per_dir((2, w)), per_dir((1, c)), per_dir((2 * RWKV_DECAY_RANK, c)), per_dir((1, c)),
                  per_dir((2 * RWKV_ICLR_RANK, c)), const((RWKV_GATE_RANK, c)), const((1, c)), const((1, c)),
                  const((1, c)), const((c, c))],
        out_specs=(tile, tile, tile, tile, tile, chunk_spec, chunk_spec, chunk_spec, tile, tile),
        compiler_params=_cparams("parallel", "parallel", "parallel"), name="rwkv_prep",
    )(prw, prw, prw, mud, w0.reshape(2, 1, c), w2p, a0.reshape(2, 1, c), a2p, g2.astype(BF16), k_k.reshape(1, c),
      k_a.reshape(1, c), r_k.reshape(1, c), _head_ones(c))


def _rwkv_scan_kernel(aq_ref, vp_ref, y0_ref, rt_ref, mrb_ref, bht_ref, gm_ref, pc_ref, y_ref, s_ref, *, ns):
    @pl.when(pl.program_id(1) == 0)
    def _():
        s_ref[...] = jnp.zeros_like(s_ref)

    c = RWKV_WIDTH
    ch = RWKV_CHUNK
    same_head = (lax.broadcasted_iota(jnp.int32, (c, c), 0) // HEAD_DIM
                 == lax.broadcasted_iota(jnp.int32, (c, c), 1) // HEAD_DIM)

    def head_blockdiag(t):
        return jnp.where(same_head, jnp.concatenate([t] * RWKV_HEADS, axis=0), 0.0).astype(BF16)

    for i in range(ns):
        s0 = s_ref[i]
        r1 = _dot(jnp.concatenate([aq_ref[i, 0], rt_ref[i, 0]], axis=0), head_blockdiag(s0))
        u = r1[:ch] + vp_ref[i, 0]
        r2 = _dot(jnp.concatenate([mrb_ref[i, 0], bht_ref[i, 0, 0]], axis=0), head_blockdiag(u))
        y_ref[i, 0] = r1[ch:] + y0_ref[i, 0] - r2[:ch]
        s_ref[i] = pc_ref[i, 0, 0] * s0 + gm_ref[i, 0, 0] - r2[ch:]


def _rwkv_scan(aq, vp, y0, rt, mrb, bht, gm, pc):
    _, b, lt, c = aq.shape
    nstream = 2 * b
    ns = nstream
    ch = RWKV_CHUNK
    merge = lambda t: t.reshape(nstream, 1, *t.shape[2:])
    tile = pl.BlockSpec((ns, 1, ch, c), lambda s, j: (s, 0, j, 0))
    chunk_spec = pl.BlockSpec((ns, 1, 1, HEAD_DIM, c), lambda s, j: (s, 0, j, 0, 0))
    y = pl.pallas_call(
        functools.partial(_rwkv_scan_kernel, ns=ns),
        out_shape=jax.ShapeDtypeStruct((nstream, 1, lt, c), F32), grid=(nstream // ns, lt // ch),
        in_specs=[tile, tile, tile, tile, tile, chunk_spec, chunk_spec, chunk_spec],
        out_specs=tile,
        scratch_shapes=[pltpu.VMEM((ns, HEAD_DIM, c), F32)],
        compiler_params=_cparams("parallel", "arbitrary"), name="rwkv_scan",
    )(*(merge(t) for t in (aq, vp, y0, rt, mrb, bht, gm, pc)))
    return y.reshape(2, b, lt, c)


def _rwkv_readout_kernel(yf_ref, yr_ref, bf_ref, br_ref, g_ref, lw_ref, lb_ref, ones_ref, o_ref, *, tr):
    unflip = _row_perm(tr, True)
    y = yf_ref[0, 0] + _dot_exact_lhs(unflip, yr_ref[0, 0])
    bonus = bf_ref[0, 0] + _dot_exact_lhs(unflip, br_ref[0, 0])
    inv = 1.0 / HEAD_DIM
    mu = _dot_exact_rhs(y, ones_ref[...]) * inv
    yc = y - mu
    var = _dot_exact_rhs(yc * yc, ones_ref[...]) * inv
    yn = yc * lax.rsqrt(var + RWKV_GN_EPS) * lw_ref[...] + lb_ref[...]
    o_ref[0] = ((yn + bonus) * g_ref[0, 0]).astype(o_ref.dtype)


def _rwkv_readout(y, bonus, g, ln_w, ln_b, cl):
    _, b, lt, c = y.shape
    tr = _rwkv_tile(lt, cl)
    fwd = pl.BlockSpec((1, 1, tr, c), lambda i, r: (0, i, r, 0))
    rev = pl.BlockSpec((1, 1, tr, c), lambda i, r: (1, i, _mirror_tile(r, True, cl // tr, lt // tr), 0))
    row = pl.BlockSpec((1, c), lambda i, r: (0, 0))
    return pl.pallas_call(
        functools.partial(_rwkv_readout_kernel, tr=tr),
        out_shape=jax.ShapeDtypeStruct((b, lt, c), BF16), grid=(b, lt // tr),
        in_specs=[fwd, rev, fwd, rev, fwd, row, row, pl.BlockSpec((c, c), lambda i, r: (0, 0))],
        out_specs=pl.BlockSpec((1, tr, c), lambda i, r: (i, r, 0)),
        compiler_params=_cparams("parallel", "parallel"), name="rwkv_readout",
    )(y, y, bonus, bonus, g, ln_w.reshape(1, c), ln_b.reshape(1, c), _head_ones(c))


def _pad_rank_rows(w):
    z = jnp.zeros_like(w[0])
    return jnp.stack([jnp.concatenate([w[0], z], 0), jnp.concatenate([z, w[1]], 0)])


def _rwkv_mixer(prw, mu, w0, w2, a0, a2, g2, k_k, k_a, r_k, ln_w, ln_b, cl):
    mud = jnp.stack([mu, mu[::-1]])
    outs = _rwkv_prep(prw, mud, w0, _pad_rank_rows(w2).astype(BF16), a0, _pad_rank_rows(a2).astype(BF16), g2,
                      k_k, k_a, r_k, cl)
    *scan_in, g, bonus = outs
    y = _rwkv_scan(*scan_in)
    return _rwkv_readout(y, bonus, g, ln_w, ln_b, cl)


def _merge_kernel(ya_ref, yh_ref, yr_ref, yl_ref, gt_ref, x_ref, modx_ref, modc_ref,
                  wa_ref, wh_ref, wr_ref, wl_ref, wo_ref, o_ref, *, cl, tm, d):
    is_ctx = _row_ids(tm, pl.program_id(1)) < cl
    m = None
    for i, (y_ref, w_ref) in enumerate(((ya_ref, wa_ref), (yh_ref, wh_ref), (yr_ref, wr_ref), (yl_ref, wl_ref))):
        gate = _sigmoid(gt_ref[0, :, i * d:(i + 1) * d].astype(F32))
        term = gate * _dot(y_ref[0], w_ref[...])
        m = term if m is None else m + term
    g1 = _mod_rows(modx_ref, modc_ref, 2, is_ctx)
    o_ref[0] = x_ref[0] + g1 * _dot(m.astype(BF16), wo_ref[...])


def _merge(ys, gates, xc, mod, w_brs, w_out, cl):
    b, lt, d = xc.shape
    tm = _largest_tile(lt, 544, 16)
    row = lambda w: pl.BlockSpec((1, tm, w), lambda i, r: (i, r, 0))
    const = lambda a: pl.BlockSpec(a.shape, lambda i, r: (0, 0))
    ws = [w.astype(BF16) for w in w_brs] + [w_out.astype(BF16)]
    return pl.pallas_call(
        functools.partial(_merge_kernel, cl=cl, tm=tm, d=d),
        out_shape=jax.ShapeDtypeStruct((b, lt, d), F32), grid=(b, lt // tm),
        in_specs=[row(y.shape[-1]) for y in ys] + [row(N_BRANCH * d), row(d),
                  pl.BlockSpec((1, 6, d), lambda i, r: (i, 0, 0)), pl.BlockSpec((1, 6, d), lambda i, r: (b, 0, 0))]
                 + [const(w) for w in ws],
        out_specs=row(d), compiler_params=_cparams("parallel", "parallel"), name="merge",
    )(*ys, gates, xc, mod, mod, *ws)


def _route_kernel(x_ref, modx_ref, modc_ref, wr_ref, br_ref, h_ref, cmb_ref, *, cl, tm):
    is_ctx = _row_ids(tm, pl.program_id(1)) < cl
    h = _rms_modulate(x_ref[0], _mod_rows(modx_ref, modc_ref, 3, is_ctx), _mod_rows(modx_ref, modc_ref, 4, is_ctx))
    h_ref[0] = h.astype(h_ref.dtype)
    lg = jnp.dot(h, wr_ref[...], preferred_element_type=F32, precision=lax.Precision.HIGHEST) + br_ref[...]
    lane = lax.broadcasted_iota(jnp.int32, lg.shape, 1)
    lane_f = lane.astype(F32)
    neg = -jnp.inf
    big = 1e9

    def first_lane(cond):
        return jnp.min(jnp.where(cond, lane_f, big), axis=-1, keepdims=True)

    is_grp = (lane >= N_EXPERTS) & (lane < N_EXPERTS + N_GROUPS)
    gl = jnp.where(is_grp, lg, neg)
    gmax = jnp.max(gl, axis=-1, keepdims=True)
    ge = jnp.where(is_grp, jnp.exp(gl - gmax), 0.0)
    gp = ge / jnp.sum(ge, axis=-1, keepdims=True)
    g_val = jnp.max(gp, axis=-1, keepdims=True)
    g_idx = first_lane(is_grp & (gp == g_val)) - N_EXPERTS
    lo = g_idx * EXPERTS_PER_GROUP
    in_grp = (lane_f >= lo) & (lane_f < lo + EXPERTS_PER_GROUP)
    el = jnp.where(in_grp, lg, neg)
    emax = jnp.max(el, axis=-1, keepdims=True)
    ee = jnp.where(in_grp, jnp.exp(el - emax), 0.0)
    pe = ee / jnp.sum(ee, axis=-1, keepdims=True)
    v1 = jnp.max(jnp.where(in_grp, pe, -1.0), axis=-1, keepdims=True)
    i1 = first_lane(in_grp & (pe == v1))
    rest = in_grp & (lane_f != i1)
    v2 = jnp.max(jnp.where(rest, pe, -1.0), axis=-1, keepdims=True)
    i2 = first_lane(rest & (pe == v2))
    den = v1 + v2
    cmb_ref[0] = (jnp.where(lane_f == i1, g_val * v1 / den, 0.0) + jnp.where(lane_f == i2, g_val * v2 / den, 0.0)
                  + jnp.where(lane == N_EXPERTS, g_idx, 0.0))


def _route(xc, mod, w_grp, b_grp, w_rt, b_rt, cl):
    b, lt, d = xc.shape
    tm = _largest_tile(lt, 544, 16)
    wr = jnp.zeros((d, LANES), F32).at[:, :N_EXPERTS].set(w_rt).at[:, N_EXPERTS:N_EXPERTS + N_GROUPS].set(w_grp)
    br = jnp.zeros((1, LANES), F32).at[0, :N_EXPERTS].set(b_rt).at[0, N_EXPERTS:N_EXPERTS + N_GROUPS].set(b_grp)
    return pl.pallas_call(
        functools.partial(_route_kernel, cl=cl, tm=tm),
        out_shape=(jax.ShapeDtypeStruct((b, lt, d), BF16), jax.ShapeDtypeStruct((b, lt, LANES), F32)),
        grid=(b, lt // tm),
        in_specs=[pl.BlockSpec((1, tm, d), lambda i, r: (i, r, 0)),
                  pl.BlockSpec((1, 6, d), lambda i, r: (i, 0, 0)), pl.BlockSpec((1, 6, d), lambda i, r: (b, 0, 0)),
                  pl.BlockSpec((d, LANES), lambda i, r: (0, 0)), pl.BlockSpec((1, LANES), lambda i, r: (0, 0))],
        out_specs=(pl.BlockSpec((1, tm, d), lambda i, r: (i, r, 0)), pl.BlockSpec((1, tm, LANES), lambda i, r: (i, r, 0))),
        compiler_params=_cparams("parallel", "parallel"), name="moe_route",
    )(xc, mod, mod, wr, br)


def _moe_kernel(grp_ref, nvalid_ref, h_ref, cmb_ref, w1_ref, w3_ref, w2_ref, o_ref, acc_ref, *, bm):
    j = pl.program_id(0)
    e = pl.program_id(1)

    @pl.when(e == 0)
    def _():
        acc_ref[...] = jnp.zeros_like(acc_ref)

    @pl.when(j < nvalid_ref[0])
    def _():
        h = h_ref[...]
        t = _silu(_dot(h, w1_ref[0, 0])) * _dot(h, w3_ref[0, 0])
        y = _dot(t.astype(BF16), w2_ref[0, 0])
        lane = lax.broadcasted_iota(jnp.int32, (bm, LANES), 1)
        expert = grp_ref[j] * EXPERTS_PER_GROUP + e
        wcol = jnp.sum(jnp.where(lane == expert, cmb_ref[...], 0.0), axis=-1, keepdims=True)
        acc_ref[...] += wcol * y

    @pl.when(e == pl.num_programs(1) - 1)
    def _():
        o_ref[...] = acc_ref[...].astype(o_ref.dtype)


def _moe_grouped(hs, ws, blk_group, nvalid, w1, w3, w2, layer):
    s_rows, d = hs.shape
    hid = w1.shape[3]
    bm = MOE_BLOCK_ROWS
    wspec = lambda shape: pl.BlockSpec(shape, lambda j, e, grp, nv: (layer, grp[j] * EXPERTS_PER_GROUP + e, 0, 0))
    return pl.pallas_call(
        functools.partial(_moe_kernel, bm=bm),
        out_shape=jax.ShapeDtypeStruct((s_rows, d), BF16),
        grid_spec=pltpu.PrefetchScalarGridSpec(
            num_scalar_prefetch=2, grid=(s_rows // bm, EXPERTS_PER_GROUP),
            in_specs=[pl.BlockSpec((bm, d), lambda j, e, grp, nv: (j, 0)),
                      pl.BlockSpec((bm, LANES), lambda j, e, grp, nv: (j, 0)),
                      wspec((1, 1, d, hid)), wspec((1, 1, d, hid)), wspec((1, 1, hid, d))],
            out_specs=pl.BlockSpec((bm, d), lambda j, e, grp, nv: (j, 0)),
            scratch_shapes=[pltpu.VMEM((bm, d), F32)]),
        compiler_params=_cparams("parallel", "arbitrary"), name="moe_experts",
    )(blk_group, nvalid, hs, ws, w1, w3, w2)


def _moe_finish_kernel(x_ref, y_ref, modx_ref, modc_ref, o_ref, *, cl, tm):
    is_ctx = _row_ids(tm, pl.program_id(1)) < cl
    o_ref[0] = x_ref[0] + _mod_rows(modx_ref, modc_ref, 5, is_ctx) * y_ref[0].astype(F32)


def _moe_finish(xc, y, mod, cl):
    b, lt, d = xc.shape
    tm = _largest_tile(lt, 1088, 16)
    tile = pl.BlockSpec((1, tm, d), lambda i, r: (i, r, 0))
    return pl.pallas_call(
        functools.partial(_moe_finish_kernel, cl=cl, tm=tm),
        out_shape=jax.ShapeDtypeStruct((b, lt, d), F32), grid=(b, lt // tm),
        in_specs=[tile, tile, pl.BlockSpec((1, 6, d), lambda i, r: (i, 0, 0)),
                  pl.BlockSpec((1, 6, d), lambda i, r: (b, 0, 0))],
        out_specs=tile, compiler_params=_cparams("parallel", "parallel"), name="moe_finish",
    )(xc, y, mod, mod)


def _moe(h2, cmb, xc, mod, w1, w3, w2, layer, cl):
    b, lt, d = xc.shape
    t = b * lt
    bm = MOE_BLOCK_ROWS
    s_rows = -(-t // bm) * bm + N_GROUPS * bm
    cmb2 = cmb.reshape(t, LANES)
    gid = cmb2[:, N_EXPERTS].astype(jnp.int32)
    onehot = (gid[:, None] == jnp.arange(N_GROUPS, dtype=jnp.int32)).astype(jnp.int32)
    csum = jnp.cumsum(onehot, axis=0)
    rank = jnp.sum(onehot * csum, axis=1) - 1
    padded = -(-csum[-1] // bm) * bm
    ends = jnp.cumsum(padded)
    pos = (ends - padded)[gid] + rank
    slot_token = jnp.zeros((s_rows,), jnp.int32).at[pos].set(jnp.arange(t, dtype=jnp.int32))
    starts = jnp.arange(s_rows // bm, dtype=jnp.int32) * bm
    blk_group = jnp.minimum(jnp.sum((starts[:, None] >= ends[None, :]).astype(jnp.int32), axis=1), N_GROUPS - 1)
    nvalid = (ends[-1] // bm).reshape(1).astype(jnp.int32)
    take_rows = lambda a, idx: a.at[idx].get(mode="promise_in_bounds")
    hs = take_rows(h2.reshape(t, d), slot_token)
    ws = take_rows(cmb2, slot_token)
    ys = _moe_grouped(hs, ws, blk_group.astype(jnp.int32), nvalid, w1, w3, w2, layer)
    y = take_rows(ys, pos).reshape(b, lt, d)
    return _moe_finish(xc, y, mod, cl)


def kernel(x, c, ctx, c_ctx, ada_w, ada_b, w_in, q_norm, k_norm, hy_conv_w, hy_conv_b, hy_f1, hy_fb1, hy_f2, hy_fb2, hy_f3, hy_skip, rw_mu, rw_w0, rw_w2, rw_a0, rw_a2, rw_g2, rw_k_k, rw_k_a, rw_r_k, rw_ln_w, rw_ln_b, lru_conv_w, lru_conv_b, lru_wa, lru_ba, lru_wx, lru_bx, lru_lambda, w_br_attn, w_br_hyena, w_br_rwkv, w_br_lru, w_out, moe_w_grp, moe_b_grp, moe_w_rt, moe_b_rt, moe_w1, moe_w3, moe_w2):
    b, l, d = x.shape
    cl = ctx.shape[1]
    depth = ada_w.shape[0]
    assert b < MOD_ROWS and cl % RWKV_CHUNK == 0 and l % RWKV_CHUNK == 0

    xc = jnp.concatenate([ctx, x], axis=1)
    cc = jnp.zeros((MOD_ROWS, d), F32).at[:b].set(c).at[b].set(c_ctx)
    mod_all = _ada_mod(cc, ada_w, ada_b).reshape(depth, MOD_ROWS, 6, d)

    moe_w1b, moe_w3b, moe_w2b = (w.astype(BF16) for w in (moe_w1, moe_w3, moe_w2))
    cos2, sin2 = _rope_tables(l, cl)
    mats_x = _dft_mats(l)
    mats_c = _dft_mats(cl)
    qkv_w = ATTN_WIDTH + 2 * ATTN_KV_WIDTH
    col = np.cumsum([0, qkv_w, 3 * HYENA_WIDTH, RWKV_PROJ, 2 * LRU_WIDTH, N_BRANCH * d])

    for i in range(depth):
        need_ctx = i < depth - 1
        mod = mod_all[i]
        w_i = w_in[i].astype(BF16)
        h1 = _modnorm(xc, mod, cl)
        pqkv, phy, prw, plr, gates = (_proj(h1, w_i[:, col[j]:col[j + 1]]) for j in range(5))

        qn, kt, vx = _attn_prep(pqkv, cos2, sin2, q_norm[i], k_norm[i])
        y_att_x = _attention(qn, kt, vx, cl, 'x', cl + l)
        if need_ctx:
            y_att_c = _attention(qn, kt, vx, cl, 'ctx', cl)
        else:
            y_att_c = jnp.zeros((b, cl, ATTN_WIDTH), BF16)
        y_att = jnp.concatenate([y_att_c, y_att_x], axis=1)

        hv, hx1, hx2 = _hyena_pre(phy, hy_conv_w[i], hy_conv_b[i], cl)
        filt = (hy_f1[i], hy_fb1[i], hy_f2[i], hy_fb2[i], hy_f3[i])
        y_hx = _hyena_run(hv[:, cl:], hx1[:, cl:], hx2[:, cl:], _hyena_spectra(l, filt, mats_x), hy_skip[i], mats_x)
        if need_ctx:
            y_hc = _hyena_run(hv[:, :cl], hx1[:, :cl], hx2[:, :cl], _hyena_spectra(cl, filt, mats_c), hy_skip[i],
                              mats_c)
        else:
            y_hc = jnp.zeros((b, cl, HYENA_WIDTH), BF16)
        y_hy = jnp.concatenate([y_hc, y_hx], axis=1)

        y_rw = _rwkv_mixer(prw, rw_mu[i], rw_w0[i], rw_w2[i], rw_a0[i], rw_a2[i], rw_g2[i], rw_k_k[i], rw_k_a[i],
                           rw_r_k[i].reshape(-1), rw_ln_w[i], rw_ln_b[i], cl)

        la, lb = _lru_pre(plr, lru_conv_w[i], lru_conv_b[i], lru_wa[i], lru_ba[i], lru_wx[i], lru_bx[i],
                          lru_lambda[i], cl)
        y_lr = _lru_scan(la, lb, plr, cl)

        xc = _merge((y_att, y_hy, y_rw, y_lr), gates, xc, mod,
                    (w_br_attn[i], w_br_hyena[i], w_br_rwkv[i], w_br_lru[i]), w_out[i], cl)

        h2, cmb = _route(xc, mod, moe_w_grp[i], moe_b_grp[i], moe_w_rt[i], moe_b_rt[i], cl)
        xc = _moe(h2, cmb, xc, mod, moe_w1b, moe_w3b, moe_w2b, i, cl)
    return xc[:, cl:]
```

```python
import functools
import math

import numpy as np
import jax
import jax.numpy as jnp
from jax import lax
from jax.experimental import pallas as pl
from jax.experimental.pallas import tpu as pltpu

F32 = jnp.float32
BF16 = jnp.bfloat16

HEAD_DIM = 64
GRID_W = 64
EPS = 1e-6
ATTN_HEADS = 8
ATTN_KV_HEADS = 2
ATTN_GROUP = ATTN_HEADS // ATTN_KV_HEADS
ATTN_Q_BLOCKS = 2
ATTN_WIDTH = ATTN_HEADS * HEAD_DIM
ATTN_KV_WIDTH = ATTN_KV_HEADS * HEAD_DIM
ROPE_THETA = 10000.0
HYENA_WIDTH = 256
HYENA_ORDER = 2
HYENA_BANDS = 16
HYENA_DECAY_TARGET = 1e-2
HYENA_FAST_DECAY = 0.3
HYENA_SLOW_DECAY = 1.5
RWKV_HEADS = 4
RWKV_WIDTH = RWKV_HEADS * HEAD_DIM
RWKV_DECAY_RANK = 64
RWKV_ICLR_RANK = 64
RWKV_GATE_RANK = 128
RWKV_GN_EPS = 64e-5
RWKV_PROJ = 3 * RWKV_WIDTH + 2 * RWKV_DECAY_RANK + 2 * RWKV_ICLR_RANK + RWKV_GATE_RANK
RWKV_CHUNK = 64
RWKV_INV_BASE = 4
RWKV_STAGE_ROWS = 128
LRU_WIDTH = 256
LRU_BLOCKS = 4
LRU_C = 8.0
LRU_GROUPS_PER_STEP = 4
N_BRANCH = 4
N_GROUPS = 4
EXPERTS_PER_GROUP = 4
N_EXPERTS = N_GROUPS * EXPERTS_PER_GROUP
MOE_BLOCK_ROWS = 512

V7X_VMEM_LIMIT_BYTES = 52 * 1024 * 1024
SUBLANES = 8
LANES = 128
MOD_ROWS = 16


def _cparams(*sem):
    return pltpu.CompilerParams(dimension_semantics=sem, vmem_limit_bytes=V7X_VMEM_LIMIT_BYTES)


def _dot(a, b):
    return jnp.dot(a, b, preferred_element_type=F32)


def _dot_nt(a, b):
    return lax.dot_general(a, b, (((1,), (1,)), ((), ())), preferred_element_type=F32)


def _split3(x):
    hi = x.astype(BF16)
    r1 = x - hi.astype(F32)
    mid = r1.astype(BF16)
    lo = (r1 - mid.astype(F32)).astype(BF16)
    return hi, mid, lo


def _dot_exact_lhs(m_bf16, x):
    hi, mid, lo = _split3(x)
    return _dot(m_bf16, hi) + _dot(m_bf16, mid) + _dot(m_bf16, lo)


def _dot_exact_rhs(x, m_bf16):
    hi, mid, lo = _split3(x)
    return _dot(hi, m_bf16) + _dot(mid, m_bf16) + _dot(lo, m_bf16)


def _dot_bf(a, b):
    return _dot(a.astype(BF16), b.astype(BF16))


def _sigmoid(x):
    return 1.0 / (1.0 + jnp.exp(-x))


def _softplus(x):
    return jnp.maximum(x, 0.0) + jnp.log(1.0 + jnp.exp(-jnp.abs(x)))


def _silu(x):
    return x * _sigmoid(x)


def _largest_tile(n, cap, mult):
    best = None
    for t in range(mult, min(n, cap) + 1, mult):
        if n % t == 0:
            best = t
    assert best is not None, (n, cap, mult)
    return best


def _head_ones(width):
    idx = np.arange(width) // HEAD_DIM
    return jnp.asarray((idx[:, None] == idx[None, :]).astype(np.float32), dtype=BF16)


def _row_ids(tile_rows, tile_idx):
    return tile_idx * tile_rows + lax.broadcasted_iota(jnp.int32, (tile_rows, 1), 0)


def _mod_rows(modx_ref, modc_ref, idx, is_ctx):
    return jnp.where(is_ctx, modc_ref[0, idx:idx + 1, :], modx_ref[0, idx:idx + 1, :])


def _rms_modulate(x, shift, scale):
    ms = jnp.mean(x * x, axis=-1, keepdims=True)
    return (x * lax.rsqrt(ms + EPS)) * (1.0 + scale) + shift


def _ada_kernel(c_ref, w_ref, b_ref, o_ref):
    s = _silu(c_ref[...])
    o_ref[0] = jnp.dot(s, w_ref[0], preferred_element_type=F32, precision=lax.Precision.HIGHEST) + b_ref[0]


def _ada_mod(cc, ada_w, ada_b):
    depth, d, n6 = ada_w.shape
    tn = _largest_tile(n6, 1024, LANES)
    return pl.pallas_call(
        _ada_kernel,
        out_shape=jax.ShapeDtypeStruct((depth, MOD_ROWS, n6), F32),
        grid=(depth, n6 // tn),
        in_specs=[pl.BlockSpec((MOD_ROWS, d), lambda i, j: (0, 0)),
                  pl.BlockSpec((1, d, tn), lambda i, j: (i, 0, j)),
                  pl.BlockSpec((1, 1, tn), lambda i, j: (i, 0, j))],
        out_specs=pl.BlockSpec((1, MOD_ROWS, tn), lambda i, j: (i, 0, j)),
        compiler_params=_cparams("parallel", "parallel"),
        name="ada_mod",
    )(cc, ada_w, ada_b.reshape(depth, 1, n6))


def _modnorm_kernel(x_ref, modx_ref, modc_ref, o_ref, *, cl, tm):
    is_ctx = _row_ids(tm, pl.program_id(1)) < cl
    h = _rms_modulate(x_ref[0], _mod_rows(modx_ref, modc_ref, 0, is_ctx), _mod_rows(modx_ref, modc_ref, 1, is_ctx))
    o_ref[0] = h.astype(o_ref.dtype)


def _modnorm(xc, mod, cl):
    b, lt, d = xc.shape
    tm = _largest_tile(lt, 1088, 16)
    return pl.pallas_call(
        functools.partial(_modnorm_kernel, cl=cl, tm=tm),
        out_shape=jax.ShapeDtypeStruct((b, lt, d), BF16), grid=(b, lt // tm),
        in_specs=[pl.BlockSpec((1, tm, d), lambda i, r: (i, r, 0)),
                  pl.BlockSpec((1, 6, d), lambda i, r: (i, 0, 0)),
                  pl.BlockSpec((1, 6, d), lambda i, r: (b, 0, 0))],
        out_specs=pl.BlockSpec((1, tm, d), lambda i, r: (i, r, 0)),
        compiler_params=_cparams("parallel", "parallel"), name="modnorm",
    )(xc, mod, mod)


def _proj_kernel(h_ref, w_ref, o_ref):
    o_ref[0] = _dot(h_ref[0], w_ref[...]).astype(o_ref.dtype)


def _proj(h, w):
    b, lt, d = h.shape
    n = w.shape[1]
    tm = _largest_tile(lt, 2176, 16)
    tn = n if n <= 1280 else _largest_tile(n, 1024, 2 * LANES)
    return pl.pallas_call(
        _proj_kernel, out_shape=jax.ShapeDtypeStruct((b, lt, n), BF16), grid=(b, lt // tm, n // tn),
        in_specs=[pl.BlockSpec((1, tm, d), lambda i, r, j: (i, r, 0)), pl.BlockSpec((d, tn), lambda i, r, j: (0, j))],
        out_specs=pl.BlockSpec((1, tm, tn), lambda i, r, j: (i, r, j)),
        compiler_params=_cparams("parallel", "parallel", "parallel"), name="proj",
    )(h, w)


def _rope_tables(l, cl):
    n_freq = HEAD_DIM // 4
    t = jnp.arange(l)
    freqs = ROPE_THETA ** (-jnp.arange(n_freq, dtype=F32) / n_freq)
    pos = jnp.stack([t // GRID_W, t % GRID_W], -1).astype(F32)
    ang = pos[..., None] * freqs
    cos64 = jnp.stack([jnp.cos(ang), jnp.cos(ang)], axis=2).reshape(l, HEAD_DIM)
    sin64 = jnp.stack([-jnp.sin(ang), jnp.sin(ang)], axis=2).reshape(l, HEAD_DIM)
    cos64 = jnp.concatenate([jnp.ones((cl, HEAD_DIM), F32), cos64], 0)
    sin64 = jnp.concatenate([jnp.zeros((cl, HEAD_DIM), F32), sin64], 0)
    return jnp.tile(cos64, (1, 2)), jnp.tile(sin64, (1, 2))


def _head_rms(t, ones_ref):
    ms = _dot_exact_rhs(t * t, ones_ref[...]) * (1.0 / HEAD_DIM)
    return t * lax.rsqrt(ms + EPS)


def _rope(t, cos, sin):
    w = t.shape[-1]
    lane = lax.broadcasted_iota(jnp.int32, t.shape, 1)
    q4 = HEAD_DIM // 4
    first_half = (lane % (2 * q4)) < q4
    partner = jnp.where(first_half, pltpu.roll(t, w - q4, 1), pltpu.roll(t, q4, 1))
    return t * cos + partner * sin


def _attn_prep_kernel(p_ref, cos_ref, sin_ref, qg_ref, kg_ref, oq_ref, ok_ref, q_ref, kt_ref, vx_ref):
    p = p_ref[0].astype(F32)
    v = p[:, ATTN_WIDTH + ATTN_KV_WIDTH:]
    low = lax.broadcasted_iota(jnp.int32, v.shape, 1) < HEAD_DIM
    vx_ref[0, 0] = jnp.where(low, v, 1.0).astype(vx_ref.dtype)
    vx_ref[0, 1] = jnp.where(low, pltpu.roll(v, HEAD_DIM, 1), 1.0).astype(vx_ref.dtype)
    cos2, sin2 = cos_ref[...], sin_ref[...]
    reps = ATTN_WIDTH // (2 * HEAD_DIM)
    cos_q = jnp.concatenate([cos2] * reps, axis=1)
    sin_q = jnp.concatenate([sin2] * reps, axis=1)
    q = _head_rms(p[:, :ATTN_WIDTH], oq_ref) * qg_ref[...]
    q_ref[0] = _rope(q, cos_q, sin_q).astype(q_ref.dtype)
    k = _head_rms(p[:, ATTN_WIDTH:ATTN_WIDTH + ATTN_KV_WIDTH], ok_ref) * kg_ref[...]
    kt_ref[0] = _rope(k, cos2, sin2).T.astype(kt_ref.dtype)


def _attn_prep(pqkv, cos2, sin2, q_gain, k_gain):
    b, lt, wtot = pqkv.shape
    tr = _largest_tile(lt, 2176, LANES)
    qg = jnp.tile(q_gain * (HEAD_DIM ** -0.5 * math.log2(math.e)), ATTN_HEADS).reshape(1, ATTN_WIDTH)
    kg = jnp.tile(k_gain, ATTN_KV_HEADS).reshape(1, ATTN_KV_WIDTH)
    return pl.pallas_call(
        _attn_prep_kernel,
        out_shape=(jax.ShapeDtypeStruct((b, lt, ATTN_WIDTH), BF16),
                   jax.ShapeDtypeStruct((b, ATTN_KV_WIDTH, lt), BF16),
                   jax.ShapeDtypeStruct((b, ATTN_KV_HEADS, lt, 2 * HEAD_DIM), BF16)),
        grid=(b, lt // tr),
        in_specs=[pl.BlockSpec((1, tr, wtot), lambda i, r: (i, r, 0)),
                  pl.BlockSpec((tr, 2 * HEAD_DIM), lambda i, r: (r, 0)),
                  pl.BlockSpec((tr, 2 * HEAD_DIM), lambda i, r: (r, 0)),
                  pl.BlockSpec((1, ATTN_WIDTH), lambda i, r: (0, 0)),
                  pl.BlockSpec((1, ATTN_KV_WIDTH), lambda i, r: (0, 0)),
                  pl.BlockSpec((ATTN_WIDTH, ATTN_WIDTH), lambda i, r: (0, 0)),
                  pl.BlockSpec((ATTN_KV_WIDTH, ATTN_KV_WIDTH), lambda i, r: (0, 0))],
        out_specs=(pl.BlockSpec((1, tr, ATTN_WIDTH), lambda i, r: (i, r, 0)),
                   pl.BlockSpec((1, ATTN_KV_WIDTH, tr), lambda i, r: (i, 0, r)),
                   pl.BlockSpec((1, ATTN_KV_HEADS, tr, 2 * HEAD_DIM), lambda i, r: (i, 0, r, 0))),
        compiler_params=_cparams("parallel", "parallel"),
        name="attn_prep",
    )(pqkv, cos2, sin2, qg, kg, _head_ones(ATTN_WIDTH), _head_ones(ATTN_KV_WIDTH))


def _attn_kernel(*refs, nq, nk):
    q_refs, (kt_ref, v_ref, o_ref) = refs[:nq], refs[nq:]
    outs = []
    for h in range(ATTN_HEADS):
        kv = h // ATTN_GROUP
        hs = slice(h * HEAD_DIM, (h + 1) * HEAD_DIM)
        qh = jnp.concatenate([q_ref[0, :, hs] for q_ref in q_refs], axis=0)
        s = _dot(qh, kt_ref[0, kv * HEAD_DIM:(kv + 1) * HEAD_DIM, :nk])
        m = jnp.max(s, axis=-1, keepdims=True)
        p = jnp.exp2(s - m)
        o = _dot(p.astype(BF16), v_ref[0, kv, :nk, :])
        outs.append(o[:, :HEAD_DIM] / o[:, HEAD_DIM:])
    o_ref[0] = jnp.concatenate(outs, axis=-1).astype(o_ref.dtype)


def _attention(qn, kt, vx, cl, rows, nk):
    b, lt, _ = qn.shape
    blk = 256 if (cl % 256 == 0 and lt % 256 == 0) else 128
    if rows == 'ctx':
        n_rows, first, nq = cl, 0, 1
    else:
        n_rows, first = lt - cl, cl // blk
        nq = ATTN_Q_BLOCKS if n_rows % (ATTN_Q_BLOCKS * blk) == 0 else 1
    tq = nq * blk
    q_specs = [pl.BlockSpec((1, blk, ATTN_WIDTH), lambda i, t, j=j: (i, first + nq * t + j, 0)) for j in range(nq)]
    return pl.pallas_call(
        functools.partial(_attn_kernel, nq=nq, nk=nk),
        out_shape=jax.ShapeDtypeStruct((b, n_rows, ATTN_WIDTH), BF16),
        grid=(b, n_rows // tq),
        in_specs=q_specs + [pl.BlockSpec((1, ATTN_KV_WIDTH, lt), lambda i, t: (i, 0, 0)),
                            pl.BlockSpec((1, ATTN_KV_HEADS, lt, 2 * HEAD_DIM), lambda i, t: (i, 0, 0, 0))],
        out_specs=pl.BlockSpec((1, tq, ATTN_WIDTH), lambda i, t: (i, t, 0)),
        compiler_params=_cparams("parallel", "parallel"),
        name="attention_" + rows,
    )(*([qn] * nq), kt, vx)


def _halo_specs(tr, lt, width, lead):
    per = tr // SUBLANES
    last = lt // SUBLANES - 1
    nlead = len(lead)

    def prev_map(*ids):
        return (*ids[:nlead], jnp.maximum(ids[nlead] * per - 1, 0), 0)

    def next_map(*ids):
        return (*ids[:nlead], jnp.minimum((ids[nlead] + 1) * per, last), 0)

    blk = (*lead, SUBLANES, width)
    return pl.BlockSpec(blk, prev_map), pl.BlockSpec(blk, next_map)


def _shift_rows(x, halo, offset, rows, cl, lt):
    tr = x.shape[0]
    local = lax.broadcasted_iota(jnp.int32, (tr, 1), 0)
    if offset < 0:
        y = pltpu.roll(x, -offset, 0)
        y = jnp.where(local == 0, halo, y)
        bad = (rows == 0) | (rows == cl)
    else:
        y = pltpu.roll(x, tr - offset, 0)
        for j in range(offset):
            y = jnp.where(local == tr - offset + j, halo[j:j + 1, :], y)
        bad = (rows >= lt - offset) | ((rows >= cl - offset) & (rows < cl))
    return jnp.where(bad, 0.0, y)


def _hyena_pre_kernel(p_ref, pv_ref, nx_ref, w_ref, b_ref, v_ref, x1_ref, x2_ref, *, cl, lt, tr):
    rows = _row_ids(tr, pl.program_id(1))
    p = p_ref[0].astype(F32)
    pm = _shift_rows(p, pv_ref[0, SUBLANES - 1:, :].astype(F32), -1, rows, cl, lt)
    pp = _shift_rows(p, nx_ref[0, :1, :].astype(F32), 1, rows, cl, lt)
    z = pm * w_ref[0:1, :] + p * w_ref[1:2, :] + pp * w_ref[2:3, :] + b_ref[...]
    c = HYENA_WIDTH
    v_ref[0] = z[:, :c].astype(v_ref.dtype)
    x1_ref[0] = z[:, c:2 * c].astype(x1_ref.dtype)
    x2_ref[0] = z[:, 2 * c:].astype(x2_ref.dtype)


def _hyena_pre(phy, conv_w, conv_b, cl):
    b, lt, w = phy.shape
    tr = _largest_tile(lt, 1088, 16)
    prev_spec, next_spec = _halo_specs(tr, lt, w, (1,))
    out = jax.ShapeDtypeStruct((b, lt, HYENA_WIDTH), BF16)
    ospec = pl.BlockSpec((1, tr, HYENA_WIDTH), lambda i, r: (i, r, 0))
    return pl.pallas_call(
        functools.partial(_hyena_pre_kernel, cl=cl, lt=lt, tr=tr),
        out_shape=(out, out, out),
        grid=(b, lt // tr),
        in_specs=[pl.BlockSpec((1, tr, w), lambda i, r: (i, r, 0)), prev_spec, next_spec,
                  pl.BlockSpec(conv_w.shape, lambda i, r: (0, 0)),
                  pl.BlockSpec((1, w), lambda i, r: (0, 0))],
        out_specs=(ospec, ospec, ospec),
        compiler_params=_cparams("parallel", "parallel"),
        name="hyena_pre",
    )(phy, phy, phy, conv_w, conv_b.reshape(1, w))


def _trig_mats_kernel(cb_ref, sb_ref, co_ref, so_ref, c_ref, s_ref):
    cb, sb, co, so = cb_ref[0], sb_ref[0], co_ref[...], so_ref[...]
    c_ref[...] = (cb * co - sb * so).astype(c_ref.dtype)
    s_ref[...] = (sb * co + cb * so).astype(s_ref.dtype)


def _trig_mats(n, h, row_mult, col_mult):
    tm = _largest_tile(h, 256, 16)
    tn = _largest_tile(h, 1024, LANES) if h % LANES == 0 else h
    b = col_mult(jnp.arange(h, dtype=jnp.int32))[None, :]

    def tables(a):
        ang = ((a[:, None] * b) % (2 * n)).astype(F32) * (math.pi / n)
        return jnp.cos(ang), jnp.sin(ang)

    r0 = jnp.arange(0, h, tm, dtype=jnp.int32)
    cb, sb = tables(row_mult(r0))
    co, so = tables(row_mult(jnp.arange(tm, dtype=jnp.int32)) - row_mult(jnp.zeros((tm,), jnp.int32)))
    base = pl.BlockSpec((1, 1, tn), lambda j, c: (j, 0, c))
    off = pl.BlockSpec((tm, tn), lambda j, c: (0, c))
    out = jax.ShapeDtypeStruct((h, h), BF16)
    ospec = pl.BlockSpec((tm, tn), lambda j, c: (j, c))
    return pl.pallas_call(
        _trig_mats_kernel, out_shape=(out, out), grid=(h // tm, h // tn),
        in_specs=[base, base, off, off], out_specs=(ospec, ospec),
        compiler_params=_cparams("parallel", "parallel"), name="trig_mats",
    )(cb.reshape(h // tm, 1, h), sb.reshape(h // tm, 1, h), co, so)


def _dft_mats(n):
    h = n // 2
    ident = lambda r: r
    ce, se = _trig_mats(n, h, ident, lambda j: 2 * j)
    co, so = _trig_mats(n, h, ident, lambda j: 2 * j + 1)
    cot, sot = _trig_mats(n, h, lambda j: 2 * j + 1, ident)
    return ce, se, co, so, cot, sot


def _alt_sum(z):
    j = lax.broadcasted_iota(jnp.int32, (z.shape[0], 1), 0)
    return jnp.sum(z * (1 - 2 * (j % 2)).astype(F32), axis=0, keepdims=True)


def _half_spectra(ce_ref, se_ref, co_ref, so_ref, z_ev, z_od):
    p, q = _dot(ce_ref[...], z_ev), _dot(co_ref[...], z_od)
    ps, qs = _dot(se_ref[...], z_ev), _dot(so_ref[...], z_od)
    return p + q, ps + qs, p - q, qs - ps


def _dft_raw_kernel(ce_ref, se_ref, co_ref, so_ref, z_ref, rl_ref, il_ref, ru_ref, iu_ref, mid_ref):
    z_ev, z_od = z_ref[0], z_ref[1]
    rl_ref[...], il_ref[...], ru_ref[...], iu_ref[...] = _half_spectra(ce_ref, se_ref, co_ref, so_ref, z_ev, z_od)

    @pl.when(pl.program_id(0) == 0)
    def _():
        mid = jnp.concatenate([_alt_sum(z_ev.astype(F32)), _alt_sum(z_od.astype(F32))], axis=0)
        mid_ref[...] = jnp.concatenate([mid, jnp.zeros((SUBLANES - 2, mid.shape[1]), F32)], axis=0)


def _dft_raw(mats, zs):
    ce, se, co, so, _, _ = mats
    _, h, c = zs.shape
    tm = _largest_tile(h, 256, 16)
    mat = pl.BlockSpec((tm, h), lambda j: (j, 0))
    out = jax.ShapeDtypeStruct((h, c), F32)
    ospec = pl.BlockSpec((tm, c), lambda j: (j, 0))
    return pl.pallas_call(
        _dft_raw_kernel, out_shape=(out, out, out, out, jax.ShapeDtypeStruct((SUBLANES, c), F32)), grid=(h // tm,),
        in_specs=[mat, mat, mat, mat, pl.BlockSpec((2, h, c), lambda j: (0, 0, 0))],
        out_specs=(ospec, ospec, ospec, ospec, pl.BlockSpec((SUBLANES, c), lambda j: (0, 0))),
        compiler_params=_cparams("arbitrary"), name="dft_raw",
    )(ce, se, co, so, zs)


def _dft_fwd_kernel(ce_ref, se_ref, co_ref, so_ref, z_ref, hrl_ref, hil_ref, hru_ref, hiu_ref, hmid_ref,
                    ea_ref, eb_ref, oa_ref, ob_ref, mid_ref, *, bb):
    hrl, hil, hru, hiu = hrl_ref[...], hil_ref[...], hru_ref[...], hiu_ref[...]
    for i in range(bb):
        z_ev, z_od = z_ref[i, 0], z_ref[i, 1]
        zrl, zil, zru, ziu = _half_spectra(ce_ref, se_ref, co_ref, so_ref, z_ev, z_od)
        yrl, yil = zrl * hrl + zil * hil, zil * hrl - zrl * hil
        yru, yiu = zru * hru + ziu * hiu, ziu * hru - zru * hiu
        ea_ref[i] = (yrl + yru).astype(ea_ref.dtype)
        eb_ref[i] = (yil - yiu).astype(eb_ref.dtype)
        oa_ref[i] = (yrl - yru).astype(oa_ref.dtype)
        ob_ref[i] = (yil + yiu).astype(ob_ref.dtype)

    @pl.when(pl.program_id(1) == 0)
    def _():
        hr, hi = hmid_ref[0:1, :], hmid_ref[1:2, :]
        for i in range(bb):
            zr, zi = _alt_sum(z_ref[i, 0].astype(F32)), _alt_sum(z_ref[i, 1].astype(F32))
            mid = jnp.concatenate([zr * hr + zi * hi, zi * hr - zr * hi], axis=0)
            mid_ref[i] = jnp.concatenate([mid, jnp.zeros((SUBLANES - 2, mid.shape[1]), F32)], axis=0)


def _dft_inv_kernel(ce_ref, se_ref, cot_ref, sot_ref, ea_ref, eb_ref, oa_ref, ob_ref, mid_ref, z_ref, g_ref, skip_ref,
                    o_ref, *, bb, tm):
    alt = (1 - 2 * (_row_ids(tm, pl.program_id(1)) % 2)).astype(F32)
    for i in range(bb):
        y_ev = _dot(ce_ref[...], ea_ref[i]) + _dot(se_ref[...], eb_ref[i]) + alt * mid_ref[i, 0:1, :]
        y_od = _dot(cot_ref[...], oa_ref[i]) + _dot(sot_ref[...], ob_ref[i]) + alt * mid_ref[i, 1:2, :]
        for par, y in enumerate((y_ev, y_od)):
            y = y + z_ref[i, par].astype(F32) * skip_ref[...]
            o_ref[i, par] = (g_ref[i, par].astype(F32) * y).astype(o_ref.dtype)


def _longconv_gated(zs, gates, spectrum, skip, mats):
    ce, se, co, so, cot, sot = mats
    b, _, h, c = zs.shape
    bb = 2 if b % 2 == 0 else 1
    tm = _largest_tile(h, 256, 16)
    mat = pl.BlockSpec((tm, h), lambda i, j: (j, 0))
    full4 = pl.BlockSpec((bb, 2, h, c), lambda i, j: (i, 0, 0, 0))
    full = pl.BlockSpec((bb, h, c), lambda i, j: (i, 0, 0))
    tile = pl.BlockSpec((bb, tm, c), lambda i, j: (i, j, 0))
    tile4 = pl.BlockSpec((bb, 2, tm, c), lambda i, j: (i, 0, j, 0))
    filt = pl.BlockSpec((tm, c), lambda i, j: (j, 0))
    mid_spec = pl.BlockSpec((bb, SUBLANES, c), lambda i, j: (i, 0, 0))
    half = jax.ShapeDtypeStruct((b, h, c), BF16)
    *combos, mid = pl.pallas_call(
        functools.partial(_dft_fwd_kernel, bb=bb),
        out_shape=(half, half, half, half, jax.ShapeDtypeStruct((b, SUBLANES, c), F32)), grid=(b // bb, h // tm),
        in_specs=[mat, mat, mat, mat, full4, filt, filt, filt, filt, pl.BlockSpec((SUBLANES, c), lambda i, j: (0, 0))],
        out_specs=(tile, tile, tile, tile, mid_spec),
        compiler_params=_cparams("parallel", "arbitrary"), name="dft_fwd",
    )(ce, se, co, so, zs, *spectrum)
    return pl.pallas_call(
        functools.partial(_dft_inv_kernel, bb=bb, tm=tm),
        out_shape=jax.ShapeDtypeStruct((b, 2, h, c), BF16), grid=(b // bb, h // tm),
        in_specs=[mat, mat, mat, mat, full, full, full, full, mid_spec, tile4, tile4,
                  pl.BlockSpec((1, c), lambda i, j: (0, 0))],
        out_specs=tile4,
        compiler_params=_cparams("parallel", "parallel"), name="dft_inv",
    )(ce, se, cot, sot, *combos, mid, zs, gates, skip.reshape(1, c))


def _hyena_filters(n, f1, fb1, f2, fb2, f3):
    t = jnp.arange(n, dtype=F32) / n
    bands = jnp.arange(1, HYENA_BANDS + 1, dtype=F32)
    ang = 2.0 * math.pi * t[:, None] * bands
    feat = jnp.concatenate([t[:, None], jnp.sin(ang), jnp.cos(ang)], axis=-1)
    hp = lax.Precision.HIGHEST
    h = jnp.sin(jnp.dot(feat, f1, precision=hp) + fb1)
    h = jnp.sin(jnp.dot(h, f2, precision=hp) + fb2)
    h = jnp.dot(h, f3, precision=hp).reshape(n, HYENA_ORDER, 2, HYENA_WIDTH)
    deltas = jnp.linspace(-math.log(HYENA_DECAY_TARGET) / HYENA_SLOW_DECAY,
                          -math.log(HYENA_DECAY_TARGET) / HYENA_FAST_DECAY, HYENA_WIDTH, dtype=F32)
    h = h * jnp.exp(-t[:, None] * deltas)[:, None, None, :]
    return h / jnp.sum(jnp.abs(h), axis=(0, 2), keepdims=True)


def _parity_split(t):
    *lead, n, c = t.shape
    return jnp.swapaxes(t.reshape(*lead, n // 2, 2, c), -2, -3)


def _parity_merge(t):
    *lead, _, h, c = t.shape
    return jnp.swapaxes(t, -2, -3).reshape(*lead, 2 * h, c)


def _hyena_spectra(n, filt_params, mats):
    h = _hyena_filters(n, *filt_params)
    oc = HYENA_ORDER * HYENA_WIDTH
    hf = h[:, :, 0].reshape(n, oc)
    hb = h[:, :, 1].reshape(n, oc)
    hb = jnp.where(jnp.arange(n)[:, None] == 0, 0.0, hb)
    sig = _parity_split(jnp.concatenate([hf + hb, hb - hf], axis=1).astype(BF16))
    rl, il, ru, iu, mid = _dft_raw(mats, sig)
    scale = 1.0 / n
    ends = jnp.where(jnp.arange(n // 2)[:, None] == 0, 0.5 * scale, scale)
    spectra = []
    for o in range(HYENA_ORDER):
        re = slice(o * HYENA_WIDTH, (o + 1) * HYENA_WIDTH)
        im = slice(oc + o * HYENA_WIDTH, oc + (o + 1) * HYENA_WIDTH)
        hmid = jnp.zeros((SUBLANES, HYENA_WIDTH), F32).at[0].set(mid[0, re] * scale).at[1].set(mid[1, im] * scale)
        spectra.append((rl[:, re] * ends, il[:, im] * scale, ru[:, re] * ends, iu[:, im] * scale, hmid))
    return spectra


def _hyena_run(v, x1, x2, spectra, skip, mats):
    y = _parity_split(v)
    for o, gate in enumerate((x1, x2)):
        y = _longconv_gated(y, _parity_split(gate), spectra[o], skip[o], mats)
    return _parity_merge(y)


def _blockdiag_dense(w):
    nb, blk = w.shape[1], w.shape[2]
    eye = jnp.eye(nb, dtype=w.dtype)
    return jnp.einsum('dncf,nm->dncmf', w, eye).reshape(w.shape[0], nb * blk, nb * blk)


def _lru_pre_kernel(p_ref, pv_ref, nx_ref, cw_ref, cb_ref, wa_ref, ba_ref, wx_ref, bx_ref, lam_ref,
                    a_ref, b_ref, *, cl, lt, tr):
    rows = _row_ids(tr, pl.program_id(1))
    c = LRU_WIDTH
    x = p_ref[0][:, c:].astype(F32)
    pv = pv_ref[0][SUBLANES - 1:, c:].astype(F32)
    nx = nx_ref[0][:, c:].astype(F32)
    xc = (_shift_rows(x, pv, -1, rows, cl, lt) * cw_ref[0:1, :] + x * cw_ref[1:2, :]
          + _shift_rows(x, nx[:1], 1, rows, cl, lt) * cw_ref[2:3, :]
          + _shift_rows(x, nx[:2], 2, rows, cl, lt) * cw_ref[3:4, :] + cb_ref[...])
    xcb = xc.astype(BF16)
    for d in range(2):
        r = _sigmoid(_dot(xcb, wa_ref[d]) + ba_ref[d])
        i = _sigmoid(_dot(xcb, wx_ref[d]) + bx_ref[d])
        log_a = -LRU_C * r * _softplus(-lam_ref[d])
        a_ref[d, 0] = jnp.exp(log_a)
        b_ref[d, 0] = jnp.sqrt(1.0 - jnp.exp(2.0 * log_a)) * (i * xc)


def _lru_pre(plr, conv_w, conv_b, wa, ba, wx, bx, lam, cl):
    b, lt, w = plr.shape
    c = LRU_WIDTH
    tr = _largest_tile(lt, 1088, 16)
    prev_spec, next_spec = _halo_specs(tr, lt, w, (1,))
    out = jax.ShapeDtypeStruct((2, b, lt, c), F32)
    ospec = pl.BlockSpec((2, 1, tr, c), lambda i, r: (0, i, r, 0))
    const2 = lambda shape: pl.BlockSpec(shape, lambda i, r: (0,) * len(shape))
    return pl.pallas_call(
        functools.partial(_lru_pre_kernel, cl=cl, lt=lt, tr=tr),
        out_shape=(out, out), grid=(b, lt // tr),
        in_specs=[pl.BlockSpec((1, tr, w), lambda i, r: (i, r, 0)), prev_spec, next_spec,
                  const2(conv_w.shape), const2((1, c)), const2((2, c, c)), const2((2, 1, c)),
                  const2((2, c, c)), const2((2, 1, c)), const2((2, 1, c))],
        out_specs=(ospec, ospec),
        compiler_params=_cparams("parallel", "parallel"), name="lru_pre",
    )(plr, plr, plr, conv_w, conv_b.reshape(1, c), _blockdiag_dense(wa).astype(BF16), ba.reshape(2, 1, c),
      _blockdiag_dense(wx).astype(BF16), bx.reshape(2, 1, c), lam.reshape(2, 1, c))


def _gelu_tanh(x):
    return 0.5 * x * (1.0 + jnp.tanh(math.sqrt(2.0 / math.pi) * (x + 0.044715 * (x * x * x))))


def _lru_scan_kernel(af_ref, bf_ref, ar_ref, br_ref, g_ref, o_ref, acc_ref, *, cl, lt):
    row = lax.broadcasted_iota(jnp.int32, (SUBLANES, LANES), 0)

    def group_scan(a, b, reverse):
        for s in (1, 2, 4):
            if reverse:
                keep = row < SUBLANES - s
                a_s = jnp.where(keep, pltpu.roll(a, SUBLANES - s, 0), 1.0)
                b_s = jnp.where(keep, pltpu.roll(b, SUBLANES - s, 0), 0.0)
            else:
                keep = row >= s
                a_s = jnp.where(keep, pltpu.roll(a, s, 0), 1.0)
                b_s = jnp.where(keep, pltpu.roll(b, s, 0), 0.0)
            b = a * b_s + b
            a = a * a_s
        return a, b

    ng = math.gcd(math.gcd(cl // SUBLANES, (lt - cl) // SUBLANES), LRU_GROUPS_PER_STEP)
    span = ng * SUBLANES

    def fwd_body(i, h):
        sl = pl.ds(pl.multiple_of(i * span, span), span)
        a_all, b_all = af_ref[0, 0, sl, :], bf_ref[0, 0, sl, :]
        scans = [group_scan(a_all[j * SUBLANES:(j + 1) * SUBLANES], b_all[j * SUBLANES:(j + 1) * SUBLANES], False)
                 for j in range(ng)]
        outs = []
        for a, b in scans:
            hh = a * h + b
            outs.append(hh)
            h = hh[SUBLANES - 1:SUBLANES, :]
        acc_ref[sl, :] = jnp.concatenate(outs, axis=0)
        return h

    lax.fori_loop(0, lt // span, fwd_body, jnp.zeros((1, LANES), F32))

    def rev_body(i, h, top):
        sl = pl.ds(pl.multiple_of((top - 1 - i) * span, span), span)
        a_all, b_all = ar_ref[0, 0, sl, :], br_ref[0, 0, sl, :]
        scans = [group_scan(a_all[j * SUBLANES:(j + 1) * SUBLANES], b_all[j * SUBLANES:(j + 1) * SUBLANES], True)
                 for j in range(ng)]
        outs = [None] * ng
        for j in reversed(range(ng)):
            a, b = scans[j]
            hh = a * h + b
            outs[j] = hh
            h = hh[0:1, :]
        gate = g_ref[0, sl, :].astype(F32)
        o_ref[0, sl, :] = ((acc_ref[sl, :] + jnp.concatenate(outs, axis=0)) * _gelu_tanh(gate)).astype(o_ref.dtype)
        return h

    h = lax.fori_loop(0, cl // span, functools.partial(rev_body, top=cl // span), jnp.zeros((1, LANES), F32))
    lax.fori_loop(0, (lt - cl) // span, functools.partial(rev_body, top=lt // span), h)


def _lru_scan(a, b_, plr, cl):
    _, b, lt, c = a.shape
    nl = c // LANES
    fwd = pl.BlockSpec((1, 1, lt, LANES), lambda i, j: (0, i, 0, j))
    rev = pl.BlockSpec((1, 1, lt, LANES), lambda i, j: (1, i, 0, j))
    return pl.pallas_call(
        functools.partial(_lru_scan_kernel, cl=cl, lt=lt),
        out_shape=jax.ShapeDtypeStruct((b, lt, c), BF16), grid=(b, nl),
        in_specs=[fwd, fwd, rev, rev, pl.BlockSpec((1, lt, LANES), lambda i, j: (i, 0, j))],
        out_specs=pl.BlockSpec((1, lt, LANES), lambda i, j: (i, 0, j)),
        scratch_shapes=[pltpu.VMEM((lt, LANES), F32)],
        compiler_params=_cparams("parallel", "parallel"), name="lru_scan",
    )(a, b_, a, b_, plr)


def _row_perm(tr, rev):
    t = lax.broadcasted_iota(jnp.int32, (tr, tr), 0)
    s = lax.broadcasted_iota(jnp.int32, (tr, tr), 1)
    return jnp.where(s == jnp.where(rev, tr - 1 - t, t), 1.0, 0.0).astype(BF16)


def _mirror_tile(r, rev, n_ctx_tiles, n_tiles):
    m = jnp.where(r < n_ctx_tiles, n_ctx_tiles - 1 - r, n_tiles - 1 + n_ctx_tiles - r)
    return jnp.where(rev, m, r)


def _chunk_masks(tr):
    t = lax.broadcasted_iota(jnp.int32, (tr, tr), 0)
    s = lax.broadcasted_iota(jnp.int32, (tr, tr), 1)
    same = (t // RWKV_CHUNK) == (s // RWKV_CHUNK)
    return same, same & (s <= t), same & (s < t)


def _rwkv_prep_kernel(p_ref, pv_ref, nx_ref, mu_ref, w0_ref, w2_ref, a0_ref, a2_ref, g2_ref, kk_ref, ka_ref,
                      rk_ref, ones_ref,
                      aq_ref, vp_ref, y0_ref, rt_ref, mrb_ref, bht_ref, gm_ref, pc_ref, g_ref, bonus_ref,
                      *, cl, lt, tr):
    c = RWKV_WIDTH
    rows = _row_ids(tr, pl.program_id(2))
    rev = pl.program_id(0) == 1
    p = _dot(_row_perm(tr, rev), p_ref[0])
    before = jnp.where(rev, nx_ref[0, :1, :], pv_ref[0, SUBLANES - 1:, :]).astype(F32)
    after = jnp.where(rev, pv_ref[0, SUBLANES - 1:, :], nx_ref[0, :1, :]).astype(F32)
    prev = _shift_rows(p, before, -1, rows, cl, lt)
    nxt = _shift_rows(p, after, 1, rows, cl, lt)
    xm = p + (prev - p) * mu_ref[0, 0:1, :] + (nxt - p) * mu_ref[0, 1:2, :]
    r, k, v = xm[:, :c], xm[:, c:2 * c], xm[:, 2 * c:3 * c]
    o = 3 * c
    w1 = xm[:, o:o + 2 * RWKV_DECAY_RANK]
    a1 = xm[:, o + 2 * RWKV_DECAY_RANK:o + 2 * RWKV_DECAY_RANK + 2 * RWKV_ICLR_RANK]
    g1 = xm[:, o + 2 * RWKV_DECAY_RANK + 2 * RWKV_ICLR_RANK:]
    wlog = -_softplus(-(w0_ref[0] + _dot(jnp.tanh(w1).astype(BF16), w2_ref[0]))) - 0.5
    ld = -jnp.exp(wlog)
    a = _sigmoid(a0_ref[0] + _dot(a1.astype(BF16), a2_ref[0]))
    g_ref[0, 0] = _dot(_sigmoid(g1).astype(BF16), g2_ref[...])
    kk = k * kk_ref[...]
    kk = kk * lax.rsqrt(_dot_exact_rhs(kk * kk, ones_ref[...]) + 1e-12)
    kd = k * (1.0 + (a - 1.0) * ka_ref[...])
    bonus_ref[0, 0] = _dot_exact_rhs(r * kd * rk_ref[...], ones_ref[...]) * v
    beta = kk * a

    _rwkv_chunk_stage(ld, kk, r, kd, beta, v, aq_ref, vp_ref, y0_ref, rt_ref, mrb_ref, bht_ref, gm_ref, pc_ref)


def _rwkv_chunk_stage(ld, kk, r, kd, beta, v, aq_ref, vp_ref, y0_ref, rt_ref, mrb_ref, bht_ref, gm_ref, pc_ref):
    tr = ld.shape[0]
    n = min(RWKV_STAGE_ROWS, tr)
    nparts = tr // n
    hd = HEAD_DIM
    ch = RWKV_CHUNK
    nch = n // ch
    same_t, incl_t, _ = _chunk_masks(tr)
    cum = _dot_exact_lhs(jnp.where(incl_t, 1.0, 0.0).astype(BF16), ld)
    tot = _dot_exact_lhs(jnp.where(same_t, 1.0, 0.0).astype(BF16), ld)
    _, incl, strict = _chunk_masks(n)
    alpha_t = kk * jnp.exp(cum - ld)
    r_t = r * jnp.exp(cum)
    e_neg = jnp.exp(-cum)
    k_t = kd * e_neg
    b_t = beta * e_neg
    e_rem = jnp.exp(tot - cum)
    k_hat_t = (kd * e_rem).T
    b_hat_t = (beta * e_rem).T
    pc_t = jnp.exp(tot).T
    t_i = lax.broadcasted_iota(jnp.int32, (n, n), 0)
    s_i = lax.broadcasted_iota(jnp.int32, (n, n), 1)
    eye_f = jnp.where(t_i == s_i, 1.0, 0.0)
    same_blk = []
    size = RWKV_INV_BASE
    while size <= ch:
        same_blk.append((t_i // size) == (s_i // size))
        size *= 2
    col_chunk = lax.broadcasted_iota(jnp.int32, (n, nch * hd), 1) // hd
    row_chunk = lax.broadcasted_iota(jnp.int32, (n, nch * hd), 0) // ch

    def diag_blocks(m):
        out = m[:, :ch]
        for j in range(1, nch):
            out = out + m[:, j * ch:(j + 1) * ch]
        return out

    chains = [(p, h) for p in range(nparts) for h in range(RWKV_HEADS)]

    def blk(t, p, h):
        return t[p * n:(p + 1) * n, h * hd:(h + 1) * hd]

    prods = [_dot_nt(jnp.concatenate([blk(alpha_t, p, h), blk(r_t, p, h)], axis=0).astype(BF16),
                     jnp.concatenate([blk(b_t, p, h), blk(k_t, p, h)], axis=0).astype(BF16))
             for p, h in chains]
    l_ab = [jnp.where(strict, pr[:n, :n], 0.0) for pr in prods]
    pw = [jnp.where(same_blk[0], l, 0.0) for l in l_ab]
    t_inv = [eye_f - l for l in pw]
    for _ in range(int(math.log2(RWKV_INV_BASE)) - 1):
        pw = [_dot_bf(q, q) for q in pw]
        t_inv = [t + _dot_bf(t, q) for t, q in zip(t_inv, pw)]
    for lvl in range(1, len(same_blk)):
        off = same_blk[lvl] & jnp.logical_not(same_blk[lvl - 1])
        half = [_dot_bf(t, jnp.where(off, l, 0.0)) for t, l in zip(t_inv, l_ab)]
        t_inv = [t - _dot_bf(hf, t) for t, hf in zip(t_inv, half)]
    vh = [blk(v, p, h).astype(BF16) for p, h in chains]
    lakv = [_dot(jnp.where(strict, pr[:n, n:], 0.0).astype(BF16), vv) for pr, vv in zip(prods, vh)]
    x = [_dot(t.astype(BF16), jnp.concatenate([blk(alpha_t, p, h), lv], axis=1).astype(BF16))
         for t, (p, h), lv in zip(t_inv, chains, lakv)]
    y0 = [_dot(jnp.where(incl, pr[n:, n:], 0.0).astype(BF16), vv) for pr, vv in zip(prods, vh)]
    mrb = [diag_blocks(jnp.where(incl, pr[n:, :n], 0.0)) for pr in prods]
    gms = [_dot(k_hat_t[h * hd:(h + 1) * hd, p * n:(p + 1) * n].astype(BF16),
                jnp.where(col_chunk == row_chunk, jnp.concatenate([blk(v, p, h)] * nch, axis=1), 0.0).astype(BF16))
           for p, h in chains]

    def assemble(parts):
        return jnp.concatenate([jnp.concatenate(parts[p * RWKV_HEADS:(p + 1) * RWKV_HEADS], axis=1)
                                for p in range(nparts)], axis=0)

    aq_ref[0, 0] = assemble([t[:, :hd] for t in x]).astype(aq_ref.dtype)
    vp_ref[0, 0] = assemble([t[:, hd:] for t in x])
    y0_ref[0, 0] = assemble(y0)
    rt_ref[0, 0] = r_t.astype(rt_ref.dtype)
    mrb_ref[0, 0] = assemble(mrb).astype(mrb_ref.dtype)
    for p in range(nparts):
        for j in range(nch):
            jj = p * nch + j
            cs = slice(jj * ch, (jj + 1) * ch)
            bht_ref[0, 0, jj] = jnp.concatenate([b_hat_t[h * hd:(h + 1) * hd, cs] for h in range(RWKV_HEADS)],
                                                axis=1).astype(bht_ref.dtype)
            gm_ref[0, 0, jj] = jnp.concatenate([gms[p * RWKV_HEADS + h][:, j * hd:(j + 1) * hd]
                                                for h in range(RWKV_HEADS)], axis=1)
            pc_ref[0, 0, jj] = jnp.concatenate([pc_t[h * hd:(h + 1) * hd, cs] for h in range(RWKV_HEADS)], axis=1)


def _rwkv_tile(lt, cl):
    tr = 256 if (lt % 256 == 0 and cl % 256 == 0) else 128
    assert lt % tr == 0 and cl % tr == 0
    return tr


def _rwkv_prep(prw, mud, w0, w2p, a0, a2p, g2, k_k, k_a, r_k, cl):
    b, lt, w = prw.shape
    c = RWKV_WIDTH
    tr = _rwkv_tile(lt, cl)
    nch = tr // RWKV_CHUNK
    per = tr // SUBLANES
    src = lambda d, r: _mirror_tile(r, d == 1, cl // tr, lt // tr)
    prev_spec = pl.BlockSpec((1, SUBLANES, w), lambda d, i, r: (i, jnp.maximum(src(d, r) * per - 1, 0), 0))
    next_spec = pl.BlockSpec((1, SUBLANES, w),
                             lambda d, i, r: (i, jnp.minimum((src(d, r) + 1) * per, lt // SUBLANES - 1), 0))
    tile = pl.BlockSpec((1, 1, tr, c), lambda d, i, r: (d, i, r, 0))
    per_dir = lambda shape: pl.BlockSpec((1, *shape), lambda d, i, r: (d,) + (0,) * len(shape))
    const = lambda shape: pl.BlockSpec(shape, lambda d, i, r: (0,) * len(shape))
    seq = lambda dt: jax.ShapeDtypeStruct((2, b, lt, c), dt)
    chunked = jax.ShapeDtypeStruct((2, b, lt // RWKV_CHUNK, HEAD_DIM, c), F32)
    chunk_spec = pl.BlockSpec((1, 1, nch, HEAD_DIM, c), lambda d, i, r: (d, i, r, 0, 0))
    return pl.pallas_call(
        functools.partial(_rwkv_prep_kernel, cl=cl, lt=lt, tr=tr),
        out_shape=(seq(BF16), seq(F32), seq(F32), seq(BF16), seq(BF16),
                   jax.ShapeDtypeStruct(chunked.shape, BF16), chunked, chunked, seq(F32), seq(F32)),
        grid=(2, b, lt // tr),
        in_specs=[pl.BlockSpec((1, tr, w), lambda d, i, r: (i, src(d, r), 0)), prev_spec, next_spec,
                  per_dir((2, w)), per_dir((1, c)), per_dir((2 * RWKV_DECAY_RANK, c)), per_dir((1, c)),
                  per_dir((2 * RWKV_ICLR_RANK, c)), const((RWKV_GATE_RANK, c)), const((1, c)), const((1, c)),
                  const((1, c)), const((c, c))],
        out_specs=(tile, tile, tile, tile, tile, chunk_spec, chunk_spec, chunk_spec, tile, tile),
        compiler_params=_cparams("parallel", "parallel", "parallel"), name="rwkv_prep",
    )(prw, prw, prw, mud, w0.reshape(2, 1, c), w2p, a0.reshape(2, 1, c), a2p, g2.astype(BF16), k_k.reshape(1, c),
      k_a.reshape(1, c), r_k.reshape(1, c), _head_ones(c))


def _rwkv_scan_kernel(aq_ref, vp_ref, y0_ref, rt_ref, mrb_ref, bht_ref, gm_ref, pc_ref, y_ref, s_ref, *, ns):
    @pl.when(pl.program_id(1) == 0)
    def _():
        s_ref[...] = jnp.zeros_like(s_ref)

    c = RWKV_WIDTH
    ch = RWKV_CHUNK
    same_head = (lax.broadcasted_iota(jnp.int32, (c, c), 0) // HEAD_DIM
                 == lax.broadcasted_iota(jnp.int32, (c, c), 1) // HEAD_DIM)

    def head_blockdiag(t):
        return jnp.where(same_head, jnp.concatenate([t] * RWKV_HEADS, axis=0), 0.0).astype(BF16)

    for i in range(ns):
        s0 = s_ref[i]
        r1 = _dot(jnp.concatenate([aq_ref[i, 0], rt_ref[i, 0]], axis=0), head_blockdiag(s0))
        u = r1[:ch] + vp_ref[i, 0]
        r2 = _dot(jnp.concatenate([mrb_ref[i, 0], bht_ref[i, 0, 0]], axis=0), head_blockdiag(u))
        y_ref[i, 0] = r1[ch:] + y0_ref[i, 0] - r2[:ch]
        s_ref[i] = pc_ref[i, 0, 0] * s0 + gm_ref[i, 0, 0] - r2[ch:]


def _rwkv_scan(aq, vp, y0, rt, mrb, bht, gm, pc):
    _, b, lt, c = aq.shape
    nstream = 2 * b
    ns = nstream
    ch = RWKV_CHUNK
    merge = lambda t: t.reshape(nstream, 1, *t.shape[2:])
    tile = pl.BlockSpec((ns, 1, ch, c), lambda s, j: (s, 0, j, 0))
    chunk_spec = pl.BlockSpec((ns, 1, 1, HEAD_DIM, c), lambda s, j: (s, 0, j, 0, 0))
    y = pl.pallas_call(
        functools.partial(_rwkv_scan_kernel, ns=ns),
        out_shape=jax.ShapeDtypeStruct((nstream, 1, lt, c), F32), grid=(nstream // ns, lt // ch),
        in_specs=[tile, tile, tile, tile, tile, chunk_spec, chunk_spec, chunk_spec],
        out_specs=tile,
        scratch_shapes=[pltpu.VMEM((ns, HEAD_DIM, c), F32)],
        compiler_params=_cparams("parallel", "arbitrary"), name="rwkv_scan",
    )(*(merge(t) for t in (aq, vp, y0, rt, mrb, bht, gm, pc)))
    return y.reshape(2, b, lt, c)


def _rwkv_readout_kernel(yf_ref, yr_ref, bf_ref, br_ref, g_ref, lw_ref, lb_ref, ones_ref, o_ref, *, tr):
    unflip = _row_perm(tr, True)
    y = yf_ref[0, 0] + _dot_exact_lhs(unflip, yr_ref[0, 0])
    bonus = bf_ref[0, 0] + _dot_exact_lhs(unflip, br_ref[0, 0])
    inv = 1.0 / HEAD_DIM
    mu = _dot_exact_rhs(y, ones_ref[...]) * inv
    yc = y - mu
    var = _dot_exact_rhs(yc * yc, ones_ref[...]) * inv
    yn = yc * lax.rsqrt(var + RWKV_GN_EPS) * lw_ref[...] + lb_ref[...]
    o_ref[0] = ((yn + bonus) * g_ref[0, 0]).astype(o_ref.dtype)


def _rwkv_readout(y, bonus, g, ln_w, ln_b, cl):
    _, b, lt, c = y.shape
    tr = _rwkv_tile(lt, cl)
    fwd = pl.BlockSpec((1, 1, tr, c), lambda i, r: (0, i, r, 0))
    rev = pl.BlockSpec((1, 1, tr, c), lambda i, r: (1, i, _mirror_tile(r, True, cl // tr, lt // tr), 0))
    row = pl.BlockSpec((1, c), lambda i, r: (0, 0))
    return pl.pallas_call(
        functools.partial(_rwkv_readout_kernel, tr=tr),
        out_shape=jax.ShapeDtypeStruct((b, lt, c), BF16), grid=(b, lt // tr),
        in_specs=[fwd, rev, fwd, rev, fwd, row, row, pl.BlockSpec((c, c), lambda i, r: (0, 0))],
        out_specs=pl.BlockSpec((1, tr, c), lambda i, r: (i, r, 0)),
        compiler_params=_cparams("parallel", "parallel"), name="rwkv_readout",
    )(y, y, bonus, bonus, g, ln_w.reshape(1, c), ln_b.reshape(1, c), _head_ones(c))


def _pad_rank_rows(w):
    z = jnp.zeros_like(w[0])
    return jnp.stack([jnp.concatenate([w[0], z], 0), jnp.concatenate([z, w[1]], 0)])


def _rwkv_mixer(prw, mu, w0, w2, a0, a2, g2, k_k, k_a, r_k, ln_w, ln_b, cl):
    mud = jnp.stack([mu, mu[::-1]])
    outs = _rwkv_prep(prw, mud, w0, _pad_rank_rows(w2).astype(BF16), a0, _pad_rank_rows(a2).astype(BF16), g2,
                      k_k, k_a, r_k, cl)
    *scan_in, g, bonus = outs
    y = _rwkv_scan(*scan_in)
    return _rwkv_readout(y, bonus, g, ln_w, ln_b, cl)


def _merge_kernel(ya_ref, yh_ref, yr_ref, yl_ref, gt_ref, x_ref, modx_ref, modc_ref,
                  wa_ref, wh_ref, wr_ref, wl_ref, wo_ref, o_ref, *, cl, tm, d):
    is_ctx = _row_ids(tm, pl.program_id(1)) < cl
    m = None
    for i, (y_ref, w_ref) in enumerate(((ya_ref, wa_ref), (yh_ref, wh_ref), (yr_ref, wr_ref), (yl_ref, wl_ref))):
        gate = _sigmoid(gt_ref[0, :, i * d:(i + 1) * d].astype(F32))
        term = gate * _dot(y_ref[0], w_ref[...])
        m = term if m is None else m + term
    g1 = _mod_rows(modx_ref, modc_ref, 2, is_ctx)
    o_ref[0] = x_ref[0] + g1 * _dot(m.astype(BF16), wo_ref[...])


def _merge(ys, gates, xc, mod, w_brs, w_out, cl):
    b, lt, d = xc.shape
    tm = _largest_tile(lt, 544, 16)
    row = lambda w: pl.BlockSpec((1, tm, w), lambda i, r: (i, r, 0))
    const = lambda a: pl.BlockSpec(a.shape, lambda i, r: (0, 0))
    ws = [w.astype(BF16) for w in w_brs] + [w_out.astype(BF16)]
    return pl.pallas_call(
        functools.partial(_merge_kernel, cl=cl, tm=tm, d=d),
        out_shape=jax.ShapeDtypeStruct((b, lt, d), F32), grid=(b, lt // tm),
        in_specs=[row(y.shape[-1]) for y in ys] + [row(N_BRANCH * d), row(d),
                  pl.BlockSpec((1, 6, d), lambda i, r: (i, 0, 0)), pl.BlockSpec((1, 6, d), lambda i, r: (b, 0, 0))]
                 + [const(w) for w in ws],
        out_specs=row(d), compiler_params=_cparams("parallel", "parallel"), name="merge",
    )(*ys, gates, xc, mod, mod, *ws)


def _route_kernel(x_ref, modx_ref, modc_ref, wr_ref, br_ref, h_ref, cmb_ref, *, cl, tm):
    is_ctx = _row_ids(tm, pl.program_id(1)) < cl
    h = _rms_modulate(x_ref[0], _mod_rows(modx_ref, modc_ref, 3, is_ctx), _mod_rows(modx_ref, modc_ref, 4, is_ctx))
    h_ref[0] = h.astype(h_ref.dtype)
    lg = jnp.dot(h, wr_ref[...], preferred_element_type=F32, precision=lax.Precision.HIGHEST) + br_ref[...]
    lane = lax.broadcasted_iota(jnp.int32, lg.shape, 1)
    lane_f = lane.astype(F32)
    neg = -jnp.inf
    big = 1e9

    def first_lane(cond):
        return jnp.min(jnp.where(cond, lane_f, big), axis=-1, keepdims=True)

    is_grp = (lane >= N_EXPERTS) & (lane < N_EXPERTS + N_GROUPS)
    gl = jnp.where(is_grp, lg, neg)
    gmax = jnp.max(gl, axis=-1, keepdims=True)
    ge = jnp.where(is_grp, jnp.exp(gl - gmax), 0.0)
    gp = ge / jnp.sum(ge, axis=-1, keepdims=True)
    g_val = jnp.max(gp, axis=-1, keepdims=True)
    g_idx = first_lane(is_grp & (gp == g_val)) - N_EXPERTS
    lo = g_idx * EXPERTS_PER_GROUP
    in_grp = (lane_f >= lo) & (lane_f < lo + EXPERTS_PER_GROUP)
    el = jnp.where(in_grp, lg, neg)
    emax = jnp.max(el, axis=-1, keepdims=True)
    ee = jnp.where(in_grp, jnp.exp(el - emax), 0.0)
    pe = ee / jnp.sum(ee, axis=-1, keepdims=True)
    v1 = jnp.max(jnp.where(in_grp, pe, -1.0), axis=-1, keepdims=True)
    i1 = first_lane(in_grp & (pe == v1))
    rest = in_grp & (lane_f != i1)
    v2 = jnp.max(jnp.where(rest, pe, -1.0), axis=-1, keepdims=True)
    i2 = first_lane(rest & (pe == v2))
    den = v1 + v2
    cmb_ref[0] = (jnp.where(lane_f == i1, g_val * v1 / den, 0.0) + jnp.where(lane_f == i2, g_val * v2 / den, 0.0)
                  + jnp.where(lane == N_EXPERTS, g_idx, 0.0))


def _route(xc, mod, w_grp, b_grp, w_rt, b_rt, cl):
    b, lt, d = xc.shape
    tm = _largest_tile(lt, 544, 16)
    wr = jnp.zeros((d, LANES), F32).at[:, :N_EXPERTS].set(w_rt).at[:, N_EXPERTS:N_EXPERTS + N_GROUPS].set(w_grp)
    br = jnp.zeros((1, LANES), F32).at[0, :N_EXPERTS].set(b_rt).at[0, N_EXPERTS:N_EXPERTS + N_GROUPS].set(b_grp)
    return pl.pallas_call(
        functools.partial(_route_kernel, cl=cl, tm=tm),
        out_shape=(jax.ShapeDtypeStruct((b, lt, d), BF16), jax.ShapeDtypeStruct((b, lt, LANES), F32)),
        grid=(b, lt // tm),
        in_specs=[pl.BlockSpec((1, tm, d), lambda i, r: (i, r, 0)),
                  pl.BlockSpec((1, 6, d), lambda i, r: (i, 0, 0)), pl.BlockSpec((1, 6, d), lambda i, r: (b, 0, 0)),
                  pl.BlockSpec((d, LANES), lambda i, r: (0, 0)), pl.BlockSpec((1, LANES), lambda i, r: (0, 0))],
        out_specs=(pl.BlockSpec((1, tm, d), lambda i, r: (i, r, 0)), pl.BlockSpec((1, tm, LANES), lambda i, r: (i, r, 0))),
        compiler_params=_cparams("parallel", "parallel"), name="moe_route",
    )(xc, mod, mod, wr, br)


def _moe_kernel(grp_ref, nvalid_ref, h_ref, cmb_ref, w1_ref, w3_ref, w2_ref, o_ref, acc_ref, *, bm):
    j = pl.program_id(0)
    e = pl.program_id(1)

    @pl.when(e == 0)
    def _():
        acc_ref[...] = jnp.zeros_like(acc_ref)

    @pl.when(j < nvalid_ref[0])
    def _():
        h = h_ref[...]
        t = _silu(_dot(h, w1_ref[0, 0])) * _dot(h, w3_ref[0, 0])
        y = _dot(t.astype(BF16), w2_ref[0, 0])
        lane = lax.broadcasted_iota(jnp.int32, (bm, LANES), 1)
        expert = grp_ref[j] * EXPERTS_PER_GROUP + e
        wcol = jnp.sum(jnp.where(lane == expert, cmb_ref[...], 0.0), axis=-1, keepdims=True)
        acc_ref[...] += wcol * y

    @pl.when(e == pl.num_programs(1) - 1)
    def _():
        o_ref[...] = acc_ref[...].astype(o_ref.dtype)


def _moe_grouped(hs, ws, blk_group, nvalid, w1, w3, w2, layer):
    s_rows, d = hs.shape
    hid = w1.shape[3]
    bm = MOE_BLOCK_ROWS
    wspec = lambda shape: pl.BlockSpec(shape, lambda j, e, grp, nv: (layer, grp[j] * EXPERTS_PER_GROUP + e, 0, 0))
    return pl.pallas_call(
        functools.partial(_moe_kernel, bm=bm),
        out_shape=jax.ShapeDtypeStruct((s_rows, d), BF16),
        grid_spec=pltpu.PrefetchScalarGridSpec(
            num_scalar_prefetch=2, grid=(s_rows // bm, EXPERTS_PER_GROUP),
            in_specs=[pl.BlockSpec((bm, d), lambda j, e, grp, nv: (j, 0)),
                      pl.BlockSpec((bm, LANES), lambda j, e, grp, nv: (j, 0)),
                      wspec((1, 1, d, hid)), wspec((1, 1, d, hid)), wspec((1, 1, hid, d))],
            out_specs=pl.BlockSpec((bm, d), lambda j, e, grp, nv: (j, 0)),
            scratch_shapes=[pltpu.VMEM((bm, d), F32)]),
        compiler_params=_cparams("parallel", "arbitrary"), name="moe_experts",
    )(blk_group, nvalid, hs, ws, w1, w3, w2)


def _moe_finish_kernel(x_ref, y_ref, modx_ref, modc_ref, o_ref, *, cl, tm):
    is_ctx = _row_ids(tm, pl.program_id(1)) < cl
    o_ref[0] = x_ref[0] + _mod_rows(modx_ref, modc_ref, 5, is_ctx) * y_ref[0].astype(F32)


def _moe_finish(xc, y, mod, cl):
    b, lt, d = xc.shape
    tm = _largest_tile(lt, 1088, 16)
    tile = pl.BlockSpec((1, tm, d), lambda i, r: (i, r, 0))
    return pl.pallas_call(
        functools.partial(_moe_finish_kernel, cl=cl, tm=tm),
        out_shape=jax.ShapeDtypeStruct((b, lt, d), F32), grid=(b, lt // tm),
        in_specs=[tile, tile, pl.BlockSpec((1, 6, d), lambda i, r: (i, 0, 0)),
                  pl.BlockSpec((1, 6, d), lambda i, r: (b, 0, 0))],
        out_specs=tile, compiler_params=_cparams("parallel", "parallel"), name="moe_finish",
    )(xc, y, mod, mod)


def _moe(h2, cmb, xc, mod, w1, w3, w2, layer, cl):
    b, lt, d = xc.shape
    t = b * lt
    bm = MOE_BLOCK_ROWS
    s_rows = -(-t // bm) * bm + N_GROUPS * bm
    cmb2 = cmb.reshape(t, LANES)
    gid = cmb2[:, N_EXPERTS].astype(jnp.int32)
    onehot = (gid[:, None] == jnp.arange(N_GROUPS, dtype=jnp.int32)).astype(jnp.int32)
    csum = jnp.cumsum(onehot, axis=0)
    rank = jnp.sum(onehot * csum, axis=1) - 1
    padded = -(-csum[-1] // bm) * bm
    ends = jnp.cumsum(padded)
    pos = (ends - padded)[gid] + rank
    slot_token = jnp.zeros((s_rows,), jnp.int32).at[pos].set(jnp.arange(t, dtype=jnp.int32))
    starts = jnp.arange(s_rows // bm, dtype=jnp.int32) * bm
    blk_group = jnp.minimum(jnp.sum((starts[:, None] >= ends[None, :]).astype(jnp.int32), axis=1), N_GROUPS - 1)
    nvalid = (ends[-1] // bm).reshape(1).astype(jnp.int32)
    take_rows = lambda a, idx: a.at[idx].get(mode="promise_in_bounds")
    hs = take_rows(h2.reshape(t, d), slot_token)
    ws = take_rows(cmb2, slot_token)
    ys = _moe_grouped(hs, ws, blk_group.astype(jnp.int32), nvalid, w1, w3, w2, layer)
    y = take_rows(ys, pos).reshape(b, lt, d)
    return _moe_finish(xc, y, mod, cl)


def kernel(x, c, ctx, c_ctx, ada_w, ada_b, w_in, q_norm, k_norm, hy_conv_w, hy_conv_b, hy_f1, hy_fb1, hy_f2, hy_fb2, hy_f3, hy_skip, rw_mu, rw_w0, rw_w2, rw_a0, rw_a2, rw_g2, rw_k_k, rw_k_a, rw_r_k, rw_ln_w, rw_ln_b, lru_conv_w, lru_conv_b, lru_wa, lru_ba, lru_wx, lru_bx, lru_lambda, w_br_attn, w_br_hyena, w_br_rwkv, w_br_lru, w_out, moe_w_grp, moe_b_grp, moe_w_rt, moe_b_rt, moe_w1, moe_w3, moe_w2):
    b, l, d = x.shape
    cl = ctx.shape[1]
    depth = ada_w.shape[0]
    assert b < MOD_ROWS and cl % RWKV_CHUNK == 0 and l % RWKV_CHUNK == 0

    xc = jnp.concatenate([ctx, x], axis=1)
    cc = jnp.zeros((MOD_ROWS, d), F32).at[:b].set(c).at[b].set(c_ctx)
    mod_all = _ada_mod(cc, ada_w, ada_b).reshape(depth, MOD_ROWS, 6, d)

    moe_w1b, moe_w3b, moe_w2b = (w.astype(BF16) for w in (moe_w1, moe_w3, moe_w2))
    cos2, sin2 = _rope_tables(l, cl)
    mats_x = _dft_mats(l)
    mats_c = _dft_mats(cl)
    qkv_w = ATTN_WIDTH + 2 * ATTN_KV_WIDTH
    col = np.cumsum([0, qkv_w, 3 * HYENA_WIDTH, RWKV_PROJ, 2 * LRU_WIDTH, N_BRANCH * d])

    for i in range(depth):
        need_ctx = i < depth - 1
        mod = mod_all[i]
        w_i = w_in[i].astype(BF16)
        h1 = _modnorm(xc, mod, cl)
        pqkv, phy, prw, plr, gates = (_proj(h1, w_i[:, col[j]:col[j + 1]]) for j in range(5))

        qn, kt, vx = _attn_prep(pqkv, cos2, sin2, q_norm[i], k_norm[i])
        y_att_x = _attention(qn, kt, vx, cl, 'x', cl + l)
        if need_ctx:
            y_att_c = _attention(qn, kt, vx, cl, 'ctx', cl)
        else:
            y_att_c = jnp.zeros((b, cl, ATTN_WIDTH), BF16)
        y_att = jnp.concatenate([y_att_c, y_att_x], axis=1)

        hv, hx1, hx2 = _hyena_pre(phy, hy_conv_w[i], hy_conv_b[i], cl)
        filt = (hy_f1[i], hy_fb1[i], hy_f2[i], hy_fb2[i], hy_f3[i])
        y_hx = _hyena_run(hv[:, cl:], hx1[:, cl:], hx2[:, cl:], _hyena_spectra(l, filt, mats_x), hy_skip[i], mats_x)
        if need_ctx:
            y_hc = _hyena_run(hv[:, :cl], hx1[:, :cl], hx2[:, :cl], _hyena_spectra(cl, filt, mats_c), hy_skip[i],
                              mats_c)
        else:
            y_hc = jnp.zeros((b, cl, HYENA_WIDTH), BF16)
        y_hy = jnp.concatenate([y_hc, y_hx], axis=1)

        y_rw = _rwkv_mixer(prw, rw_mu[i], rw_w0[i], rw_w2[i], rw_a0[i], rw_a2[i], rw_g2[i], rw_k_k[i], rw_k_a[i],
                           rw_r_k[i].reshape(-1), rw_ln_w[i], rw_ln_b[i], cl)

        la, lb = _lru_pre(plr, lru_conv_w[i], lru_conv_b[i], lru_wa[i], lru_ba[i], lru_wx[i], lru_bx[i],
                          lru_lambda[i], cl)
        y_lr = _lru_scan(la, lb, plr, cl)

        xc = _merge((y_att, y_hy, y_rw, y_lr), gates, xc, mod,
                    (w_br_attn[i], w_br_hyena[i], w_br_rwkv[i], w_br_lru[i]), w_out[i], cl)

        h2, cmb = _route(xc, mod, moe_w_grp[i], moe_b_grp[i], moe_w_rt[i], moe_b_rt[i], cl)
        xc = _moe(h2, cmb, xc, mod, moe_w1b, moe_w3b, moe_w2b, i, cl)
    return xc[:, cl:]
```

```python
import functools
import math

import numpy as np
import jax
import jax.numpy as jnp
from jax import lax
from jax.experimental import pallas as pl
from jax.experimental.pallas import tpu as pltpu

F32 = jnp.float32
BF16 = jnp.bfloat16

HEAD_DIM = 64
GRID_W = 64
EPS = 1e-6
ATTN_HEADS = 8
ATTN_KV_HEADS = 2
ATTN_GROUP = ATTN_HEADS // ATTN_KV_HEADS
ATTN_Q_BLOCKS = 2
ATTN_WIDTH = ATTN_HEADS * HEAD_DIM
ATTN_KV_WIDTH = ATTN_KV_HEADS * HEAD_DIM
ROPE_THETA = 10000.0
HYENA_WIDTH = 256
HYENA_ORDER = 2
HYENA_BANDS = 16
HYENA_DECAY_TARGET = 1e-2
HYENA_FAST_DECAY = 0.3
HYENA_SLOW_DECAY = 1.5
RWKV_HEADS = 4
RWKV_WIDTH = RWKV_HEADS * HEAD_DIM
RWKV_DECAY_RANK = 64
RWKV_ICLR_RANK = 64
RWKV_GATE_RANK = 128
RWKV_GN_EPS = 64e-5
RWKV_PROJ = 3 * RWKV_WIDTH + 2 * RWKV_DECAY_RANK + 2 * RWKV_ICLR_RANK + RWKV_GATE_RANK
RWKV_CHUNK = 64
RWKV_INV_BASE = 4
RWKV_STAGE_ROWS = 128
LRU_WIDTH = 256
LRU_BLOCKS = 4
LRU_C = 8.0
LRU_GROUPS_PER_STEP = 4
N_BRANCH = 4
N_GROUPS = 4
EXPERTS_PER_GROUP = 4
N_EXPERTS = N_GROUPS * EXPERTS_PER_GROUP
MOE_BLOCK_ROWS = 512

V7X_VMEM_LIMIT_BYTES = 52 * 1024 * 1024
SUBLANES = 8
LANES = 128
MOD_ROWS = 16


def _cparams(*sem):
    return pltpu.CompilerParams(dimension_semantics=sem, vmem_limit_bytes=V7X_VMEM_LIMIT_BYTES)


def _dot(a, b):
    return jnp.dot(a, b, preferred_element_type=F32)


def _dot_nt(a, b):
    return lax.dot_general(a, b, (((1,), (1,)), ((), ())), preferred_element_type=F32)


def _split3(x):
    hi = x.astype(BF16)
    r1 = x - hi.astype(F32)
    mid = r1.astype(BF16)
    lo = (r1 - mid.astype(F32)).astype(BF16)
    return hi, mid, lo


def _dot_exact_lhs(m_bf16, x):
    hi, mid, lo = _split3(x)
    return _dot(m_bf16, hi) + _dot(m_bf16, mid) + _dot(m_bf16, lo)


def _dot_exact_rhs(x, m_bf16):
    hi = x.astype(BF16)
    lo = (x - hi.astype(F32)).astype(BF16)
    return _dot(hi, m_bf16) + _dot(lo, m_bf16)


def _dot_bf(a, b):
    return _dot(a.astype(BF16), b.astype(BF16))


def _sigmoid(x):
    return 0.5 * jnp.tanh(0.5 * x) + 0.5


def _softplus(x):
    return jnp.maximum(x, 0.0) + jnp.log(1.0 + jnp.exp(-jnp.abs(x)))


def _silu(x):
    return x * _sigmoid(x)


def _largest_tile(n, cap, mult):
    best = None
    for t in range(mult, min(n, cap) + 1, mult):
        if n % t == 0:
            best = t
    assert best is not None, (n, cap, mult)
    return best


def _head_ones(width):
    idx = np.arange(width) // HEAD_DIM
    return jnp.asarray((idx[:, None] == idx[None, :]).astype(np.float32), dtype=BF16)


def _row_ids(tile_rows, tile_idx):
    return tile_idx * tile_rows + lax.broadcasted_iota(jnp.int32, (tile_rows, 1), 0)


def _mod_rows(modx_ref, modc_ref, idx, is_ctx):
    return jnp.where(is_ctx, modc_ref[0, idx:idx + 1, :], modx_ref[0, idx:idx + 1, :])


def _rms_modulate(x, shift, scale):
    ms = jnp.mean(x * x, axis=-1, keepdims=True)
    return (x * lax.rsqrt(ms + EPS)) * (1.0 + scale) + shift


def _ada_kernel(c_ref, w_ref, b_ref, o_ref):
    s = _silu(c_ref[...])
    o_ref[0] = jnp.dot(s, w_ref[0], preferred_element_type=F32, precision=lax.Precision.HIGHEST) + b_ref[0]


def _ada_mod(cc, ada_w, ada_b):
    depth, d, n6 = ada_w.shape
    tn = _largest_tile(n6, 1024, LANES)
    return pl.pallas_call(
        _ada_kernel,
        out_shape=jax.ShapeDtypeStruct((depth, MOD_ROWS, n6), F32),
        grid=(depth, n6 // tn),
        in_specs=[pl.BlockSpec((MOD_ROWS, d), lambda i, j: (0, 0)),
                  pl.BlockSpec((1, d, tn), lambda i, j: (i, 0, j)),
                  pl.BlockSpec((1, 1, tn), lambda i, j: (i, 0, j))],
        out_specs=pl.BlockSpec((1, MOD_ROWS, tn), lambda i, j: (i, 0, j)),
        compiler_params=_cparams("parallel", "parallel"),
        name="ada_mod",
    )(cc, ada_w, ada_b.reshape(depth, 1, n6))


def _modnorm_kernel(x_ref, modx_ref, modc_ref, o_ref, *, cl, tm):
    is_ctx = _row_ids(tm, pl.program_id(1)) < cl
    h = _rms_modulate(x_ref[0], _mod_rows(modx_ref, modc_ref, 0, is_ctx), _mod_rows(modx_ref, modc_ref, 1, is_ctx))
    o_ref[0] = h.astype(o_ref.dtype)


def _modnorm(xc, mod, cl):
    b, lt, d = xc.shape
    tm = _largest_tile(lt, 1088, 16)
    return pl.pallas_call(
        functools.partial(_modnorm_kernel, cl=cl, tm=tm),
        out_shape=jax.ShapeDtypeStruct((b, lt, d), BF16), grid=(b, lt // tm),
        in_specs=[pl.BlockSpec((1, tm, d), lambda i, r: (i, r, 0)),
                  pl.BlockSpec((1, 6, d), lambda i, r: (i, 0, 0)),
                  pl.BlockSpec((1, 6, d), lambda i, r: (b, 0, 0))],
        out_specs=pl.BlockSpec((1, tm, d), lambda i, r: (i, r, 0)),
        compiler_params=_cparams("parallel", "parallel"), name="modnorm",
    )(xc, mod, mod)


def _proj_kernel(h_ref, w_ref, o_ref):
    o_ref[0] = _dot(h_ref[0], w_ref[...]).astype(o_ref.dtype)


def _proj(h, w):
    b, lt, d = h.shape
    n = w.shape[1]
    tm = _largest_tile(lt, 2176, 16)
    tn = n if n <= 1280 else _largest_tile(n, 1024, 2 * LANES)
    return pl.pallas_call(
        _proj_kernel, out_shape=jax.ShapeDtypeStruct((b, lt, n), BF16), grid=(b, lt // tm, n // tn),
        in_specs=[pl.BlockSpec((1, tm, d), lambda i, r, j: (i, r, 0)), pl.BlockSpec((d, tn), lambda i, r, j: (0, j))],
        out_specs=pl.BlockSpec((1, tm, tn), lambda i, r, j: (i, r, j)),
        compiler_params=_cparams("parallel", "parallel", "parallel"), name="proj",
    )(h, w)


def _rope_tables(l, cl):
    n_freq = HEAD_DIM // 4
    t = jnp.arange(l)
    freqs = ROPE_THETA ** (-jnp.arange(n_freq, dtype=F32) / n_freq)
    pos = jnp.stack([t // GRID_W, t % GRID_W], -1).astype(F32)
    ang = pos[..., None] * freqs
    cos64 = jnp.stack([jnp.cos(ang), jnp.cos(ang)], axis=2).reshape(l, HEAD_DIM)
    sin64 = jnp.stack([-jnp.sin(ang), jnp.sin(ang)], axis=2).reshape(l, HEAD_DIM)
    cos64 = jnp.concatenate([jnp.ones((cl, HEAD_DIM), F32), cos64], 0)
    sin64 = jnp.concatenate([jnp.zeros((cl, HEAD_DIM), F32), sin64], 0)
    return jnp.tile(cos64, (1, 2)), jnp.tile(sin64, (1, 2))


def _head_rms(t, ones_ref):
    ms = _dot_exact_rhs(t * t, ones_ref[...]) * (1.0 / HEAD_DIM)
    return t * lax.rsqrt(ms + EPS)


def _rope(t, cos, sin):
    w = t.shape[-1]
    lane = lax.broadcasted_iota(jnp.int32, t.shape, 1)
    q4 = HEAD_DIM // 4
    first_half = (lane % (2 * q4)) < q4
    partner = jnp.where(first_half, pltpu.roll(t, w - q4, 1), pltpu.roll(t, q4, 1))
    return t * cos + partner * sin


def _attn_prep_kernel(p_ref, cos_ref, sin_ref, qg_ref, kg_ref, oq_ref, ok_ref, q_ref, kt_ref, vx_ref):
    p = p_ref[0].astype(F32)
    v = p[:, ATTN_WIDTH + ATTN_KV_WIDTH:]
    low = lax.broadcasted_iota(jnp.int32, v.shape, 1) < HEAD_DIM
    vx_ref[0, 0] = jnp.where(low, v, 1.0).astype(vx_ref.dtype)
    vx_ref[0, 1] = jnp.where(low, pltpu.roll(v, HEAD_DIM, 1), 1.0).astype(vx_ref.dtype)
    cos2, sin2 = cos_ref[...], sin_ref[...]
    reps = ATTN_WIDTH // (2 * HEAD_DIM)
    cos_q = jnp.concatenate([cos2] * reps, axis=1)
    sin_q = jnp.concatenate([sin2] * reps, axis=1)
    q = _head_rms(p[:, :ATTN_WIDTH], oq_ref) * qg_ref[...]
    q_ref[0] = _rope(q, cos_q, sin_q).astype(q_ref.dtype)
    k = _head_rms(p[:, ATTN_WIDTH:ATTN_WIDTH + ATTN_KV_WIDTH], ok_ref) * kg_ref[...]
    kt_ref[0] = _rope(k, cos2, sin2).T.astype(kt_ref.dtype)


def _attn_prep(pqkv, cos2, sin2, q_gain, k_gain):
    b, lt, wtot = pqkv.shape
    tr = _largest_tile(lt, 2176, LANES)
    qg = jnp.tile(q_gain * (HEAD_DIM ** -0.5 * math.log2(math.e)), ATTN_HEADS).reshape(1, ATTN_WIDTH)
    kg = jnp.tile(k_gain, ATTN_KV_HEADS).reshape(1, ATTN_KV_WIDTH)
    return pl.pallas_call(
        _attn_prep_kernel,
        out_shape=(jax.ShapeDtypeStruct((b, lt, ATTN_WIDTH), BF16),
                   jax.ShapeDtypeStruct((b, ATTN_KV_WIDTH, lt), BF16),
                   jax.ShapeDtypeStruct((b, ATTN_KV_HEADS, lt, 2 * HEAD_DIM), BF16)),
        grid=(b, lt // tr),
        in_specs=[pl.BlockSpec((1, tr, wtot), lambda i, r: (i, r, 0)),
                  pl.BlockSpec((tr, 2 * HEAD_DIM), lambda i, r: (r, 0)),
                  pl.BlockSpec((tr, 2 * HEAD_DIM), lambda i, r: (r, 0)),
                  pl.BlockSpec((1, ATTN_WIDTH), lambda i, r: (0, 0)),
                  pl.BlockSpec((1, ATTN_KV_WIDTH), lambda i, r: (0, 0)),
                  pl.BlockSpec((ATTN_WIDTH, ATTN_WIDTH), lambda i, r: (0, 0)),
                  pl.BlockSpec((ATTN_KV_WIDTH, ATTN_KV_WIDTH), lambda i, r: (0, 0))],
        out_specs=(pl.BlockSpec((1, tr, ATTN_WIDTH), lambda i, r: (i, r, 0)),
                   pl.BlockSpec((1, ATTN_KV_WIDTH, tr), lambda i, r: (i, 0, r)),
                   pl.BlockSpec((1, ATTN_KV_HEADS, tr, 2 * HEAD_DIM), lambda i, r: (i, 0, r, 0))),
        compiler_params=_cparams("parallel", "parallel"),
        name="attn_prep",
    )(pqkv, cos2, sin2, qg, kg, _head_ones(ATTN_WIDTH), _head_ones(ATTN_KV_WIDTH))


def _attn_kernel(*refs, nq, nk):
    q_refs, (kt_ref, v_ref, o_ref) = refs[:nq], refs[nq:]
    outs = []
    for h in range(ATTN_HEADS):
        kv = h // ATTN_GROUP
        hs = slice(h * HEAD_DIM, (h + 1) * HEAD_DIM)
        qh = jnp.concatenate([q_ref[0, :, hs] for q_ref in q_refs], axis=0)
        s = _dot(qh, kt_ref[0, kv * HEAD_DIM:(kv + 1) * HEAD_DIM, :nk])
        m = jnp.max(s, axis=-1, keepdims=True)
        p = jnp.exp2(s - m)
        o = _dot(p.astype(BF16), v_ref[0, kv, :nk, :])
        outs.append(o[:, :HEAD_DIM] / o[:, HEAD_DIM:])
    o_ref[0] = jnp.concatenate(outs, axis=-1).astype(o_ref.dtype)


def _attention(qn, kt, vx, cl, rows, nk):
    b, lt, _ = qn.shape
    blk = 256 if (cl % 256 == 0 and lt % 256 == 0) else 128
    if rows == 'ctx':
        n_rows, first, nq = cl, 0, 1
    else:
        n_rows, first = lt - cl, cl // blk
        nq = ATTN_Q_BLOCKS if n_rows % (ATTN_Q_BLOCKS * blk) == 0 else 1
    tq = nq * blk
    q_specs = [pl.BlockSpec((1, blk, ATTN_WIDTH), lambda i, t, j=j: (i, first + nq * t + j, 0)) for j in range(nq)]
    return pl.pallas_call(
        functools.partial(_attn_kernel, nq=nq, nk=nk),
        out_shape=jax.ShapeDtypeStruct((b, n_rows, ATTN_WIDTH), BF16),
        grid=(b, n_rows // tq),
        in_specs=q_specs + [pl.BlockSpec((1, ATTN_KV_WIDTH, lt), lambda i, t: (i, 0, 0)),
                            pl.BlockSpec((1, ATTN_KV_HEADS, lt, 2 * HEAD_DIM), lambda i, t: (i, 0, 0, 0))],
        out_specs=pl.BlockSpec((1, tq, ATTN_WIDTH), lambda i, t: (i, t, 0)),
        compiler_params=_cparams("parallel", "parallel"),
        name="attention_" + rows,
    )(*([qn] * nq), kt, vx)


def _halo_specs(tr, lt, width, lead):
    per = tr // SUBLANES
    last = lt // SUBLANES - 1
    nlead = len(lead)

    def prev_map(*ids):
        return (*ids[:nlead], jnp.maximum(ids[nlead] * per - 1, 0), 0)

    def next_map(*ids):
        return (*ids[:nlead], jnp.minimum((ids[nlead] + 1) * per, last), 0)

    blk = (*lead, SUBLANES, width)
    return pl.BlockSpec(blk, prev_map), pl.BlockSpec(blk, next_map)


def _shift_rows(x, halo, offset, rows, cl, lt):
    tr = x.shape[0]
    local = lax.broadcasted_iota(jnp.int32, (tr, 1), 0)
    if offset < 0:
        y = pltpu.roll(x, -offset, 0)
        y = jnp.where(local == 0, halo, y)
        bad = (rows == 0) | (rows == cl)
    else:
        y = pltpu.roll(x, tr - offset, 0)
        for j in range(offset):
            y = jnp.where(local == tr - offset + j, halo[j:j + 1, :], y)
        bad = (rows >= lt - offset) | ((rows >= cl - offset) & (rows < cl))
    return jnp.where(bad, 0.0, y)


def _hyena_pre_kernel(p_ref, pv_ref, nx_ref, w_ref, b_ref, v_ref, x1_ref, x2_ref, *, cl, lt, tr):
    rows = _row_ids(tr, pl.program_id(1))
    p = p_ref[0].astype(F32)
    pm = _shift_rows(p, pv_ref[0, SUBLANES - 1:, :].astype(F32), -1, rows, cl, lt)
    pp = _shift_rows(p, nx_ref[0, :1, :].astype(F32), 1, rows, cl, lt)
    z = pm * w_ref[0:1, :] + p * w_ref[1:2, :] + pp * w_ref[2:3, :] + b_ref[...]
    c = HYENA_WIDTH
    v_ref[0] = z[:, :c].astype(v_ref.dtype)
    x1_ref[0] = z[:, c:2 * c].astype(x1_ref.dtype)
    x2_ref[0] = z[:, 2 * c:].astype(x2_ref.dtype)


def _hyena_pre(phy, conv_w, conv_b, cl):
    b, lt, w = phy.shape
    tr = _largest_tile(lt, 1088, 16)
    prev_spec, next_spec = _halo_specs(tr, lt, w, (1,))
    out = jax.ShapeDtypeStruct((b, lt, HYENA_WIDTH), BF16)
    ospec = pl.BlockSpec((1, tr, HYENA_WIDTH), lambda i, r: (i, r, 0))
    return pl.pallas_call(
        functools.partial(_hyena_pre_kernel, cl=cl, lt=lt, tr=tr),
        out_shape=(out, out, out),
        grid=(b, lt // tr),
        in_specs=[pl.BlockSpec((1, tr, w), lambda i, r: (i, r, 0)), prev_spec, next_spec,
                  pl.BlockSpec(conv_w.shape, lambda i, r: (0, 0)),
                  pl.BlockSpec((1, w), lambda i, r: (0, 0))],
        out_specs=(ospec, ospec, ospec),
        compiler_params=_cparams("parallel", "parallel"),
        name="hyena_pre",
    )(phy, phy, phy, conv_w, conv_b.reshape(1, w))


def _trig_mats_kernel(cb_ref, sb_ref, co_ref, so_ref, c_ref, s_ref):
    cb, sb, co, so = cb_ref[0], sb_ref[0], co_ref[...], so_ref[...]
    c_ref[...] = (cb * co - sb * so).astype(c_ref.dtype)
    s_ref[...] = (sb * co + cb * so).astype(s_ref.dtype)


def _trig_mats(n, h, row_mult, col_mult):
    tm = _largest_tile(h, 256, 16)
    tn = _largest_tile(h, 1024, LANES) if h % LANES == 0 else h
    b = col_mult(jnp.arange(h, dtype=jnp.int32))[None, :]

    def tables(a):
        ang = ((a[:, None] * b) % (2 * n)).astype(F32) * (math.pi / n)
        return jnp.cos(ang), jnp.sin(ang)

    r0 = jnp.arange(0, h, tm, dtype=jnp.int32)
    cb, sb = tables(row_mult(r0))
    co, so = tables(row_mult(jnp.arange(tm, dtype=jnp.int32)) - row_mult(jnp.zeros((tm,), jnp.int32)))
    base = pl.BlockSpec((1, 1, tn), lambda j, c: (j, 0, c))
    off = pl.BlockSpec((tm, tn), lambda j, c: (0, c))
    out = jax.ShapeDtypeStruct((h, h), BF16)
    ospec = pl.BlockSpec((tm, tn), lambda j, c: (j, c))
    return pl.pallas_call(
        _trig_mats_kernel, out_shape=(out, out), grid=(h // tm, h // tn),
        in_specs=[base, base, off, off], out_specs=(ospec, ospec),
        compiler_params=_cparams("parallel", "parallel"), name="trig_mats",
    )(cb.reshape(h // tm, 1, h), sb.reshape(h // tm, 1, h), co, so)


def _dft_mats(n):
    h = n // 2
    ident = lambda r: r
    ce, se = _trig_mats(n, h, ident, lambda j: 2 * j)
    co, so = _trig_mats(n, h, ident, lambda j: 2 * j + 1)
    cot, sot = _trig_mats(n, h, lambda j: 2 * j + 1, ident)
    return ce, se, co, so, cot, sot


def _alt_sum(z):
    j = lax.broadcasted_iota(jnp.int32, (z.shape[0], 1), 0)
    return jnp.sum(z * (1 - 2 * (j % 2)).astype(F32), axis=0, keepdims=True)


def _half_spectra(ce_ref, se_ref, co_ref, so_ref, z_ev, z_od):
    p, q = _dot(ce_ref[...], z_ev), _dot(co_ref[...], z_od)
    ps, qs = _dot(se_ref[...], z_ev), _dot(so_ref[...], z_od)
    return p + q, ps + qs, p - q, qs - ps


def _dft_raw_kernel(ce_ref, se_ref, co_ref, so_ref, z_ref, rl_ref, il_ref, ru_ref, iu_ref, mid_ref):
    z_ev, z_od = z_ref[0], z_ref[1]
    rl_ref[...], il_ref[...], ru_ref[...], iu_ref[...] = _half_spectra(ce_ref, se_ref, co_ref, so_ref, z_ev, z_od)

    @pl.when(pl.program_id(0) == 0)
    def _():
        mid = jnp.concatenate([_alt_sum(z_ev.astype(F32)), _alt_sum(z_od.astype(F32))], axis=0)
        mid_ref[...] = jnp.concatenate([mid, jnp.zeros((SUBLANES - 2, mid.shape[1]), F32)], axis=0)


def _dft_raw(mats, zs):
    ce, se, co, so, _, _ = mats
    _, h, c = zs.shape
    tm = _largest_tile(h, 256, 16)
    mat = pl.BlockSpec((tm, h), lambda j: (j, 0))
    out = jax.ShapeDtypeStruct((h, c), F32)
    ospec = pl.BlockSpec((tm, c), lambda j: (j, 0))
    return pl.pallas_call(
        _dft_raw_kernel, out_shape=(out, out, out, out, jax.ShapeDtypeStruct((SUBLANES, c), F32)), grid=(h // tm,),
        in_specs=[mat, mat, mat, mat, pl.BlockSpec((2, h, c), lambda j: (0, 0, 0))],
        out_specs=(ospec, ospec, ospec, ospec, pl.BlockSpec((SUBLANES, c), lambda j: (0, 0))),
        compiler_params=_cparams("arbitrary"), name="dft_raw",
    )(ce, se, co, so, zs)


def _dft_fwd_kernel(ce_ref, se_ref, co_ref, so_ref, z_ref, hrl_ref, hil_ref, hru_ref, hiu_ref, hmid_ref,
                    ea_ref, eb_ref, oa_ref, ob_ref, mid_ref, *, bb):
    hrl, hil, hru, hiu = hrl_ref[...], hil_ref[...], hru_ref[...], hiu_ref[...]
    for i in range(bb):
        z_ev, z_od = z_ref[i, 0], z_ref[i, 1]
        zrl, zil, zru, ziu = _half_spectra(ce_ref, se_ref, co_ref, so_ref, z_ev, z_od)
        yrl, yil = zrl * hrl + zil * hil, zil * hrl - zrl * hil
        yru, yiu = zru * hru + ziu * hiu, ziu * hru - zru * hiu
        ea_ref[i] = (yrl + yru).astype(ea_ref.dtype)
        eb_ref[i] = (yil - yiu).astype(eb_ref.dtype)
        oa_ref[i] = (yrl - yru).astype(oa_ref.dtype)
        ob_ref[i] = (yil + yiu).astype(ob_ref.dtype)

    @pl.when(pl.program_id(1) == 0)
    def _():
        hr, hi = hmid_ref[0:1, :], hmid_ref[1:2, :]
        for i in range(bb):
            zr, zi = _alt_sum(z_ref[i, 0].astype(F32)), _alt_sum(z_ref[i, 1].astype(F32))
            mid = jnp.concatenate([zr * hr + zi * hi, zi * hr - zr * hi], axis=0)
            mid_ref[i] = jnp.concatenate([mid, jnp.zeros((SUBLANES - 2, mid.shape[1]), F32)], axis=0)


def _dft_inv_kernel(ce_ref, se_ref, cot_ref, sot_ref, ea_ref, eb_ref, oa_ref, ob_ref, mid_ref, z_ref, g_ref, skip_ref,
                    o_ref, *, bb, tm):
    alt = (1 - 2 * (_row_ids(tm, pl.program_id(1)) % 2)).astype(F32)
    for i in range(bb):
        y_ev = _dot(ce_ref[...], ea_ref[i]) + _dot(se_ref[...], eb_ref[i]) + alt * mid_ref[i, 0:1, :]
        y_od = _dot(cot_ref[...], oa_ref[i]) + _dot(sot_ref[...], ob_ref[i]) + alt * mid_ref[i, 1:2, :]
        for par, y in enumerate((y_ev, y_od)):
            y = y + z_ref[i, par].astype(F32) * skip_ref[...]
            o_ref[i, par] = (g_ref[i, par].astype(F32) * y).astype(o_ref.dtype)


def _longconv_gated(zs, gates, spectrum, skip, mats):
    ce, se, co, so, cot, sot = mats
    b, _, h, c = zs.shape
    bb = 2 if b % 2 == 0 else 1
    tm = _largest_tile(h, 256, 16)
    mat = pl.BlockSpec((tm, h), lambda i, j: (j, 0))
    full4 = pl.BlockSpec((bb, 2, h, c), lambda i, j: (i, 0, 0, 0))
    full = pl.BlockSpec((bb, h, c), lambda i, j: (i, 0, 0))
    tile = pl.BlockSpec((bb, tm, c), lambda i, j: (i, j, 0))
    tile4 = pl.BlockSpec((bb, 2, tm, c), lambda i, j: (i, 0, j, 0))
    filt = pl.BlockSpec((tm, c), lambda i, j: (j, 0))
    mid_spec = pl.BlockSpec((bb, SUBLANES, c), lambda i, j: (i, 0, 0))
    half = jax.ShapeDtypeStruct((b, h, c), BF16)
    *combos, mid = pl.pallas_call(
        functools.partial(_dft_fwd_kernel, bb=bb),
        out_shape=(half, half, half, half, jax.ShapeDtypeStruct((b, SUBLANES, c), F32)), grid=(b // bb, h // tm),
        in_specs=[mat, mat, mat, mat, full4, filt, filt, filt, filt, pl.BlockSpec((SUBLANES, c), lambda i, j: (0, 0))],
        out_specs=(tile, tile, tile, tile, mid_spec),
        compiler_params=_cparams("parallel", "arbitrary"), name="dft_fwd",
    )(ce, se, co, so, zs, *spectrum)
    return pl.pallas_call(
        functools.partial(_dft_inv_kernel, bb=bb, tm=tm),
        out_shape=jax.ShapeDtypeStruct((b, 2, h, c), BF16), grid=(b // bb, h // tm),
        in_specs=[mat, mat, mat, mat, full, full, full, full, mid_spec, tile4, tile4,
                  pl.BlockSpec((1, c), lambda i, j: (0, 0))],
        out_specs=tile4,
        compiler_params=_cparams("parallel", "parallel"), name="dft_inv",
    )(ce, se, cot, sot, *combos, mid, zs, gates, skip.reshape(1, c))


def _hyena_filters(n, f1, fb1, f2, fb2, f3):
    t = jnp.arange(n, dtype=F32) / n
    bands = jnp.arange(1, HYENA_BANDS + 1, dtype=F32)
    ang = 2.0 * math.pi * t[:, None] * bands
    feat = jnp.concatenate([t[:, None], jnp.sin(ang), jnp.cos(ang)], axis=-1)
    hp = lax.Precision.HIGHEST
    h = jnp.sin(jnp.dot(feat, f1, precision=hp) + fb1)
    h = jnp.sin(jnp.dot(h, f2, precision=hp) + fb2)
    h = jnp.dot(h, f3, precision=hp).reshape(n, HYENA_ORDER, 2, HYENA_WIDTH)
    deltas = jnp.linspace(-math.log(HYENA_DECAY_TARGET) / HYENA_SLOW_DECAY,
                          -math.log(HYENA_DECAY_TARGET) / HYENA_FAST_DECAY, HYENA_WIDTH, dtype=F32)
    h = h * jnp.exp(-t[:, None] * deltas)[:, None, None, :]
    return h / jnp.sum(jnp.abs(h), axis=(0, 2), keepdims=True)


def _parity_split(t):
    *lead, n, c = t.shape
    return jnp.swapaxes(t.reshape(*lead, n // 2, 2, c), -2, -3)


def _parity_merge(t):
    *lead, _, h, c = t.shape
    return jnp.swapaxes(t, -2, -3).reshape(*lead, 2 * h, c)


def _hyena_spectra(n, filt_params, mats):
    h = _hyena_filters(n, *filt_params)
    oc = HYENA_ORDER * HYENA_WIDTH
    hf = h[:, :, 0].reshape(n, oc)
    hb = h[:, :, 1].reshape(n, oc)
    hb = jnp.where(jnp.arange(n)[:, None] == 0, 0.0, hb)
    sig = _parity_split(jnp.concatenate([hf + hb, hb - hf], axis=1).astype(BF16))
    rl, il, ru, iu, mid = _dft_raw(mats, sig)
    scale = 1.0 / n
    ends = jnp.where(jnp.arange(n // 2)[:, None] == 0, 0.5 * scale, scale)
    spectra = []
    for o in range(HYENA_ORDER):
        re = slice(o * HYENA_WIDTH, (o + 1) * HYENA_WIDTH)
        im = slice(oc + o * HYENA_WIDTH, oc + (o + 1) * HYENA_WIDTH)
        hmid = jnp.zeros((SUBLANES, HYENA_WIDTH), F32).at[0].set(mid[0, re] * scale).at[1].set(mid[1, im] * scale)
        spectra.append((rl[:, re] * ends, il[:, im] * scale, ru[:, re] * ends, iu[:, im] * scale, hmid))
    return spectra


def _hyena_run(v, x1, x2, spectra, skip, mats):
    y = _parity_split(v)
    for o, gate in enumerate((x1, x2)):
        y = _longconv_gated(y, _parity_split(gate), spectra[o], skip[o], mats)
    return _parity_merge(y)


def _blockdiag_dense(w):
    nb, blk = w.shape[1], w.shape[2]
    eye = jnp.eye(nb, dtype=w.dtype)
    return jnp.einsum('dncf,nm->dncmf', w, eye).reshape(w.shape[0], nb * blk, nb * blk)


def _lru_pre_kernel(p_ref, pv_ref, nx_ref, cw_ref, cb_ref, wa_ref, ba_ref, wx_ref, bx_ref, lam_ref,
                    a_ref, b_ref, *, cl, lt, tr):
    rows = _row_ids(tr, pl.program_id(1))
    c = LRU_WIDTH
    x = p_ref[0][:, c:].astype(F32)
    pv = pv_ref[0][SUBLANES - 1:, c:].astype(F32)
    nx = nx_ref[0][:, c:].astype(F32)
    xc = (_shift_rows(x, pv, -1, rows, cl, lt) * cw_ref[0:1, :] + x * cw_ref[1:2, :]
          + _shift_rows(x, nx[:1], 1, rows, cl, lt) * cw_ref[2:3, :]
          + _shift_rows(x, nx[:2], 2, rows, cl, lt) * cw_ref[3:4, :] + cb_ref[...])
    xcb = xc.astype(BF16)
    for d in range(2):
        r = _sigmoid(_dot(xcb, wa_ref[d]) + ba_ref[d])
        i = _sigmoid(_dot(xcb, wx_ref[d]) + bx_ref[d])
        log_a = -LRU_C * r * _softplus(-lam_ref[d])
        a_ref[d, 0] = jnp.exp(log_a)
        b_ref[d, 0] = jnp.sqrt(1.0 - jnp.exp(2.0 * log_a)) * (i * xc)


def _lru_pre(plr, conv_w, conv_b, wa, ba, wx, bx, lam, cl):
    b, lt, w = plr.shape
    c = LRU_WIDTH
    tr = _largest_tile(lt, 1088, 16)
    prev_spec, next_spec = _halo_specs(tr, lt, w, (1,))
    out = jax.ShapeDtypeStruct((2, b, lt, c), F32)
    ospec = pl.BlockSpec((2, 1, tr, c), lambda i, r: (0, i, r, 0))
    const2 = lambda shape: pl.BlockSpec(shape, lambda i, r: (0,) * len(shape))
    return pl.pallas_call(
        functools.partial(_lru_pre_kernel, cl=cl, lt=lt, tr=tr),
        out_shape=(out, out), grid=(b, lt // tr),
        in_specs=[pl.BlockSpec((1, tr, w), lambda i, r: (i, r, 0)), prev_spec, next_spec,
                  const2(conv_w.shape), const2((1, c)), const2((2, c, c)), const2((2, 1, c)),
                  const2((2, c, c)), const2((2, 1, c)), const2((2, 1, c))],
        out_specs=(ospec, ospec),
        compiler_params=_cparams("parallel", "parallel"), name="lru_pre",
    )(plr, plr, plr, conv_w, conv_b.reshape(1, c), _blockdiag_dense(wa).astype(BF16), ba.reshape(2, 1, c),
      _blockdiag_dense(wx).astype(BF16), bx.reshape(2, 1, c), lam.reshape(2, 1, c))


def _gelu_tanh(x):
    return 0.5 * x * (1.0 + jnp.tanh(math.sqrt(2.0 / math.pi) * (x + 0.044715 * (x * x * x))))


def _lru_scan_kernel(af_ref, bf_ref, ar_ref, br_ref, g_ref, o_ref, acc_ref, *, cl, lt):
    row = lax.broadcasted_iota(jnp.int32, (SUBLANES, LANES), 0)

    def group_scan(a, b, reverse):
        for s in (1, 2, 4):
            if reverse:
                keep = row < SUBLANES - s
                a_s = jnp.where(keep, pltpu.roll(a, SUBLANES - s, 0), 1.0)
                b_s = jnp.where(keep, pltpu.roll(b, SUBLANES - s, 0), 0.0)
            else:
                keep = row >= s
                a_s = jnp.where(keep, pltpu.roll(a, s, 0), 1.0)
                b_s = jnp.where(keep, pltpu.roll(b, s, 0), 0.0)
            b = a * b_s + b
            a = a * a_s
        return a, b

    ng = math.gcd(math.gcd(cl // SUBLANES, (lt - cl) // SUBLANES), LRU_GROUPS_PER_STEP)
    span = ng * SUBLANES

    def fwd_body(i, h):
        sl = pl.ds(pl.multiple_of(i * span, span), span)
        a_all, b_all = af_ref[0, 0, sl, :], bf_ref[0, 0, sl, :]
        scans = [group_scan(a_all[j * SUBLANES:(j + 1) * SUBLANES], b_all[j * SUBLANES:(j + 1) * SUBLANES], False)
                 for j in range(ng)]
        outs = []
        for a, b in scans:
            hh = a * h + b
            outs.append(hh)
            h = hh[SUBLANES - 1:SUBLANES, :]
        acc_ref[sl, :] = jnp.concatenate(outs, axis=0)
        return h

    lax.fori_loop(0, lt // span, fwd_body, jnp.zeros((1, LANES), F32))

    def rev_body(i, h, top):
        sl = pl.ds(pl.multiple_of((top - 1 - i) * span, span), span)
        a_all, b_all = ar_ref[0, 0, sl, :], br_ref[0, 0, sl, :]
        scans = [group_scan(a_all[j * SUBLANES:(j + 1) * SUBLANES], b_all[j * SUBLANES:(j + 1) * SUBLANES], True)
                 for j in range(ng)]
        outs = [None] * ng
        for j in reversed(range(ng)):
            a, b = scans[j]
            hh = a * h + b
            outs[j] = hh
            h = hh[0:1, :]
        gate = g_ref[0, sl, :].astype(F32)
        o_ref[0, sl, :] = ((acc_ref[sl, :] + jnp.concatenate(outs, axis=0)) * _gelu_tanh(gate)).astype(o_ref.dtype)
        return h

    h = lax.fori_loop(0, cl // span, functools.partial(rev_body, top=cl // span), jnp.zeros((1, LANES), F32))
    lax.fori_loop(0, (lt - cl) // span, functools.partial(rev_body, top=lt // span), h)


def _lru_scan(a, b_, plr, cl):
    _, b, lt, c = a.shape
    nl = c // LANES
    fwd = pl.BlockSpec((1, 1, lt, LANES), lambda i, j: (0, i, 0, j))
    rev = pl.BlockSpec((1, 1, lt, LANES), lambda i, j: (1, i, 0, j))
    return pl.pallas_call(
        functools.partial(_lru_scan_kernel, cl=cl, lt=lt),
        out_shape=jax.ShapeDtypeStruct((b, lt, c), BF16), grid=(b, nl),
        in_specs=[fwd, fwd, rev, rev, pl.BlockSpec((1, lt, LANES), lambda i, j: (i, 0, j))],
        out_specs=pl.BlockSpec((1, lt, LANES), lambda i, j: (i, 0, j)),
        scratch_shapes=[pltpu.VMEM((lt, LANES), F32)],
        compiler_params=_cparams("parallel", "parallel"), name="lru_scan",
    )(a, b_, a, b_, plr)


def _row_perm(tr, rev):
    t = lax.broadcasted_iota(jnp.int32, (tr, tr), 0)
    s = lax.broadcasted_iota(jnp.int32, (tr, tr), 1)
    return jnp.where(s == jnp.where(rev, tr - 1 - t, t), 1.0, 0.0).astype(BF16)


def _mirror_tile(r, rev, n_ctx_tiles, n_tiles):
    m = jnp.where(r < n_ctx_tiles, n_ctx_tiles - 1 - r, n_tiles - 1 + n_ctx_tiles - r)
    return jnp.where(rev, m, r)


def _chunk_masks(tr):
    t = lax.broadcasted_iota(jnp.int32, (tr, tr), 0)
    s = lax.broadcasted_iota(jnp.int32, (tr, tr), 1)
    same = (t // RWKV_CHUNK) == (s // RWKV_CHUNK)
    return same, same & (s <= t), same & (s < t)


def _rwkv_prep_kernel(p_ref, pv_ref, nx_ref, mu_ref, w0_ref, w2_ref, a0_ref, a2_ref, g2_ref, kk_ref, ka_ref,
                      rk_ref, ones_ref,
                      aq_ref, vp_ref, y0_ref, rt_ref, mrb_ref, bht_ref, gm_ref, pc_ref, g_ref, bonus_ref,
                      *, cl, lt, tr):
    c = RWKV_WIDTH
    rows = _row_ids(tr, pl.program_id(2))
    rev = pl.program_id(0) == 1
    p = _dot(_row_perm(tr, rev), p_ref[0])
    before = jnp.where(rev, nx_ref[0, :1, :], pv_ref[0, SUBLANES - 1:, :]).astype(F32)
    after = jnp.where(rev, pv_ref[0, SUBLANES - 1:, :], nx_ref[0, :1, :]).astype(F32)
    prev = _shift_rows(p, before, -1, rows, cl, lt)
    nxt = _shift_rows(p, after, 1, rows, cl, lt)
    xm = p + (prev - p) * mu_ref[0, 0:1, :] + (nxt - p) * mu_ref[0, 1:2, :]
    r, k, v = xm[:, :c], xm[:, c:2 * c], xm[:, 2 * c:3 * c]
    o = 3 * c
    w1 = xm[:, o:o + 2 * RWKV_DECAY_RANK]
    a1 = xm[:, o + 2 * RWKV_DECAY_RANK:o + 2 * RWKV_DECAY_RANK + 2 * RWKV_ICLR_RANK]
    g1 = xm[:, o + 2 * RWKV_DECAY_RANK + 2 * RWKV_ICLR_RANK:]
    wlog = -_softplus(-(w0_ref[0] + _dot(jnp.tanh(w1).astype(BF16), w2_ref[0]))) - 0.5
    ld = -jnp.exp(wlog)
    a = _sigmoid(a0_ref[0] + _dot(a1.astype(BF16), a2_ref[0]))
    g_ref[0, 0] = _dot(_sigmoid(g1).astype(BF16), g2_ref[...])
    kk = k * kk_ref[...]
    kk = kk * lax.rsqrt(_dot_exact_rhs(kk * kk, ones_ref[...]) + 1e-12)
    kd = k * (1.0 + (a - 1.0) * ka_ref[...])
    bonus_ref[0, 0] = _dot_exact_rhs(r * kd * rk_ref[...], ones_ref[...]) * v
    beta = kk * a

    _rwkv_chunk_stage(ld, kk, r, kd, beta, v, aq_ref, vp_ref, y0_ref, rt_ref, mrb_ref, bht_ref, gm_ref, pc_ref)


def _rwkv_chunk_stage(ld, kk, r, kd, beta, v, aq_ref, vp_ref, y0_ref, rt_ref, mrb_ref, bht_ref, gm_ref, pc_ref):
    tr = ld.shape[0]
    n = min(RWKV_STAGE_ROWS, tr)
    nparts = tr // n
    hd = HEAD_DIM
    ch = RWKV_CHUNK
    nch = n // ch
    same_t, incl_t, _ = _chunk_masks(tr)
    cum = _dot_exact_lhs(jnp.where(incl_t, 1.0, 0.0).astype(BF16), ld)
    tot = _dot_exact_lhs(jnp.where(same_t, 1.0, 0.0).astype(BF16), ld)
    _, incl, strict = _chunk_masks(n)
    alpha_t = kk * jnp.exp(cum - ld)
    r_t = r * jnp.exp(cum)
    e_neg = jnp.exp(-cum)
    k_t = kd * e_neg
    b_t = beta * e_neg
    e_rem = jnp.exp(tot - cum)
    k_hat_t = (kd * e_rem).T
    b_hat_t = (beta * e_rem).T
    pc_t = jnp.exp(tot).T
    t_i = lax.broadcasted_iota(jnp.int32, (n, n), 0)
    s_i = lax.broadcasted_iota(jnp.int32, (n, n), 1)
    eye_f = jnp.where(t_i == s_i, 1.0, 0.0)
    same_blk = []
    size = RWKV_INV_BASE
    while size <= ch:
        same_blk.append((t_i // size) == (s_i // size))
        size *= 2
    col_chunk = lax.broadcasted_iota(jnp.int32, (n, nch * hd), 1) // hd
    row_chunk = lax.broadcasted_iota(jnp.int32, (n, nch * hd), 0) // ch

    def diag_blocks(m):
        out = m[:, :ch]
        for j in range(1, nch):
            out = out + m[:, j * ch:(j + 1) * ch]
        return out

    chains = [(p, h) for p in range(nparts) for h in range(RWKV_HEADS)]

    def blk(t, p, h):
        return t[p * n:(p + 1) * n, h * hd:(h + 1) * hd]

    prods = [_dot_nt(jnp.concatenate([blk(alpha_t, p, h), blk(r_t, p, h)], axis=0).astype(BF16),
                     jnp.concatenate([blk(b_t, p, h), blk(k_t, p, h)], axis=0).astype(BF16))
             for p, h in chains]
    l_ab = [jnp.where(strict, pr[:n, :n], 0.0) for pr in prods]
    pw = [jnp.where(same_blk[0], l, 0.0) for l in l_ab]
    t_inv = [eye_f - l for l in pw]
    for _ in range(int(math.log2(RWKV_INV_BASE)) - 1):
        pw = [_dot_bf(q, q) for q in pw]
        t_inv = [t + _dot_bf(t, q) for t, q in zip(t_inv, pw)]
    for lvl in range(1, len(same_blk)):
        off = same_blk[lvl] & jnp.logical_not(same_blk[lvl - 1])
        half = [_dot_bf(t, jnp.where(off, l, 0.0)) for t, l in zip(t_inv, l_ab)]
        t_inv = [t - _dot_bf(hf, t) for t, hf in zip(t_inv, half)]
    vh = [blk(v, p, h).astype(BF16) for p, h in chains]
    lakv = [_dot(jnp.where(strict, pr[:n, n:], 0.0).astype(BF16), vv) for pr, vv in zip(prods, vh)]
    x = [_dot(t.astype(BF16), jnp.concatenate([blk(alpha_t, p, h), lv], axis=1).astype(BF16))
         for t, (p, h), lv in zip(t_inv, chains, lakv)]
    y0 = [_dot(jnp.where(incl, pr[n:, n:], 0.0).astype(BF16), vv) for pr, vv in zip(prods, vh)]
    mrb = [diag_blocks(jnp.where(incl, pr[n:, :n], 0.0)) for pr in prods]
    gms = [_dot(k_hat_t[h * hd:(h + 1) * hd, p * n:(p + 1) * n].astype(BF16),
                jnp.where(col_chunk == row_chunk, jnp.concatenate([blk(v, p, h)] * nch, axis=1), 0.0).astype(BF16))
           for p, h in chains]

    def assemble(parts):
        return jnp.concatenate([jnp.concatenate(parts[p * RWKV_HEADS:(p + 1) * RWKV_HEADS], axis=1)
                                for p in range(nparts)], axis=0)

    aq_ref[0, 0] = assemble([t[:, :hd] for t in x]).astype(aq_ref.dtype)
    vp_ref[0, 0] = assemble([t[:, hd:] for t in x])
    y0_ref[0, 0] = assemble(y0)
    rt_ref[0, 0] = r_t.astype(rt_ref.dtype)
    mrb_ref[0, 0] = assemble(mrb).astype(mrb_ref.dtype)
    for p in range(nparts):
        for j in range(nch):
            jj = p * nch + j
            cs = slice(jj * ch, (jj + 1) * ch)
            bht_ref[0, 0, jj] = jnp.concatenate([b_hat_t[h * hd:(h + 1) * hd, cs] for h in range(RWKV_HEADS)],
                                                axis=1).astype(bht_ref.dtype)
            gm_ref[0, 0, jj] = jnp.concatenate([gms[p * RWKV_HEADS + h][:, j * hd:(j + 1) * hd]
                                                for h in range(RWKV_HEADS)], axis=1)
            pc_ref[0, 0, jj] = jnp.concatenate([pc_t[h * hd:(h + 1) * hd, cs] for h in range(RWKV_HEADS)], axis=1)


def _rwkv_tile(lt, cl):
    tr = 256 if (lt % 256 == 0 and cl % 256 == 0) else 128
    assert lt % tr == 0 and cl % tr == 0
    return tr


def _rwkv_prep(prw, mud, w0, w2p, a0, a2p, g2, k_k, k_a, r_k, cl):
    b, lt, w = prw.shape
    c = RWKV_WIDTH
    tr = _rwkv_tile(lt, cl)
    nch = tr // RWKV_CHUNK
    per = tr // SUBLANES
    src = lambda d, r: _mirror_tile(r, d == 1, cl // tr, lt // tr)
    prev_spec = pl.BlockSpec((1, SUBLANES, w), lambda d, i, r: (i, jnp.maximum(src(d, r) * per - 1, 0), 0))
    next_spec = pl.BlockSpec((1, SUBLANES, w),
                             lambda d, i, r: (i, jnp.minimum((src(d, r) + 1) * per, lt // SUBLANES - 1), 0))
    tile = pl.BlockSpec((1, 1, tr, c), lambda d, i, r: (d, i, r, 0))
    per_dir = lambda shape: pl.BlockSpec((1, *shape), lambda d, i, r: (d,) + (0,) * len(shape))
    const = lambda shape: pl.BlockSpec(shape, lambda d, i, r: (0,) * len(shape))
    seq = lambda dt: jax.ShapeDtypeStruct((2, b, lt, c), dt)
    chunked = jax.ShapeDtypeStruct((2, b, lt // RWKV_CHUNK, HEAD_DIM, c), F32)
    chunk_spec = pl.BlockSpec((1, 1, nch, HEAD_DIM, c), lambda d, i, r: (d, i, r, 0, 0))
    return pl.pallas_call(
        functools.partial(_rwkv_prep_kernel, cl=cl, lt=lt, tr=tr),
        out_shape=(seq(BF16), seq(F32), seq(F32), seq(BF16), seq(BF16),
                   jax.ShapeDtypeStruct(chunked.shape, BF16), chunked, chunked, seq(F32), seq(F32)),
        grid=(2, b, lt // tr),
        in_specs=[pl.BlockSpec((1, tr, w), lambda d, i, r: (i, src(d, r), 0)), prev_spec, next_spec,
                  per_dir((2, w)), per_dir((1, c)), per_dir((2 * RWKV_DECAY_RANK, c)), per_dir((1, c)),
                  per_dir((2 * RWKV_ICLR_RANK, c)), const((RWKV_GATE_RANK, c)), const((1, c)), const((1, c)),
                  const((1, c)), const((c, c))],
        out_specs=(tile, tile, tile, tile, tile, chunk_spec, chunk_spec, chunk_spec, tile, tile),
        compiler_params=_cparams("parallel", "parallel", "parallel"), name="rwkv_prep",
    )(prw, prw, prw, mud, w0.reshape(2, 1, c), w2p, a0.reshape(2, 1, c), a2p, g2.astype(BF16), k_k.reshape(1, c),
      k_a.reshape(1, c), r_k.reshape(1, c), _head_ones(c))


def _rwkv_scan_kernel(aq_ref, vp_ref, y0_ref, rt_ref, mrb_ref, bht_ref, gm_ref, pc_ref, y_ref, s_ref, *, ns):
    @pl.when(pl.program_id(1) == 0)
    def _():
        s_ref[...] = jnp.zeros_like(s_ref)

    c = RWKV_WIDTH
    ch = RWKV_CHUNK
    same_head = (lax.broadcasted_iota(jnp.int32, (c, c), 0) // HEAD_DIM
                 == lax.broadcasted_iota(jnp.int32, (c, c), 1) // HEAD_DIM)

    def head_blockdiag(t):
        return jnp.where(same_head, jnp.concatenate([t] * RWKV_HEADS, axis=0), 0.0).astype(BF16)

    for i in range(ns):
        s0 = s_ref[i]
        r1 = _dot(jnp.concatenate([aq_ref[i, 0], rt_ref[i, 0]], axis=0), head_blockdiag(s0))
        u = r1[:ch] + vp_ref[i, 0]
        r2 = _dot(jnp.concatenate([mrb_ref[i, 0], bht_ref[i, 0, 0]], axis=0), head_blockdiag(u))
        y_ref[i, 0] = r1[ch:] + y0_ref[i, 0] - r2[:ch]
        s_ref[i] = pc_ref[i, 0, 0] * s0 + gm_ref[i, 0, 0] - r2[ch:]


def _rwkv_scan(aq, vp, y0, rt, mrb, bht, gm, pc):
    _, b, lt, c = aq.shape
    nstream = 2 * b
    ns = nstream
    ch = RWKV_CHUNK
    merge = lambda t: t.reshape(nstream, 1, *t.shape[2:])
    tile = pl.BlockSpec((ns, 1, ch, c), lambda s, j: (s, 0, j, 0))
    chunk_spec = pl.BlockSpec((ns, 1, 1, HEAD_DIM, c), lambda s, j: (s, 0, j, 0, 0))
    y = pl.pallas_call(
        functools.partial(_rwkv_scan_kernel, ns=ns),
        out_shape=jax.ShapeDtypeStruct((nstream, 1, lt, c), F32), grid=(nstream // ns, lt // ch),
        in_specs=[tile, tile, tile, tile, tile, chunk_spec, chunk_spec, chunk_spec],
        out_specs=tile,
        scratch_shapes=[pltpu.VMEM((ns, HEAD_DIM, c), F32)],
        compiler_params=_cparams("parallel", "arbitrary"), name="rwkv_scan",
    )(*(merge(t) for t in (aq, vp, y0, rt, mrb, bht, gm, pc)))
    return y.reshape(2, b, lt, c)


def _rwkv_readout_kernel(yf_ref, yr_ref, bf_ref, br_ref, g_ref, lw_ref, lb_ref, ones_ref, o_ref, *, tr):
    unflip = _row_perm(tr, True)
    y = yf_ref[0, 0] + _dot_exact_lhs(unflip, yr_ref[0, 0])
    bonus = bf_ref[0, 0] + _dot_exact_lhs(unflip, br_ref[0, 0])
    inv = 1.0 / HEAD_DIM
    mu = _dot_exact_rhs(y, ones_ref[...]) * inv
    yc = y - mu
    var = _dot_exact_rhs(yc * yc, ones_ref[...]) * inv
    yn = yc * lax.rsqrt(var + RWKV_GN_EPS) * lw_ref[...] + lb_ref[...]
    o_ref[0] = ((yn + bonus) * g_ref[0, 0]).astype(o_ref.dtype)


def _rwkv_readout(y, bonus, g, ln_w, ln_b, cl):
    _, b, lt, c = y.shape
    tr = _rwkv_tile(lt, cl)
    fwd = pl.BlockSpec((1, 1, tr, c), lambda i, r: (0, i, r, 0))
    rev = pl.BlockSpec((1, 1, tr, c), lambda i, r: (1, i, _mirror_tile(r, True, cl // tr, lt // tr), 0))
    row = pl.BlockSpec((1, c), lambda i, r: (0, 0))
    return pl.pallas_call(
        functools.partial(_rwkv_readout_kernel, tr=tr),
        out_shape=jax.ShapeDtypeStruct((b, lt, c), BF16), grid=(b, lt // tr),
        in_specs=[fwd, rev, fwd, rev, fwd, row, row, pl.BlockSpec((c, c), lambda i, r: (0, 0))],
        out_specs=pl.BlockSpec((1, tr, c), lambda i, r: (i, r, 0)),
        compiler_params=_cparams("parallel", "parallel"), name="rwkv_readout",
    )(y, y, bonus, bonus, g, ln_w.reshape(1, c), ln_b.reshape(1, c), _head_ones(c))


def _pad_rank_rows(w):
    z = jnp.zeros_like(w[0])
    return jnp.stack([jnp.concatenate([w[0], z], 0), jnp.concatenate([z, w[1]], 0)])


def _rwkv_mixer(prw, mu, w0, w2, a0, a2, g2, k_k, k_a, r_k, ln_w, ln_b, cl):
    mud = jnp.stack([mu, mu[::-1]])
    outs = _rwkv_prep(prw, mud, w0, _pad_rank_rows(w2).astype(BF16), a0, _pad_rank_rows(a2).astype(BF16), g2,
                      k_k, k_a, r_k, cl)
    *scan_in, g, bonus = outs
    y = _rwkv_scan(*scan_in)
    return _rwkv_readout(y, bonus, g, ln_w, ln_b, cl)


def _merge_kernel(ya_ref, yh_ref, yr_ref, yl_ref, gt_ref, x_ref, modx_ref, modc_ref,
                  wa_ref, wh_ref, wr_ref, wl_ref, wo_ref, o_ref, *, cl, tm, d):
    is_ctx = _row_ids(tm, pl.program_id(1)) < cl
    m = None
    for i, (y_ref, w_ref) in enumerate(((ya_ref, wa_ref), (yh_ref, wh_ref), (yr_ref, wr_ref), (yl_ref, wl_ref))):
        gate = _sigmoid(gt_ref[0, :, i * d:(i + 1) * d].astype(F32))
        term = gate * _dot(y_ref[0], w_ref[...])
        m = term if m is None else m + term
    g1 = _mod_rows(modx_ref, modc_ref, 2, is_ctx)
    o_ref[0] = x_ref[0] + g1 * _dot(m.astype(BF16), wo_ref[...])


def _merge(ys, gates, xc, mod, w_brs, w_out, cl):
    b, lt, d = xc.shape
    tm = _largest_tile(lt, 544, 16)
    row = lambda w: pl.BlockSpec((1, tm, w), lambda i, r: (i, r, 0))
    const = lambda a: pl.BlockSpec(a.shape, lambda i, r: (0, 0))
    ws = [w.astype(BF16) for w in w_brs] + [w_out.astype(BF16)]
    return pl.pallas_call(
        functools.partial(_merge_kernel, cl=cl, tm=tm, d=d),
        out_shape=jax.ShapeDtypeStruct((b, lt, d), F32), grid=(b, lt // tm),
        in_specs=[row(y.shape[-1]) for y in ys] + [row(N_BRANCH * d), row(d),
                  pl.BlockSpec((1, 6, d), lambda i, r: (i, 0, 0)), pl.BlockSpec((1, 6, d), lambda i, r: (b, 0, 0))]
                 + [const(w) for w in ws],
        out_specs=row(d), compiler_params=_cparams("parallel", "parallel"), name="merge",
    )(*ys, gates, xc, mod, mod, *ws)


def _route_kernel(x_ref, modx_ref, modc_ref, wr_ref, br_ref, h_ref, cmb_ref, *, cl, tm):
    is_ctx = _row_ids(tm, pl.program_id(1)) < cl
    h = _rms_modulate(x_ref[0], _mod_rows(modx_ref, modc_ref, 3, is_ctx), _mod_rows(modx_ref, modc_ref, 4, is_ctx))
    h_ref[0] = h.astype(h_ref.dtype)
    lg = jnp.dot(h, wr_ref[...], preferred_element_type=F32, precision=lax.Precision.HIGHEST) + br_ref[...]
    lane = lax.broadcasted_iota(jnp.int32, lg.shape, 1)
    lane_f = lane.astype(F32)
    neg = -jnp.inf
    big = 1e9

    def first_lane(cond):
        return jnp.min(jnp.where(cond, lane_f, big), axis=-1, keepdims=True)

    is_grp = (lane >= N_EXPERTS) & (lane < N_EXPERTS + N_GROUPS)
    gl = jnp.where(is_grp, lg, neg)
    gmax = jnp.max(gl, axis=-1, keepdims=True)
    ge = jnp.where(is_grp, jnp.exp(gl - gmax), 0.0)
    gp = ge / jnp.sum(ge, axis=-1, keepdims=True)
    g_val = jnp.max(gp, axis=-1, keepdims=True)
    g_idx = first_lane(is_grp & (gp == g_val)) - N_EXPERTS
    lo = g_idx * EXPERTS_PER_GROUP
    in_grp = (lane_f >= lo) & (lane_f < lo + EXPERTS_PER_GROUP)
    el = jnp.where(in_grp, lg, neg)
    emax = jnp.max(el, axis=-1, keepdims=True)
    ee = jnp.where(in_grp, jnp.exp(el - emax), 0.0)
    pe = ee / jnp.sum(ee, axis=-1, keepdims=True)
    v1 = jnp.max(jnp.where(in_grp, pe, -1.0), axis=-1, keepdims=True)
    i1 = first_lane(in_grp & (pe == v1))
    rest = in_grp & (lane_f != i1)
    v2 = jnp.max(jnp.where(rest, pe, -1.0), axis=-1, keepdims=True)
    i2 = first_lane(rest & (pe == v2))
    den = v1 + v2
    cmb_ref[0] = (jnp.where(lane_f == i1, g_val * v1 / den, 0.0) + jnp.where(lane_f == i2, g_val * v2 / den, 0.0)
                  + jnp.where(lane == N_EXPERTS, g_idx, 0.0))


def _route(xc, mod, w_grp, b_grp, w_rt, b_rt, cl):
    b, lt, d = xc.shape
    tm = _largest_tile(lt, 544, 16)
    wr = jnp.zeros((d, LANES), F32).at[:, :N_EXPERTS].set(w_rt).at[:, N_EXPERTS:N_EXPERTS + N_GROUPS].set(w_grp)
    br = jnp.zeros((1, LANES), F32).at[0, :N_EXPERTS].set(b_rt).at[0, N_EXPERTS:N_EXPERTS + N_GROUPS].set(b_grp)
    return pl.pallas_call(
        functools.partial(_route_kernel, cl=cl, tm=tm),
        out_shape=(jax.ShapeDtypeStruct((b, lt, d), BF16), jax.ShapeDtypeStruct((b, lt, LANES), F32)),
        grid=(b, lt // tm),
        in_specs=[pl.BlockSpec((1, tm, d), lambda i, r: (i, r, 0)),
                  pl.BlockSpec((1, 6, d), lambda i, r: (i, 0, 0)), pl.BlockSpec((1, 6, d), lambda i, r: (b, 0, 0)),
                  pl.BlockSpec((d, LANES), lambda i, r: (0, 0)), pl.BlockSpec((1, LANES), lambda i, r: (0, 0))],
        out_specs=(pl.BlockSpec((1, tm, d), lambda i, r: (i, r, 0)), pl.BlockSpec((1, tm, LANES), lambda i, r: (i, r, 0))),
        compiler_params=_cparams("parallel", "parallel"), name="moe_route",
    )(xc, mod, mod, wr, br)


def _moe_kernel(grp_ref, nvalid_ref, h_ref, cmb_ref, w1_ref, w3_ref, w2_ref, o_ref, acc_ref, *, bm):
    j = pl.program_id(0)
    e = pl.program_id(1)

    @pl.when(e == 0)
    def _():
        acc_ref[...] = jnp.zeros_like(acc_ref)

    @pl.when(j < nvalid_ref[0])
    def _():
        h = h_ref[...]
        t = _silu(_dot(h, w1_ref[0, 0].astype(BF16))) * _dot(h, w3_ref[0, 0].astype(BF16))
        y = _dot(t.astype(BF16), w2_ref[0, 0].astype(BF16))
        lane = lax.broadcasted_iota(jnp.int32, (bm, LANES), 1)
        expert = grp_ref[j] * EXPERTS_PER_GROUP + e
        wcol = jnp.sum(jnp.where(lane == expert, cmb_ref[...], 0.0), axis=-1, keepdims=True)
        acc_ref[...] += wcol * y

    @pl.when(e == pl.num_programs(1) - 1)
    def _():
        o_ref[...] = acc_ref[...].astype(o_ref.dtype)


def _moe_grouped(hs, ws, blk_group, nvalid, w1, w3, w2, layer):
    s_rows, d = hs.shape
    hid = w1.shape[3]
    bm = MOE_BLOCK_ROWS
    wspec = lambda shape: pl.BlockSpec(shape, lambda j, e, grp, nv: (layer, grp[j] * EXPERTS_PER_GROUP + e, 0, 0))
    return pl.pallas_call(
        functools.partial(_moe_kernel, bm=bm),
        out_shape=jax.ShapeDtypeStruct((s_rows, d), BF16),
        grid_spec=pltpu.PrefetchScalarGridSpec(
            num_scalar_prefetch=2, grid=(s_rows // bm, EXPERTS_PER_GROUP),
            in_specs=[pl.BlockSpec((bm, d), lambda j, e, grp, nv: (j, 0)),
                      pl.BlockSpec((bm, LANES), lambda j, e, grp, nv: (j, 0)),
                      wspec((1, 1, d, hid)), wspec((1, 1, d, hid)), wspec((1, 1, hid, d))],
            out_specs=pl.BlockSpec((bm, d), lambda j, e, grp, nv: (j, 0)),
            scratch_shapes=[pltpu.VMEM((bm, d), F32)]),
        compiler_params=_cparams("parallel", "arbitrary"), name="moe_experts",
    )(blk_group, nvalid, hs, ws, w1, w3, w2)


def _moe_finish_kernel(x_ref, y_ref, modx_ref, modc_ref, o_ref, *, cl, tm):
    is_ctx = _row_ids(tm, pl.program_id(1)) < cl
    o_ref[0] = x_ref[0] + _mod_rows(modx_ref, modc_ref, 5, is_ctx) * y_ref[0].astype(F32)


def _moe_finish(xc, y, mod, cl):
    b, lt, d = xc.shape
    tm = _largest_tile(lt, 1088, 16)
    tile = pl.BlockSpec((1, tm, d), lambda i, r: (i, r, 0))
    return pl.pallas_call(
        functools.partial(_moe_finish_kernel, cl=cl, tm=tm),
        out_shape=jax.ShapeDtypeStruct((b, lt, d), F32), grid=(b, lt // tm),
        in_specs=[tile, tile, pl.BlockSpec((1, 6, d), lambda i, r: (i, 0, 0)),
                  pl.BlockSpec((1, 6, d), lambda i, r: (b, 0, 0))],
        out_specs=tile, compiler_params=_cparams("parallel", "parallel"), name="moe_finish",
    )(xc, y, mod, mod)


def _moe(h2, cmb, xc, mod, w1, w3, w2, layer, cl):
    b, lt, d = xc.shape
    t = b * lt
    bm = MOE_BLOCK_ROWS
    s_rows = -(-t // bm) * bm + N_GROUPS * bm
    cmb2 = cmb.reshape(t, LANES)
    gid = cmb2[:, N_EXPERTS].astype(jnp.int32)
    onehot = (gid[:, None] == jnp.arange(N_GROUPS, dtype=jnp.int32)).astype(jnp.int32)
    csum = jnp.cumsum(onehot, axis=0)
    rank = jnp.sum(onehot * csum, axis=1) - 1
    padded = -(-csum[-1] // bm) * bm
    ends = jnp.cumsum(padded)
    pos = (ends - padded)[gid] + rank
    slot_token = jnp.zeros((s_rows,), jnp.int32).at[pos].set(jnp.arange(t, dtype=jnp.int32))
    starts = jnp.arange(s_rows // bm, dtype=jnp.int32) * bm
    blk_group = jnp.minimum(jnp.sum((starts[:, None] >= ends[None, :]).astype(jnp.int32), axis=1), N_GROUPS - 1)
    nvalid = (ends[-1] // bm).reshape(1).astype(jnp.int32)
    take_rows = lambda a, idx: a.at[idx].get(mode="promise_in_bounds")
    hs = take_rows(h2.reshape(t, d), slot_token)
    ws = take_rows(cmb2, slot_token)
    ys = _moe_grouped(hs, ws, blk_group.astype(jnp.int32), nvalid, w1, w3, w2, layer)
    y = take_rows(ys, pos).reshape(b, lt, d)
    return _moe_finish(xc, y, mod, cl)


def kernel(x, c, ctx, c_ctx, ada_w, ada_b, w_in, q_norm, k_norm, hy_conv_w, hy_conv_b, hy_f1, hy_fb1, hy_f2, hy_fb2, hy_f3, hy_skip, rw_mu, rw_w0, rw_w2, rw_a0, rw_a2, rw_g2, rw_k_k, rw_k_a, rw_r_k, rw_ln_w, rw_ln_b, lru_conv_w, lru_conv_b, lru_wa, lru_ba, lru_wx, lru_bx, lru_lambda, w_br_attn, w_br_hyena, w_br_rwkv, w_br_lru, w_out, moe_w_grp, moe_b_grp, moe_w_rt, moe_b_rt, moe_w1, moe_w3, moe_w2):
    b, l, d = x.shape
    cl = ctx.shape[1]
    depth = ada_w.shape[0]
    assert b < MOD_ROWS and cl % RWKV_CHUNK == 0 and l % RWKV_CHUNK == 0

    xc = jnp.concatenate([ctx, x], axis=1)
    cc = jnp.zeros((MOD_ROWS, d), F32).at[:b].set(c).at[b].set(c_ctx)
    mod_all = _ada_mod(cc, ada_w, ada_b).reshape(depth, MOD_ROWS, 6, d)

    cos2, sin2 = _rope_tables(l, cl)
    mats_x = _dft_mats(l)
    mats_c = _dft_mats(cl)
    qkv_w = ATTN_WIDTH + 2 * ATTN_KV_WIDTH
    col = np.cumsum([0, qkv_w, 3 * HYENA_WIDTH, RWKV_PROJ, 2 * LRU_WIDTH, N_BRANCH * d])

    for i in range(depth):
        need_ctx = i < depth - 1
        mod = mod_all[i]
        w_i = w_in[i].astype(BF16)
        h1 = _modnorm(xc, mod, cl)
        pqkv, phy, prw, plr, gates = (_proj(h1, w_i[:, col[j]:col[j + 1]]) for j in range(5))

        qn, kt, vx = _attn_prep(pqkv, cos2, sin2, q_norm[i], k_norm[i])
        y_att_x = _attention(qn, kt, vx, cl, 'x', cl + l)
        if need_ctx:
            y_att_c = _attention(qn, kt, vx, cl, 'ctx', cl)
        else:
            y_att_c = jnp.zeros((b, cl, ATTN_WIDTH), BF16)
        y_att = jnp.concatenate([y_att_c, y_att_x], axis=1)

        hv, hx1, hx2 = _hyena_pre(phy, hy_conv_w[i], hy_conv_b[i], cl)
        filt = (hy_f1[i], hy_fb1[i], hy_f2[i], hy_fb2[i], hy_f3[i])
        y_hx = _hyena_run(hv[:, cl:], hx1[:, cl:], hx2[:, cl:], _hyena_spectra(l, filt, mats_x), hy_skip[i], mats_x)
        if need_ctx:
            y_hc = _hyena_run(hv[:, :cl], hx1[:, :cl], hx2[:, :cl], _hyena_spectra(cl, filt, mats_c), hy_skip[i],
                              mats_c)
        else:
            y_hc = jnp.zeros((b, cl, HYENA_WIDTH), BF16)
        y_hy = jnp.concatenate([y_hc, y_hx], axis=1)

        y_rw = _rwkv_mixer(prw, rw_mu[i], rw_w0[i], rw_w2[i], rw_a0[i], rw_a2[i], rw_g2[i], rw_k_k[i], rw_k_a[i],
                           rw_r_k[i].reshape(-1), rw_ln_w[i], rw_ln_b[i], cl)

        la, lb = _lru_pre(plr, lru_conv_w[i], lru_conv_b[i], lru_wa[i], lru_ba[i], lru_wx[i], lru_bx[i],
                          lru_lambda[i], cl)
        y_lr = _lru_scan(la, lb, plr, cl)

        xc = _merge((y_att, y_hy, y_rw, y_lr), gates, xc, mod,
                    (w_br_attn[i], w_br_hyena[i], w_br_rwkv[i], w_br_lru[i]), w_out[i], cl)

        h2, cmb = _route(xc, mod, moe_w_grp[i], moe_b_grp[i], moe_w_rt[i], moe_b_rt[i], cl)
        xc = _moe(h2, cmb, xc, mod, moe_w1, moe_w3, moe_w2, i, cl)
    return xc[:, cl:]
```

```python
import functools
import math

import numpy as np
import jax
import jax.numpy as jnp
from jax import lax
from jax.experimental import pallas as pl
from jax.experimental.pallas import tpu as pltpu

F32 = jnp.float32
BF16 = jnp.bfloat16

HEAD_DIM = 64
GRID_W = 64
EPS = 1e-6
ATTN_HEADS = 8
ATTN_KV_HEADS = 2
ATTN_GROUP = ATTN_HEADS // ATTN_KV_HEADS
ATTN_Q_BLOCKS = 2
ATTN_WIDTH = ATTN_HEADS * HEAD_DIM
ATTN_KV_WIDTH = ATTN_KV_HEADS * HEAD_DIM
ROPE_THETA = 10000.0
HYENA_WIDTH = 256
HYENA_ORDER = 2
HYENA_BANDS = 16
HYENA_DECAY_TARGET = 1e-2
HYENA_FAST_DECAY = 0.3
HYENA_SLOW_DECAY = 1.5
RWKV_HEADS = 4
RWKV_WIDTH = RWKV_HEADS * HEAD_DIM
RWKV_DECAY_RANK = 64
RWKV_ICLR_RANK = 64
RWKV_GATE_RANK = 128
RWKV_GN_EPS = 64e-5
RWKV_PROJ = 3 * RWKV_WIDTH + 2 * RWKV_DECAY_RANK + 2 * RWKV_ICLR_RANK + RWKV_GATE_RANK
RWKV_CHUNK = 64
RWKV_INV_BASE = 4
RWKV_STAGE_ROWS = 128
LRU_WIDTH = 256
LRU_BLOCKS = 4
LRU_C = 8.0
LRU_GROUPS_PER_STEP = 4
N_BRANCH = 4
N_GROUPS = 4
EXPERTS_PER_GROUP = 4
N_EXPERTS = N_GROUPS * EXPERTS_PER_GROUP
MOE_BLOCK_ROWS = 512

V7X_VMEM_LIMIT_BYTES = 52 * 1024 * 1024
SUBLANES = 8
LANES = 128
MOD_ROWS = 16


def _cparams(*sem):
    return pltpu.CompilerParams(dimension_semantics=sem, vmem_limit_bytes=V7X_VMEM_LIMIT_BYTES)


def _dot(a, b):
    return jnp.dot(a, b, preferred_element_type=F32)


def _dot_nt(a, b):
    return lax.dot_general(a, b, (((1,), (1,)), ((), ())), preferred_element_type=F32)


def _split3(x):
    hi = x.astype(BF16)
    r1 = x - hi.astype(F32)
    mid = r1.astype(BF16)
    lo = (r1 - mid.astype(F32)).astype(BF16)
    return hi, mid, lo


def _dot_exact_lhs(m_bf16, x):
    hi, mid, lo = _split3(x)
    return _dot(m_bf16, hi) + _dot(m_bf16, mid) + _dot(m_bf16, lo)


def _dot_exact_rhs(x, m_bf16):
    hi = x.astype(BF16)
    lo = (x - hi.astype(F32)).astype(BF16)
    return _dot(hi, m_bf16) + _dot(lo, m_bf16)


def _dot_bf(a, b):
    return _dot(a.astype(BF16), b.astype(BF16))


def _sigmoid(x):
    return 0.5 * jnp.tanh(0.5 * x) + 0.5


def _softplus(x):
    return jnp.maximum(x, 0.0) + jnp.log(1.0 + jnp.exp(-jnp.abs(x)))


def _silu(x):
    return x * _sigmoid(x)


def _largest_tile(n, cap, mult):
    best = None
    for t in range(mult, min(n, cap) + 1, mult):
        if n % t == 0:
            best = t
    assert best is not None, (n, cap, mult)
    return best


def _head_ones(width):
    idx = np.arange(width) // HEAD_DIM
    return jnp.asarray((idx[:, None] == idx[None, :]).astype(np.float32), dtype=BF16)


def _row_ids(tile_rows, tile_idx):
    return tile_idx * tile_rows + lax.broadcasted_iota(jnp.int32, (tile_rows, 1), 0)


def _mod_rows(modx_ref, modc_ref, idx, is_ctx):
    return jnp.where(is_ctx, modc_ref[0, idx:idx + 1, :], modx_ref[0, idx:idx + 1, :])


def _rms_modulate(x, shift, scale):
    ms = jnp.mean(x * x, axis=-1, keepdims=True)
    return (x * lax.rsqrt(ms + EPS)) * (1.0 + scale) + shift


def _ada_kernel(c_ref, w_ref, b_ref, o_ref):
    s = _silu(c_ref[...])
    o_ref[0] = jnp.dot(s, w_ref[0], preferred_element_type=F32, precision=lax.Precision.HIGHEST) + b_ref[0]


def _ada_mod(cc, ada_w, ada_b):
    depth, d, n6 = ada_w.shape
    tn = _largest_tile(n6, 1024, LANES)
    return pl.pallas_call(
        _ada_kernel,
        out_shape=jax.ShapeDtypeStruct((depth, MOD_ROWS, n6), F32),
        grid=(depth, n6 // tn),
        in_specs=[pl.BlockSpec((MOD_ROWS, d), lambda i, j: (0, 0)),
                  pl.BlockSpec((1, d, tn), lambda i, j: (i, 0, j)),
                  pl.BlockSpec((1, 1, tn), lambda i, j: (i, 0, j))],
        out_specs=pl.BlockSpec((1, MOD_ROWS, tn), lambda i, j: (i, 0, j)),
        compiler_params=_cparams("parallel", "parallel"),
        name="ada_mod",
    )(cc, ada_w, ada_b.reshape(depth, 1, n6))


def _modnorm_kernel(x_ref, modx_ref, modc_ref, o_ref, *, cl, tm):
    is_ctx = _row_ids(tm, pl.program_id(1)) < cl
    h = _rms_modulate(x_ref[0], _mod_rows(modx_ref, modc_ref, 0, is_ctx), _mod_rows(modx_ref, modc_ref, 1, is_ctx))
    o_ref[0] = h.astype(o_ref.dtype)


def _modnorm(xc, mod, cl):
    b, lt, d = xc.shape
    tm = _largest_tile(lt, 1088, 16)
    return pl.pallas_call(
        functools.partial(_modnorm_kernel, cl=cl, tm=tm),
        out_shape=jax.ShapeDtypeStruct((b, lt, d), BF16), grid=(b, lt // tm),
        in_specs=[pl.BlockSpec((1, tm, d), lambda i, r: (i, r, 0)),
                  pl.BlockSpec((1, 6, d), lambda i, r: (i, 0, 0)),
                  pl.BlockSpec((1, 6, d), lambda i, r: (b, 0, 0))],
        out_specs=pl.BlockSpec((1, tm, d), lambda i, r: (i, r, 0)),
        compiler_params=_cparams("parallel", "parallel"), name="modnorm",
    )(xc, mod, mod)


def _proj_kernel(h_ref, w_ref, o_ref):
    o_ref[0] = _dot(h_ref[0], w_ref[...]).astype(o_ref.dtype)


def _proj(h, w):
    b, lt, d = h.shape
    n = w.shape[1]
    tm = _largest_tile(lt, 2176, 16)
    tn = n if n <= 1280 else _largest_tile(n, 1024, 2 * LANES)
    return pl.pallas_call(
        _proj_kernel, out_shape=jax.ShapeDtypeStruct((b, lt, n), BF16), grid=(b, lt // tm, n // tn),
        in_specs=[pl.BlockSpec((1, tm, d), lambda i, r, j: (i, r, 0)), pl.BlockSpec((d, tn), lambda i, r, j: (0, j))],
        out_specs=pl.BlockSpec((1, tm, tn), lambda i, r, j: (i, r, j)),
        compiler_params=_cparams("parallel", "parallel", "parallel"), name="proj",
    )(h, w)


def _rope_tables(l, cl):
    n_freq = HEAD_DIM // 4
    t = jnp.arange(l)
    freqs = ROPE_THETA ** (-jnp.arange(n_freq, dtype=F32) / n_freq)
    pos = jnp.stack([t // GRID_W, t % GRID_W], -1).astype(F32)
    ang = pos[..., None] * freqs
    cos64 = jnp.stack([jnp.cos(ang), jnp.cos(ang)], axis=2).reshape(l, HEAD_DIM)
    sin64 = jnp.stack([-jnp.sin(ang), jnp.sin(ang)], axis=2).reshape(l, HEAD_DIM)
    cos64 = jnp.concatenate([jnp.ones((cl, HEAD_DIM), F32), cos64], 0)
    sin64 = jnp.concatenate([jnp.zeros((cl, HEAD_DIM), F32), sin64], 0)
    return jnp.tile(cos64, (1, 2)), jnp.tile(sin64, (1, 2))


def _head_rms(t, ones_ref):
    ms = _dot_exact_rhs(t * t, ones_ref[...]) * (1.0 / HEAD_DIM)
    return t * lax.rsqrt(ms + EPS)


def _rope(t, cos, sin):
    w = t.shape[-1]
    lane = lax.broadcasted_iota(jnp.int32, t.shape, 1)
    q4 = HEAD_DIM // 4
    first_half = (lane % (2 * q4)) < q4
    partner = jnp.where(first_half, pltpu.roll(t, w - q4, 1), pltpu.roll(t, q4, 1))
    return t * cos + partner * sin


def _attn_prep_kernel(p_ref, cos_ref, sin_ref, qg_ref, kg_ref, oq_ref, ok_ref, q_ref, kt_ref, vx_ref):
    p = p_ref[0].astype(F32)
    v = p[:, ATTN_WIDTH + ATTN_KV_WIDTH:]
    low = lax.broadcasted_iota(jnp.int32, v.shape, 1) < HEAD_DIM
    vx_ref[0, 0] = jnp.where(low, v, 1.0).astype(vx_ref.dtype)
    vx_ref[0, 1] = jnp.where(low, pltpu.roll(v, HEAD_DIM, 1), 1.0).astype(vx_ref.dtype)
    cos2, sin2 = cos_ref[...], sin_ref[...]
    reps = ATTN_WIDTH // (2 * HEAD_DIM)
    cos_q = jnp.concatenate([cos2] * reps, axis=1)
    sin_q = jnp.concatenate([sin2] * reps, axis=1)
    q = _head_rms(p[:, :ATTN_WIDTH], oq_ref) * qg_ref[...]
    q_ref[0] = _rope(q, cos_q, sin_q).astype(q_ref.dtype)
    k = _head_rms(p[:, ATTN_WIDTH:ATTN_WIDTH + ATTN_KV_WIDTH], ok_ref) * kg_ref[...]
    kt_ref[0] = _rope(k, cos2, sin2).T.astype(kt_ref.dtype)


def _attn_prep(pqkv, cos2, sin2, q_gain, k_gain):
    b, lt, wtot = pqkv.shape
    tr = _largest_tile(lt, 2176, LANES)
    qg = jnp.tile(q_gain * (HEAD_DIM ** -0.5 * math.log2(math.e)), ATTN_HEADS).reshape(1, ATTN_WIDTH)
    kg = jnp.tile(k_gain, ATTN_KV_HEADS).reshape(1, ATTN_KV_WIDTH)
    return pl.pallas_call(
        _attn_prep_kernel,
        out_shape=(jax.ShapeDtypeStruct((b, lt, ATTN_WIDTH), BF16),
                   jax.ShapeDtypeStruct((b, ATTN_KV_WIDTH, lt), BF16),
                   jax.ShapeDtypeStruct((b, ATTN_KV_HEADS, lt, 2 * HEAD_DIM), BF16)),
        grid=(b, lt // tr),
        in_specs=[pl.BlockSpec((1, tr, wtot), lambda i, r: (i, r, 0)),
                  pl.BlockSpec((tr, 2 * HEAD_DIM), lambda i, r: (r, 0)),
                  pl.BlockSpec((tr, 2 * HEAD_DIM), lambda i, r: (r, 0)),
                  pl.BlockSpec((1, ATTN_WIDTH), lambda i, r: (0, 0)),
                  pl.BlockSpec((1, ATTN_KV_WIDTH), lambda i, r: (0, 0)),
                  pl.BlockSpec((ATTN_WIDTH, ATTN_WIDTH), lambda i, r: (0, 0)),
                  pl.BlockSpec((ATTN_KV_WIDTH, ATTN_KV_WIDTH), lambda i, r: (0, 0))],
        out_specs=(pl.BlockSpec((1, tr, ATTN_WIDTH), lambda i, r: (i, r, 0)),
                   pl.BlockSpec((1, ATTN_KV_WIDTH, tr), lambda i, r: (i, 0, r)),
                   pl.BlockSpec((1, ATTN_KV_HEADS, tr, 2 * HEAD_DIM), lambda i, r: (i, 0, r, 0))),
        compiler_params=_cparams("parallel", "parallel"),
        name="attn_prep",
    )(pqkv, cos2, sin2, qg, kg, _head_ones(ATTN_WIDTH), _head_ones(ATTN_KV_WIDTH))


def _attn_kernel(*refs, nq, nk):
    q_refs, (kt_ref, v_ref, o_ref) = refs[:nq], refs[nq:]
    outs = []
    for h in range(ATTN_HEADS):
        kv = h // ATTN_GROUP
        hs = slice(h * HEAD_DIM, (h + 1) * HEAD_DIM)
        qh = jnp.concatenate([q_ref[0, :, hs] for q_ref in q_refs], axis=0)
        s = _dot(qh, kt_ref[0, kv * HEAD_DIM:(kv + 1) * HEAD_DIM, :nk])
        m = jnp.max(s, axis=-1, keepdims=True)
        p = jnp.exp2(s - m)
        o = _dot(p.astype(BF16), v_ref[0, kv, :nk, :])
        outs.append(o[:, :HEAD_DIM] / o[:, HEAD_DIM:])
    o_ref[0] = jnp.concatenate(outs, axis=-1).astype(o_ref.dtype)


def _attention(qn, kt, vx, cl, rows, nk):
    b, lt, _ = qn.shape
    blk = 256 if (cl % 256 == 0 and lt % 256 == 0) else 128
    if rows == 'ctx':
        n_rows, first, nq = cl, 0, 1
    else:
        n_rows, first = lt - cl, cl // blk
        nq = ATTN_Q_BLOCKS if n_rows % (ATTN_Q_BLOCKS * blk) == 0 else 1
    tq = nq * blk
    q_specs = [pl.BlockSpec((1, blk, ATTN_WIDTH), lambda i, t, j=j: (i, first + nq * t + j, 0)) for j in range(nq)]
    return pl.pallas_call(
        functools.partial(_attn_kernel, nq=nq, nk=nk),
        out_shape=jax.ShapeDtypeStruct((b, n_rows, ATTN_WIDTH), BF16),
        grid=(b, n_rows // tq),
        in_specs=q_specs + [pl.BlockSpec((1, ATTN_KV_WIDTH, lt), lambda i, t: (i, 0, 0)),
                            pl.BlockSpec((1, ATTN_KV_HEADS, lt, 2 * HEAD_DIM), lambda i, t: (i, 0, 0, 0))],
        out_specs=pl.BlockSpec((1, tq, ATTN_WIDTH), lambda i, t: (i, t, 0)),
        compiler_params=_cparams("parallel", "parallel"),
        name="attention_" + rows,
    )(*([qn] * nq), kt, vx)


def _halo_specs(tr, lt, width, lead):
    per = tr // SUBLANES
    last = lt // SUBLANES - 1
    nlead = len(lead)

    def prev_map(*ids):
        return (*ids[:nlead], jnp.maximum(ids[nlead] * per - 1, 0), 0)

    def next_map(*ids):
        return (*ids[:nlead], jnp.minimum((ids[nlead] + 1) * per, last), 0)

    blk = (*lead, SUBLANES, width)
    return pl.BlockSpec(blk, prev_map), pl.BlockSpec(blk, next_map)


def _shift_rows(x, halo, offset, rows, cl, lt):
    tr = x.shape[0]
    local = lax.broadcasted_iota(jnp.int32, (tr, 1), 0)
    if offset < 0:
        y = pltpu.roll(x, -offset, 0)
        y = jnp.where(local == 0, halo, y)
        bad = (rows == 0) | (rows == cl)
    else:
        y = pltpu.roll(x, tr - offset, 0)
        for j in range(offset):
            y = jnp.where(local == tr - offset + j, halo[j:j + 1, :], y)
        bad = (rows >= lt - offset) | ((rows >= cl - offset) & (rows < cl))
    return jnp.where(bad, 0.0, y)


def _hyena_pre_kernel(p_ref, pv_ref, nx_ref, w_ref, b_ref, v_ref, x1_ref, x2_ref, *, cl, lt, tr):
    rows = _row_ids(tr, pl.program_id(1))
    p = p_ref[0].astype(F32)
    pm = _shift_rows(p, pv_ref[0, SUBLANES - 1:, :].astype(F32), -1, rows, cl, lt)
    pp = _shift_rows(p, nx_ref[0, :1, :].astype(F32), 1, rows, cl, lt)
    z = pm * w_ref[0:1, :] + p * w_ref[1:2, :] + pp * w_ref[2:3, :] + b_ref[...]
    c = HYENA_WIDTH
    v_ref[0] = z[:, :c].astype(v_ref.dtype)
    x1_ref[0] = z[:, c:2 * c].astype(x1_ref.dtype)
    x2_ref[0] = z[:, 2 * c:].astype(x2_ref.dtype)


def _hyena_pre(phy, conv_w, conv_b, cl):
    b, lt, w = phy.shape
    tr = _largest_tile(lt, 1088, 16)
    prev_spec, next_spec = _halo_specs(tr, lt, w, (1,))
    out = jax.ShapeDtypeStruct((b, lt, HYENA_WIDTH), BF16)
    ospec = pl.BlockSpec((1, tr, HYENA_WIDTH), lambda i, r: (i, r, 0))
    return pl.pallas_call(
        functools.partial(_hyena_pre_kernel, cl=cl, lt=lt, tr=tr),
        out_shape=(out, out, out),
        grid=(b, lt // tr),
        in_specs=[pl.BlockSpec((1, tr, w), lambda i, r: (i, r, 0)), prev_spec, next_spec,
                  pl.BlockSpec(conv_w.shape, lambda i, r: (0, 0)),
                  pl.BlockSpec((1, w), lambda i, r: (0, 0))],
        out_specs=(ospec, ospec, ospec),
        compiler_params=_cparams("parallel", "parallel"),
        name="hyena_pre",
    )(phy, phy, phy, conv_w, conv_b.reshape(1, w))


def _trig_mats_kernel(cb_ref, sb_ref, co_ref, so_ref, c_ref, s_ref):
    cb, sb, co, so = cb_ref[0], sb_ref[0], co_ref[...], so_ref[...]
    c_ref[...] = (cb * co - sb * so).astype(c_ref.dtype)
    s_ref[...] = (sb * co + cb * so).astype(s_ref.dtype)


def _trig_mats(n, h, row_mult, col_mult):
    tm = _largest_tile(h, 256, 16)
    tn = _largest_tile(h, 1024, LANES) if h % LANES == 0 else h
    b = col_mult(jnp.arange(h, dtype=jnp.int32))[None, :]

    def tables(a):
        ang = ((a[:, None] * b) % (2 * n)).astype(F32) * (math.pi / n)
        return jnp.cos(ang), jnp.sin(ang)

    r0 = jnp.arange(0, h, tm, dtype=jnp.int32)
    cb, sb = tables(row_mult(r0))
    co, so = tables(row_mult(jnp.arange(tm, dtype=jnp.int32)) - row_mult(jnp.zeros((tm,), jnp.int32)))
    base = pl.BlockSpec((1, 1, tn), lambda j, c: (j, 0, c))
    off = pl.BlockSpec((tm, tn), lambda j, c: (0, c))
    out = jax.ShapeDtypeStruct((h, h), BF16)
    ospec = pl.BlockSpec((tm, tn), lambda j, c: (j, c))
    return pl.pallas_call(
        _trig_mats_kernel, out_shape=(out, out), grid=(h // tm, h // tn),
        in_specs=[base, base, off, off], out_specs=(ospec, ospec),
        compiler_params=_cparams("parallel", "parallel"), name="trig_mats",
    )(cb.reshape(h // tm, 1, h), sb.reshape(h // tm, 1, h), co, so)


def _dft_mats(n):
    h = n // 2
    ident = lambda r: r
    ce, se = _trig_mats(n, h, ident, lambda j: 2 * j)
    co, so = _trig_mats(n, h, ident, lambda j: 2 * j + 1)
    cot, sot = _trig_mats(n, h, lambda j: 2 * j + 1, ident)
    return ce, se, co, so, cot, sot


def _alt_sum(z):
    j = lax.broadcasted_iota(jnp.int32, (z.shape[0], 1), 0)
    return jnp.sum(z * (1 - 2 * (j % 2)).astype(F32), axis=0, keepdims=True)


def _half_spectra(ce_ref, se_ref, co_ref, so_ref, z_ev, z_od):
    p, q = _dot(ce_ref[...], z_ev), _dot(co_ref[...], z_od)
    ps, qs = _dot(se_ref[...], z_ev), _dot(so_ref[...], z_od)
    return p + q, ps + qs, p - q, qs - ps


def _dft_raw_kernel(ce_ref, se_ref, co_ref, so_ref, z_ref, rl_ref, il_ref, ru_ref, iu_ref, mid_ref):
    z_ev, z_od = z_ref[0], z_ref[1]
    rl_ref[...], il_ref[...], ru_ref[...], iu_ref[...] = _half_spectra(ce_ref, se_ref, co_ref, so_ref, z_ev, z_od)

    @pl.when(pl.program_id(0) == 0)
    def _():
        mid = jnp.concatenate([_alt_sum(z_ev.astype(F32)), _alt_sum(z_od.astype(F32))], axis=0)
        mid_ref[...] = jnp.concatenate([mid, jnp.zeros((SUBLANES - 2, mid.shape[1]), F32)], axis=0)


def _dft_raw(mats, zs):
    ce, se, co, so, _, _ = mats
    _, h, c = zs.shape
    tm = _largest_tile(h, 256, 16)
    mat = pl.BlockSpec((tm, h), lambda j: (j, 0))
    out = jax.ShapeDtypeStruct((h, c), F32)
    ospec = pl.BlockSpec((tm, c), lambda j: (j, 0))
    return pl.pallas_call(
        _dft_raw_kernel, out_shape=(out, out, out, out, jax.ShapeDtypeStruct((SUBLANES, c), F32)), grid=(h // tm,),
        in_specs=[mat, mat, mat, mat, pl.BlockSpec((2, h, c), lambda j: (0, 0, 0))],
        out_specs=(ospec, ospec, ospec, ospec, pl.BlockSpec((SUBLANES, c), lambda j: (0, 0))),
        compiler_params=_cparams("arbitrary"), name="dft_raw",
    )(ce, se, co, so, zs)


def _dft_fwd_kernel(ce_ref, se_ref, co_ref, so_ref, z_ref, hrl_ref, hil_ref, hru_ref, hiu_ref, hmid_ref,
                    ea_ref, eb_ref, oa_ref, ob_ref, mid_ref, *, bb):
    hrl, hil, hru, hiu = hrl_ref[...], hil_ref[...], hru_ref[...], hiu_ref[...]
    for i in range(bb):
        z_ev, z_od = z_ref[i, 0], z_ref[i, 1]
        zrl, zil, zru, ziu = _half_spectra(ce_ref, se_ref, co_ref, so_ref, z_ev, z_od)
        yrl, yil = zrl * hrl + zil * hil, zil * hrl - zrl * hil
        yru, yiu = zru * hru + ziu * hiu, ziu * hru - zru * hiu
        ea_ref[i] = (yrl + yru).astype(ea_ref.dtype)
        eb_ref[i] = (yil - yiu).astype(eb_ref.dtype)
        oa_ref[i] = (yrl - yru).astype(oa_ref.dtype)
        ob_ref[i] = (yil + yiu).astype(ob_ref.dtype)

    @pl.when(pl.program_id(1) == 0)
    def _():
        hr, hi = hmid_ref[0:1, :], hmid_ref[1:2, :]
        for i in range(bb):
            zr, zi = _alt_sum(z_ref[i, 0].astype(F32)), _alt_sum(z_ref[i, 1].astype(F32))
            mid = jnp.concatenate([zr * hr + zi * hi, zi * hr - zr * hi], axis=0)
            mid_ref[i] = jnp.concatenate([mid, jnp.zeros((SUBLANES - 2, mid.shape[1]), F32)], axis=0)


def _dft_inv_kernel(ce_ref, se_ref, cot_ref, sot_ref, ea_ref, eb_ref, oa_ref, ob_ref, mid_ref, z_ref, g_ref, skip_ref,
                    o_ref, *, bb, tm):
    alt = (1 - 2 * (_row_ids(tm, pl.program_id(1)) % 2)).astype(F32)
    for i in range(bb):
        y_ev = _dot(ce_ref[...], ea_ref[i]) + _dot(se_ref[...], eb_ref[i]) + alt * mid_ref[i, 0:1, :]
        y_od = _dot(cot_ref[...], oa_ref[i]) + _dot(sot_ref[...], ob_ref[i]) + alt * mid_ref[i, 1:2, :]
        for par, y in enumerate((y_ev, y_od)):
            y = y + z_ref[i, par].astype(F32) * skip_ref[...]
            o_ref[i, par] = (g_ref[i, par].astype(F32) * y).astype(o_ref.dtype)


def _longconv_gated(zs, gates, spectrum, skip, mats):
    ce, se, co, so, cot, sot = mats
    b, _, h, c = zs.shape
    bb = 2 if b % 2 == 0 else 1
    tm = _largest_tile(h, 256, 16)
    mat = pl.BlockSpec((tm, h), lambda i, j: (j, 0))
    full4 = pl.BlockSpec((bb, 2, h, c), lambda i, j: (i, 0, 0, 0))
    full = pl.BlockSpec((bb, h, c), lambda i, j: (i, 0, 0))
    tile = pl.BlockSpec((bb, tm, c), lambda i, j: (i, j, 0))
    tile4 = pl.BlockSpec((bb, 2, tm, c), lambda i, j: (i, 0, j, 0))
    filt = pl.BlockSpec((tm, c), lambda i, j: (j, 0))
    mid_spec = pl.BlockSpec((bb, SUBLANES, c), lambda i, j: (i, 0, 0))
    half = jax.ShapeDtypeStruct((b, h, c), BF16)
    *combos, mid = pl.pallas_call(
        functools.partial(_dft_fwd_kernel, bb=bb),
        out_shape=(half, half, half, half, jax.ShapeDtypeStruct((b, SUBLANES, c), F32)), grid=(b // bb, h // tm),
        in_specs=[mat, mat, mat, mat, full4, filt, filt, filt, filt, pl.BlockSpec((SUBLANES, c), lambda i, j: (0, 0))],
        out_specs=(tile, tile, tile, tile, mid_spec),
        compiler_params=_cparams("parallel", "arbitrary"), name="dft_fwd",
    )(ce, se, co, so, zs, *spectrum)
    return pl.pallas_call(
        functools.partial(_dft_inv_kernel, bb=bb, tm=tm),
        out_shape=jax.ShapeDtypeStruct((b, 2, h, c), BF16), grid=(b // bb, h // tm),
        in_specs=[mat, mat, mat, mat, full, full, full, full, mid_spec, tile4, tile4,
                  pl.BlockSpec((1, c), lambda i, j: (0, 0))],
        out_specs=tile4,
        compiler_params=_cparams("parallel", "parallel"), name="dft_inv",
    )(ce, se, cot, sot, *combos, mid, zs, gates, skip.reshape(1, c))


def _hyena_filters(n, f1, fb1, f2, fb2, f3):
    t = jnp.arange(n, dtype=F32) / n
    bands = jnp.arange(1, HYENA_BANDS + 1, dtype=F32)
    ang = 2.0 * math.pi * t[:, None] * bands
    feat = jnp.concatenate([t[:, None], jnp.sin(ang), jnp.cos(ang)], axis=-1)
    hp = lax.Precision.HIGHEST
    h = jnp.sin(jnp.dot(feat, f1, precision=hp) + fb1)
    h = jnp.sin(jnp.dot(h, f2, precision=hp) + fb2)
    h = jnp.dot(h, f3, precision=hp).reshape(n, HYENA_ORDER, 2, HYENA_WIDTH)
    deltas = jnp.linspace(-math.log(HYENA_DECAY_TARGET) / HYENA_SLOW_DECAY,
                          -math.log(HYENA_DECAY_TARGET) / HYENA_FAST_DECAY, HYENA_WIDTH, dtype=F32)
    h = h * jnp.exp(-t[:, None] * deltas)[:, None, None, :]
    return h / jnp.sum(jnp.abs(h), axis=(0, 2), keepdims=True)


def _parity_split(t):
    *lead, n, c = t.shape
    return jnp.swapaxes(t.reshape(*lead, n // 2, 2, c), -2, -3)


def _parity_merge(t):
    *lead, _, h, c = t.shape
    return jnp.swapaxes(t, -2, -3).reshape(*lead, 2 * h, c)


def _hyena_spectra(n, filt_params, mats):
    h = _hyena_filters(n, *filt_params)
    oc = HYENA_ORDER * HYENA_WIDTH
    hf = h[:, :, 0].reshape(n, oc)
    hb = h[:, :, 1].reshape(n, oc)
    hb = jnp.where(jnp.arange(n)[:, None] == 0, 0.0, hb)
    sig = _parity_split(jnp.concatenate([hf + hb, hb - hf], axis=1).astype(BF16))
    rl, il, ru, iu, mid = _dft_raw(mats, sig)
    scale = 1.0 / n
    ends = jnp.where(jnp.arange(n // 2)[:, None] == 0, 0.5 * scale, scale)
    spectra = []
    for o in range(HYENA_ORDER):
        re = slice(o * HYENA_WIDTH, (o + 1) * HYENA_WIDTH)
        im = slice(oc + o * HYENA_WIDTH, oc + (o + 1) * HYENA_WIDTH)
        hmid = jnp.zeros((SUBLANES, HYENA_WIDTH), F32).at[0].set(mid[0, re] * scale).at[1].set(mid[1, im] * scale)
        spectra.append((rl[:, re] * ends, il[:, im] * scale, ru[:, re] * ends, iu[:, im] * scale, hmid))
    return spectra


def _hyena_run(v, x1, x2, spectra, skip, mats):
    y = _parity_split(v)
    for o, gate in enumerate((x1, x2)):
        y = _longconv_gated(y, _parity_split(gate), spectra[o], skip[o], mats)
    return _parity_merge(y)


def _blockdiag_dense(w):
    nb, blk = w.shape[1], w.shape[2]
    eye = jnp.eye(nb, dtype=w.dtype)
    return jnp.einsum('dncf,nm->dncmf', w, eye).reshape(w.shape[0], nb * blk, nb * blk)


def _lru_pre_kernel(p_ref, pv_ref, nx_ref, cw_ref, cb_ref, wa_ref, ba_ref, wx_ref, bx_ref, lam_ref,
                    a_ref, b_ref, *, cl, lt, tr):
    rows = _row_ids(tr, pl.program_id(1))
    c = LRU_WIDTH
    x = p_ref[0][:, c:].astype(F32)
    pv = pv_ref[0][SUBLANES - 1:, c:].astype(F32)
    nx = nx_ref[0][:, c:].astype(F32)
    xc = (_shift_rows(x, pv, -1, rows, cl, lt) * cw_ref[0:1, :] + x * cw_ref[1:2, :]
          + _shift_rows(x, nx[:1], 1, rows, cl, lt) * cw_ref[2:3, :]
          + _shift_rows(x, nx[:2], 2, rows, cl, lt) * cw_ref[3:4, :] + cb_ref[...])
    xcb = xc.astype(BF16)
    for d in range(2):
        r = _sigmoid(_dot(xcb, wa_ref[d]) + ba_ref[d])
        i = _sigmoid(_dot(xcb, wx_ref[d]) + bx_ref[d])
        log_a = -LRU_C * r * _softplus(-lam_ref[d])
        a_ref[d, 0] = jnp.exp(log_a)
        b_ref[d, 0] = jnp.sqrt(1.0 - jnp.exp(2.0 * log_a)) * (i * xc)


def _lru_pre(plr, conv_w, conv_b, wa, ba, wx, bx, lam, cl):
    b, lt, w = plr.shape
    c = LRU_WIDTH
    tr = _largest_tile(lt, 1088, 16)
    prev_spec, next_spec = _halo_specs(tr, lt, w, (1,))
    out = jax.ShapeDtypeStruct((2, b, lt, c), F32)
    ospec = pl.BlockSpec((2, 1, tr, c), lambda i, r: (0, i, r, 0))
    const2 = lambda shape: pl.BlockSpec(shape, lambda i, r: (0,) * len(shape))
    return pl.pallas_call(
        functools.partial(_lru_pre_kernel, cl=cl, lt=lt, tr=tr),
        out_shape=(out, out), grid=(b, lt // tr),
        in_specs=[pl.BlockSpec((1, tr, w), lambda i, r: (i, r, 0)), prev_spec, next_spec,
                  const2(conv_w.shape), const2((1, c)), const2((2, c, c)), const2((2, 1, c)),
                  const2((2, c, c)), const2((2, 1, c)), const2((2, 1, c))],
        out_specs=(ospec, ospec),
        compiler_params=_cparams("parallel", "parallel"), name="lru_pre",
    )(plr, plr, plr, conv_w, conv_b.reshape(1, c), _blockdiag_dense(wa).astype(BF16), ba.reshape(2, 1, c),
      _blockdiag_dense(wx).astype(BF16), bx.reshape(2, 1, c), lam.reshape(2, 1, c))


def _gelu_tanh(x):
    return 0.5 * x * (1.0 + jnp.tanh(math.sqrt(2.0 / math.pi) * (x + 0.044715 * (x * x * x))))


def _lru_scan_kernel(af_ref, bf_ref, ar_ref, br_ref, g_ref, o_ref, acc_ref, *, cl, lt):
    row = lax.broadcasted_iota(jnp.int32, (SUBLANES, LANES), 0)

    def group_scan(a, b, reverse):
        for s in (1, 2, 4):
            if reverse:
                keep = row < SUBLANES - s
                a_s = jnp.where(keep, pltpu.roll(a, SUBLANES - s, 0), 1.0)
                b_s = jnp.where(keep, pltpu.roll(b, SUBLANES - s, 0), 0.0)
            else:
                keep = row >= s
                a_s = jnp.where(keep, pltpu.roll(a, s, 0), 1.0)
                b_s = jnp.where(keep, pltpu.roll(b, s, 0), 0.0)
            b = a * b_s + b
            a = a * a_s
        return a, b

    ng = math.gcd(math.gcd(cl // SUBLANES, (lt - cl) // SUBLANES), LRU_GROUPS_PER_STEP)
    span = ng * SUBLANES

    def fwd_body(i, h):
        sl = pl.ds(pl.multiple_of(i * span, span), span)
        a_all, b_all = af_ref[0, 0, sl, :], bf_ref[0, 0, sl, :]
        scans = [group_scan(a_all[j * SUBLANES:(j + 1) * SUBLANES], b_all[j * SUBLANES:(j + 1) * SUBLANES], False)
                 for j in range(ng)]
        outs = []
        for a, b in scans:
            hh = a * h + b
            outs.append(hh)
            h = hh[SUBLANES - 1:SUBLANES, :]
        acc_ref[sl, :] = jnp.concatenate(outs, axis=0)
        return h

    lax.fori_loop(0, lt // span, fwd_body, jnp.zeros((1, LANES), F32))

    def rev_body(i, h, top):
        sl = pl.ds(pl.multiple_of((top - 1 - i) * span, span), span)
        a_all, b_all = ar_ref[0, 0, sl, :], br_ref[0, 0, sl, :]
        scans = [group_scan(a_all[j * SUBLANES:(j + 1) * SUBLANES], b_all[j * SUBLANES:(j + 1) * SUBLANES], True)
                 for j in range(ng)]
        outs = [None] * ng
        for j in reversed(range(ng)):
            a, b = scans[j]
            hh = a * h + b
            outs[j] = hh
            h = hh[0:1, :]
        gate = g_ref[0, sl, :].astype(F32)
        o_ref[0, sl, :] = ((acc_ref[sl, :] + jnp.concatenate(outs, axis=0)) * _gelu_tanh(gate)).astype(o_ref.dtype)
        return h

    h = lax.fori_loop(0, cl // span, functools.partial(rev_body, top=cl // span), jnp.zeros((1, LANES), F32))
    lax.fori_loop(0, (lt - cl) // span, functools.partial(rev_body, top=lt // span), h)


def _lru_scan(a, b_, plr, cl):
    _, b, lt, c = a.shape
    nl = c // LANES
    fwd = pl.BlockSpec((1, 1, lt, LANES), lambda i, j: (0, i, 0, j))
    rev = pl.BlockSpec((1, 1, lt, LANES), lambda i, j: (1, i, 0, j))
    return pl.pallas_call(
        functools.partial(_lru_scan_kernel, cl=cl, lt=lt),
        out_shape=jax.ShapeDtypeStruct((b, lt, c), BF16), grid=(b, nl),
        in_specs=[fwd, fwd, rev, rev, pl.BlockSpec((1, lt, LANES), lambda i, j: (i, 0, j))],
        out_specs=pl.BlockSpec((1, lt, LANES), lambda i, j: (i, 0, j)),
        scratch_shapes=[pltpu.VMEM((lt, LANES), F32)],
        compiler_params=_cparams("parallel", "parallel"), name="lru_scan",
    )(a, b_, a, b_, plr)


def _row_perm(tr, rev):
    t = lax.broadcasted_iota(jnp.int32, (tr, tr), 0)
    s = lax.broadcasted_iota(jnp.int32, (tr, tr), 1)
    return jnp.where(s == jnp.where(rev, tr - 1 - t, t), 1.0, 0.0).astype(BF16)


def _mirror_tile(r, rev, n_ctx_tiles, n_tiles):
    m = jnp.where(r < n_ctx_tiles, n_ctx_tiles - 1 - r, n_tiles - 1 + n_ctx_tiles - r)
    return jnp.where(rev, m, r)


def _chunk_masks(tr):
    t = lax.broadcasted_iota(jnp.int32, (tr, tr), 0)
    s = lax.broadcasted_iota(jnp.int32, (tr, tr), 1)
    same = (t // RWKV_CHUNK) == (s // RWKV_CHUNK)
    return same, same & (s <= t), same & (s < t)


def _rwkv_prep_kernel(p_ref, pv_ref, nx_ref, mu_ref, w0_ref, w2_ref, a0_ref, a2_ref, g2_ref, kk_ref, ka_ref,
                      rk_ref, ones_ref,
                      aq_ref, vp_ref, y0_ref, rt_ref, mrb_ref, bht_ref, gm_ref, pc_ref, g_ref, bonus_ref,
                      *, cl, lt, tr):
    c = RWKV_WIDTH
    rows = _row_ids(tr, pl.program_id(2))
    rev = pl.program_id(0) == 1
    p = _dot(_row_perm(tr, rev), p_ref[0])
    before = jnp.where(rev, nx_ref[0, :1, :], pv_ref[0, SUBLANES - 1:, :]).astype(F32)
    after = jnp.where(rev, pv_ref[0, SUBLANES - 1:, :], nx_ref[0, :1, :]).astype(F32)
    prev = _shift_rows(p, before, -1, rows, cl, lt)
    nxt = _shift_rows(p, after, 1, rows, cl, lt)
    xm = p + (prev - p) * mu_ref[0, 0:1, :] + (nxt - p) * mu_ref[0, 1:2, :]
    r, k, v = xm[:, :c], xm[:, c:2 * c], xm[:, 2 * c:3 * c]
    o = 3 * c
    w1 = xm[:, o:o + 2 * RWKV_DECAY_RANK]
    a1 = xm[:, o + 2 * RWKV_DECAY_RANK:o + 2 * RWKV_DECAY_RANK + 2 * RWKV_ICLR_RANK]
    g1 = xm[:, o + 2 * RWKV_DECAY_RANK + 2 * RWKV_ICLR_RANK:]
    wlog = -_softplus(-(w0_ref[0] + _dot(jnp.tanh(w1).astype(BF16), w2_ref[0]))) - 0.5
    ld = -jnp.exp(wlog)
    a = _sigmoid(a0_ref[0] + _dot(a1.astype(BF16), a2_ref[0]))
    g_ref[0, 0] = _dot(_sigmoid(g1).astype(BF16), g2_ref[...])
    kk = k * kk_ref[...]
    kk = kk * lax.rsqrt(_dot_exact_rhs(kk * kk, ones_ref[...]) + 1e-12)
    kd = k * (1.0 + (a - 1.0) * ka_ref[...])
    bonus_ref[0, 0] = _dot_exact_rhs(r * kd * rk_ref[...], ones_ref[...]) * v
    beta = kk * a

    _rwkv_chunk_stage(ld, kk, r, kd, beta, v, aq_ref, vp_ref, y0_ref, rt_ref, mrb_ref, bht_ref, gm_ref, pc_ref)


def _rwkv_chunk_stage(ld, kk, r, kd, beta, v, aq_ref, vp_ref, y0_ref, rt_ref, mrb_ref, bht_ref, gm_ref, pc_ref):
    tr = ld.shape[0]
    n = min(RWKV_STAGE_ROWS, tr)
    nparts = tr // n
    hd = HEAD_DIM
    ch = RWKV_CHUNK
    nch = n // ch
    same_t, incl_t, _ = _chunk_masks(tr)
    cum = _dot_exact_lhs(jnp.where(incl_t, 1.0, 0.0).astype(BF16), ld)
    tot = _dot_exact_lhs(jnp.where(same_t, 1.0, 0.0).astype(BF16), ld)
    _, incl, strict = _chunk_masks(n)
    alpha_t = kk * jnp.exp(cum - ld)
    r_t = r * jnp.exp(cum)
    e_neg = jnp.exp(-cum)
    k_t = kd * e_neg
    b_t = beta * e_neg
    e_rem = jnp.exp(tot - cum)
    k_hat_t = (kd * e_rem).T
    b_hat_t = (beta * e_rem).T
    pc_t = jnp.exp(tot).T
    t_i = lax.broadcasted_iota(jnp.int32, (n, n), 0)
    s_i = lax.broadcasted_iota(jnp.int32, (n, n), 1)
    eye_f = jnp.where(t_i == s_i, 1.0, 0.0)
    same_blk = []
    size = RWKV_INV_BASE
    while size <= ch:
        same_blk.append((t_i // size) == (s_i // size))
        size *= 2
    col_chunk = lax.broadcasted_iota(jnp.int32, (n, nch * hd), 1) // hd
    row_chunk = lax.broadcasted_iota(jnp.int32, (n, nch * hd), 0) // ch

    def diag_blocks(m):
        out = m[:, :ch]
        for j in range(1, nch):
            out = out + m[:, j * ch:(j + 1) * ch]
        return out

    chains = [(p, h) for p in range(nparts) for h in range(RWKV_HEADS)]

    def blk(t, p, h):
        return t[p * n:(p + 1) * n, h * hd:(h + 1) * hd]

    prods = [_dot_nt(jnp.concatenate([blk(alpha_t, p, h), blk(r_t, p, h)], axis=0).astype(BF16),
                     jnp.concatenate([blk(b_t, p, h), blk(k_t, p, h)], axis=0).astype(BF16))
             for p, h in chains]
    l_ab = [jnp.where(strict, pr[:n, :n], 0.0) for pr in prods]
    pw = [jnp.where(same_blk[0], l, 0.0) for l in l_ab]
    t_inv = [eye_f - l for l in pw]
    for _ in range(int(math.log2(RWKV_INV_BASE)) - 1):
        pw = [_dot_bf(q, q) for q in pw]
        t_inv = [t + _dot_bf(t, q) for t, q in zip(t_inv, pw)]
    for lvl in range(1, len(same_blk)):
        off = same_blk[lvl] & jnp.logical_not(same_blk[lvl - 1])
        half = [_dot_bf(t, jnp.where(off, l, 0.0)) for t, l in zip(t_inv, l_ab)]
        t_inv = [t - _dot_bf(hf, t) for t, hf in zip(t_inv, half)]
    vh = [blk(v, p, h).astype(BF16) for p, h in chains]
    lakv = [_dot(jnp.where(strict, pr[:n, n:], 0.0).astype(BF16), vv) for pr, vv in zip(prods, vh)]
    x = [_dot(t.astype(BF16), jnp.concatenate([blk(alpha_t, p, h), lv], axis=1).astype(BF16))
         for t, (p, h), lv in zip(t_inv, chains, lakv)]
    y0 = [_dot(jnp.where(incl, pr[n:, n:], 0.0).astype(BF16), vv) for pr, vv in zip(prods, vh)]
    mrb = [diag_blocks(jnp.where(incl, pr[n:, :n], 0.0)) for pr in prods]
    gms = [_dot(k_hat_t[h * hd:(h + 1) * hd, p * n:(p + 1) * n].astype(BF16),
                jnp.where(col_chunk == row_chunk, jnp.concatenate([blk(v, p, h)] * nch, axis=1), 0.0).astype(BF16))
           for p, h in chains]

    def assemble(parts):
        return jnp.concatenate([jnp.concatenate(parts[p * RWKV_HEADS:(p + 1) * RWKV_HEADS], axis=1)
                                for p in range(nparts)], axis=0)

    aq_ref[0, 0] = assemble([t[:, :hd] for t in x]).astype(aq_ref.dtype)
    vp_ref[0, 0] = assemble([t[:, hd:] for t in x])
    y0_ref[0, 0] = assemble(y0)
    rt_ref[0, 0] = r_t.astype(rt_ref.dtype)
    mrb_ref[0, 0] = assemble(mrb).astype(mrb_ref.dtype)
    for p in range(nparts):
        for j in range(nch):
            jj = p * nch + j
            cs = slice(jj * ch, (jj + 1) * ch)
            bht_ref[0, 0, jj] = jnp.concatenate([b_hat_t[h * hd:(h + 1) * hd, cs] for h in range(RWKV_HEADS)],
                                                axis=1).astype(bht_ref.dtype)
            gm_ref[0, 0, jj] = jnp.concatenate([gms[p * RWKV_HEADS + h][:, j * hd:(j + 1) * hd]
                                                for h in range(RWKV_HEADS)], axis=1)
            pc_ref[0, 0, jj] = jnp.concatenate([pc_t[h * hd:(h + 1) * hd, cs] for h in range(RWKV_HEADS)], axis=1)


def _rwkv_tile(lt, cl):
    tr = 256 if (lt % 256 == 0 and cl % 256 == 0) else 128
    assert lt % tr == 0 and cl % tr == 0
    return tr


def _rwkv_prep(prw, mud, w0, w2p, a0, a2p, g2, k_k, k_a, r_k, cl):
    b, lt, w = prw.shape
    c = RWKV_WIDTH
    tr = _rwkv_tile(lt, cl)
    nch = tr // RWKV_CHUNK
    per = tr // SUBLANES
    src = lambda d, r: _mirror_tile(r, d == 1, cl // tr, lt // tr)
    prev_spec = pl.BlockSpec((1, SUBLANES, w), lambda d, i, r: (i, jnp.maximum(src(d, r) * per - 1, 0), 0))
    next_spec = pl.BlockSpec((1, SUBLANES, w),
                             lambda d, i, r: (i, jnp.minimum((src(d, r) + 1) * per, lt // SUBLANES - 1), 0))
    tile = pl.BlockSpec((1, 1, tr, c), lambda d, i, r: (d, i, r, 0))
    per_dir = lambda shape: pl.BlockSpec((1, *shape), lambda d, i, r: (d,) + (0,) * len(shape))
    const = lambda shape: pl.BlockSpec(shape, lambda d, i, r: (0,) * len(shape))
    seq = lambda dt: jax.ShapeDtypeStruct((2, b, lt, c), dt)
    chunked = jax.ShapeDtypeStruct((2, b, lt // RWKV_CHUNK, HEAD_DIM, c), F32)
    chunk_spec = pl.BlockSpec((1, 1, nch, HEAD_DIM, c), lambda d, i, r: (d, i, r, 0, 0))
    return pl.pallas_call(
        functools.partial(_rwkv_prep_kernel, cl=cl, lt=lt, tr=tr),
        out_shape=(seq(BF16), seq(F32), seq(F32), seq(BF16), seq(BF16),
                   jax.ShapeDtypeStruct(chunked.shape, BF16), chunked, chunked, seq(F32), seq(F32)),
        grid=(2, b, lt // tr),
        in_specs=[pl.BlockSpec((1, tr, w), lambda d, i, r: (i, src(d, r), 0)), prev_spec, next_spec,
                  per_dir((2, w)), per_dir((1, c)), per_dir((2 * RWKV_DECAY_RANK, c)), per_dir((1, c)),
                  per_dir((2 * RWKV_ICLR_RANK, c)), const((RWKV_GATE_RANK, c)), const((1, c)), const((1, c)),
                  const((1, c)), const((c, c))],
        out_specs=(tile, tile, tile, tile, tile, chunk_spec, chunk_spec, chunk_spec, tile, tile),
        compiler_params=_cparams("parallel", "parallel", "parallel"), name="rwkv_prep",
    )(prw, prw, prw, mud, w0.reshape(2, 1, c), w2p, a0.reshape(2, 1, c), a2p, g2.astype(BF16), k_k.reshape(1, c),
      k_a.reshape(1, c), r_k.reshape(1, c), _head_ones(c))


def _rwkv_scan_kernel(aq_ref, vp_ref, y0_ref, rt_ref, mrb_ref, bht_ref, gm_ref, pc_ref, y_ref, s_ref, *, ns):
    @pl.when(pl.program_id(1) == 0)
    def _():
        s_ref[...] = jnp.zeros_like(s_ref)

    c = RWKV_WIDTH
    ch = RWKV_CHUNK
    same_head = (lax.broadcasted_iota(jnp.int32, (c, c), 0) // HEAD_DIM
                 == lax.broadcasted_iota(jnp.int32, (c, c), 1) // HEAD_DIM)

    def head_blockdiag(t):
        return jnp.where(same_head, jnp.concatenate([t] * RWKV_HEADS, axis=0), 0.0).astype(BF16)

    for i in range(ns):
        s0 = s_ref[i]
        r1 = _dot(jnp.concatenate([aq_ref[i, 0], rt_ref[i, 0]], axis=0), head_blockdiag(s0))
        u = r1[:ch] + vp_ref[i, 0]
        r2 = _dot(jnp.concatenate([mrb_ref[i, 0], bht_ref[i, 0, 0]], axis=0), head_blockdiag(u))
        y_ref[i, 0] = r1[ch:] + y0_ref[i, 0] - r2[:ch]
        s_ref[i] = pc_ref[i, 0, 0] * s0 + gm_ref[i, 0, 0] - r2[ch:]


def _rwkv_scan(aq, vp, y0, rt, mrb, bht, gm, pc):
    _, b, lt, c = aq.shape
    nstream = 2 * b
    ns = nstream
    ch = RWKV_CHUNK
    merge = lambda t: t.reshape(nstream, 1, *t.shape[2:])
    tile = pl.BlockSpec((ns, 1, ch, c), lambda s, j: (s, 0, j, 0))
    chunk_spec = pl.BlockSpec((ns, 1, 1, HEAD_DIM, c), lambda s, j: (s, 0, j, 0, 0))
    y = pl.pallas_call(
        functools.partial(_rwkv_scan_kernel, ns=ns),
        out_shape=jax.ShapeDtypeStruct((nstream, 1, lt, c), F32), grid=(nstream // ns, lt // ch),
        in_specs=[tile, tile, tile, tile, tile, chunk_spec, chunk_spec, chunk_spec],
        out_specs=tile,
        scratch_shapes=[pltpu.VMEM((ns, HEAD_DIM, c), F32)],
        compiler_params=_cparams("parallel", "arbitrary"), name="rwkv_scan",
    )(*(merge(t) for t in (aq, vp, y0, rt, mrb, bht, gm, pc)))
    return y.reshape(2, b, lt, c)


def _rwkv_readout_kernel(yf_ref, yr_ref, bf_ref, br_ref, g_ref, lw_ref, lb_ref, ones_ref, o_ref, *, tr):
    unflip = _row_perm(tr, True)
    y = yf_ref[0, 0] + _dot_exact_lhs(unflip, yr_ref[0, 0])
    bonus = bf_ref[0, 0] + _dot_exact_lhs(unflip, br_ref[0, 0])
    inv = 1.0 / HEAD_DIM
    mu = _dot_exact_rhs(y, ones_ref[...]) * inv
    yc = y - mu
    var = _dot_exact_rhs(yc * yc, ones_ref[...]) * inv
    yn = yc * lax.rsqrt(var + RWKV_GN_EPS) * lw_ref[...] + lb_ref[...]
    o_ref[0] = ((yn + bonus) * g_ref[0, 0]).astype(o_ref.dtype)


def _rwkv_readout(y, bonus, g, ln_w, ln_b, cl):
    _, b, lt, c = y.shape
    tr = _rwkv_tile(lt, cl)
    fwd = pl.BlockSpec((1, 1, tr, c), lambda i, r: (0, i, r, 0))
    rev = pl.BlockSpec((1, 1, tr, c), lambda i, r: (1, i, _mirror_tile(r, True, cl // tr, lt // tr), 0))
    row = pl.BlockSpec((1, c), lambda i, r: (0, 0))
    return pl.pallas_call(
        functools.partial(_rwkv_readout_kernel, tr=tr),
        out_shape=jax.ShapeDtypeStruct((b, lt, c), BF16), grid=(b, lt // tr),
        in_specs=[fwd, rev, fwd, rev, fwd, row, row, pl.BlockSpec((c, c), lambda i, r: (0, 0))],
        out_specs=pl.BlockSpec((1, tr, c), lambda i, r: (i, r, 0)),
        compiler_params=_cparams("parallel", "parallel"), name="rwkv_readout",
    )(y, y, bonus, bonus, g, ln_w.reshape(1, c), ln_b.reshape(1, c), _head_ones(c))


def _pad_rank_rows(w):
    z = jnp.zeros_like(w[0])
    return jnp.stack([jnp.concatenate([w[0], z], 0), jnp.concatenate([z, w[1]], 0)])


def _rwkv_mixer(prw, mu, w0, w2, a0, a2, g2, k_k, k_a, r_k, ln_w, ln_b, cl):
    mud = jnp.stack([mu, mu[::-1]])
    outs = _rwkv_prep(prw, mud, w0, _pad_rank_rows(w2).astype(BF16), a0, _pad_rank_rows(a2).astype(BF16), g2,
                      k_k, k_a, r_k, cl)
    *scan_in, g, bonus = outs
    y = _rwkv_scan(*scan_in)
    return _rwkv_readout(y, bonus, g, ln_w, ln_b, cl)


def _merge_kernel(ya_ref, yh_ref, yr_ref, yl_ref, gt_ref, x_ref, modx_ref, modc_ref,
                  wa_ref, wh_ref, wr_ref, wl_ref, wo_ref, o_ref, *, cl, tm, d):
    is_ctx = _row_ids(tm, pl.program_id(1)) < cl
    m = None
    for i, (y_ref, w_ref) in enumerate(((ya_ref, wa_ref), (yh_ref, wh_ref), (yr_ref, wr_ref), (yl_ref, wl_ref))):
        gate = _sigmoid(gt_ref[0, :, i * d:(i + 1) * d].astype(F32))
        term = gate * _dot(y_ref[0], w_ref[...])
        m = term if m is None else m + term
    g1 = _mod_rows(modx_ref, modc_ref, 2, is_ctx)
    o_ref[0] = x_ref[0] + g1 * _dot(m.astype(BF16), wo_ref[...])


def _merge(ys, gates, xc, mod, w_brs, w_out, cl):
    b, lt, d = xc.shape
    tm = _largest_tile(lt, 544, 16)
    row = lambda w: pl.BlockSpec((1, tm, w), lambda i, r: (i, r, 0))
    const = lambda a: pl.BlockSpec(a.shape, lambda i, r: (0, 0))
    ws = [w.astype(BF16) for w in w_brs] + [w_out.astype(BF16)]
    return pl.pallas_call(
        functools.partial(_merge_kernel, cl=cl, tm=tm, d=d),
        out_shape=jax.ShapeDtypeStruct((b, lt, d), F32), grid=(b, lt // tm),
        in_specs=[row(y.shape[-1]) for y in ys] + [row(N_BRANCH * d), row(d),
                  pl.BlockSpec((1, 6, d), lambda i, r: (i, 0, 0)), pl.BlockSpec((1, 6, d), lambda i, r: (b, 0, 0))]
                 + [const(w) for w in ws],
        out_specs=row(d), compiler_params=_cparams("parallel", "parallel"), name="merge",
    )(*ys, gates, xc, mod, mod, *ws)


def _route_kernel(x_ref, modx_ref, modc_ref, wr_ref, br_ref, h_ref, cmb_ref, *, cl, tm):
    is_ctx = _row_ids(tm, pl.program_id(1)) < cl
    h = _rms_modulate(x_ref[0], _mod_rows(modx_ref, modc_ref, 3, is_ctx), _mod_rows(modx_ref, modc_ref, 4, is_ctx))
    h_ref[0] = h.astype(h_ref.dtype)
    lg = jnp.dot(h, wr_ref[...], preferred_element_type=F32, precision=lax.Precision.HIGHEST) + br_ref[...]
    lane = lax.broadcasted_iota(jnp.int32, lg.shape, 1)
    lane_f = lane.astype(F32)
    neg = -jnp.inf
    big = 1e9

    def first_lane(cond):
        return jnp.min(jnp.where(cond, lane_f, big), axis=-1, keepdims=True)

    is_grp = (lane >= N_EXPERTS) & (lane < N_EXPERTS + N_GROUPS)
    gl = jnp.where(is_grp, lg, neg)
    gmax = jnp.max(gl, axis=-1, keepdims=True)
    ge = jnp.where(is_grp, jnp.exp(gl - gmax), 0.0)
    gp = ge / jnp.sum(ge, axis=-1, keepdims=True)
    g_val = jnp.max(gp, axis=-1, keepdims=True)
    g_idx = first_lane(is_grp & (gp == g_val)) - N_EXPERTS
    lo = g_idx * EXPERTS_PER_GROUP
    in_grp = (lane_f >= lo) & (lane_f < lo + EXPERTS_PER_GROUP)
    el = jnp.where(in_grp, lg, neg)
    emax = jnp.max(el, axis=-1, keepdims=True)
    ee = jnp.where(in_grp, jnp.exp(el - emax), 0.0)
    pe = ee / jnp.sum(ee, axis=-1, keepdims=True)
    v1 = jnp.max(jnp.where(in_grp, pe, -1.0), axis=-1, keepdims=True)
    i1 = first_lane(in_grp & (pe == v1))
    rest = in_grp & (lane_f != i1)
    v2 = jnp.max(jnp.where(rest, pe, -1.0), axis=-1, keepdims=True)
    i2 = first_lane(rest & (pe == v2))
    den = v1 + v2
    cmb_ref[0] = (jnp.where(lane_f == i1, g_val * v1 / den, 0.0) + jnp.where(lane_f == i2, g_val * v2 / den, 0.0)
                  + jnp.where(lane == N_EXPERTS, g_idx, 0.0))


def _route(xc, mod, w_grp, b_grp, w_rt, b_rt, cl):
    b, lt, d = xc.shape
    tm = _largest_tile(lt, 544, 16)
    wr = jnp.zeros((d, LANES), F32).at[:, :N_EXPERTS].set(w_rt).at[:, N_EXPERTS:N_EXPERTS + N_GROUPS].set(w_grp)
    br = jnp.zeros((1, LANES), F32).at[0, :N_EXPERTS].set(b_rt).at[0, N_EXPERTS:N_EXPERTS + N_GROUPS].set(b_grp)
    return pl.pallas_call(
        functools.partial(_route_kernel, cl=cl, tm=tm),
        out_shape=(jax.ShapeDtypeStruct((b, lt, d), BF16), jax.ShapeDtypeStruct((b, lt, LANES), F32)),
        grid=(b, lt // tm),
        in_specs=[pl.BlockSpec((1, tm, d), lambda i, r: (i, r, 0)),
                  pl.BlockSpec((1, 6, d), lambda i, r: (i, 0, 0)), pl.BlockSpec((1, 6, d), lambda i, r: (b, 0, 0)),
                  pl.BlockSpec((d, LANES), lambda i, r: (0, 0)), pl.BlockSpec((1, LANES), lambda i, r: (0, 0))],
        out_specs=(pl.BlockSpec((1, tm, d), lambda i, r: (i, r, 0)), pl.BlockSpec((1, tm, LANES), lambda i, r: (i, r, 0))),
        compiler_params=_cparams("parallel", "parallel"), name="moe_route",
    )(xc, mod, mod, wr, br)


def _moe_kernel(grp_ref, nvalid_ref, h_ref, cmb_ref, w1_ref, w3_ref, w2_ref, o_ref, acc_ref, *, bm):
    j = pl.program_id(0)
    e = pl.program_id(1)

    @pl.when(e == 0)
    def _():
        acc_ref[...] = jnp.zeros_like(acc_ref)

    @pl.when(j < nvalid_ref[0])
    def _():
        h = h_ref[...]
        t = _silu(_dot(h, w1_ref[0, 0])) * _dot(h, w3_ref[0, 0])
        y = _dot(t.astype(BF16), w2_ref[0, 0])
        lane = lax.broadcasted_iota(jnp.int32, (bm, LANES), 1)
        expert = grp_ref[j] * EXPERTS_PER_GROUP + e
        wcol = jnp.sum(jnp.where(lane == expert, cmb_ref[...], 0.0), axis=-1, keepdims=True)
        acc_ref[...] += wcol * y

    @pl.when(e == pl.num_programs(1) - 1)
    def _():
        o_ref[...] = acc_ref[...].astype(o_ref.dtype)


def _moe_grouped(hs, ws, blk_group, nvalid, w1, w3, w2, layer):
    s_rows, d = hs.shape
    hid = w1.shape[3]
    bm = MOE_BLOCK_ROWS
    wspec = lambda shape: pl.BlockSpec(shape, lambda j, e, grp, nv: (layer, grp[j] * EXPERTS_PER_GROUP + e, 0, 0))
    return pl.pallas_call(
        functools.partial(_moe_kernel, bm=bm),
        out_shape=jax.ShapeDtypeStruct((s_rows, d), BF16),
        grid_spec=pltpu.PrefetchScalarGridSpec(
            num_scalar_prefetch=2, grid=(s_rows // bm, EXPERTS_PER_GROUP),
            in_specs=[pl.BlockSpec((bm, d), lambda j, e, grp, nv: (j, 0)),
                      pl.BlockSpec((bm, LANES), lambda j, e, grp, nv: (j, 0)),
                      wspec((1, 1, d, hid)), wspec((1, 1, d, hid)), wspec((1, 1, hid, d))],
            out_specs=pl.BlockSpec((bm, d), lambda j, e, grp, nv: (j, 0)),
            scratch_shapes=[pltpu.VMEM((bm, d), F32)]),
        compiler_params=_cparams("parallel", "arbitrary"), name="moe_experts",
    )(blk_group, nvalid, hs, ws, w1, w3, w2)


def _moe_finish_kernel(x_ref, y_ref, modx_ref, modc_ref, o_ref, *, cl, tm, first):
    is_ctx = _row_ids(tm, pl.program_id(1) + first) < cl
    o_ref[0] = x_ref[0] + _mod_rows(modx_ref, modc_ref, 5, is_ctx) * y_ref[0].astype(F32)


def _moe_finish(xc, y, mod, cl, latent_only):
    b, lt, d = xc.shape
    if latent_only:
        tm = _largest_tile(math.gcd(cl, lt - cl), 1088, 16)
        first = cl // tm
    else:
        tm, first = _largest_tile(lt, 1088, 16), 0
    src = pl.BlockSpec((1, tm, d), lambda i, r: (i, r + first, 0))
    n_rows = lt - first * tm
    return pl.pallas_call(
        functools.partial(_moe_finish_kernel, cl=cl, tm=tm, first=first),
        out_shape=jax.ShapeDtypeStruct((b, n_rows, d), F32), grid=(b, n_rows // tm),
        in_specs=[src, src, pl.BlockSpec((1, 6, d), lambda i, r: (i, 0, 0)),
                  pl.BlockSpec((1, 6, d), lambda i, r: (b, 0, 0))],
        out_specs=pl.BlockSpec((1, tm, d), lambda i, r: (i, r, 0)),
        compiler_params=_cparams("parallel", "parallel"), name="moe_finish",
    )(xc, y, mod, mod)


def _moe(h2, cmb, xc, mod, w1, w3, w2, layer, cl, latent_only=False):
    b, lt, d = xc.shape
    t = b * lt
    bm = MOE_BLOCK_ROWS
    s_rows = -(-t // bm) * bm + N_GROUPS * bm
    cmb2 = cmb.reshape(t, LANES)
    gid = cmb2[:, N_EXPERTS].astype(jnp.int32)
    onehot = (gid[:, None] == jnp.arange(N_GROUPS, dtype=jnp.int32)).astype(jnp.int32)
    csum = jnp.cumsum(onehot, axis=0)
    rank = jnp.sum(onehot * csum, axis=1) - 1
    padded = -(-csum[-1] // bm) * bm
    ends = jnp.cumsum(padded)
    pos = (ends - padded)[gid] + rank
    slot_token = jnp.zeros((s_rows,), jnp.int32).at[pos].set(jnp.arange(t, dtype=jnp.int32))
    starts = jnp.arange(s_rows // bm, dtype=jnp.int32) * bm
    blk_group = jnp.minimum(jnp.sum((starts[:, None] >= ends[None, :]).astype(jnp.int32), axis=1), N_GROUPS - 1)
    nvalid = (ends[-1] // bm).reshape(1).astype(jnp.int32)
    take_rows = lambda a, idx: a.at[idx].get(mode="promise_in_bounds")
    hs = take_rows(h2.reshape(t, d), slot_token)
    ws = take_rows(cmb2, slot_token)
    ys = _moe_grouped(hs, ws, blk_group.astype(jnp.int32), nvalid, w1, w3, w2, layer)
    y = take_rows(ys, pos).reshape(b, lt, d)
    return _moe_finish(xc, y, mod, cl, latent_only)


def kernel(x, c, ctx, c_ctx, ada_w, ada_b, w_in, q_norm, k_norm, hy_conv_w, hy_conv_b, hy_f1, hy_fb1, hy_f2, hy_fb2, hy_f3, hy_skip, rw_mu, rw_w0, rw_w2, rw_a0, rw_a2, rw_g2, rw_k_k, rw_k_a, rw_r_k, rw_ln_w, rw_ln_b, lru_conv_w, lru_conv_b, lru_wa, lru_ba, lru_wx, lru_bx, lru_lambda, w_br_attn, w_br_hyena, w_br_rwkv, w_br_lru, w_out, moe_w_grp, moe_b_grp, moe_w_rt, moe_b_rt, moe_w1, moe_w3, moe_w2):
    b, l, d = x.shape
    cl = ctx.shape[1]
    depth = ada_w.shape[0]
    assert b < MOD_ROWS and cl % RWKV_CHUNK == 0 and l % RWKV_CHUNK == 0

    xc = jnp.concatenate([ctx, x], axis=1)
    cc = jnp.zeros((MOD_ROWS, d), F32).at[:b].set(c).at[b].set(c_ctx)
    mod_all = _ada_mod(cc, ada_w, ada_b).reshape(depth, MOD_ROWS, 6, d)

    moe_w1b, moe_w3b, moe_w2b = (w.astype(BF16) for w in (moe_w1, moe_w3, moe_w2))
    cos2, sin2 = _rope_tables(l, cl)
    mats_x = _dft_mats(l)
    mats_c = _dft_mats(cl)
    qkv_w = ATTN_WIDTH + 2 * ATTN_KV_WIDTH
    col = np.cumsum([0, qkv_w, 3 * HYENA_WIDTH, RWKV_PROJ, 2 * LRU_WIDTH, N_BRANCH * d])

    for i in range(depth):
        need_ctx = i < depth - 1
        mod = mod_all[i]
        w_i = w_in[i].astype(BF16)
        h1 = _modnorm(xc, mod, cl)
        pqkv, phy, prw, plr, gates = (_proj(h1, w_i[:, col[j]:col[j + 1]]) for j in range(5))

        qn, kt, vx = _attn_prep(pqkv, cos2, sin2, q_norm[i], k_norm[i])
        y_att_x = _attention(qn, kt, vx, cl, 'x', cl + l)
        if need_ctx:
            y_att_c = _attention(qn, kt, vx, cl, 'ctx', cl)
        else:
            y_att_c = jnp.zeros((b, cl, ATTN_WIDTH), BF16)
        y_att = jnp.concatenate([y_att_c, y_att_x], axis=1)

        hv, hx1, hx2 = _hyena_pre(phy, hy_conv_w[i], hy_conv_b[i], cl)
        filt = (hy_f1[i], hy_fb1[i], hy_f2[i], hy_fb2[i], hy_f3[i])
        y_hx = _hyena_run(hv[:, cl:], hx1[:, cl:], hx2[:, cl:], _hyena_spectra(l, filt, mats_x), hy_skip[i], mats_x)
        if need_ctx:
            y_hc = _hyena_run(hv[:, :cl], hx1[:, :cl], hx2[:, :cl], _hyena_spectra(cl, filt, mats_c), hy_skip[i],
                              mats_c)
        else:
            y_hc = jnp.zeros((b, cl, HYENA_WIDTH), BF16)
        y_hy = jnp.concatenate([y_hc, y_hx], axis=1)

        y_rw = _rwkv_mixer(prw, rw_mu[i], rw_w0[i], rw_w2[i], rw_a0[i], rw_a2[i], rw_g2[i], rw_k_k[i], rw_k_a[i],
                           rw_r_k[i].reshape(-1), rw_ln_w[i], rw_ln_b[i], cl)

        la, lb = _lru_pre(plr, lru_conv_w[i], lru_conv_b[i], lru_wa[i], lru_ba[i], lru_wx[i], lru_bx[i],
                          lru_lambda[i], cl)
        y_lr = _lru_scan(la, lb, plr, cl)

        xc = _merge((y_att, y_hy, y_rw, y_lr), gates, xc, mod,
                    (w_br_attn[i], w_br_hyena[i], w_br_rwkv[i], w_br_lru[i]), w_out[i], cl)

        h2, cmb = _route(xc, mod, moe_w_grp[i], moe_b_grp[i], moe_w_rt[i], moe_b_rt[i], cl)
        xc = _moe(h2, cmb, xc, mod, moe_w1b, moe_w3b, moe_w2b, i, cl, latent_only=not need_ctx)
    return xc
```

```python
import functools
import math

import numpy as np
import jax
import jax.numpy as jnp
from jax import lax
from jax.experimental import pallas as pl
from jax.experimental.pallas import tpu as pltpu

F32 = jnp.float32
BF16 = jnp.bfloat16

HEAD_DIM = 64
GRID_W = 64
EPS = 1e-6
ATTN_HEADS = 8
ATTN_KV_HEADS = 2
ATTN_GROUP = ATTN_HEADS // ATTN_KV_HEADS
ATTN_Q_BLOCKS = 2
ATTN_WIDTH = ATTN_HEADS * HEAD_DIM
ATTN_KV_WIDTH = ATTN_KV_HEADS * HEAD_DIM
ROPE_THETA = 10000.0
HYENA_WIDTH = 256
HYENA_ORDER = 2
HYENA_BANDS = 16
HYENA_DECAY_TARGET = 1e-2
HYENA_FAST_DECAY = 0.3
HYENA_SLOW_DECAY = 1.5
RWKV_HEADS = 4
RWKV_WIDTH = RWKV_HEADS * HEAD_DIM
RWKV_DECAY_RANK = 64
RWKV_ICLR_RANK = 64
RWKV_GATE_RANK = 128
RWKV_GN_EPS = 64e-5
RWKV_PROJ = 3 * RWKV_WIDTH + 2 * RWKV_DECAY_RANK + 2 * RWKV_ICLR_RANK + RWKV_GATE_RANK
RWKV_CHUNK = 64
RWKV_INV_BASE = 4
RWKV_STAGE_ROWS = 128
LRU_WIDTH = 256
LRU_BLOCKS = 4
LRU_C = 8.0
LRU_GROUPS_PER_STEP = 4
N_BRANCH = 4
N_GROUPS = 4
EXPERTS_PER_GROUP = 4
N_EXPERTS = N_GROUPS * EXPERTS_PER_GROUP
MOE_BLOCK_ROWS = 512

V7X_VMEM_LIMIT_BYTES = 52 * 1024 * 1024
SUBLANES = 8
LANES = 128
MOD_ROWS = 16


def _cparams(*sem):
    return pltpu.CompilerParams(dimension_semantics=sem, vmem_limit_bytes=V7X_VMEM_LIMIT_BYTES)


def _dot(a, b):
    return jnp.dot(a, b, preferred_element_type=F32)


def _dot_nt(a, b):
    return lax.dot_general(a, b, (((1,), (1,)), ((), ())), preferred_element_type=F32)


def _split3(x):
    hi = x.astype(BF16)
    r1 = x - hi.astype(F32)
    mid = r1.astype(BF16)
    lo = (r1 - mid.astype(F32)).astype(BF16)
    return hi, mid, lo


def _dot_exact_lhs(m_bf16, x):
    hi, mid, lo = _split3(x)
    return _dot(m_bf16, hi) + _dot(m_bf16, mid) + _dot(m_bf16, lo)


def _dot_exact_rhs(x, m_bf16):
    hi = x.astype(BF16)
    lo = (x - hi.astype(F32)).astype(BF16)
    return _dot(hi, m_bf16) + _dot(lo, m_bf16)


def _dot_bf(a, b):
    return _dot(a.astype(BF16), b.astype(BF16))


def _sigmoid(x):
    return 0.5 * jnp.tanh(0.5 * x) + 0.5


def _softplus(x):
    return jnp.maximum(x, 0.0) + jnp.log(1.0 + jnp.exp(-jnp.abs(x)))


def _silu(x):
    return x * _sigmoid(x)


def _largest_tile(n, cap, mult):
    best = None
    for t in range(mult, min(n, cap) + 1, mult):
        if n % t == 0:
            best = t
    assert best is not None, (n, cap, mult)
    return best


def _head_ones(width):
    idx = np.arange(width) // HEAD_DIM
    return jnp.asarray((idx[:, None] == idx[None, :]).astype(np.float32), dtype=BF16)


def _row_ids(tile_rows, tile_idx):
    return tile_idx * tile_rows + lax.broadcasted_iota(jnp.int32, (tile_rows, 1), 0)


def _mod_rows(modx_ref, modc_ref, idx, is_ctx):
    return jnp.where(is_ctx, modc_ref[0, idx:idx + 1, :], modx_ref[0, idx:idx + 1, :])


def _rms_modulate(x, shift, scale):
    ms = jnp.mean(x * x, axis=-1, keepdims=True)
    return (x * lax.rsqrt(ms + EPS)) * (1.0 + scale) + shift


def _ada_kernel(c_ref, w_ref, b_ref, o_ref):
    s = _silu(c_ref[...])
    o_ref[0] = jnp.dot(s, w_ref[0], preferred_element_type=F32, precision=lax.Precision.HIGHEST) + b_ref[0]


def _ada_mod(cc, ada_w, ada_b):
    depth, d, n6 = ada_w.shape
    tn = _largest_tile(n6, 1024, LANES)
    return pl.pallas_call(
        _ada_kernel,
        out_shape=jax.ShapeDtypeStruct((depth, MOD_ROWS, n6), F32),
        grid=(depth, n6 // tn),
        in_specs=[pl.BlockSpec((MOD_ROWS, d), lambda i, j: (0, 0)),
                  pl.BlockSpec((1, d, tn), lambda i, j: (i, 0, j)),
                  pl.BlockSpec((1, 1, tn), lambda i, j: (i, 0, j))],
        out_specs=pl.BlockSpec((1, MOD_ROWS, tn), lambda i, j: (i, 0, j)),
        compiler_params=_cparams("parallel", "parallel"),
        name="ada_mod",
    )(cc, ada_w, ada_b.reshape(depth, 1, n6))


def _modnorm_kernel(x_ref, modx_ref, modc_ref, o_ref, *, cl, tm):
    is_ctx = _row_ids(tm, pl.program_id(1)) < cl
    h = _rms_modulate(x_ref[0], _mod_rows(modx_ref, modc_ref, 0, is_ctx), _mod_rows(modx_ref, modc_ref, 1, is_ctx))
    o_ref[0] = h.astype(o_ref.dtype)


def _modnorm(xc, mod, cl):
    b, lt, d = xc.shape
    tm = _largest_tile(lt, 1088, 16)
    return pl.pallas_call(
        functools.partial(_modnorm_kernel, cl=cl, tm=tm),
        out_shape=jax.ShapeDtypeStruct((b, lt, d), BF16), grid=(b, lt // tm),
        in_specs=[pl.BlockSpec((1, tm, d), lambda i, r: (i, r, 0)),
                  pl.BlockSpec((1, 6, d), lambda i, r: (i, 0, 0)),
                  pl.BlockSpec((1, 6, d), lambda i, r: (b, 0, 0))],
        out_specs=pl.BlockSpec((1, tm, d), lambda i, r: (i, r, 0)),
        compiler_params=_cparams("parallel", "parallel"), name="modnorm",
    )(xc, mod, mod)


def _proj_kernel(h_ref, w_ref, o_ref):
    o_ref[0] = _dot(h_ref[0], w_ref[...]).astype(o_ref.dtype)


def _proj(h, w):
    b, lt, d = h.shape
    n = w.shape[1]
    tm = _largest_tile(lt, 2176, 16)
    tn = n if n <= 1280 else _largest_tile(n, 1024, 2 * LANES)
    return pl.pallas_call(
        _proj_kernel, out_shape=jax.ShapeDtypeStruct((b, lt, n), BF16), grid=(b, lt // tm, n // tn),
        in_specs=[pl.BlockSpec((1, tm, d), lambda i, r, j: (i, r, 0)), pl.BlockSpec((d, tn), lambda i, r, j: (0, j))],
        out_specs=pl.BlockSpec((1, tm, tn), lambda i, r, j: (i, r, j)),
        compiler_params=_cparams("parallel", "parallel", "parallel"), name="proj",
    )(h, w)


def _rope_tables(l, cl):
    n_freq = HEAD_DIM // 4
    t = jnp.arange(l)
    freqs = ROPE_THETA ** (-jnp.arange(n_freq, dtype=F32) / n_freq)
    pos = jnp.stack([t // GRID_W, t % GRID_W], -1).astype(F32)
    ang = pos[..., None] * freqs
    cos64 = jnp.stack([jnp.cos(ang), jnp.cos(ang)], axis=2).reshape(l, HEAD_DIM)
    sin64 = jnp.stack([-jnp.sin(ang), jnp.sin(ang)], axis=2).reshape(l, HEAD_DIM)
    cos64 = jnp.concatenate([jnp.ones((cl, HEAD_DIM), F32), cos64], 0)
    sin64 = jnp.concatenate([jnp.zeros((cl, HEAD_DIM), F32), sin64], 0)
    return jnp.tile(cos64, (1, 2)), jnp.tile(sin64, (1, 2))


def _head_rms(t, ones_ref):
    ms = _dot_exact_rhs(t * t, ones_ref[...]) * (1.0 / HEAD_DIM)
    return t * lax.rsqrt(ms + EPS)


def _rope(t, cos, sin):
    w = t.shape[-1]
    lane = lax.broadcasted_iota(jnp.int32, t.shape, 1)
    q4 = HEAD_DIM // 4
    first_half = (lane % (2 * q4)) < q4
    partner = jnp.where(first_half, pltpu.roll(t, w - q4, 1), pltpu.roll(t, q4, 1))
    return t * cos + partner * sin


def _attn_prep_kernel(p_ref, cos_ref, sin_ref, qg_ref, kg_ref, oq_ref, ok_ref, q_ref, kt_ref, vx_ref):
    p = p_ref[0].astype(F32)
    v = p[:, ATTN_WIDTH + ATTN_KV_WIDTH:]
    low = lax.broadcasted_iota(jnp.int32, v.shape, 1) < HEAD_DIM
    vx_ref[0, 0] = jnp.where(low, v, 1.0).astype(vx_ref.dtype)
    vx_ref[0, 1] = jnp.where(low, pltpu.roll(v, HEAD_DIM, 1), 1.0).astype(vx_ref.dtype)
    cos2, sin2 = cos_ref[...], sin_ref[...]
    reps = ATTN_WIDTH // (2 * HEAD_DIM)
    cos_q = jnp.concatenate([cos2] * reps, axis=1)
    sin_q = jnp.concatenate([sin2] * reps, axis=1)
    q = _head_rms(p[:, :ATTN_WIDTH], oq_ref) * qg_ref[...]
    q_ref[0] = _rope(q, cos_q, sin_q).astype(q_ref.dtype)
    k = _head_rms(p[:, ATTN_WIDTH:ATTN_WIDTH + ATTN_KV_WIDTH], ok_ref) * kg_ref[...]
    kt_ref[0] = _rope(k, cos2, sin2).T.astype(kt_ref.dtype)


def _attn_prep(pqkv, cos2, sin2, q_gain, k_gain):
    b, lt, wtot = pqkv.shape
    tr = _largest_tile(lt, 2176, LANES)
    qg = jnp.tile(q_gain * (HEAD_DIM ** -0.5 * math.log2(math.e)), ATTN_HEADS).reshape(1, ATTN_WIDTH)
    kg = jnp.tile(k_gain, ATTN_KV_HEADS).reshape(1, ATTN_KV_WIDTH)
    return pl.pallas_call(
        _attn_prep_kernel,
        out_shape=(jax.ShapeDtypeStruct((b, lt, ATTN_WIDTH), BF16),
                   jax.ShapeDtypeStruct((b, ATTN_KV_WIDTH, lt), BF16),
                   jax.ShapeDtypeStruct((b, ATTN_KV_HEADS, lt, 2 * HEAD_DIM), BF16)),
        grid=(b, lt // tr),
        in_specs=[pl.BlockSpec((1, tr, wtot), lambda i, r: (i, r, 0)),
                  pl.BlockSpec((tr, 2 * HEAD_DIM), lambda i, r: (r, 0)),
                  pl.BlockSpec((tr, 2 * HEAD_DIM), lambda i, r: (r, 0)),
                  pl.BlockSpec((1, ATTN_WIDTH), lambda i, r: (0, 0)),
                  pl.BlockSpec((1, ATTN_KV_WIDTH), lambda i, r: (0, 0)),
                  pl.BlockSpec((ATTN_WIDTH, ATTN_WIDTH), lambda i, r: (0, 0)),
                  pl.BlockSpec((ATTN_KV_WIDTH, ATTN_KV_WIDTH), lambda i, r: (0, 0))],
        out_specs=(pl.BlockSpec((1, tr, ATTN_WIDTH), lambda i, r: (i, r, 0)),
                   pl.BlockSpec((1, ATTN_KV_WIDTH, tr), lambda i, r: (i, 0, r)),
                   pl.BlockSpec((1, ATTN_KV_HEADS, tr, 2 * HEAD_DIM), lambda i, r: (i, 0, r, 0))),
        compiler_params=_cparams("parallel", "parallel"),
        name="attn_prep",
    )(pqkv, cos2, sin2, qg, kg, _head_ones(ATTN_WIDTH), _head_ones(ATTN_KV_WIDTH))


def _attn_kernel(*refs, nq, nk):
    q_refs, (kt_ref, v_ref, o_ref) = refs[:nq], refs[nq:]
    outs = []
    for h in range(ATTN_HEADS):
        kv = h // ATTN_GROUP
        hs = slice(h * HEAD_DIM, (h + 1) * HEAD_DIM)
        qh = jnp.concatenate([q_ref[0, :, hs] for q_ref in q_refs], axis=0)
        s = _dot(qh, kt_ref[0, kv * HEAD_DIM:(kv + 1) * HEAD_DIM, :nk])
        m = jnp.max(s, axis=-1, keepdims=True)
        p = jnp.exp2(s - m)
        o = _dot(p.astype(BF16), v_ref[0, kv, :nk, :])
        outs.append(o[:, :HEAD_DIM] / o[:, HEAD_DIM:])
    o_ref[0] = jnp.concatenate(outs, axis=-1).astype(o_ref.dtype)


def _attention(qn, kt, vx, cl, rows, nk):
    b, lt, _ = qn.shape
    blk = 256 if (cl % 256 == 0 and lt % 256 == 0) else 128
    if rows == 'ctx':
        n_rows, first, nq = cl, 0, 1
    else:
        n_rows, first = lt - cl, cl // blk
        nq = ATTN_Q_BLOCKS if n_rows % (ATTN_Q_BLOCKS * blk) == 0 else 1
    tq = nq * blk
    q_specs = [pl.BlockSpec((1, blk, ATTN_WIDTH), lambda i, t, j=j: (i, first + nq * t + j, 0)) for j in range(nq)]
    return pl.pallas_call(
        functools.partial(_attn_kernel, nq=nq, nk=nk),
        out_shape=jax.ShapeDtypeStruct((b, n_rows, ATTN_WIDTH), BF16),
        grid=(b, n_rows // tq),
        in_specs=q_specs + [pl.BlockSpec((1, ATTN_KV_WIDTH, lt), lambda i, t: (i, 0, 0)),
                            pl.BlockSpec((1, ATTN_KV_HEADS, lt, 2 * HEAD_DIM), lambda i, t: (i, 0, 0, 0))],
        out_specs=pl.BlockSpec((1, tq, ATTN_WIDTH), lambda i, t: (i, t, 0)),
        compiler_params=_cparams("parallel", "parallel"),
        name="attention_" + rows,
    )(*([qn] * nq), kt, vx)


def _halo_specs(tr, lt, width, lead):
    per = tr // SUBLANES
    last = lt // SUBLANES - 1
    nlead = len(lead)

    def prev_map(*ids):
        return (*ids[:nlead], jnp.maximum(ids[nlead] * per - 1, 0), 0)

    def next_map(*ids):
        return (*ids[:nlead], jnp.minimum((ids[nlead] + 1) * per, last), 0)

    blk = (*lead, SUBLANES, width)
    return pl.BlockSpec(blk, prev_map), pl.BlockSpec(blk, next_map)


def _shift_rows(x, halo, offset, rows, cl, lt):
    tr = x.shape[0]
    local = lax.broadcasted_iota(jnp.int32, (tr, 1), 0)
    if offset < 0:
        y = pltpu.roll(x, -offset, 0)
        y = jnp.where(local == 0, halo, y)
        bad = (rows == 0) | (rows == cl)
    else:
        y = pltpu.roll(x, tr - offset, 0)
        for j in range(offset):
            y = jnp.where(local == tr - offset + j, halo[j:j + 1, :], y)
        bad = (rows >= lt - offset) | ((rows >= cl - offset) & (rows < cl))
    return jnp.where(bad, 0.0, y)


def _hyena_pre_kernel(p_ref, pv_ref, nx_ref, w_ref, b_ref, v_ref, x1_ref, x2_ref, *, cl, lt, tr):
    rows = _row_ids(tr, pl.program_id(1))
    p = p_ref[0].astype(F32)
    pm = _shift_rows(p, pv_ref[0, SUBLANES - 1:, :].astype(F32), -1, rows, cl, lt)
    pp = _shift_rows(p, nx_ref[0, :1, :].astype(F32), 1, rows, cl, lt)
    z = pm * w_ref[0:1, :] + p * w_ref[1:2, :] + pp * w_ref[2:3, :] + b_ref[...]
    c = HYENA_WIDTH
    v_ref[0] = z[:, :c].astype(v_ref.dtype)
    x1_ref[0] = z[:, c:2 * c].astype(x1_ref.dtype)
    x2_ref[0] = z[:, 2 * c:].astype(x2_ref.dtype)


def _hyena_pre(phy, conv_w, conv_b, cl):
    b, lt, w = phy.shape
    tr = _largest_tile(lt, 1088, 16)
    prev_spec, next_spec = _halo_specs(tr, lt, w, (1,))
    out = jax.ShapeDtypeStruct((b, lt, HYENA_WIDTH), BF16)
    ospec = pl.BlockSpec((1, tr, HYENA_WIDTH), lambda i, r: (i, r, 0))
    return pl.pallas_call(
        functools.partial(_hyena_pre_kernel, cl=cl, lt=lt, tr=tr),
        out_shape=(out, out, out),
        grid=(b, lt // tr),
        in_specs=[pl.BlockSpec((1, tr, w), lambda i, r: (i, r, 0)), prev_spec, next_spec,
                  pl.BlockSpec(conv_w.shape, lambda i, r: (0, 0)),
                  pl.BlockSpec((1, w), lambda i, r: (0, 0))],
        out_specs=(ospec, ospec, ospec),
        compiler_params=_cparams("parallel", "parallel"),
        name="hyena_pre",
    )(phy, phy, phy, conv_w, conv_b.reshape(1, w))


def _trig_mats_kernel(cb_ref, sb_ref, co_ref, so_ref, c_ref, s_ref):
    cb, sb, co, so = cb_ref[0], sb_ref[0], co_ref[...], so_ref[...]
    c_ref[...] = (cb * co - sb * so).astype(c_ref.dtype)
    s_ref[...] = (sb * co + cb * so).astype(s_ref.dtype)


def _trig_mats(n, h, row_mult, col_mult):
    tm = _largest_tile(h, 256, 16)
    tn = _largest_tile(h, 1024, LANES) if h % LANES == 0 else h
    b = col_mult(jnp.arange(h, dtype=jnp.int32))[None, :]

    def tables(a):
        ang = ((a[:, None] * b) % (2 * n)).astype(F32) * (math.pi / n)
        return jnp.cos(ang), jnp.sin(ang)

    r0 = jnp.arange(0, h, tm, dtype=jnp.int32)
    cb, sb = tables(row_mult(r0))
    co, so = tables(row_mult(jnp.arange(tm, dtype=jnp.int32)) - row_mult(jnp.zeros((tm,), jnp.int32)))
    base = pl.BlockSpec((1, 1, tn), lambda j, c: (j, 0, c))
    off = pl.BlockSpec((tm, tn), lambda j, c: (0, c))
    out = jax.ShapeDtypeStruct((h, h), BF16)
    ospec = pl.BlockSpec((tm, tn), lambda j, c: (j, c))
    return pl.pallas_call(
        _trig_mats_kernel, out_shape=(out, out), grid=(h // tm, h // tn),
        in_specs=[base, base, off, off], out_specs=(ospec, ospec),
        compiler_params=_cparams("parallel", "parallel"), name="trig_mats",
    )(cb.reshape(h // tm, 1, h), sb.reshape(h // tm, 1, h), co, so)


def _dft_mats(n):
    h = n // 2
    ident = lambda r: r
    ce, se = _trig_mats(n, h, ident, lambda j: 2 * j)
    co, so = _trig_mats(n, h, ident, lambda j: 2 * j + 1)
    cot, sot = _trig_mats(n, h, lambda j: 2 * j + 1, ident)
    return ce, se, co, so, cot, sot


def _alt_sum(z):
    j = lax.broadcasted_iota(jnp.int32, (z.shape[0], 1), 0)
    return jnp.sum(z * (1 - 2 * (j % 2)).astype(F32), axis=0, keepdims=True)


def _half_spectra(ce_ref, se_ref, co_ref, so_ref, z_ev, z_od):
    p, q = _dot(ce_ref[...], z_ev), _dot(co_ref[...], z_od)
    ps, qs = _dot(se_ref[...], z_ev), _dot(so_ref[...], z_od)
    return p + q, ps + qs, p - q, qs - ps


def _dft_raw_kernel(ce_ref, se_ref, co_ref, so_ref, z_ref, rl_ref, il_ref, ru_ref, iu_ref, mid_ref):
    z_ev, z_od = z_ref[0], z_ref[1]
    rl_ref[...], il_ref[...], ru_ref[...], iu_ref[...] = _half_spectra(ce_ref, se_ref, co_ref, so_ref, z_ev, z_od)

    @pl.when(pl.program_id(0) == 0)
    def _():
        mid = jnp.concatenate([_alt_sum(z_ev.astype(F32)), _alt_sum(z_od.astype(F32))], axis=0)
        mid_ref[...] = jnp.concatenate([mid, jnp.zeros((SUBLANES - 2, mid.shape[1]), F32)], axis=0)


def _dft_raw(mats, zs):
    ce, se, co, so, _, _ = mats
    _, h, c = zs.shape
    tm = _largest_tile(h, 256, 16)
    mat = pl.BlockSpec((tm, h), lambda j: (j, 0))
    out = jax.ShapeDtypeStruct((h, c), F32)
    ospec = pl.BlockSpec((tm, c), lambda j: (j, 0))
    return pl.pallas_call(
        _dft_raw_kernel, out_shape=(out, out, out, out, jax.ShapeDtypeStruct((SUBLANES, c), F32)), grid=(h // tm,),
        in_specs=[mat, mat, mat, mat, pl.BlockSpec((2, h, c), lambda j: (0, 0, 0))],
        out_specs=(ospec, ospec, ospec, ospec, pl.BlockSpec((SUBLANES, c), lambda j: (0, 0))),
        compiler_params=_cparams("arbitrary"), name="dft_raw",
    )(ce, se, co, so, zs)


def _dft_fwd_kernel(ce_ref, se_ref, co_ref, so_ref, z_ref, hrl_ref, hil_ref, hru_ref, hiu_ref, hmid_ref,
                    ea_ref, eb_ref, oa_ref, ob_ref, mid_ref, *, bb):
    hrl, hil, hru, hiu = hrl_ref[...], hil_ref[...], hru_ref[...], hiu_ref[...]
    for i in range(bb):
        z_ev, z_od = z_ref[i, 0], z_ref[i, 1]
        zrl, zil, zru, ziu = _half_spectra(ce_ref, se_ref, co_ref, so_ref, z_ev, z_od)
        yrl, yil = zrl * hrl + zil * hil, zil * hrl - zrl * hil
        yru, yiu = zru * hru + ziu * hiu, ziu * hru - zru * hiu
        ea_ref[i] = (yrl + yru).astype(ea_ref.dtype)
        eb_ref[i] = (yil - yiu).astype(eb_ref.dtype)
        oa_ref[i] = (yrl - yru).astype(oa_ref.dtype)
        ob_ref[i] = (yil + yiu).astype(ob_ref.dtype)

    @pl.when(pl.program_id(1) == 0)
    def _():
        hr, hi = hmid_ref[0:1, :], hmid_ref[1:2, :]
        for i in range(bb):
            zr, zi = _alt_sum(z_ref[i, 0].astype(F32)), _alt_sum(z_ref[i, 1].astype(F32))
            mid = jnp.concatenate([zr * hr + zi * hi, zi * hr - zr * hi], axis=0)
            mid_ref[i] = jnp.concatenate([mid, jnp.zeros((SUBLANES - 2, mid.shape[1]), F32)], axis=0)


def _dft_inv_kernel(ce_ref, se_ref, cot_ref, sot_ref, ea_ref, eb_ref, oa_ref, ob_ref, mid_ref, z_ref, g_ref, skip_ref,
                    o_ref, *, bb, tm):
    alt = (1 - 2 * (_row_ids(tm, pl.program_id(1)) % 2)).astype(F32)
    for i in range(bb):
        y_ev = _dot(ce_ref[...], ea_ref[i]) + _dot(se_ref[...], eb_ref[i]) + alt * mid_ref[i, 0:1, :]
        y_od = _dot(cot_ref[...], oa_ref[i]) + _dot(sot_ref[...], ob_ref[i]) + alt * mid_ref[i, 1:2, :]
        for par, y in enumerate((y_ev, y_od)):
            y = y + z_ref[i, par].astype(F32) * skip_ref[...]
            o_ref[i, par] = (g_ref[i, par].astype(F32) * y).astype(o_ref.dtype)


def _longconv_gated(zs, gates, spectrum, skip, mats):
    ce, se, co, so, cot, sot = mats
    b, _, h, c = zs.shape
    bb = 2 if b % 2 == 0 else 1
    tm = _largest_tile(h, 256, 16)
    mat = pl.BlockSpec((tm, h), lambda i, j: (j, 0))
    full4 = pl.BlockSpec((bb, 2, h, c), lambda i, j: (i, 0, 0, 0))
    full = pl.BlockSpec((bb, h, c), lambda i, j: (i, 0, 0))
    tile = pl.BlockSpec((bb, tm, c), lambda i, j: (i, j, 0))
    tile4 = pl.BlockSpec((bb, 2, tm, c), lambda i, j: (i, 0, j, 0))
    filt = pl.BlockSpec((tm, c), lambda i, j: (j, 0))
    mid_spec = pl.BlockSpec((bb, SUBLANES, c), lambda i, j: (i, 0, 0))
    half = jax.ShapeDtypeStruct((b, h, c), BF16)
    *combos, mid = pl.pallas_call(
        functools.partial(_dft_fwd_kernel, bb=bb),
        out_shape=(half, half, half, half, jax.ShapeDtypeStruct((b, SUBLANES, c), F32)), grid=(b // bb, h // tm),
        in_specs=[mat, mat, mat, mat, full4, filt, filt, filt, filt, pl.BlockSpec((SUBLANES, c), lambda i, j: (0, 0))],
        out_specs=(tile, tile, tile, tile, mid_spec),
        compiler_params=_cparams("parallel", "arbitrary"), name="dft_fwd",
    )(ce, se, co, so, zs, *spectrum)
    return pl.pallas_call(
        functools.partial(_dft_inv_kernel, bb=bb, tm=tm),
        out_shape=jax.ShapeDtypeStruct((b, 2, h, c), BF16), grid=(b // bb, h // tm),
        in_specs=[mat, mat, mat, mat, full, full, full, full, mid_spec, tile4, tile4,
                  pl.BlockSpec((1, c), lambda i, j: (0, 0))],
        out_specs=tile4,
        compiler_params=_cparams("parallel", "parallel"), name="dft_inv",
    )(ce, se, cot, sot, *combos, mid, zs, gates, skip.reshape(1, c))


def _hyena_filters(n, f1, fb1, f2, fb2, f3):
    t = jnp.arange(n, dtype=F32) / n
    bands = jnp.arange(1, HYENA_BANDS + 1, dtype=F32)
    ang = 2.0 * math.pi * t[:, None] * bands
    feat = jnp.concatenate([t[:, None], jnp.sin(ang), jnp.cos(ang)], axis=-1)
    hp = lax.Precision.HIGHEST
    h = jnp.sin(jnp.dot(feat, f1, precision=hp) + fb1)
    h = jnp.sin(jnp.dot(h, f2, precision=hp) + fb2)
    h = jnp.dot(h, f3, precision=hp).reshape(n, HYENA_ORDER, 2, HYENA_WIDTH)
    deltas = jnp.linspace(-math.log(HYENA_DECAY_TARGET) / HYENA_SLOW_DECAY,
                          -math.log(HYENA_DECAY_TARGET) / HYENA_FAST_DECAY, HYENA_WIDTH, dtype=F32)
    h = h * jnp.exp(-t[:, None] * deltas)[:, None, None, :]
    return h / jnp.sum(jnp.abs(h), axis=(0, 2), keepdims=True)


def _parity_split(t):
    *lead, n, c = t.shape
    return jnp.swapaxes(t.reshape(*lead, n // 2, 2, c), -2, -3)


def _parity_merge(t):
    *lead, _, h, c = t.shape
    return jnp.swapaxes(t, -2, -3).reshape(*lead, 2 * h, c)


def _hyena_spectra(n, filt_params, mats):
    h = _hyena_filters(n, *filt_params)
    oc = HYENA_ORDER * HYENA_WIDTH
    hf = h[:, :, 0].reshape(n, oc)
    hb = h[:, :, 1].reshape(n, oc)
    hb = jnp.where(jnp.arange(n)[:, None] == 0, 0.0, hb)
    sig = _parity_split(jnp.concatenate([hf + hb, hb - hf], axis=1).astype(BF16))
    rl, il, ru, iu, mid = _dft_raw(mats, sig)
    scale = 1.0 / n
    ends = jnp.where(jnp.arange(n // 2)[:, None] == 0, 0.5 * scale, scale)
    spectra = []
    for o in range(HYENA_ORDER):
        re = slice(o * HYENA_WIDTH, (o + 1) * HYENA_WIDTH)
        im = slice(oc + o * HYENA_WIDTH, oc + (o + 1) * HYENA_WIDTH)
        hmid = jnp.zeros((SUBLANES, HYENA_WIDTH), F32).at[0].set(mid[0, re] * scale).at[1].set(mid[1, im] * scale)
        spectra.append((rl[:, re] * ends, il[:, im] * scale, ru[:, re] * ends, iu[:, im] * scale, hmid))
    return spectra


def _hyena_run(v, x1, x2, spectra, skip, mats):
    y = _parity_split(v)
    for o, gate in enumerate((x1, x2)):
        y = _longconv_gated(y, _parity_split(gate), spectra[o], skip[o], mats)
    return _parity_merge(y)


def _blockdiag_dense(w):
    nb, blk = w.shape[1], w.shape[2]
    eye = jnp.eye(nb, dtype=w.dtype)
    return jnp.einsum('dncf,nm->dncmf', w, eye).reshape(w.shape[0], nb * blk, nb * blk)


def _lru_pre_kernel(p_ref, pv_ref, nx_ref, cw_ref, cb_ref, wa_ref, ba_ref, wx_ref, bx_ref, lam_ref,
                    a_ref, b_ref, *, cl, lt, tr):
    rows = _row_ids(tr, pl.program_id(1))
    c = LRU_WIDTH
    x = p_ref[0][:, c:].astype(F32)
    pv = pv_ref[0][SUBLANES - 1:, c:].astype(F32)
    nx = nx_ref[0][:, c:].astype(F32)
    xc = (_shift_rows(x, pv, -1, rows, cl, lt) * cw_ref[0:1, :] + x * cw_ref[1:2, :]
          + _shift_rows(x, nx[:1], 1, rows, cl, lt) * cw_ref[2:3, :]
          + _shift_rows(x, nx[:2], 2, rows, cl, lt) * cw_ref[3:4, :] + cb_ref[...])
    xcb = xc.astype(BF16)
    for d in range(2):
        r = _sigmoid(_dot(xcb, wa_ref[d]) + ba_ref[d])
        i = _sigmoid(_dot(xcb, wx_ref[d]) + bx_ref[d])
        log_a = -LRU_C * r * _softplus(-lam_ref[d])
        a_ref[d, 0] = jnp.exp(log_a)
        b_ref[d, 0] = jnp.sqrt(1.0 - jnp.exp(2.0 * log_a)) * (i * xc)


def _lru_pre(plr, conv_w, conv_b, wa, ba, wx, bx, lam, cl):
    b, lt, w = plr.shape
    c = LRU_WIDTH
    tr = _largest_tile(lt, 1088, 16)
    prev_spec, next_spec = _halo_specs(tr, lt, w, (1,))
    out = jax.ShapeDtypeStruct((2, b, lt, c), F32)
    ospec = pl.BlockSpec((2, 1, tr, c), lambda i, r: (0, i, r, 0))
    const2 = lambda shape: pl.BlockSpec(shape, lambda i, r: (0,) * len(shape))
    return pl.pallas_call(
        functools.partial(_lru_pre_kernel, cl=cl, lt=lt, tr=tr),
        out_shape=(out, out), grid=(b, lt // tr),
        in_specs=[pl.BlockSpec((1, tr, w), lambda i, r: (i, r, 0)), prev_spec, next_spec,
                  const2(conv_w.shape), const2((1, c)), const2((2, c, c)), const2((2, 1, c)),
                  const2((2, c, c)), const2((2, 1, c)), const2((2, 1, c))],
        out_specs=(ospec, ospec),
        compiler_params=_cparams("parallel", "parallel"), name="lru_pre",
    )(plr, plr, plr, conv_w, conv_b.reshape(1, c), _blockdiag_dense(wa).astype(BF16), ba.reshape(2, 1, c),
      _blockdiag_dense(wx).astype(BF16), bx.reshape(2, 1, c), lam.reshape(2, 1, c))


def _gelu_tanh(x):
    return 0.5 * x * (1.0 + jnp.tanh(math.sqrt(2.0 / math.pi) * (x + 0.044715 * (x * x * x))))


def _lru_scan_kernel(af_ref, bf_ref, ar_ref, br_ref, g_ref, o_ref, acc_ref, *, cl, lt):
    row = lax.broadcasted_iota(jnp.int32, (SUBLANES, LANES), 0)

    def group_scan(a, b, reverse):
        for s in (1, 2, 4):
            if reverse:
                keep = row < SUBLANES - s
                a_s = jnp.where(keep, pltpu.roll(a, SUBLANES - s, 0), 1.0)
                b_s = jnp.where(keep, pltpu.roll(b, SUBLANES - s, 0), 0.0)
            else:
                keep = row >= s
                a_s = jnp.where(keep, pltpu.roll(a, s, 0), 1.0)
                b_s = jnp.where(keep, pltpu.roll(b, s, 0), 0.0)
            b = a * b_s + b
            a = a * a_s
        return a, b

    ng = math.gcd(math.gcd(cl // SUBLANES, (lt - cl) // SUBLANES), LRU_GROUPS_PER_STEP)
    span = ng * SUBLANES

    def fwd_body(i, h):
        sl = pl.ds(pl.multiple_of(i * span, span), span)
        a_all, b_all = af_ref[0, 0, sl, :], bf_ref[0, 0, sl, :]
        scans = [group_scan(a_all[j * SUBLANES:(j + 1) * SUBLANES], b_all[j * SUBLANES:(j + 1) * SUBLANES], False)
                 for j in range(ng)]
        outs = []
        for a, b in scans:
            hh = a * h + b
            outs.append(hh)
            h = hh[SUBLANES - 1:SUBLANES, :]
        acc_ref[sl, :] = jnp.concatenate(outs, axis=0)
        return h

    lax.fori_loop(0, lt // span, fwd_body, jnp.zeros((1, LANES), F32))

    def rev_body(i, h, top):
        sl = pl.ds(pl.multiple_of((top - 1 - i) * span, span), span)
        a_all, b_all = ar_ref[0, 0, sl, :], br_ref[0, 0, sl, :]
        scans = [group_scan(a_all[j * SUBLANES:(j + 1) * SUBLANES], b_all[j * SUBLANES:(j + 1) * SUBLANES], True)
                 for j in range(ng)]
        outs = [None] * ng
        for j in reversed(range(ng)):
            a, b = scans[j]
            hh = a * h + b
            outs[j] = hh
            h = hh[0:1, :]
        gate = g_ref[0, sl, :].astype(F32)
        o_ref[0, sl, :] = ((acc_ref[sl, :] + jnp.concatenate(outs, axis=0)) * _gelu_tanh(gate)).astype(o_ref.dtype)
        return h

    h = lax.fori_loop(0, cl // span, functools.partial(rev_body, top=cl // span), jnp.zeros((1, LANES), F32))
    lax.fori_loop(0, (lt - cl) // span, functools.partial(rev_body, top=lt // span), h)


def _lru_scan(a, b_, plr, cl):
    _, b, lt, c = a.shape
    nl = c // LANES
    fwd = pl.BlockSpec((1, 1, lt, LANES), lambda i, j: (0, i, 0, j))
    rev = pl.BlockSpec((1, 1, lt, LANES), lambda i, j: (1, i, 0, j))
    return pl.pallas_call(
        functools.partial(_lru_scan_kernel, cl=cl, lt=lt),
        out_shape=jax.ShapeDtypeStruct((b, lt, c), BF16), grid=(b, nl),
        in_specs=[fwd, fwd, rev, rev, pl.BlockSpec((1, lt, LANES), lambda i, j: (i, 0, j))],
        out_specs=pl.BlockSpec((1, lt, LANES), lambda i, j: (i, 0, j)),
        scratch_shapes=[pltpu.VMEM((lt, LANES), F32)],
        compiler_params=_cparams("parallel", "parallel"), name="lru_scan",
    )(a, b_, a, b_, plr)


def _row_perm(tr, rev):
    t = lax.broadcasted_iota(jnp.int32, (tr, tr), 0)
    s = lax.broadcasted_iota(jnp.int32, (tr, tr), 1)
    return jnp.where(s == jnp.where(rev, tr - 1 - t, t), 1.0, 0.0).astype(BF16)


def _mirror_tile(r, rev, n_ctx_tiles, n_tiles):
    m = jnp.where(r < n_ctx_tiles, n_ctx_tiles - 1 - r, n_tiles - 1 + n_ctx_tiles - r)
    return jnp.where(rev, m, r)


def _chunk_masks(tr):
    t = lax.broadcasted_iota(jnp.int32, (tr, tr), 0)
    s = lax.broadcasted_iota(jnp.int32, (tr, tr), 1)
    same = (t // RWKV_CHUNK) == (s // RWKV_CHUNK)
    return same, same & (s <= t), same & (s < t)


def _rwkv_prep_kernel(p_ref, pv_ref, nx_ref, mu_ref, w0_ref, w2_ref, a0_ref, a2_ref, g2_ref, kk_ref, ka_ref,
                      rk_ref, ones_ref,
                      aq_ref, vp_ref, y0_ref, rt_ref, mrb_ref, bht_ref, gm_ref, pc_ref, g_ref, bonus_ref,
                      *, cl, lt, tr):
    c = RWKV_WIDTH
    rows = _row_ids(tr, pl.program_id(2))
    rev = pl.program_id(0) == 1
    p = _dot(_row_perm(tr, rev), p_ref[0])
    before = jnp.where(rev, nx_ref[0, :1, :], pv_ref[0, SUBLANES - 1:, :]).astype(F32)
    after = jnp.where(rev, pv_ref[0, SUBLANES - 1:, :], nx_ref[0, :1, :]).astype(F32)
    prev = _shift_rows(p, before, -1, rows, cl, lt)
    nxt = _shift_rows(p, after, 1, rows, cl, lt)
    xm = p + (prev - p) * mu_ref[0, 0:1, :] + (nxt - p) * mu_ref[0, 1:2, :]
    r, k, v = xm[:, :c], xm[:, c:2 * c], xm[:, 2 * c:3 * c]
    o = 3 * c
    w1 = xm[:, o:o + 2 * RWKV_DECAY_RANK]
    a1 = xm[:, o + 2 * RWKV_DECAY_RANK:o + 2 * RWKV_DECAY_RANK + 2 * RWKV_ICLR_RANK]
    g1 = xm[:, o + 2 * RWKV_DECAY_RANK + 2 * RWKV_ICLR_RANK:]
    wlog = -_softplus(-(w0_ref[0] + _dot(jnp.tanh(w1).astype(BF16), w2_ref[0]))) - 0.5
    ld = -jnp.exp(wlog)
    a = _sigmoid(a0_ref[0] + _dot(a1.astype(BF16), a2_ref[0]))
    g_ref[0, 0] = _dot(_sigmoid(g1).astype(BF16), g2_ref[...])
    kk = k * kk_ref[...]
    kk = kk * lax.rsqrt(_dot_exact_rhs(kk * kk, ones_ref[...]) + 1e-12)
    kd = k * (1.0 + (a - 1.0) * ka_ref[...])
    bonus_ref[0, 0] = _dot_exact_rhs(r * kd * rk_ref[...], ones_ref[...]) * v
    beta = kk * a

    _rwkv_chunk_stage(ld, kk, r, kd, beta, v, aq_ref, vp_ref, y0_ref, rt_ref, mrb_ref, bht_ref, gm_ref, pc_ref)


def _rwkv_chunk_stage(ld, kk, r, kd, beta, v, aq_ref, vp_ref, y0_ref, rt_ref, mrb_ref, bht_ref, gm_ref, pc_ref):
    tr = ld.shape[0]
    n = min(RWKV_STAGE_ROWS, tr)
    nparts = tr // n
    hd = HEAD_DIM
    ch = RWKV_CHUNK
    nch = n // ch
    same_t, incl_t, _ = _chunk_masks(tr)
    cum = _dot_exact_lhs(jnp.where(incl_t, 1.0, 0.0).astype(BF16), ld)
    tot = _dot_exact_lhs(jnp.where(same_t, 1.0, 0.0).astype(BF16), ld)
    _, incl, strict = _chunk_masks(n)
    alpha_t = kk * jnp.exp(cum - ld)
    r_t = r * jnp.exp(cum)
    e_neg = jnp.exp(-cum)
    k_t = kd * e_neg
    b_t = beta * e_neg
    e_rem = jnp.exp(tot - cum)
    k_hat_t = (kd * e_rem).T
    b_hat_t = (beta * e_rem).T
    pc_t = jnp.exp(tot).T
    t_i = lax.broadcasted_iota(jnp.int32, (n, n), 0)
    s_i = lax.broadcasted_iota(jnp.int32, (n, n), 1)
    eye_f = jnp.where(t_i == s_i, 1.0, 0.0)
    same_blk = []
    size = RWKV_INV_BASE
    while size <= ch:
        same_blk.append((t_i // size) == (s_i // size))
        size *= 2
    col_chunk = lax.broadcasted_iota(jnp.int32, (n, nch * hd), 1) // hd
    row_chunk = lax.broadcasted_iota(jnp.int32, (n, nch * hd), 0) // ch

    def diag_blocks(m):
        out = m[:, :ch]
        for j in range(1, nch):
            out = out + m[:, j * ch:(j + 1) * ch]
        return out

    chains = [(p, h) for p in range(nparts) for h in range(RWKV_HEADS)]

    def blk(t, p, h):
        return t[p * n:(p + 1) * n, h * hd:(h + 1) * hd]

    prods = [_dot_nt(jnp.concatenate([blk(alpha_t, p, h), blk(r_t, p, h)], axis=0).astype(BF16),
                     jnp.concatenate([blk(b_t, p, h), blk(k_t, p, h)], axis=0).astype(BF16))
             for p, h in chains]
    l_ab = [jnp.where(strict, pr[:n, :n], 0.0) for pr in prods]
    pw = [jnp.where(same_blk[0], l, 0.0) for l in l_ab]
    t_inv = [eye_f - l for l in pw]
    for _ in range(int(math.log2(RWKV_INV_BASE)) - 1):
        pw = [_dot_bf(q, q) for q in pw]
        t_inv = [t + _dot_bf(t, q) for t, q in zip(t_inv, pw)]
    for lvl in range(1, len(same_blk)):
        off = same_blk[lvl] & jnp.logical_not(same_blk[lvl - 1])
        half = [_dot_bf(t, jnp.where(off, l, 0.0)) for t, l in zip(t_inv, l_ab)]
        t_inv = [t - _dot_bf(hf, t) for t, hf in zip(t_inv, half)]
    vh = [blk(v, p, h).astype(BF16) for p, h in chains]
    lakv = [_dot(jnp.where(strict, pr[:n, n:], 0.0).astype(BF16), vv) for pr, vv in zip(prods, vh)]
    x = [_dot(t.astype(BF16), jnp.concatenate([blk(alpha_t, p, h), lv], axis=1).astype(BF16))
         for t, (p, h), lv in zip(t_inv, chains, lakv)]
    y0 = [_dot(jnp.where(incl, pr[n:, n:], 0.0).astype(BF16), vv) for pr, vv in zip(prods, vh)]
    mrb = [diag_blocks(jnp.where(incl, pr[n:, :n], 0.0)) for pr in prods]
    gms = [_dot(k_hat_t[h * hd:(h + 1) * hd, p * n:(p + 1) * n].astype(BF16),
                jnp.where(col_chunk == row_chunk, jnp.concatenate([blk(v, p, h)] * nch, axis=1), 0.0).astype(BF16))
           for p, h in chains]

    def assemble(parts):
        return jnp.concatenate([jnp.concatenate(parts[p * RWKV_HEADS:(p + 1) * RWKV_HEADS], axis=1)
                                for p in range(nparts)], axis=0)

    aq_ref[0, 0] = assemble([t[:, :hd] for t in x]).astype(aq_ref.dtype)
    vp_ref[0, 0] = assemble([t[:, hd:] for t in x])
    y0_ref[0, 0] = assemble(y0)
    rt_ref[0, 0] = r_t.astype(rt_ref.dtype)
    mrb_ref[0, 0] = assemble(mrb).astype(mrb_ref.dtype)
    for p in range(nparts):
        for j in range(nch):
            jj = p * nch + j
            cs = slice(jj * ch, (jj + 1) * ch)
            bht_ref[0, 0, jj] = jnp.concatenate([b_hat_t[h * hd:(h + 1) * hd, cs] for h in range(RWKV_HEADS)],
                                                axis=1).astype(bht_ref.dtype)
            gm_ref[0, 0, jj] = jnp.concatenate([gms[p * RWKV_HEADS + h][:, j * hd:(j + 1) * hd]
                                                for h in range(RWKV_HEADS)], axis=1)
            pc_ref[0, 0, jj] = jnp.concatenate([pc_t[h * hd:(h + 1) * hd, cs] for h in range(RWKV_HEADS)], axis=1)


def _rwkv_tile(lt, cl):
    tr = 256 if (lt % 256 == 0 and cl % 256 == 0) else 128
    assert lt % tr == 0 and cl % tr == 0
    return tr


def _rwkv_prep(prw, mud, w0, w2p, a0, a2p, g2, k_k, k_a, r_k, cl):
    b, lt, w = prw.shape
    c = RWKV_WIDTH
    tr = _rwkv_tile(lt, cl)
    nch = tr // RWKV_CHUNK
    per = tr // SUBLANES
    src = lambda d, r: _mirror_tile(r, d == 1, cl // tr, lt // tr)
    prev_spec = pl.BlockSpec((1, SUBLANES, w), lambda d, i, r: (i, jnp.maximum(src(d, r) * per - 1, 0), 0))
    next_spec = pl.BlockSpec((1, SUBLANES, w),
                             lambda d, i, r: (i, jnp.minimum((src(d, r) + 1) * per, lt // SUBLANES - 1), 0))
    tile = pl.BlockSpec((1, 1, tr, c), lambda d, i, r: (d, i, r, 0))
    per_dir = lambda shape: pl.BlockSpec((1, *shape), lambda d, i, r: (d,) + (0,) * len(shape))
    const = lambda shape: pl.BlockSpec(shape, lambda d, i, r: (0,) * len(shape))
    seq = lambda dt: jax.ShapeDtypeStruct((2, b, lt, c), dt)
    chunked = jax.ShapeDtypeStruct((2, b, lt // RWKV_CHUNK, HEAD_DIM, c), F32)
    chunk_spec = pl.BlockSpec((1, 1, nch, HEAD_DIM, c), lambda d, i, r: (d, i, r, 0, 0))
    return pl.pallas_call(
        functools.partial(_rwkv_prep_kernel, cl=cl, lt=lt, tr=tr),
        out_shape=(seq(BF16), seq(F32), seq(F32), seq(BF16), seq(BF16),
                   jax.ShapeDtypeStruct(chunked.shape, BF16), chunked, chunked, seq(F32), seq(F32)),
        grid=(2, b, lt // tr),
        in_specs=[pl.BlockSpec((1, tr, w), lambda d, i, r: (i, src(d, r), 0)), prev_spec, next_spec,
                  per_dir((2, w)), per_dir((1, c)), per_dir((2 * RWKV_DECAY_RANK, c)), per_dir((1, c)),
                  per_dir((2 * RWKV_ICLR_RANK, c)), const((RWKV_GATE_RANK, c)), const((1, c)), const((1, c)),
                  const((1, c)), const((c, c))],
        out_specs=(tile, tile, tile, tile, tile, chunk_spec, chunk_spec, chunk_spec, tile, tile),
        compiler_params=_cparams("parallel", "parallel", "parallel"), name="rwkv_prep",
    )(prw, prw, prw, mud, w0.reshape(2, 1, c), w2p, a0.reshape(2, 1, c), a2p, g2.astype(BF16), k_k.reshape(1, c),
      k_a.reshape(1, c), r_k.reshape(1, c), _head_ones(c))


def _rwkv_scan_kernel(aq_ref, vp_ref, y0_ref, rt_ref, mrb_ref, bht_ref, gm_ref, pc_ref, y_ref, s_ref, *, ns):
    @pl.when(pl.program_id(1) == 0)
    def _():
        s_ref[...] = jnp.zeros_like(s_ref)

    c = RWKV_WIDTH
    ch = RWKV_CHUNK
    same_head = (lax.broadcasted_iota(jnp.int32, (c, c), 0) // HEAD_DIM
                 == lax.broadcasted_iota(jnp.int32, (c, c), 1) // HEAD_DIM)

    def head_blockdiag(t):
        return jnp.where(same_head, jnp.concatenate([t] * RWKV_HEADS, axis=0), 0.0).astype(BF16)

    for i in range(ns):
        s0 = s_ref[i]
        r1 = _dot(jnp.concatenate([aq_ref[i, 0], rt_ref[i, 0]], axis=0), head_blockdiag(s0))
        u = r1[:ch] + vp_ref[i, 0]
        r2 = _dot(jnp.concatenate([mrb_ref[i, 0], bht_ref[i, 0, 0]], axis=0), head_blockdiag(u))
        y_ref[i, 0] = r1[ch:] + y0_ref[i, 0] - r2[:ch]
        s_ref[i] = pc_ref[i, 0, 0] * s0 + gm_ref[i, 0, 0] - r2[ch:]


def _rwkv_scan(aq, vp, y0, rt, mrb, bht, gm, pc):
    _, b, lt, c = aq.shape
    nstream = 2 * b
    ns = nstream
    ch = RWKV_CHUNK
    merge = lambda t: t.reshape(nstream, 1, *t.shape[2:])
    tile = pl.BlockSpec((ns, 1, ch, c), lambda s, j: (s, 0, j, 0))
    chunk_spec = pl.BlockSpec((ns, 1, 1, HEAD_DIM, c), lambda s, j: (s, 0, j, 0, 0))
    y = pl.pallas_call(
        functools.partial(_rwkv_scan_kernel, ns=ns),
        out_shape=jax.ShapeDtypeStruct((nstream, 1, lt, c), F32), grid=(nstream // ns, lt // ch),
        in_specs=[tile, tile, tile, tile, tile, chunk_spec, chunk_spec, chunk_spec],
        out_specs=tile,
        scratch_shapes=[pltpu.VMEM((ns, HEAD_DIM, c), F32)],
        compiler_params=_cparams("parallel", "arbitrary"), name="rwkv_scan",
    )(*(merge(t) for t in (aq, vp, y0, rt, mrb, bht, gm, pc)))
    return y.reshape(2, b, lt, c)


def _rwkv_readout_kernel(yf_ref, yr_ref, bf_ref, br_ref, g_ref, lw_ref, lb_ref, ones_ref, o_ref, *, tr):
    unflip = _row_perm(tr, True)
    y = yf_ref[0, 0] + _dot_exact_lhs(unflip, yr_ref[0, 0])
    bonus = bf_ref[0, 0] + _dot_exact_lhs(unflip, br_ref[0, 0])
    inv = 1.0 / HEAD_DIM
    mu = _dot_exact_rhs(y, ones_ref[...]) * inv
    yc = y - mu
    var = _dot_exact_rhs(yc * yc, ones_ref[...]) * inv
    yn = yc * lax.rsqrt(var + RWKV_GN_EPS) * lw_ref[...] + lb_ref[...]
    o_ref[0] = ((yn + bonus) * g_ref[0, 0]).astype(o_ref.dtype)


def _rwkv_readout(y, bonus, g, ln_w, ln_b, cl):
    _, b, lt, c = y.shape
    tr = _rwkv_tile(lt, cl)
    fwd = pl.BlockSpec((1, 1, tr, c), lambda i, r: (0, i, r, 0))
    rev = pl.BlockSpec((1, 1, tr, c), lambda i, r: (1, i, _mirror_tile(r, True, cl // tr, lt // tr), 0))
    row = pl.BlockSpec((1, c), lambda i, r: (0, 0))
    return pl.pallas_call(
        functools.partial(_rwkv_readout_kernel, tr=tr),
        out_shape=jax.ShapeDtypeStruct((b, lt, c), BF16), grid=(b, lt // tr),
        in_specs=[fwd, rev, fwd, rev, fwd, row, row, pl.BlockSpec((c, c), lambda i, r: (0, 0))],
        out_specs=pl.BlockSpec((1, tr, c), lambda i, r: (i, r, 0)),
        compiler_params=_cparams("parallel", "parallel"), name="rwkv_readout",
    )(y, y, bonus, bonus, g, ln_w.reshape(1, c), ln_b.reshape(1, c), _head_ones(c))


def _pad_rank_rows(w):
    z = jnp.zeros_like(w[0])
    return jnp.stack([jnp.concatenate([w[0], z], 0), jnp.concatenate([z, w[1]], 0)])


def _rwkv_mixer(prw, mu, w0, w2, a0, a2, g2, k_k, k_a, r_k, ln_w, ln_b, cl):
    mud = jnp.stack([mu, mu[::-1]])
    outs = _rwkv_prep(prw, mud, w0, _pad_rank_rows(w2).astype(BF16), a0, _pad_rank_rows(a2).astype(BF16), g2,
                      k_k, k_a, r_k, cl)
    *scan_in, g, bonus = outs
    y = _rwkv_scan(*scan_in)
    return _rwkv_readout(y, bonus, g, ln_w, ln_b, cl)


def _merge_kernel(ya_ref, yh_ref, yr_ref, yl_ref, gt_ref, x_ref, modx_ref, modc_ref,
                  wa_ref, wh_ref, wr_ref, wl_ref, wo_ref, rth_ref, rtl_ref, rtb_ref, o_ref, h_ref, cmb_ref, *, cl, tm, d):
    is_ctx = _row_ids(tm, pl.program_id(1)) < cl
    m = None
    for i, (y_ref, w_ref) in enumerate(((ya_ref, wa_ref), (yh_ref, wh_ref), (yr_ref, wr_ref), (yl_ref, wl_ref))):
        gate = _sigmoid(gt_ref[0, :, i * d:(i + 1) * d].astype(F32))
        term = gate * _dot(y_ref[0], w_ref[...])
        m = term if m is None else m + term
    g1 = _mod_rows(modx_ref, modc_ref, 2, is_ctx)
    x_new = x_ref[0] + g1 * _dot(m.astype(BF16), wo_ref[...])
    o_ref[0] = x_new
    h, cmb = _route_rows(x_new, is_ctx, modx_ref, modc_ref, rth_ref, rtl_ref, rtb_ref)
    h_ref[0] = h
    cmb_ref[0] = cmb


def _merge(ys, gates, xc, mod, w_brs, w_out, router, cl):
    b, lt, d = xc.shape
    tm = _largest_tile(lt, 544, 16)
    row = lambda w: pl.BlockSpec((1, tm, w), lambda i, r: (i, r, 0))
    const = lambda a: pl.BlockSpec(a.shape, lambda i, r: (0, 0))
    ws = [w.astype(BF16) for w in w_brs] + [w_out.astype(BF16)] + list(router)
    return pl.pallas_call(
        functools.partial(_merge_kernel, cl=cl, tm=tm, d=d),
        out_shape=(jax.ShapeDtypeStruct((b, lt, d), F32), jax.ShapeDtypeStruct((b, lt, d), BF16),
                   jax.ShapeDtypeStruct((b, lt, LANES), F32)),
        grid=(b, lt // tm),
        in_specs=[row(y.shape[-1]) for y in ys] + [row(N_BRANCH * d), row(d),
                  pl.BlockSpec((1, 6, d), lambda i, r: (i, 0, 0)), pl.BlockSpec((1, 6, d), lambda i, r: (b, 0, 0))]
                 + [const(w) for w in ws],
        out_specs=(row(d), row(d), row(LANES)), compiler_params=_cparams("parallel", "parallel"), name="merge",
    )(*ys, gates, xc, mod, mod, *ws)


def _route_rows(x, is_ctx, modx_ref, modc_ref, wrh_ref, wrl_ref, br_ref):
    h = _rms_modulate(x, _mod_rows(modx_ref, modc_ref, 3, is_ctx), _mod_rows(modx_ref, modc_ref, 4, is_ctx))
    h_hi = h.astype(BF16)
    h_lo = (h - h_hi.astype(F32)).astype(BF16)
    lg = _dot(h_hi, wrh_ref[...]) + (_dot(h_hi, wrl_ref[...]) + _dot(h_lo, wrh_ref[...])) + br_ref[...]
    lane = lax.broadcasted_iota(jnp.int32, lg.shape, 1)
    lane_f = lane.astype(F32)
    neg = -jnp.inf
    big = 1e9

    def first_lane(cond):
        return jnp.min(jnp.where(cond, lane_f, big), axis=-1, keepdims=True)

    is_grp = (lane >= N_EXPERTS) & (lane < N_EXPERTS + N_GROUPS)
    gl = jnp.where(is_grp, lg, neg)
    gmax = jnp.max(gl, axis=-1, keepdims=True)
    ge = jnp.where(is_grp, jnp.exp(gl - gmax), 0.0)
    gp = ge / jnp.sum(ge, axis=-1, keepdims=True)
    g_val = jnp.max(gp, axis=-1, keepdims=True)
    g_idx = first_lane(is_grp & (gp == g_val)) - N_EXPERTS
    lo = g_idx * EXPERTS_PER_GROUP
    in_grp = (lane_f >= lo) & (lane_f < lo + EXPERTS_PER_GROUP)
    el = jnp.where(in_grp, lg, neg)
    emax = jnp.max(el, axis=-1, keepdims=True)
    ee = jnp.where(in_grp, jnp.exp(el - emax), 0.0)
    pe = ee / jnp.sum(ee, axis=-1, keepdims=True)
    v1 = jnp.max(jnp.where(in_grp, pe, -1.0), axis=-1, keepdims=True)
    i1 = first_lane(in_grp & (pe == v1))
    rest = in_grp & (lane_f != i1)
    v2 = jnp.max(jnp.where(rest, pe, -1.0), axis=-1, keepdims=True)
    i2 = first_lane(rest & (pe == v2))
    den = v1 + v2
    cmb = (jnp.where(lane_f == i1, g_val * v1 / den, 0.0) + jnp.where(lane_f == i2, g_val * v2 / den, 0.0)
           + jnp.where(lane == N_EXPERTS, g_idx, 0.0))
    return h_hi, cmb


def _router_weights(w_grp, b_grp, w_rt, b_rt):
    d = w_rt.shape[0]
    wr = jnp.zeros((d, LANES), F32).at[:, :N_EXPERTS].set(w_rt).at[:, N_EXPERTS:N_EXPERTS + N_GROUPS].set(w_grp)
    br = jnp.zeros((1, LANES), F32).at[0, :N_EXPERTS].set(b_rt).at[0, N_EXPERTS:N_EXPERTS + N_GROUPS].set(b_grp)
    hi = wr.astype(BF16)
    return hi, (wr - hi.astype(F32)).astype(BF16), br


def _moe_kernel(grp_ref, nvalid_ref, h_ref, cmb_ref, w1_ref, w3_ref, w2_ref, o_ref, acc_ref, *, bm):
    j = pl.program_id(0)
    e = pl.program_id(1)

    @pl.when(e == 0)
    def _():
        acc_ref[...] = jnp.zeros_like(acc_ref)

    @pl.when(j < nvalid_ref[0])
    def _():
        h = h_ref[...]
        t = _silu(_dot(h, w1_ref[0, 0])) * _dot(h, w3_ref[0, 0])
        y = _dot(t.astype(BF16), w2_ref[0, 0])
        lane = lax.broadcasted_iota(jnp.int32, (bm, LANES), 1)
        expert = grp_ref[j] * EXPERTS_PER_GROUP + e
        wcol = jnp.sum(jnp.where(lane == expert, cmb_ref[...], 0.0), axis=-1, keepdims=True)
        acc_ref[...] += wcol * y

    @pl.when(e == pl.num_programs(1) - 1)
    def _():
        o_ref[...] = acc_ref[...].astype(o_ref.dtype)


def _moe_grouped(hs, ws, blk_group, nvalid, w1, w3, w2, layer):
    s_rows, d = hs.shape
    hid = w1.shape[3]
    bm = MOE_BLOCK_ROWS
    wspec = lambda shape: pl.BlockSpec(shape, lambda j, e, grp, nv: (layer, grp[j] * EXPERTS_PER_GROUP + e, 0, 0))
    return pl.pallas_call(
        functools.partial(_moe_kernel, bm=bm),
        out_shape=jax.ShapeDtypeStruct((s_rows, d), BF16),
        grid_spec=pltpu.PrefetchScalarGridSpec(
            num_scalar_prefetch=2, grid=(s_rows // bm, EXPERTS_PER_GROUP),
            in_specs=[pl.BlockSpec((bm, d), lambda j, e, grp, nv: (j, 0)),
                      pl.BlockSpec((bm, LANES), lambda j, e, grp, nv: (j, 0)),
                      wspec((1, 1, d, hid)), wspec((1, 1, d, hid)), wspec((1, 1, hid, d))],
            out_specs=pl.BlockSpec((bm, d), lambda j, e, grp, nv: (j, 0)),
            scratch_shapes=[pltpu.VMEM((bm, d), F32)]),
        compiler_params=_cparams("parallel", "arbitrary"), name="moe_experts",
    )(blk_group, nvalid, hs, ws, w1, w3, w2)


def _moe_finish_kernel(x_ref, y_ref, modx_ref, modc_ref, o_ref, *, cl, tm, first):
    is_ctx = _row_ids(tm, pl.program_id(1) + first) < cl
    o_ref[0] = x_ref[0] + _mod_rows(modx_ref, modc_ref, 5, is_ctx) * y_ref[0].astype(F32)


def _moe_finish(xc, y, mod, cl, latent_only):
    b, lt, d = xc.shape
    if latent_only:
        tm = _largest_tile(math.gcd(cl, lt - cl), 1088, 16)
        first = cl // tm
    else:
        tm, first = _largest_tile(lt, 1088, 16), 0
    src = pl.BlockSpec((1, tm, d), lambda i, r: (i, r + first, 0))
    n_rows = lt - first * tm
    return pl.pallas_call(
        functools.partial(_moe_finish_kernel, cl=cl, tm=tm, first=first),
        out_shape=jax.ShapeDtypeStruct((b, n_rows, d), F32), grid=(b, n_rows // tm),
        in_specs=[src, src, pl.BlockSpec((1, 6, d), lambda i, r: (i, 0, 0)),
                  pl.BlockSpec((1, 6, d), lambda i, r: (b, 0, 0))],
        out_specs=pl.BlockSpec((1, tm, d), lambda i, r: (i, r, 0)),
        compiler_params=_cparams("parallel", "parallel"), name="moe_finish",
    )(xc, y, mod, mod)


def _moe(h2, cmb, xc, mod, w1, w3, w2, layer, cl, latent_only=False):
    b, lt, d = xc.shape
    t = b * lt
    bm = MOE_BLOCK_ROWS
    s_rows = -(-t // bm) * bm + N_GROUPS * bm
    cmb2 = cmb.reshape(t, LANES)
    gid = cmb2[:, N_EXPERTS].astype(jnp.int32)
    onehot = (gid[:, None] == jnp.arange(N_GROUPS, dtype=jnp.int32)).astype(jnp.int32)
    csum = jnp.cumsum(onehot, axis=0)
    rank = jnp.sum(onehot * csum, axis=1) - 1
    padded = -(-csum[-1] // bm) * bm
    ends = jnp.cumsum(padded)
    pos = (ends - padded)[gid] + rank
    slot_token = jnp.zeros((s_rows,), jnp.int32).at[pos].set(jnp.arange(t, dtype=jnp.int32))
    starts = jnp.arange(s_rows // bm, dtype=jnp.int32) * bm
    blk_group = jnp.minimum(jnp.sum((starts[:, None] >= ends[None, :]).astype(jnp.int32), axis=1), N_GROUPS - 1)
    nvalid = (ends[-1] // bm).reshape(1).astype(jnp.int32)
    take_rows = lambda a, idx: a.at[idx].get(mode="promise_in_bounds")
    hs = take_rows(h2.reshape(t, d), slot_token)
    ws = take_rows(cmb2, slot_token)
    ys = _moe_grouped(hs, ws, blk_group.astype(jnp.int32), nvalid, w1, w3, w2, layer)
    y = take_rows(ys, pos).reshape(b, lt, d)
    return _moe_finish(xc, y, mod, cl, latent_only)


def kernel(x, c, ctx, c_ctx, ada_w, ada_b, w_in, q_norm, k_norm, hy_conv_w, hy_conv_b, hy_f1, hy_fb1, hy_f2, hy_fb2, hy_f3, hy_skip, rw_mu, rw_w0, rw_w2, rw_a0, rw_a2, rw_g2, rw_k_k, rw_k_a, rw_r_k, rw_ln_w, rw_ln_b, lru_conv_w, lru_conv_b, lru_wa, lru_ba, lru_wx, lru_bx, lru_lambda, w_br_attn, w_br_hyena, w_br_rwkv, w_br_lru, w_out, moe_w_grp, moe_b_grp, moe_w_rt, moe_b_rt, moe_w1, moe_w3, moe_w2):
    b, l, d = x.shape
    cl = ctx.shape[1]
    depth = ada_w.shape[0]
    assert b < MOD_ROWS and cl % RWKV_CHUNK == 0 and l % RWKV_CHUNK == 0

    xc = jnp.concatenate([ctx, x], axis=1)
    cc = jnp.zeros((MOD_ROWS, d), F32).at[:b].set(c).at[b].set(c_ctx)
    mod_all = _ada_mod(cc, ada_w, ada_b).reshape(depth, MOD_ROWS, 6, d)

    moe_w1b, moe_w3b, moe_w2b = (w.astype(BF16) for w in (moe_w1, moe_w3, moe_w2))
    cos2, sin2 = _rope_tables(l, cl)
    mats_x = _dft_mats(l)
    mats_c = _dft_mats(cl)
    qkv_w = ATTN_WIDTH + 2 * ATTN_KV_WIDTH
    col = np.cumsum([0, qkv_w, 3 * HYENA_WIDTH, RWKV_PROJ, 2 * LRU_WIDTH, N_BRANCH * d])

    for i in range(depth):
        need_ctx = i < depth - 1
        mod = mod_all[i]
        w_i = w_in[i].astype(BF16)
        h1 = _modnorm(xc, mod, cl)
        pqkv, phy, prw, plr, gates = (_proj(h1, w_i[:, col[j]:col[j + 1]]) for j in range(5))

        qn, kt, vx = _attn_prep(pqkv, cos2, sin2, q_norm[i], k_norm[i])
        y_att_x = _attention(qn, kt, vx, cl, 'x', cl + l)
        if need_ctx:
            y_att_c = _attention(qn, kt, vx, cl, 'ctx', cl)
        else:
            y_att_c = jnp.zeros((b, cl, ATTN_WIDTH), BF16)
        y_att = jnp.concatenate([y_att_c, y_att_x], axis=1)

        hv, hx1, hx2 = _hyena_pre(phy, hy_conv_w[i], hy_conv_b[i], cl)
        filt = (hy_f1[i], hy_fb1[i], hy_f2[i], hy_fb2[i], hy_f3[i])
        y_hx = _hyena_run(hv[:, cl:], hx1[:, cl:], hx2[:, cl:], _hyena_spectra(l, filt, mats_x), hy_skip[i], mats_x)
        if need_ctx:
            y_hc = _hyena_run(hv[:, :cl], hx1[:, :cl], hx2[:, :cl], _hyena_spectra(cl, filt, mats_c), hy_skip[i],
                              mats_c)
        else:
            y_hc = jnp.zeros((b, cl, HYENA_WIDTH), BF16)
        y_hy = jnp.concatenate([y_hc, y_hx], axis=1)

        y_rw = _rwkv_mixer(prw, rw_mu[i], rw_w0[i], rw_w2[i], rw_a0[i], rw_a2[i], rw_g2[i], rw_k_k[i], rw_k_a[i],
                           rw_r_k[i].reshape(-1), rw_ln_w[i], rw_ln_b[i], cl)

        la, lb = _lru_pre(plr, lru_conv_w[i], lru_conv_b[i], lru_wa[i], lru_ba[i], lru_wx[i], lru_bx[i],
                          lru_lambda[i], cl)
        y_lr = _lru_scan(la, lb, plr, cl)

        xc, h2, cmb = _merge((y_att, y_hy, y_rw, y_lr), gates, xc, mod,
                             (w_br_attn[i], w_br_hyena[i], w_br_rwkv[i], w_br_lru[i]), w_out[i],
                             _router_weights(moe_w_grp[i], moe_b_grp[i], moe_w_rt[i], moe_b_rt[i]), cl)
        xc = _moe(h2, cmb, xc, mod, moe_w1b, moe_w3b, moe_w2b, i, cl, latent_only=not need_ctx)
    return xc
```

```python
import functools
import math

import numpy as np
import jax
import jax.numpy as jnp
from jax import lax
from jax.experimental import pallas as pl
from jax.experimental.pallas import tpu as pltpu

F32 = jnp.float32
BF16 = jnp.bfloat16

HEAD_DIM = 64
GRID_W = 64
EPS = 1e-6
ATTN_HEADS = 8
ATTN_KV_HEADS = 2
ATTN_GROUP = ATTN_HEADS // ATTN_KV_HEADS
ATTN_Q_BLOCKS = 2
ATTN_WIDTH = ATTN_HEADS * HEAD_DIM
ATTN_KV_WIDTH = ATTN_KV_HEADS * HEAD_DIM
ROPE_THETA = 10000.0
HYENA_WIDTH = 256
HYENA_ORDER = 2
HYENA_BANDS = 16
HYENA_DECAY_TARGET = 1e-2
HYENA_FAST_DECAY = 0.3
HYENA_SLOW_DECAY = 1.5
RWKV_HEADS = 4
RWKV_WIDTH = RWKV_HEADS * HEAD_DIM
RWKV_DECAY_RANK = 64
RWKV_ICLR_RANK = 64
RWKV_GATE_RANK = 128
RWKV_GN_EPS = 64e-5
RWKV_PROJ = 3 * RWKV_WIDTH + 2 * RWKV_DECAY_RANK + 2 * RWKV_ICLR_RANK + RWKV_GATE_RANK
RWKV_CHUNK = 64
RWKV_INV_BASE = 4
RWKV_STAGE_ROWS = 128
LRU_WIDTH = 256
LRU_BLOCKS = 4
LRU_C = 8.0
LRU_GROUPS_PER_STEP = 4
N_BRANCH = 4
N_GROUPS = 4
EXPERTS_PER_GROUP = 4
N_EXPERTS = N_GROUPS * EXPERTS_PER_GROUP
MOE_BLOCK_ROWS = 512
MOE_PARTS = 2

V7X_VMEM_LIMIT_BYTES = 52 * 1024 * 1024
SUBLANES = 8
LANES = 128
MOD_ROWS = 16


def _cparams(*sem):
    return pltpu.CompilerParams(dimension_semantics=sem, vmem_limit_bytes=V7X_VMEM_LIMIT_BYTES)


def _dot(a, b):
    return jnp.dot(a, b, preferred_element_type=F32)


def _dot_nt(a, b):
    return lax.dot_general(a, b, (((1,), (1,)), ((), ())), preferred_element_type=F32)


def _split3(x):
    hi = x.astype(BF16)
    r1 = x - hi.astype(F32)
    mid = r1.astype(BF16)
    lo = (r1 - mid.astype(F32)).astype(BF16)
    return hi, mid, lo


def _dot_exact_lhs(m_bf16, x):
    hi, mid, lo = _split3(x)
    return _dot(m_bf16, hi) + _dot(m_bf16, mid) + _dot(m_bf16, lo)


def _dot_exact_rhs(x, m_bf16):
    hi = x.astype(BF16)
    lo = (x - hi.astype(F32)).astype(BF16)
    return _dot(hi, m_bf16) + _dot(lo, m_bf16)


def _dot_bf(a, b):
    return _dot(a.astype(BF16), b.astype(BF16))


def _sigmoid(x):
    return 0.5 * jnp.tanh(0.5 * x) + 0.5


def _softplus(x):
    return jnp.maximum(x, 0.0) + jnp.log(1.0 + jnp.exp(-jnp.abs(x)))


def _silu(x):
    return x * _sigmoid(x)


def _largest_tile(n, cap, mult):
    best = None
    for t in range(mult, min(n, cap) + 1, mult):
        if n % t == 0:
            best = t
    assert best is not None, (n, cap, mult)
    return best


def _head_ones(width):
    idx = np.arange(width) // HEAD_DIM
    return jnp.asarray((idx[:, None] == idx[None, :]).astype(np.float32), dtype=BF16)


def _row_ids(tile_rows, tile_idx):
    return tile_idx * tile_rows + lax.broadcasted_iota(jnp.int32, (tile_rows, 1), 0)


def _mod_rows(modx_ref, modc_ref, idx, is_ctx):
    return jnp.where(is_ctx, modc_ref[0, idx:idx + 1, :], modx_ref[0, idx:idx + 1, :])


def _rms_modulate(x, shift, scale):
    ms = jnp.mean(x * x, axis=-1, keepdims=True)
    return (x * lax.rsqrt(ms + EPS)) * (1.0 + scale) + shift


def _ada_kernel(c_ref, w_ref, b_ref, o_ref):
    s = _silu(c_ref[...])
    o_ref[0] = jnp.dot(s, w_ref[0], preferred_element_type=F32, precision=lax.Precision.HIGHEST) + b_ref[0]


def _ada_mod(cc, ada_w, ada_b):
    depth, d, n6 = ada_w.shape
    tn = _largest_tile(n6, 1024, LANES)
    return pl.pallas_call(
        _ada_kernel,
        out_shape=jax.ShapeDtypeStruct((depth, MOD_ROWS, n6), F32),
        grid=(depth, n6 // tn),
        in_specs=[pl.BlockSpec((MOD_ROWS, d), lambda i, j: (0, 0)),
                  pl.BlockSpec((1, d, tn), lambda i, j: (i, 0, j)),
                  pl.BlockSpec((1, 1, tn), lambda i, j: (i, 0, j))],
        out_specs=pl.BlockSpec((1, MOD_ROWS, tn), lambda i, j: (i, 0, j)),
        compiler_params=_cparams("parallel", "parallel"),
        name="ada_mod",
    )(cc, ada_w, ada_b.reshape(depth, 1, n6))


def _modnorm_kernel(x_ref, modx_ref, modc_ref, o_ref, *, cl, tm):
    is_ctx = _row_ids(tm, pl.program_id(1)) < cl
    h = _rms_modulate(x_ref[0], _mod_rows(modx_ref, modc_ref, 0, is_ctx), _mod_rows(modx_ref, modc_ref, 1, is_ctx))
    o_ref[0] = h.astype(o_ref.dtype)


def _modnorm(xc, mod, cl):
    b, lt, d = xc.shape
    tm = _largest_tile(lt, 1088, 16)
    return pl.pallas_call(
        functools.partial(_modnorm_kernel, cl=cl, tm=tm),
        out_shape=jax.ShapeDtypeStruct((b, lt, d), BF16), grid=(b, lt // tm),
        in_specs=[pl.BlockSpec((1, tm, d), lambda i, r: (i, r, 0)),
                  pl.BlockSpec((1, 6, d), lambda i, r: (i, 0, 0)),
                  pl.BlockSpec((1, 6, d), lambda i, r: (b, 0, 0))],
        out_specs=pl.BlockSpec((1, tm, d), lambda i, r: (i, r, 0)),
        compiler_params=_cparams("parallel", "parallel"), name="modnorm",
    )(xc, mod, mod)


def _proj_kernel(h_ref, w_ref, o_ref):
    o_ref[0] = _dot(h_ref[0], w_ref[...]).astype(o_ref.dtype)


def _proj(h, w):
    b, lt, d = h.shape
    n = w.shape[1]
    tm = _largest_tile(lt, 2176, 16)
    tn = n if n <= 1280 else _largest_tile(n, 1024, 2 * LANES)
    return pl.pallas_call(
        _proj_kernel, out_shape=jax.ShapeDtypeStruct((b, lt, n), BF16), grid=(b, lt // tm, n // tn),
        in_specs=[pl.BlockSpec((1, tm, d), lambda i, r, j: (i, r, 0)), pl.BlockSpec((d, tn), lambda i, r, j: (0, j))],
        out_specs=pl.BlockSpec((1, tm, tn), lambda i, r, j: (i, r, j)),
        compiler_params=_cparams("parallel", "parallel", "parallel"), name="proj",
    )(h, w)


def _rope_tables(l, cl):
    n_freq = HEAD_DIM // 4
    t = jnp.arange(l)
    freqs = ROPE_THETA ** (-jnp.arange(n_freq, dtype=F32) / n_freq)
    pos = jnp.stack([t // GRID_W, t % GRID_W], -1).astype(F32)
    ang = pos[..., None] * freqs
    cos64 = jnp.stack([jnp.cos(ang), jnp.cos(ang)], axis=2).reshape(l, HEAD_DIM)
    sin64 = jnp.stack([-jnp.sin(ang), jnp.sin(ang)], axis=2).reshape(l, HEAD_DIM)
    cos64 = jnp.concatenate([jnp.ones((cl, HEAD_DIM), F32), cos64], 0)
    sin64 = jnp.concatenate([jnp.zeros((cl, HEAD_DIM), F32), sin64], 0)
    return jnp.tile(cos64, (1, 2)), jnp.tile(sin64, (1, 2))


def _head_rms(t, ones_ref):
    ms = _dot_exact_rhs(t * t, ones_ref[...]) * (1.0 / HEAD_DIM)
    return t * lax.rsqrt(ms + EPS)


def _rope(t, cos, sin):
    w = t.shape[-1]
    lane = lax.broadcasted_iota(jnp.int32, t.shape, 1)
    q4 = HEAD_DIM // 4
    first_half = (lane % (2 * q4)) < q4
    partner = jnp.where(first_half, pltpu.roll(t, w - q4, 1), pltpu.roll(t, q4, 1))
    return t * cos + partner * sin


def _attn_prep_kernel(p_ref, cos_ref, sin_ref, qg_ref, kg_ref, oq_ref, ok_ref, q_ref, kt_ref, vx_ref):
    p = p_ref[0].astype(F32)
    v = p[:, ATTN_WIDTH + ATTN_KV_WIDTH:]
    low = lax.broadcasted_iota(jnp.int32, v.shape, 1) < HEAD_DIM
    vx_ref[0, 0] = jnp.where(low, v, 1.0).astype(vx_ref.dtype)
    vx_ref[0, 1] = jnp.where(low, pltpu.roll(v, HEAD_DIM, 1), 1.0).astype(vx_ref.dtype)
    cos2, sin2 = cos_ref[...], sin_ref[...]
    reps = ATTN_WIDTH // (2 * HEAD_DIM)
    cos_q = jnp.concatenate([cos2] * reps, axis=1)
    sin_q = jnp.concatenate([sin2] * reps, axis=1)
    q = _head_rms(p[:, :ATTN_WIDTH], oq_ref) * qg_ref[...]
    q_ref[0] = _rope(q, cos_q, sin_q).astype(q_ref.dtype)
    k = _head_rms(p[:, ATTN_WIDTH:ATTN_WIDTH + ATTN_KV_WIDTH], ok_ref) * kg_ref[...]
    kt_ref[0] = _rope(k, cos2, sin2).T.astype(kt_ref.dtype)


def _attn_prep(pqkv, cos2, sin2, q_gain, k_gain):
    b, lt, wtot = pqkv.shape
    tr = _largest_tile(lt, 2176, LANES)
    qg = jnp.tile(q_gain * (HEAD_DIM ** -0.5 * math.log2(math.e)), ATTN_HEADS).reshape(1, ATTN_WIDTH)
    kg = jnp.tile(k_gain, ATTN_KV_HEADS).reshape(1, ATTN_KV_WIDTH)
    return pl.pallas_call(
        _attn_prep_kernel,
        out_shape=(jax.ShapeDtypeStruct((b, lt, ATTN_WIDTH), BF16),
                   jax.ShapeDtypeStruct((b, ATTN_KV_WIDTH, lt), BF16),
                   jax.ShapeDtypeStruct((b, ATTN_KV_HEADS, lt, 2 * HEAD_DIM), BF16)),
        grid=(b, lt // tr),
        in_specs=[pl.BlockSpec((1, tr, wtot), lambda i, r: (i, r, 0)),
                  pl.BlockSpec((tr, 2 * HEAD_DIM), lambda i, r: (r, 0)),
                  pl.BlockSpec((tr, 2 * HEAD_DIM), lambda i, r: (r, 0)),
                  pl.BlockSpec((1, ATTN_WIDTH), lambda i, r: (0, 0)),
                  pl.BlockSpec((1, ATTN_KV_WIDTH), lambda i, r: (0, 0)),
                  pl.BlockSpec((ATTN_WIDTH, ATTN_WIDTH), lambda i, r: (0, 0)),
                  pl.BlockSpec((ATTN_KV_WIDTH, ATTN_KV_WIDTH), lambda i, r: (0, 0))],
        out_specs=(pl.BlockSpec((1, tr, ATTN_WIDTH), lambda i, r: (i, r, 0)),
                   pl.BlockSpec((1, ATTN_KV_WIDTH, tr), lambda i, r: (i, 0, r)),
                   pl.BlockSpec((1, ATTN_KV_HEADS, tr, 2 * HEAD_DIM), lambda i, r: (i, 0, r, 0))),
        compiler_params=_cparams("parallel", "parallel"),
        name="attn_prep",
    )(pqkv, cos2, sin2, qg, kg, _head_ones(ATTN_WIDTH), _head_ones(ATTN_KV_WIDTH))


def _attn_kernel(*refs, nq, nk):
    q_refs, (kt_ref, v_ref, o_ref) = refs[:nq], refs[nq:]
    outs = []
    for h in range(ATTN_HEADS):
        kv = h // ATTN_GROUP
        hs = slice(h * HEAD_DIM, (h + 1) * HEAD_DIM)
        qh = jnp.concatenate([q_ref[0, :, hs] for q_ref in q_refs], axis=0)
        s = _dot(qh, kt_ref[0, kv * HEAD_DIM:(kv + 1) * HEAD_DIM, :nk])
        m = jnp.max(s, axis=-1, keepdims=True)
        p = jnp.exp2(s - m)
        o = _dot(p.astype(BF16), v_ref[0, kv, :nk, :])
        outs.append(o[:, :HEAD_DIM] / o[:, HEAD_DIM:])
    o_ref[0] = jnp.concatenate(outs, axis=-1).astype(o_ref.dtype)


def _attention(qn, kt, vx, cl, rows, nk):
    b, lt, _ = qn.shape
    blk = 256 if (cl % 256 == 0 and lt % 256 == 0) else 128
    if rows == 'ctx':
        n_rows, first, nq = cl, 0, 1
    else:
        n_rows, first = lt - cl, cl // blk
        nq = ATTN_Q_BLOCKS if n_rows % (ATTN_Q_BLOCKS * blk) == 0 else 1
    tq = nq * blk
    q_specs = [pl.BlockSpec((1, blk, ATTN_WIDTH), lambda i, t, j=j: (i, first + nq * t + j, 0)) for j in range(nq)]
    return pl.pallas_call(
        functools.partial(_attn_kernel, nq=nq, nk=nk),
        out_shape=jax.ShapeDtypeStruct((b, n_rows, ATTN_WIDTH), BF16),
        grid=(b, n_rows // tq),
        in_specs=q_specs + [pl.BlockSpec((1, ATTN_KV_WIDTH, lt), lambda i, t: (i, 0, 0)),
                            pl.BlockSpec((1, ATTN_KV_HEADS, lt, 2 * HEAD_DIM), lambda i, t: (i, 0, 0, 0))],
        out_specs=pl.BlockSpec((1, tq, ATTN_WIDTH), lambda i, t: (i, t, 0)),
        compiler_params=_cparams("parallel", "parallel"),
        name="attention_" + rows,
    )(*([qn] * nq), kt, vx)


def _halo_specs(tr, lt, width, lead):
    per = tr // SUBLANES
    last = lt // SUBLANES - 1
    nlead = len(lead)

    def prev_map(*ids):
        return (*ids[:nlead], jnp.maximum(ids[nlead] * per - 1, 0), 0)

    def next_map(*ids):
        return (*ids[:nlead], jnp.minimum((ids[nlead] + 1) * per, last), 0)

    blk = (*lead, SUBLANES, width)
    return pl.BlockSpec(blk, prev_map), pl.BlockSpec(blk, next_map)


def _shift_rows(x, halo, offset, rows, cl, lt):
    tr = x.shape[0]
    local = lax.broadcasted_iota(jnp.int32, (tr, 1), 0)
    if offset < 0:
        y = pltpu.roll(x, -offset, 0)
        y = jnp.where(local == 0, halo, y)
        bad = (rows == 0) | (rows == cl)
    else:
        y = pltpu.roll(x, tr - offset, 0)
        for j in range(offset):
            y = jnp.where(local == tr - offset + j, halo[j:j + 1, :], y)
        bad = (rows >= lt - offset) | ((rows >= cl - offset) & (rows < cl))
    return jnp.where(bad, 0.0, y)


def _hyena_pre_kernel(p_ref, pv_ref, nx_ref, w_ref, b_ref, v_ref, x1_ref, x2_ref, *, cl, lt, tr):
    rows = _row_ids(tr, pl.program_id(1))
    p = p_ref[0].astype(F32)
    pm = _shift_rows(p, pv_ref[0, SUBLANES - 1:, :].astype(F32), -1, rows, cl, lt)
    pp = _shift_rows(p, nx_ref[0, :1, :].astype(F32), 1, rows, cl, lt)
    z = pm * w_ref[0:1, :] + p * w_ref[1:2, :] + pp * w_ref[2:3, :] + b_ref[...]
    c = HYENA_WIDTH
    v_ref[0] = z[:, :c].astype(v_ref.dtype)
    x1_ref[0] = z[:, c:2 * c].astype(x1_ref.dtype)
    x2_ref[0] = z[:, 2 * c:].astype(x2_ref.dtype)


def _hyena_pre(phy, conv_w, conv_b, cl):
    b, lt, w = phy.shape
    tr = _largest_tile(lt, 1088, 16)
    prev_spec, next_spec = _halo_specs(tr, lt, w, (1,))
    out = jax.ShapeDtypeStruct((b, lt, HYENA_WIDTH), BF16)
    ospec = pl.BlockSpec((1, tr, HYENA_WIDTH), lambda i, r: (i, r, 0))
    return pl.pallas_call(
        functools.partial(_hyena_pre_kernel, cl=cl, lt=lt, tr=tr),
        out_shape=(out, out, out),
        grid=(b, lt // tr),
        in_specs=[pl.BlockSpec((1, tr, w), lambda i, r: (i, r, 0)), prev_spec, next_spec,
                  pl.BlockSpec(conv_w.shape, lambda i, r: (0, 0)),
                  pl.BlockSpec((1, w), lambda i, r: (0, 0))],
        out_specs=(ospec, ospec, ospec),
        compiler_params=_cparams("parallel", "parallel"),
        name="hyena_pre",
    )(phy, phy, phy, conv_w, conv_b.reshape(1, w))


def _trig_mats_kernel(cb_ref, sb_ref, co_ref, so_ref, c_ref, s_ref):
    cb, sb, co, so = cb_ref[0], sb_ref[0], co_ref[...], so_ref[...]
    c_ref[...] = (cb * co - sb * so).astype(c_ref.dtype)
    s_ref[...] = (sb * co + cb * so).astype(s_ref.dtype)


def _trig_mats(n, h, row_mult, col_mult):
    tm = _largest_tile(h, 256, 16)
    tn = _largest_tile(h, 1024, LANES) if h % LANES == 0 else h
    b = col_mult(jnp.arange(h, dtype=jnp.int32))[None, :]

    def tables(a):
        ang = ((a[:, None] * b) % (2 * n)).astype(F32) * (math.pi / n)
        return jnp.cos(ang), jnp.sin(ang)

    r0 = jnp.arange(0, h, tm, dtype=jnp.int32)
    cb, sb = tables(row_mult(r0))
    co, so = tables(row_mult(jnp.arange(tm, dtype=jnp.int32)) - row_mult(jnp.zeros((tm,), jnp.int32)))
    base = pl.BlockSpec((1, 1, tn), lambda j, c: (j, 0, c))
    off = pl.BlockSpec((tm, tn), lambda j, c: (0, c))
    out = jax.ShapeDtypeStruct((h, h), BF16)
    ospec = pl.BlockSpec((tm, tn), lambda j, c: (j, c))
    return pl.pallas_call(
        _trig_mats_kernel, out_shape=(out, out), grid=(h // tm, h // tn),
        in_specs=[base, base, off, off], out_specs=(ospec, ospec),
        compiler_params=_cparams("parallel", "parallel"), name="trig_mats",
    )(cb.reshape(h // tm, 1, h), sb.reshape(h // tm, 1, h), co, so)


def _dft_mats(n):
    h = n // 2
    ident = lambda r: r
    ce, se = _trig_mats(n, h, ident, lambda j: 2 * j)
    co, so = _trig_mats(n, h, ident, lambda j: 2 * j + 1)
    cot, sot = _trig_mats(n, h, lambda j: 2 * j + 1, ident)
    return ce, se, co, so, cot, sot


def _alt_sum(z):
    j = lax.broadcasted_iota(jnp.int32, (z.shape[0], 1), 0)
    return jnp.sum(z * (1 - 2 * (j % 2)).astype(F32), axis=0, keepdims=True)


def _half_spectra(ce_ref, se_ref, co_ref, so_ref, z_ev, z_od):
    p, q = _dot(ce_ref[...], z_ev), _dot(co_ref[...], z_od)
    ps, qs = _dot(se_ref[...], z_ev), _dot(so_ref[...], z_od)
    return p + q, ps + qs, p - q, qs - ps


def _dft_raw_kernel(ce_ref, se_ref, co_ref, so_ref, z_ref, rl_ref, il_ref, ru_ref, iu_ref, mid_ref):
    z_ev, z_od = z_ref[0], z_ref[1]
    rl_ref[...], il_ref[...], ru_ref[...], iu_ref[...] = _half_spectra(ce_ref, se_ref, co_ref, so_ref, z_ev, z_od)

    @pl.when(pl.program_id(0) == 0)
    def _():
        mid = jnp.concatenate([_alt_sum(z_ev.astype(F32)), _alt_sum(z_od.astype(F32))], axis=0)
        mid_ref[...] = jnp.concatenate([mid, jnp.zeros((SUBLANES - 2, mid.shape[1]), F32)], axis=0)


def _dft_raw(mats, zs):
    ce, se, co, so, _, _ = mats
    _, h, c = zs.shape
    tm = _largest_tile(h, 256, 16)
    mat = pl.BlockSpec((tm, h), lambda j: (j, 0))
    out = jax.ShapeDtypeStruct((h, c), F32)
    ospec = pl.BlockSpec((tm, c), lambda j: (j, 0))
    return pl.pallas_call(
        _dft_raw_kernel, out_shape=(out, out, out, out, jax.ShapeDtypeStruct((SUBLANES, c), F32)), grid=(h // tm,),
        in_specs=[mat, mat, mat, mat, pl.BlockSpec((2, h, c), lambda j: (0, 0, 0))],
        out_specs=(ospec, ospec, ospec, ospec, pl.BlockSpec((SUBLANES, c), lambda j: (0, 0))),
        compiler_params=_cparams("arbitrary"), name="dft_raw",
    )(ce, se, co, so, zs)


def _dft_fwd_kernel(ce_ref, se_ref, co_ref, so_ref, z_ref, hrl_ref, hil_ref, hru_ref, hiu_ref, hmid_ref,
                    ea_ref, eb_ref, oa_ref, ob_ref, mid_ref, *, bb):
    hrl, hil, hru, hiu = hrl_ref[...], hil_ref[...], hru_ref[...], hiu_ref[...]
    for i in range(bb):
        z_ev, z_od = z_ref[i, 0], z_ref[i, 1]
        zrl, zil, zru, ziu = _half_spectra(ce_ref, se_ref, co_ref, so_ref, z_ev, z_od)
        yrl, yil = zrl * hrl + zil * hil, zil * hrl - zrl * hil
        yru, yiu = zru * hru + ziu * hiu, ziu * hru - zru * hiu
        ea_ref[i] = (yrl + yru).astype(ea_ref.dtype)
        eb_ref[i] = (yil - yiu).astype(eb_ref.dtype)
        oa_ref[i] = (yrl - yru).astype(oa_ref.dtype)
        ob_ref[i] = (yil + yiu).astype(ob_ref.dtype)

    @pl.when(pl.program_id(1) == 0)
    def _():
        hr, hi = hmid_ref[0:1, :], hmid_ref[1:2, :]
        for i in range(bb):
            zr, zi = _alt_sum(z_ref[i, 0].astype(F32)), _alt_sum(z_ref[i, 1].astype(F32))
            mid = jnp.concatenate([zr * hr + zi * hi, zi * hr - zr * hi], axis=0)
            mid_ref[i] = jnp.concatenate([mid, jnp.zeros((SUBLANES - 2, mid.shape[1]), F32)], axis=0)


def _dft_inv_kernel(ce_ref, se_ref, cot_ref, sot_ref, ea_ref, eb_ref, oa_ref, ob_ref, mid_ref, z_ref, g_ref, skip_ref,
                    o_ref, *, bb, tm):
    alt = (1 - 2 * (_row_ids(tm, pl.program_id(1)) % 2)).astype(F32)
    for i in range(bb):
        y_ev = _dot(ce_ref[...], ea_ref[i]) + _dot(se_ref[...], eb_ref[i]) + alt * mid_ref[i, 0:1, :]
        y_od = _dot(cot_ref[...], oa_ref[i]) + _dot(sot_ref[...], ob_ref[i]) + alt * mid_ref[i, 1:2, :]
        for par, y in enumerate((y_ev, y_od)):
            y = y + z_ref[i, par].astype(F32) * skip_ref[...]
            o_ref[i, par] = (g_ref[i, par].astype(F32) * y).astype(o_ref.dtype)


def _longconv_gated(zs, gates, spectrum, skip, mats):
    ce, se, co, so, cot, sot = mats
    b, _, h, c = zs.shape
    bb = 2 if b % 2 == 0 else 1
    tm = _largest_tile(h, 256, 16)
    mat = pl.BlockSpec((tm, h), lambda i, j: (j, 0))
    full4 = pl.BlockSpec((bb, 2, h, c), lambda i, j: (i, 0, 0, 0))
    full = pl.BlockSpec((bb, h, c), lambda i, j: (i, 0, 0))
    tile = pl.BlockSpec((bb, tm, c), lambda i, j: (i, j, 0))
    tile4 = pl.BlockSpec((bb, 2, tm, c), lambda i, j: (i, 0, j, 0))
    filt = pl.BlockSpec((tm, c), lambda i, j: (j, 0))
    mid_spec = pl.BlockSpec((bb, SUBLANES, c), lambda i, j: (i, 0, 0))
    half = jax.ShapeDtypeStruct((b, h, c), BF16)
    *combos, mid = pl.pallas_call(
        functools.partial(_dft_fwd_kernel, bb=bb),
        out_shape=(half, half, half, half, jax.ShapeDtypeStruct((b, SUBLANES, c), F32)), grid=(b // bb, h // tm),
        in_specs=[mat, mat, mat, mat, full4, filt, filt, filt, filt, pl.BlockSpec((SUBLANES, c), lambda i, j: (0, 0))],
        out_specs=(tile, tile, tile, tile, mid_spec),
        compiler_params=_cparams("parallel", "arbitrary"), name="dft_fwd",
    )(ce, se, co, so, zs, *spectrum)
    return pl.pallas_call(
        functools.partial(_dft_inv_kernel, bb=bb, tm=tm),
        out_shape=jax.ShapeDtypeStruct((b, 2, h, c), BF16), grid=(b // bb, h // tm),
        in_specs=[mat, mat, mat, mat, full, full, full, full, mid_spec, tile4, tile4,
                  pl.BlockSpec((1, c), lambda i, j: (0, 0))],
        out_specs=tile4,
        compiler_params=_cparams("parallel", "parallel"), name="dft_inv",
    )(ce, se, cot, sot, *combos, mid, zs, gates, skip.reshape(1, c))


def _hyena_filters(n, f1, fb1, f2, fb2, f3):
    t = jnp.arange(n, dtype=F32) / n
    bands = jnp.arange(1, HYENA_BANDS + 1, dtype=F32)
    ang = 2.0 * math.pi * t[:, None] * bands
    feat = jnp.concatenate([t[:, None], jnp.sin(ang), jnp.cos(ang)], axis=-1)
    hp = lax.Precision.HIGHEST
    h = jnp.sin(jnp.dot(feat, f1, precision=hp) + fb1)
    h = jnp.sin(jnp.dot(h, f2, precision=hp) + fb2)
    h = jnp.dot(h, f3, precision=hp).reshape(n, HYENA_ORDER, 2, HYENA_WIDTH)
    deltas = jnp.linspace(-math.log(HYENA_DECAY_TARGET) / HYENA_SLOW_DECAY,
                          -math.log(HYENA_DECAY_TARGET) / HYENA_FAST_DECAY, HYENA_WIDTH, dtype=F32)
    h = h * jnp.exp(-t[:, None] * deltas)[:, None, None, :]
    return h / jnp.sum(jnp.abs(h), axis=(0, 2), keepdims=True)


def _parity_split(t):
    *lead, n, c = t.shape
    return jnp.swapaxes(t.reshape(*lead, n // 2, 2, c), -2, -3)


def _parity_merge(t):
    *lead, _, h, c = t.shape
    return jnp.swapaxes(t, -2, -3).reshape(*lead, 2 * h, c)


def _hyena_spectra(n, filt_params, mats):
    h = _hyena_filters(n, *filt_params)
    oc = HYENA_ORDER * HYENA_WIDTH
    hf = h[:, :, 0].reshape(n, oc)
    hb = h[:, :, 1].reshape(n, oc)
    hb = jnp.where(jnp.arange(n)[:, None] == 0, 0.0, hb)
    sig = _parity_split(jnp.concatenate([hf + hb, hb - hf], axis=1).astype(BF16))
    rl, il, ru, iu, mid = _dft_raw(mats, sig)
    scale = 1.0 / n
    ends = jnp.where(jnp.arange(n // 2)[:, None] == 0, 0.5 * scale, scale)
    spectra = []
    for o in range(HYENA_ORDER):
        re = slice(o * HYENA_WIDTH, (o + 1) * HYENA_WIDTH)
        im = slice(oc + o * HYENA_WIDTH, oc + (o + 1) * HYENA_WIDTH)
        hmid = jnp.zeros((SUBLANES, HYENA_WIDTH), F32).at[0].set(mid[0, re] * scale).at[1].set(mid[1, im] * scale)
        spectra.append((rl[:, re] * ends, il[:, im] * scale, ru[:, re] * ends, iu[:, im] * scale, hmid))
    return spectra


def _hyena_run(v, x1, x2, spectra, skip, mats):
    y = _parity_split(v)
    for o, gate in enumerate((x1, x2)):
        y = _longconv_gated(y, _parity_split(gate), spectra[o], skip[o], mats)
    return _parity_merge(y)


def _blockdiag_dense(w):
    nb, blk = w.shape[1], w.shape[2]
    eye = jnp.eye(nb, dtype=w.dtype)
    return jnp.einsum('dncf,nm->dncmf', w, eye).reshape(w.shape[0], nb * blk, nb * blk)


def _lru_pre_kernel(p_ref, pv_ref, nx_ref, cw_ref, cb_ref, wa_ref, ba_ref, wx_ref, bx_ref, lam_ref,
                    a_ref, b_ref, *, cl, lt, tr):
    rows = _row_ids(tr, pl.program_id(1))
    c = LRU_WIDTH
    x = p_ref[0][:, c:].astype(F32)
    pv = pv_ref[0][SUBLANES - 1:, c:].astype(F32)
    nx = nx_ref[0][:, c:].astype(F32)
    xc = (_shift_rows(x, pv, -1, rows, cl, lt) * cw_ref[0:1, :] + x * cw_ref[1:2, :]
          + _shift_rows(x, nx[:1], 1, rows, cl, lt) * cw_ref[2:3, :]
          + _shift_rows(x, nx[:2], 2, rows, cl, lt) * cw_ref[3:4, :] + cb_ref[...])
    xcb = xc.astype(BF16)
    for d in range(2):
        r = _sigmoid(_dot(xcb, wa_ref[d]) + ba_ref[d])
        i = _sigmoid(_dot(xcb, wx_ref[d]) + bx_ref[d])
        log_a = -LRU_C * r * _softplus(-lam_ref[d])
        a_ref[d, 0] = jnp.exp(log_a)
        b_ref[d, 0] = jnp.sqrt(1.0 - jnp.exp(2.0 * log_a)) * (i * xc)


def _lru_pre(plr, conv_w, conv_b, wa, ba, wx, bx, lam, cl):
    b, lt, w = plr.shape
    c = LRU_WIDTH
    tr = _largest_tile(lt, 1088, 16)
    prev_spec, next_spec = _halo_specs(tr, lt, w, (1,))
    out = jax.ShapeDtypeStruct((2, b, lt, c), F32)
    ospec = pl.BlockSpec((2, 1, tr, c), lambda i, r: (0, i, r, 0))
    const2 = lambda shape: pl.BlockSpec(shape, lambda i, r: (0,) * len(shape))
    return pl.pallas_call(
        functools.partial(_lru_pre_kernel, cl=cl, lt=lt, tr=tr),
        out_shape=(out, out), grid=(b, lt // tr),
        in_specs=[pl.BlockSpec((1, tr, w), lambda i, r: (i, r, 0)), prev_spec, next_spec,
                  const2(conv_w.shape), const2((1, c)), const2((2, c, c)), const2((2, 1, c)),
                  const2((2, c, c)), const2((2, 1, c)), const2((2, 1, c))],
        out_specs=(ospec, ospec),
        compiler_params=_cparams("parallel", "parallel"), name="lru_pre",
    )(plr, plr, plr, conv_w, conv_b.reshape(1, c), _blockdiag_dense(wa).astype(BF16), ba.reshape(2, 1, c),
      _blockdiag_dense(wx).astype(BF16), bx.reshape(2, 1, c), lam.reshape(2, 1, c))


def _gelu_tanh(x):
    return 0.5 * x * (1.0 + jnp.tanh(math.sqrt(2.0 / math.pi) * (x + 0.044715 * (x * x * x))))


def _lru_scan_kernel(af_ref, bf_ref, ar_ref, br_ref, g_ref, o_ref, acc_ref, *, cl, lt):
    row = lax.broadcasted_iota(jnp.int32, (SUBLANES, LANES), 0)

    def group_scan(a, b, reverse):
        for s in (1, 2, 4):
            if reverse:
                keep = row < SUBLANES - s
                a_s = jnp.where(keep, pltpu.roll(a, SUBLANES - s, 0), 1.0)
                b_s = jnp.where(keep, pltpu.roll(b, SUBLANES - s, 0), 0.0)
            else:
                keep = row >= s
                a_s = jnp.where(keep, pltpu.roll(a, s, 0), 1.0)
                b_s = jnp.where(keep, pltpu.roll(b, s, 0), 0.0)
            b = a * b_s + b
            a = a * a_s
        return a, b

    ng = math.gcd(math.gcd(cl // SUBLANES, (lt - cl) // SUBLANES), LRU_GROUPS_PER_STEP)
    span = ng * SUBLANES

    def fwd_body(i, h):
        sl = pl.ds(pl.multiple_of(i * span, span), span)
        a_all, b_all = af_ref[0, 0, sl, :], bf_ref[0, 0, sl, :]
        scans = [group_scan(a_all[j * SUBLANES:(j + 1) * SUBLANES], b_all[j * SUBLANES:(j + 1) * SUBLANES], False)
                 for j in range(ng)]
        outs = []
        for a, b in scans:
            hh = a * h + b
            outs.append(hh)
            h = hh[SUBLANES - 1:SUBLANES, :]
        acc_ref[sl, :] = jnp.concatenate(outs, axis=0)
        return h

    lax.fori_loop(0, lt // span, fwd_body, jnp.zeros((1, LANES), F32))

    def rev_body(i, h, top):
        sl = pl.ds(pl.multiple_of((top - 1 - i) * span, span), span)
        a_all, b_all = ar_ref[0, 0, sl, :], br_ref[0, 0, sl, :]
        scans = [group_scan(a_all[j * SUBLANES:(j + 1) * SUBLANES], b_all[j * SUBLANES:(j + 1) * SUBLANES], True)
                 for j in range(ng)]
        outs = [None] * ng
        for j in reversed(range(ng)):
            a, b = scans[j]
            hh = a * h + b
            outs[j] = hh
            h = hh[0:1, :]
        gate = g_ref[0, sl, :].astype(F32)
        o_ref[0, sl, :] = ((acc_ref[sl, :] + jnp.concatenate(outs, axis=0)) * _gelu_tanh(gate)).astype(o_ref.dtype)
        return h

    h = lax.fori_loop(0, cl // span, functools.partial(rev_body, top=cl // span), jnp.zeros((1, LANES), F32))
    lax.fori_loop(0, (lt - cl) // span, functools.partial(rev_body, top=lt // span), h)


def _lru_scan(a, b_, plr, cl):
    _, b, lt, c = a.shape
    nl = c // LANES
    fwd = pl.BlockSpec((1, 1, lt, LANES), lambda i, j: (0, i, 0, j))
    rev = pl.BlockSpec((1, 1, lt, LANES), lambda i, j: (1, i, 0, j))
    return pl.pallas_call(
        functools.partial(_lru_scan_kernel, cl=cl, lt=lt),
        out_shape=jax.ShapeDtypeStruct((b, lt, c), BF16), grid=(b, nl),
        in_specs=[fwd, fwd, rev, rev, pl.BlockSpec((1, lt, LANES), lambda i, j: (i, 0, j))],
        out_specs=pl.BlockSpec((1, lt, LANES), lambda i, j: (i, 0, j)),
        scratch_shapes=[pltpu.VMEM((lt, LANES), F32)],
        compiler_params=_cparams("parallel", "parallel"), name="lru_scan",
    )(a, b_, a, b_, plr)


def _row_perm(tr, rev):
    t = lax.broadcasted_iota(jnp.int32, (tr, tr), 0)
    s = lax.broadcasted_iota(jnp.int32, (tr, tr), 1)
    return jnp.where(s == jnp.where(rev, tr - 1 - t, t), 1.0, 0.0).astype(BF16)


def _mirror_tile(r, rev, n_ctx_tiles, n_tiles):
    m = jnp.where(r < n_ctx_tiles, n_ctx_tiles - 1 - r, n_tiles - 1 + n_ctx_tiles - r)
    return jnp.where(rev, m, r)


def _chunk_masks(tr):
    t = lax.broadcasted_iota(jnp.int32, (tr, tr), 0)
    s = lax.broadcasted_iota(jnp.int32, (tr, tr), 1)
    same = (t // RWKV_CHUNK) == (s // RWKV_CHUNK)
    return same, same & (s <= t), same & (s < t)


def _rwkv_prep_kernel(p_ref, pv_ref, nx_ref, mu_ref, w0_ref, w2_ref, a0_ref, a2_ref, g2_ref, kk_ref, ka_ref,
                      rk_ref, ones_ref,
                      aq_ref, vp_ref, y0_ref, rt_ref, mrb_ref, bht_ref, gm_ref, pc_ref, g_ref, bonus_ref,
                      *, cl, lt, tr):
    c = RWKV_WIDTH
    rows = _row_ids(tr, pl.program_id(2))
    rev = pl.program_id(0) == 1
    p = _dot(_row_perm(tr, rev), p_ref[0])
    before = jnp.where(rev, nx_ref[0, :1, :], pv_ref[0, SUBLANES - 1:, :]).astype(F32)
    after = jnp.where(rev, pv_ref[0, SUBLANES - 1:, :], nx_ref[0, :1, :]).astype(F32)
    prev = _shift_rows(p, before, -1, rows, cl, lt)
    nxt = _shift_rows(p, after, 1, rows, cl, lt)
    xm = p + (prev - p) * mu_ref[0, 0:1, :] + (nxt - p) * mu_ref[0, 1:2, :]
    r, k, v = xm[:, :c], xm[:, c:2 * c], xm[:, 2 * c:3 * c]
    o = 3 * c
    w1 = xm[:, o:o + 2 * RWKV_DECAY_RANK]
    a1 = xm[:, o + 2 * RWKV_DECAY_RANK:o + 2 * RWKV_DECAY_RANK + 2 * RWKV_ICLR_RANK]
    g1 = xm[:, o + 2 * RWKV_DECAY_RANK + 2 * RWKV_ICLR_RANK:]
    wlog = -_softplus(-(w0_ref[0] + _dot(jnp.tanh(w1).astype(BF16), w2_ref[0]))) - 0.5
    ld = -jnp.exp(wlog)
    a = _sigmoid(a0_ref[0] + _dot(a1.astype(BF16), a2_ref[0]))
    g_ref[0, 0] = _dot(_sigmoid(g1).astype(BF16), g2_ref[...])
    kk = k * kk_ref[...]
    kk = kk * lax.rsqrt(_dot_exact_rhs(kk * kk, ones_ref[...]) + 1e-12)
    kd = k * (1.0 + (a - 1.0) * ka_ref[...])
    bonus_ref[0, 0] = _dot_exact_rhs(r * kd * rk_ref[...], ones_ref[...]) * v
    beta = kk * a

    _rwkv_chunk_stage(ld, kk, r, kd, beta, v, aq_ref, vp_ref, y0_ref, rt_ref, mrb_ref, bht_ref, gm_ref, pc_ref)


def _rwkv_chunk_stage(ld, kk, r, kd, beta, v, aq_ref, vp_ref, y0_ref, rt_ref, mrb_ref, bht_ref, gm_ref, pc_ref):
    tr = ld.shape[0]
    n = min(RWKV_STAGE_ROWS, tr)
    nparts = tr // n
    hd = HEAD_DIM
    ch = RWKV_CHUNK
    nch = n // ch
    same_t, incl_t, _ = _chunk_masks(tr)
    cum = _dot_exact_lhs(jnp.where(incl_t, 1.0, 0.0).astype(BF16), ld)
    tot = _dot_exact_lhs(jnp.where(same_t, 1.0, 0.0).astype(BF16), ld)
    _, incl, strict = _chunk_masks(n)
    alpha_t = kk * jnp.exp(cum - ld)
    r_t = r * jnp.exp(cum)
    e_neg = jnp.exp(-cum)
    k_t = kd * e_neg
    b_t = beta * e_neg
    e_rem = jnp.exp(tot - cum)
    k_hat_t = (kd * e_rem).T
    b_hat_t = (beta * e_rem).T
    pc_t = jnp.exp(tot).T
    t_i = lax.broadcasted_iota(jnp.int32, (n, n), 0)
    s_i = lax.broadcasted_iota(jnp.int32, (n, n), 1)
    eye_f = jnp.where(t_i == s_i, 1.0, 0.0)
    same_blk = []
    size = RWKV_INV_BASE
    while size <= ch:
        same_blk.append((t_i // size) == (s_i // size))
        size *= 2
    col_chunk = lax.broadcasted_iota(jnp.int32, (n, nch * hd), 1) // hd
    row_chunk = lax.broadcasted_iota(jnp.int32, (n, nch * hd), 0) // ch

    def diag_blocks(m):
        out = m[:, :ch]
        for j in range(1, nch):
            out = out + m[:, j * ch:(j + 1) * ch]
        return out

    chains = [(p, h) for p in range(nparts) for h in range(RWKV_HEADS)]

    def blk(t, p, h):
        return t[p * n:(p + 1) * n, h * hd:(h + 1) * hd]

    prods = [_dot_nt(jnp.concatenate([blk(alpha_t, p, h), blk(r_t, p, h)], axis=0).astype(BF16),
                     jnp.concatenate([blk(b_t, p, h), blk(k_t, p, h)], axis=0).astype(BF16))
             for p, h in chains]
    l_ab = [jnp.where(strict, pr[:n, :n], 0.0) for pr in prods]
    pw = [jnp.where(same_blk[0], l, 0.0) for l in l_ab]
    t_inv = [eye_f - l for l in pw]
    for _ in range(int(math.log2(RWKV_INV_BASE)) - 1):
        pw = [_dot_bf(q, q) for q in pw]
        t_inv = [t + _dot_bf(t, q) for t, q in zip(t_inv, pw)]
    for lvl in range(1, len(same_blk)):
        off = same_blk[lvl] & jnp.logical_not(same_blk[lvl - 1])
        half = [_dot_bf(t, jnp.where(off, l, 0.0)) for t, l in zip(t_inv, l_ab)]
        t_inv = [t - _dot_bf(hf, t) for t, hf in zip(t_inv, half)]
    vh = [blk(v, p, h).astype(BF16) for p, h in chains]
    lakv = [_dot(jnp.where(strict, pr[:n, n:], 0.0).astype(BF16), vv) for pr, vv in zip(prods, vh)]
    x = [_dot(t.astype(BF16), jnp.concatenate([blk(alpha_t, p, h), lv], axis=1).astype(BF16))
         for t, (p, h), lv in zip(t_inv, chains, lakv)]
    y0 = [_dot(jnp.where(incl, pr[n:, n:], 0.0).astype(BF16), vv) for pr, vv in zip(prods, vh)]
    mrb = [diag_blocks(jnp.where(incl, pr[n:, :n], 0.0)) for pr in prods]
    gms = [_dot(k_hat_t[h * hd:(h + 1) * hd, p * n:(p + 1) * n].astype(BF16),
                jnp.where(col_chunk == row_chunk, jnp.concatenate([blk(v, p, h)] * nch, axis=1), 0.0).astype(BF16))
           for p, h in chains]

    def assemble(parts):
        return jnp.concatenate([jnp.concatenate(parts[p * RWKV_HEADS:(p + 1) * RWKV_HEADS], axis=1)
                                for p in range(nparts)], axis=0)

    aq_ref[0, 0] = assemble([t[:, :hd] for t in x]).astype(aq_ref.dtype)
    vp_ref[0, 0] = assemble([t[:, hd:] for t in x])
    y0_ref[0, 0] = assemble(y0)
    rt_ref[0, 0] = r_t.astype(rt_ref.dtype)
    mrb_ref[0, 0] = assemble(mrb).astype(mrb_ref.dtype)
    for p in range(nparts):
        for j in range(nch):
            jj = p * nch + j
            cs = slice(jj * ch, (jj + 1) * ch)
            bht_ref[0, 0, jj] = jnp.concatenate([b_hat_t[h * hd:(h + 1) * hd, cs] for h in range(RWKV_HEADS)],
                                                axis=1).astype(bht_ref.dtype)
            gm_ref[0, 0, jj] = jnp.concatenate([gms[p * RWKV_HEADS + h][:, j * hd:(j + 1) * hd]
                                                for h in range(RWKV_HEADS)], axis=1)
            pc_ref[0, 0, jj] = jnp.concatenate([pc_t[h * hd:(h + 1) * hd, cs] for h in range(RWKV_HEADS)], axis=1)


def _rwkv_tile(lt, cl):
    tr = 256 if (lt % 256 == 0 and cl % 256 == 0) else 128
    assert lt % tr == 0 and cl % tr == 0
    return tr


def _rwkv_prep(prw, mud, w0, w2p, a0, a2p, g2, k_k, k_a, r_k, cl):
    b, lt, w = prw.shape
    c = RWKV_WIDTH
    tr = _rwkv_tile(lt, cl)
    nch = tr // RWKV_CHUNK
    per = tr // SUBLANES
    src = lambda d, r: _mirror_tile(r, d == 1, cl // tr, lt // tr)
    prev_spec = pl.BlockSpec((1, SUBLANES, w), lambda d, i, r: (i, jnp.maximum(src(d, r) * per - 1, 0), 0))
    next_spec = pl.BlockSpec((1, SUBLANES, w),
                             lambda d, i, r: (i, jnp.minimum((src(d, r) + 1) * per, lt // SUBLANES - 1), 0))
    tile = pl.BlockSpec((1, 1, tr, c), lambda d, i, r: (d, i, r, 0))
    per_dir = lambda shape: pl.BlockSpec((1, *shape), lambda d, i, r: (d,) + (0,) * len(shape))
    const = lambda shape: pl.BlockSpec(shape, lambda d, i, r: (0,) * len(shape))
    seq = lambda dt: jax.ShapeDtypeStruct((2, b, lt, c), dt)
    chunked = jax.ShapeDtypeStruct((2, b, lt // RWKV_CHUNK, HEAD_DIM, c), F32)
    chunk_spec = pl.BlockSpec((1, 1, nch, HEAD_DIM, c), lambda d, i, r: (d, i, r, 0, 0))
    return pl.pallas_call(
        functools.partial(_rwkv_prep_kernel, cl=cl, lt=lt, tr=tr),
        out_shape=(seq(BF16), seq(F32), seq(F32), seq(BF16), seq(BF16),
                   jax.ShapeDtypeStruct(chunked.shape, BF16), chunked, chunked, seq(F32), seq(F32)),
        grid=(2, b, lt // tr),
        in_specs=[pl.BlockSpec((1, tr, w), lambda d, i, r: (i, src(d, r), 0)), prev_spec, next_spec,
                  per_dir((2, w)), per_dir((1, c)), per_dir((2 * RWKV_DECAY_RANK, c)), per_dir((1, c)),
                  per_dir((2 * RWKV_ICLR_RANK, c)), const((RWKV_GATE_RANK, c)), const((1, c)), const((1, c)),
                  const((1, c)), const((c, c))],
        out_specs=(tile, tile, tile, tile, tile, chunk_spec, chunk_spec, chunk_spec, tile, tile),
        compiler_params=_cparams("parallel", "parallel", "parallel"), name="rwkv_prep",
    )(prw, prw, prw, mud, w0.reshape(2, 1, c), w2p, a0.reshape(2, 1, c), a2p, g2.astype(BF16), k_k.reshape(1, c),
      k_a.reshape(1, c), r_k.reshape(1, c), _head_ones(c))


def _rwkv_scan_kernel(aq_ref, vp_ref, y0_ref, rt_ref, mrb_ref, bht_ref, gm_ref, pc_ref, y_ref, s_ref, *, ns):
    @pl.when(pl.program_id(1) == 0)
    def _():
        s_ref[...] = jnp.zeros_like(s_ref)

    c = RWKV_WIDTH
    ch = RWKV_CHUNK
    same_head = (lax.broadcasted_iota(jnp.int32, (c, c), 0) // HEAD_DIM
                 == lax.broadcasted_iota(jnp.int32, (c, c), 1) // HEAD_DIM)

    def head_blockdiag(t):
        return jnp.where(same_head, jnp.concatenate([t] * RWKV_HEADS, axis=0), 0.0).astype(BF16)

    for i in range(ns):
        s0 = s_ref[i]
        r1 = _dot(jnp.concatenate([aq_ref[i, 0], rt_ref[i, 0]], axis=0), head_blockdiag(s0))
        u = r1[:ch] + vp_ref[i, 0]
        r2 = _dot(jnp.concatenate([mrb_ref[i, 0], bht_ref[i, 0, 0]], axis=0), head_blockdiag(u))
        y_ref[i, 0] = r1[ch:] + y0_ref[i, 0] - r2[:ch]
        s_ref[i] = pc_ref[i, 0, 0] * s0 + gm_ref[i, 0, 0] - r2[ch:]


def _rwkv_scan(aq, vp, y0, rt, mrb, bht, gm, pc):
    _, b, lt, c = aq.shape
    nstream = 2 * b
    ns = nstream
    ch = RWKV_CHUNK
    merge = lambda t: t.reshape(nstream, 1, *t.shape[2:])
    tile = pl.BlockSpec((ns, 1, ch, c), lambda s, j: (s, 0, j, 0))
    chunk_spec = pl.BlockSpec((ns, 1, 1, HEAD_DIM, c), lambda s, j: (s, 0, j, 0, 0))
    y = pl.pallas_call(
        functools.partial(_rwkv_scan_kernel, ns=ns),
        out_shape=jax.ShapeDtypeStruct((nstream, 1, lt, c), F32), grid=(nstream // ns, lt // ch),
        in_specs=[tile, tile, tile, tile, tile, chunk_spec, chunk_spec, chunk_spec],
        out_specs=tile,
        scratch_shapes=[pltpu.VMEM((ns, HEAD_DIM, c), F32)],
        compiler_params=_cparams("parallel", "arbitrary"), name="rwkv_scan",
    )(*(merge(t) for t in (aq, vp, y0, rt, mrb, bht, gm, pc)))
    return y.reshape(2, b, lt, c)


def _rwkv_readout_kernel(yf_ref, yr_ref, bf_ref, br_ref, g_ref, lw_ref, lb_ref, ones_ref, o_ref, *, tr):
    unflip = _row_perm(tr, True)
    y = yf_ref[0, 0] + _dot_exact_lhs(unflip, yr_ref[0, 0])
    bonus = bf_ref[0, 0] + _dot_exact_lhs(unflip, br_ref[0, 0])
    inv = 1.0 / HEAD_DIM
    mu = _dot_exact_rhs(y, ones_ref[...]) * inv
    yc = y - mu
    var = _dot_exact_rhs(yc * yc, ones_ref[...]) * inv
    yn = yc * lax.rsqrt(var + RWKV_GN_EPS) * lw_ref[...] + lb_ref[...]
    o_ref[0] = ((yn + bonus) * g_ref[0, 0]).astype(o_ref.dtype)


def _rwkv_readout(y, bonus, g, ln_w, ln_b, cl):
    _, b, lt, c = y.shape
    tr = _rwkv_tile(lt, cl)
    fwd = pl.BlockSpec((1, 1, tr, c), lambda i, r: (0, i, r, 0))
    rev = pl.BlockSpec((1, 1, tr, c), lambda i, r: (1, i, _mirror_tile(r, True, cl // tr, lt // tr), 0))
    row = pl.BlockSpec((1, c), lambda i, r: (0, 0))
    return pl.pallas_call(
        functools.partial(_rwkv_readout_kernel, tr=tr),
        out_shape=jax.ShapeDtypeStruct((b, lt, c), BF16), grid=(b, lt // tr),
        in_specs=[fwd, rev, fwd, rev, fwd, row, row, pl.BlockSpec((c, c), lambda i, r: (0, 0))],
        out_specs=pl.BlockSpec((1, tr, c), lambda i, r: (i, r, 0)),
        compiler_params=_cparams("parallel", "parallel"), name="rwkv_readout",
    )(y, y, bonus, bonus, g, ln_w.reshape(1, c), ln_b.reshape(1, c), _head_ones(c))


def _pad_rank_rows(w):
    z = jnp.zeros_like(w[0])
    return jnp.stack([jnp.concatenate([w[0], z], 0), jnp.concatenate([z, w[1]], 0)])


def _rwkv_mixer(prw, mu, w0, w2, a0, a2, g2, k_k, k_a, r_k, ln_w, ln_b, cl):
    mud = jnp.stack([mu, mu[::-1]])
    outs = _rwkv_prep(prw, mud, w0, _pad_rank_rows(w2).astype(BF16), a0, _pad_rank_rows(a2).astype(BF16), g2,
                      k_k, k_a, r_k, cl)
    *scan_in, g, bonus = outs
    y = _rwkv_scan(*scan_in)
    return _rwkv_readout(y, bonus, g, ln_w, ln_b, cl)


def _merge_kernel(ya_ref, yh_ref, yr_ref, yl_ref, gt_ref, x_ref, modx_ref, modc_ref,
                  wa_ref, wh_ref, wr_ref, wl_ref, wo_ref, rth_ref, rtl_ref, rtb_ref, o_ref, h_ref, cmb_ref, *, cl, tm, d):
    is_ctx = _row_ids(tm, pl.program_id(1)) < cl
    m = None
    for i, (y_ref, w_ref) in enumerate(((ya_ref, wa_ref), (yh_ref, wh_ref), (yr_ref, wr_ref), (yl_ref, wl_ref))):
        gate = _sigmoid(gt_ref[0, :, i * d:(i + 1) * d].astype(F32))
        term = gate * _dot(y_ref[0], w_ref[...])
        m = term if m is None else m + term
    g1 = _mod_rows(modx_ref, modc_ref, 2, is_ctx)
    x_new = x_ref[0] + g1 * _dot(m.astype(BF16), wo_ref[...])
    o_ref[0] = x_new
    h, cmb = _route_rows(x_new, is_ctx, modx_ref, modc_ref, rth_ref, rtl_ref, rtb_ref)
    h_ref[0] = h
    cmb_ref[0] = cmb


def _merge(ys, gates, xc, mod, w_brs, w_out, router, cl):
    b, lt, d = xc.shape
    tm = _largest_tile(lt, 544, 16)
    row = lambda w: pl.BlockSpec((1, tm, w), lambda i, r: (i, r, 0))
    const = lambda a: pl.BlockSpec(a.shape, lambda i, r: (0, 0))
    ws = [w.astype(BF16) for w in w_brs] + [w_out.astype(BF16)] + list(router)
    return pl.pallas_call(
        functools.partial(_merge_kernel, cl=cl, tm=tm, d=d),
        out_shape=(jax.ShapeDtypeStruct((b, lt, d), F32), jax.ShapeDtypeStruct((b, lt, d), BF16),
                   jax.ShapeDtypeStruct((b, lt, LANES), F32)),
        grid=(b, lt // tm),
        in_specs=[row(y.shape[-1]) for y in ys] + [row(N_BRANCH * d), row(d),
                  pl.BlockSpec((1, 6, d), lambda i, r: (i, 0, 0)), pl.BlockSpec((1, 6, d), lambda i, r: (b, 0, 0))]
                 + [const(w) for w in ws],
        out_specs=(row(d), row(d), row(LANES)), compiler_params=_cparams("parallel", "parallel"), name="merge",
    )(*ys, gates, xc, mod, mod, *ws)


def _route_rows(x, is_ctx, modx_ref, modc_ref, wrh_ref, wrl_ref, br_ref):
    h = _rms_modulate(x, _mod_rows(modx_ref, modc_ref, 3, is_ctx), _mod_rows(modx_ref, modc_ref, 4, is_ctx))
    h_hi = h.astype(BF16)
    h_lo = (h - h_hi.astype(F32)).astype(BF16)
    lg = _dot(h_hi, wrh_ref[...]) + (_dot(h_hi, wrl_ref[...]) + _dot(h_lo, wrh_ref[...])) + br_ref[...]
    lane = lax.broadcasted_iota(jnp.int32, lg.shape, 1)
    lane_f = lane.astype(F32)
    neg = -jnp.inf
    big = 1e9

    def first_lane(cond):
        return jnp.min(jnp.where(cond, lane_f, big), axis=-1, keepdims=True)

    is_grp = (lane >= N_EXPERTS) & (lane < N_EXPERTS + N_GROUPS)
    gl = jnp.where(is_grp, lg, neg)
    gmax = jnp.max(gl, axis=-1, keepdims=True)
    ge = jnp.where(is_grp, jnp.exp(gl - gmax), 0.0)
    gp = ge / jnp.sum(ge, axis=-1, keepdims=True)
    g_val = jnp.max(gp, axis=-1, keepdims=True)
    g_idx = first_lane(is_grp & (gp == g_val)) - N_EXPERTS
    lo = g_idx * EXPERTS_PER_GROUP
    in_grp = (lane_f >= lo) & (lane_f < lo + EXPERTS_PER_GROUP)
    el = jnp.where(in_grp, lg, neg)
    emax = jnp.max(el, axis=-1, keepdims=True)
    ee = jnp.where(in_grp, jnp.exp(el - emax), 0.0)
    pe = ee / jnp.sum(ee, axis=-1, keepdims=True)
    v1 = jnp.max(jnp.where(in_grp, pe, -1.0), axis=-1, keepdims=True)
    i1 = first_lane(in_grp & (pe == v1))
    rest = in_grp & (lane_f != i1)
    v2 = jnp.max(jnp.where(rest, pe, -1.0), axis=-1, keepdims=True)
    i2 = first_lane(rest & (pe == v2))
    den = v1 + v2
    cmb = (jnp.where(lane_f == i1, g_val * v1 / den, 0.0) + jnp.where(lane_f == i2, g_val * v2 / den, 0.0)
           + jnp.where(lane == N_EXPERTS, g_idx, 0.0))
    return h_hi, cmb


def _router_weights(w_grp, b_grp, w_rt, b_rt):
    d = w_rt.shape[0]
    wr = jnp.zeros((d, LANES), F32).at[:, :N_EXPERTS].set(w_rt).at[:, N_EXPERTS:N_EXPERTS + N_GROUPS].set(w_grp)
    br = jnp.zeros((1, LANES), F32).at[0, :N_EXPERTS].set(b_rt).at[0, N_EXPERTS:N_EXPERTS + N_GROUPS].set(b_grp)
    hi = wr.astype(BF16)
    return hi, (wr - hi.astype(F32)).astype(BF16), br


def _moe_kernel(grp_ref, nvalid_ref, h_ref, cmb_ref, w1_ref, w3_ref, w2_ref, o_ref, acc_ref, *, bm):
    j = pl.program_id(0)
    e = pl.program_id(1)

    @pl.when(e == 0)
    def _():
        acc_ref[...] = jnp.zeros_like(acc_ref)

    @pl.when(j < nvalid_ref[0])
    def _():
        h = h_ref[...]
        t = _silu(_dot(h, w1_ref[0, 0])) * _dot(h, w3_ref[0, 0])
        y = _dot(t.astype(BF16), w2_ref[0, 0])
        lane = lax.broadcasted_iota(jnp.int32, (bm, LANES), 1)
        expert = grp_ref[j] * EXPERTS_PER_GROUP + e
        wcol = jnp.sum(jnp.where(lane == expert, cmb_ref[...], 0.0), axis=-1, keepdims=True)
        acc_ref[...] += wcol * y

    @pl.when(e == pl.num_programs(1) - 1)
    def _():
        o_ref[...] = acc_ref[...].astype(o_ref.dtype)


def _moe_grouped(hs, ws, blk_group, nvalid, w1, w3, w2, layer):
    s_rows, d = hs.shape
    hid = w1.shape[3]
    bm = MOE_BLOCK_ROWS
    wspec = lambda shape: pl.BlockSpec(shape, lambda j, e, grp, nv: (layer, grp[j] * EXPERTS_PER_GROUP + e, 0, 0))
    return pl.pallas_call(
        functools.partial(_moe_kernel, bm=bm),
        out_shape=jax.ShapeDtypeStruct((s_rows, d), BF16),
        grid_spec=pltpu.PrefetchScalarGridSpec(
            num_scalar_prefetch=2, grid=(s_rows // bm, EXPERTS_PER_GROUP),
            in_specs=[pl.BlockSpec((bm, d), lambda j, e, grp, nv: (j, 0)),
                      pl.BlockSpec((bm, LANES), lambda j, e, grp, nv: (j, 0)),
                      wspec((1, 1, d, hid)), wspec((1, 1, d, hid)), wspec((1, 1, hid, d))],
            out_specs=pl.BlockSpec((bm, d), lambda j, e, grp, nv: (j, 0)),
            scratch_shapes=[pltpu.VMEM((bm, d), F32)]),
        compiler_params=_cparams("parallel", "arbitrary"), name="moe_experts",
    )(blk_group, nvalid, hs, ws, w1, w3, w2)


def _cast_kernel(x_ref, o_ref):
    o_ref[...] = x_ref[...].astype(o_ref.dtype)


def _to_bf16(w):
    depth, ne, r, c = w.shape
    spec = pl.BlockSpec((1, 1, r, c), lambda i, e: (i, e, 0, 0))
    return pl.pallas_call(
        _cast_kernel, out_shape=jax.ShapeDtypeStruct(w.shape, BF16), grid=(depth, ne),
        in_specs=[spec], out_specs=spec, compiler_params=_cparams("parallel", "parallel"), name="to_bf16",
    )(w)


def _moe_finish_kernel(x_ref, *rest, cl, tm, first, part_batches):
    *y_refs, modx_ref, modc_ref, o_ref = rest
    is_ctx = _row_ids(tm, pl.program_id(1) + first) < cl
    y = y_refs[0][0]
    for p in range(1, len(y_refs)):
        y = jnp.where(pl.program_id(0) >= p * part_batches, y_refs[p][0], y)
    o_ref[0] = x_ref[0] + _mod_rows(modx_ref, modc_ref, 5, is_ctx) * y.astype(F32)


def _moe_finish(xc, ys, mod, cl, latent_only):
    b, lt, d = xc.shape
    pb = b // len(ys)
    if latent_only:
        tm = _largest_tile(math.gcd(cl, lt - cl), 1088, 16)
        first = cl // tm
    else:
        tm, first = _largest_tile(lt, 1088, 16), 0
    src = pl.BlockSpec((1, tm, d), lambda i, r: (i, r + first, 0))
    part = lambda p: pl.BlockSpec((1, tm, d), lambda i, r: (jnp.clip(i - p * pb, 0, pb - 1), r + first, 0))
    n_rows = lt - first * tm
    return pl.pallas_call(
        functools.partial(_moe_finish_kernel, cl=cl, tm=tm, first=first, part_batches=pb),
        out_shape=jax.ShapeDtypeStruct((b, n_rows, d), F32), grid=(b, n_rows // tm),
        in_specs=[src] + [part(p) for p in range(len(ys))]
                 + [pl.BlockSpec((1, 6, d), lambda i, r: (i, 0, 0)), pl.BlockSpec((1, 6, d), lambda i, r: (b, 0, 0))],
        out_specs=pl.BlockSpec((1, tm, d), lambda i, r: (i, r, 0)),
        compiler_params=_cparams("parallel", "parallel"), name="moe_finish",
    )(xc, *ys, mod, mod)


def _moe_part(h2_rows, cmb_rows, first_token, n_tokens, w1, w3, w2, layer):
    bm = MOE_BLOCK_ROWS
    s_rows = -(-n_tokens // bm) * bm + N_GROUPS * bm
    gid = cmb_rows[first_token:first_token + n_tokens, N_EXPERTS].astype(jnp.int32)
    onehot = (gid[:, None] == jnp.arange(N_GROUPS, dtype=jnp.int32)).astype(jnp.int32)
    csum = jnp.cumsum(onehot, axis=0)
    rank = jnp.sum(onehot * csum, axis=1) - 1
    padded = -(-csum[-1] // bm) * bm
    ends = jnp.cumsum(padded)
    pos = (ends - padded)[gid] + rank
    tokens = first_token + jnp.arange(n_tokens, dtype=jnp.int32)
    slot_token = jnp.full((s_rows,), first_token, jnp.int32).at[pos].set(tokens)
    starts = jnp.arange(s_rows // bm, dtype=jnp.int32) * bm
    blk_group = jnp.minimum(jnp.sum((starts[:, None] >= ends[None, :]).astype(jnp.int32), axis=1), N_GROUPS - 1)
    nvalid = (ends[-1] // bm).reshape(1).astype(jnp.int32)
    take_rows = lambda a, idx: a.at[idx].get(mode="promise_in_bounds")
    ys = _moe_grouped(take_rows(h2_rows, slot_token), take_rows(cmb_rows, slot_token), blk_group.astype(jnp.int32),
                      nvalid, w1, w3, w2, layer)
    return take_rows(ys, pos)


def _moe(h2, cmb, xc, mod, w1, w3, w2, layer, cl, latent_only=False):
    b, lt, d = xc.shape
    parts = MOE_PARTS if b % MOE_PARTS == 0 else 1
    n_tok = (b // parts) * lt
    h2_rows, cmb_rows = h2.reshape(b * lt, d), cmb.reshape(b * lt, LANES)
    ys = [_moe_part(h2_rows, cmb_rows, p * n_tok, n_tok, w1, w3, w2, layer).reshape(b // parts, lt, d)
          for p in range(parts)]
    return _moe_finish(xc, ys, mod, cl, latent_only)


def kernel(x, c, ctx, c_ctx, ada_w, ada_b, w_in, q_norm, k_norm, hy_conv_w, hy_conv_b, hy_f1, hy_fb1, hy_f2, hy_fb2, hy_f3, hy_skip, rw_mu, rw_w0, rw_w2, rw_a0, rw_a2, rw_g2, rw_k_k, rw_k_a, rw_r_k, rw_ln_w, rw_ln_b, lru_conv_w, lru_conv_b, lru_wa, lru_ba, lru_wx, lru_bx, lru_lambda, w_br_attn, w_br_hyena, w_br_rwkv, w_br_lru, w_out, moe_w_grp, moe_b_grp, moe_w_rt, moe_b_rt, moe_w1, moe_w3, moe_w2):
    b, l, d = x.shape
    cl = ctx.shape[1]
    depth = ada_w.shape[0]
    assert b < MOD_ROWS and cl % RWKV_CHUNK == 0 and l % RWKV_CHUNK == 0

    xc = jnp.concatenate([ctx, x], axis=1)
    cc = jnp.zeros((MOD_ROWS, d), F32).at[:b].set(c).at[b].set(c_ctx)
    mod_all = _ada_mod(cc, ada_w, ada_b).reshape(depth, MOD_ROWS, 6, d)

    moe_w1b, moe_w3b, moe_w2b = (_to_bf16(w) for w in (moe_w1, moe_w3, moe_w2))
    cos2, sin2 = _rope_tables(l, cl)
    mats_x = _dft_mats(l)
    mats_c = _dft_mats(cl)
    qkv_w = ATTN_WIDTH + 2 * ATTN_KV_WIDTH
    col = np.cumsum([0, qkv_w, 3 * HYENA_WIDTH, RWKV_PROJ, 2 * LRU_WIDTH, N_BRANCH * d])

    for i in range(depth):
        need_ctx = i < depth - 1
        mod = mod_all[i]
        w_i = w_in[i].astype(BF16)
        h1 = _modnorm(xc, mod, cl)
        pqkv, phy, prw, plr, gates = (_proj(h1, w_i[:, col[j]:col[j + 1]]) for j in range(5))

        qn, kt, vx = _attn_prep(pqkv, cos2, sin2, q_norm[i], k_norm[i])
        y_att_x = _attention(qn, kt, vx, cl, 'x', cl + l)
        if need_ctx:
            y_att_c = _attention(qn, kt, vx, cl, 'ctx', cl)
        else:
            y_att_c = jnp.zeros((b, cl, ATTN_WIDTH), BF16)
        y_att = jnp.concatenate([y_att_c, y_att_x], axis=1)

        hv, hx1, hx2 = _hyena_pre(phy, hy_conv_w[i], hy_conv_b[i], cl)
        filt = (hy_f1[i], hy_fb1[i], hy_f2[i], hy_fb2[i], hy_f3[i])
        y_hx = _hyena_run(hv[:, cl:], hx1[:, cl:], hx2[:, cl:], _hyena_spectra(l, filt, mats_x), hy_skip[i], mats_x)
        if need_ctx:
            y_hc = _hyena_run(hv[:, :cl], hx1[:, :cl], hx2[:, :cl], _hyena_spectra(cl, filt, mats_c), hy_skip[i],
                              mats_c)
        else:
            y_hc = jnp.zeros((b, cl, HYENA_WIDTH), BF16)
        y_hy = jnp.concatenate([y_hc, y_hx], axis=1)

        y_rw = _rwkv_mixer(prw, rw_mu[i], rw_w0[i], rw_w2[i], rw_a0[i], rw_a2[i], rw_g2[i], rw_k_k[i], rw_k_a[i],
                           rw_r_k[i].reshape(-1), rw_ln_w[i], rw_ln_b[i], cl)

        la, lb = _lru_pre(plr, lru_conv_w[i], lru_conv_b[i], lru_wa[i], lru_ba[i], lru_wx[i], lru_bx[i],
                          lru_lambda[i], cl)
        y_lr = _lru_scan(la, lb, plr, cl)

        xc, h2, cmb = _merge((y_att, y_hy, y_rw, y_lr), gates, xc, mod,
                             (w_br_attn[i], w_br_hyena[i], w_br_rwkv[i], w_br_lru[i]), w_out[i],
                             _router_weights(moe_w_grp[i], moe_b_grp[i], moe_w_rt[i], moe_b_rt[i]), cl)
        xc = _moe(h2, cmb, xc, mod, moe_w1b, moe_w3b, moe_w2b, i, cl, latent_only=not need_ctx)
    return xc
```

```python
import functools
import math

import numpy as np
import jax
import jax.numpy as jnp
from jax import lax
from jax.experimental import pallas as pl
from jax.experimental.pallas import tpu as pltpu

F32 = jnp.float32
BF16 = jnp.bfloat16

HEAD_DIM = 64
GRID_W = 64
EPS = 1e-6
ATTN_HEADS = 8
ATTN_KV_HEADS = 2
ATTN_GROUP = ATTN_HEADS // ATTN_KV_HEADS
ATTN_Q_BLOCKS = 2
ATTN_WIDTH = ATTN_HEADS * HEAD_DIM
ATTN_KV_WIDTH = ATTN_KV_HEADS * HEAD_DIM
ROPE_THETA = 10000.0
HYENA_WIDTH = 256
HYENA_ORDER = 2
HYENA_BANDS = 16
HYENA_DECAY_TARGET = 1e-2
HYENA_FAST_DECAY = 0.3
HYENA_SLOW_DECAY = 1.5
RWKV_HEADS = 4
RWKV_WIDTH = RWKV_HEADS * HEAD_DIM
RWKV_DECAY_RANK = 64
RWKV_ICLR_RANK = 64
RWKV_GATE_RANK = 128
RWKV_GN_EPS = 64e-5
RWKV_PROJ = 3 * RWKV_WIDTH + 2 * RWKV_DECAY_RANK + 2 * RWKV_ICLR_RANK + RWKV_GATE_RANK
RWKV_CHUNK = 64
RWKV_INV_BASE = 4
RWKV_STAGE_ROWS = 128
LRU_WIDTH = 256
LRU_BLOCKS = 4
LRU_C = 8.0
LRU_GROUPS_PER_STEP = 4
N_BRANCH = 4
N_GROUPS = 4
EXPERTS_PER_GROUP = 4
N_EXPERTS = N_GROUPS * EXPERTS_PER_GROUP
MOE_BLOCK_ROWS = 512

V7X_VMEM_LIMIT_BYTES = 52 * 1024 * 1024
SUBLANES = 8
LANES = 128
MOD_ROWS = 16


def _cparams(*sem):
    return pltpu.CompilerParams(dimension_semantics=sem, vmem_limit_bytes=V7X_VMEM_LIMIT_BYTES)


def _dot(a, b):
    return jnp.dot(a, b, preferred_element_type=F32)


def _dot_nt(a, b):
    return lax.dot_general(a, b, (((1,), (1,)), ((), ())), preferred_element_type=F32)


def _split3(x):
    hi = x.astype(BF16)
    r1 = x - hi.astype(F32)
    mid = r1.astype(BF16)
    lo = (r1 - mid.astype(F32)).astype(BF16)
    return hi, mid, lo


def _dot_exact_lhs(m_bf16, x):
    hi, mid, lo = _split3(x)
    return _dot(m_bf16, hi) + _dot(m_bf16, mid) + _dot(m_bf16, lo)


def _dot_exact_rhs(x, m_bf16):
    hi = x.astype(BF16)
    lo = (x - hi.astype(F32)).astype(BF16)
    return _dot(hi, m_bf16) + _dot(lo, m_bf16)


def _dot_bf(a, b):
    return _dot(a.astype(BF16), b.astype(BF16))


def _sigmoid(x):
    return 0.5 * jnp.tanh(0.5 * x) + 0.5


def _softplus(x):
    return jnp.maximum(x, 0.0) + jnp.log(1.0 + jnp.exp(-jnp.abs(x)))


def _silu(x):
    return x * _sigmoid(x)


def _largest_tile(n, cap, mult):
    best = None
    for t in range(mult, min(n, cap) + 1, mult):
        if n % t == 0:
            best = t
    assert best is not None, (n, cap, mult)
    return best


def _head_ones(width):
    idx = np.arange(width) // HEAD_DIM
    return jnp.asarray((idx[:, None] == idx[None, :]).astype(np.float32), dtype=BF16)


def _row_ids(tile_rows, tile_idx):
    return tile_idx * tile_rows + lax.broadcasted_iota(jnp.int32, (tile_rows, 1), 0)


def _mod_rows(modx_ref, modc_ref, idx, is_ctx):
    return jnp.where(is_ctx, modc_ref[0, idx:idx + 1, :], modx_ref[0, idx:idx + 1, :])


def _rms_modulate(x, shift, scale):
    ms = jnp.mean(x * x, axis=-1, keepdims=True)
    return (x * lax.rsqrt(ms + EPS)) * (1.0 + scale) + shift


def _ada_kernel(c_ref, w_ref, b_ref, o_ref):
    s = _silu(c_ref[...])
    o_ref[0] = jnp.dot(s, w_ref[0], preferred_element_type=F32, precision=lax.Precision.HIGHEST) + b_ref[0]


def _ada_mod(cc, ada_w, ada_b):
    depth, d, n6 = ada_w.shape
    tn = _largest_tile(n6, 1024, LANES)
    return pl.pallas_call(
        _ada_kernel,
        out_shape=jax.ShapeDtypeStruct((depth, MOD_ROWS, n6), F32),
        grid=(depth, n6 // tn),
        in_specs=[pl.BlockSpec((MOD_ROWS, d), lambda i, j: (0, 0)),
                  pl.BlockSpec((1, d, tn), lambda i, j: (i, 0, j)),
                  pl.BlockSpec((1, 1, tn), lambda i, j: (i, 0, j))],
        out_specs=pl.BlockSpec((1, MOD_ROWS, tn), lambda i, j: (i, 0, j)),
        compiler_params=_cparams("parallel", "parallel"),
        name="ada_mod",
    )(cc, ada_w, ada_b.reshape(depth, 1, n6))


def _modnorm_kernel(x_ref, modx_ref, modc_ref, o_ref, *, cl, tm):
    is_ctx = _row_ids(tm, pl.program_id(1)) < cl
    h = _rms_modulate(x_ref[0], _mod_rows(modx_ref, modc_ref, 0, is_ctx), _mod_rows(modx_ref, modc_ref, 1, is_ctx))
    o_ref[0] = h.astype(o_ref.dtype)


def _modnorm(xc, mod, cl):
    b, lt, d = xc.shape
    tm = _largest_tile(lt, 1088, 16)
    return pl.pallas_call(
        functools.partial(_modnorm_kernel, cl=cl, tm=tm),
        out_shape=jax.ShapeDtypeStruct((b, lt, d), BF16), grid=(b, lt // tm),
        in_specs=[pl.BlockSpec((1, tm, d), lambda i, r: (i, r, 0)),
                  pl.BlockSpec((1, 6, d), lambda i, r: (i, 0, 0)),
                  pl.BlockSpec((1, 6, d), lambda i, r: (b, 0, 0))],
        out_specs=pl.BlockSpec((1, tm, d), lambda i, r: (i, r, 0)),
        compiler_params=_cparams("parallel", "parallel"), name="modnorm",
    )(xc, mod, mod)


def _proj_kernel(h_ref, w_ref, o_ref):
    o_ref[0] = _dot(h_ref[0], w_ref[...]).astype(o_ref.dtype)


def _proj(h, w):
    b, lt, d = h.shape
    n = w.shape[1]
    tm = _largest_tile(lt, 2176, 16)
    tn = n if n <= 1280 else _largest_tile(n, 1024, 2 * LANES)
    return pl.pallas_call(
        _proj_kernel, out_shape=jax.ShapeDtypeStruct((b, lt, n), BF16), grid=(b, lt // tm, n // tn),
        in_specs=[pl.BlockSpec((1, tm, d), lambda i, r, j: (i, r, 0)), pl.BlockSpec((d, tn), lambda i, r, j: (0, j))],
        out_specs=pl.BlockSpec((1, tm, tn), lambda i, r, j: (i, r, j)),
        compiler_params=_cparams("parallel", "parallel", "parallel"), name="proj",
    )(h, w)


def _rope_tables(l, cl):
    n_freq = HEAD_DIM // 4
    t = jnp.arange(l)
    freqs = ROPE_THETA ** (-jnp.arange(n_freq, dtype=F32) / n_freq)
    pos = jnp.stack([t // GRID_W, t % GRID_W], -1).astype(F32)
    ang = pos[..., None] * freqs
    cos64 = jnp.stack([jnp.cos(ang), jnp.cos(ang)], axis=2).reshape(l, HEAD_DIM)
    sin64 = jnp.stack([-jnp.sin(ang), jnp.sin(ang)], axis=2).reshape(l, HEAD_DIM)
    cos64 = jnp.concatenate([jnp.ones((cl, HEAD_DIM), F32), cos64], 0)
    sin64 = jnp.concatenate([jnp.zeros((cl, HEAD_DIM), F32), sin64], 0)
    return jnp.tile(cos64, (1, 2)), jnp.tile(sin64, (1, 2))


def _head_rms(t, ones_ref):
    ms = _dot_exact_rhs(t * t, ones_ref[...]) * (1.0 / HEAD_DIM)
    return t * lax.rsqrt(ms + EPS)


def _rope(t, cos, sin):
    w = t.shape[-1]
    lane = lax.broadcasted_iota(jnp.int32, t.shape, 1)
    q4 = HEAD_DIM // 4
    first_half = (lane % (2 * q4)) < q4
    partner = jnp.where(first_half, pltpu.roll(t, w - q4, 1), pltpu.roll(t, q4, 1))
    return t * cos + partner * sin


def _attn_prep_kernel(p_ref, cos_ref, sin_ref, qg_ref, kg_ref, oq_ref, ok_ref, q_ref, kt_ref, vx_ref):
    p = p_ref[0].astype(F32)
    v = p[:, ATTN_WIDTH + ATTN_KV_WIDTH:]
    low = lax.broadcasted_iota(jnp.int32, v.shape, 1) < HEAD_DIM
    vx_ref[0, 0] = jnp.where(low, v, 1.0).astype(vx_ref.dtype)
    vx_ref[0, 1] = jnp.where(low, pltpu.roll(v, HEAD_DIM, 1), 1.0).astype(vx_ref.dtype)
    cos2, sin2 = cos_ref[...], sin_ref[...]
    reps = ATTN_WIDTH // (2 * HEAD_DIM)
    cos_q = jnp.concatenate([cos2] * reps, axis=1)
    sin_q = jnp.concatenate([sin2] * reps, axis=1)
    q = _head_rms(p[:, :ATTN_WIDTH], oq_ref) * qg_ref[...]
    q_ref[0] = _rope(q, cos_q, sin_q).astype(q_ref.dtype)
    k = _head_rms(p[:, ATTN_WIDTH:ATTN_WIDTH + ATTN_KV_WIDTH], ok_ref) * kg_ref[...]
    kt_ref[0] = _rope(k, cos2, sin2).T.astype(kt_ref.dtype)


def _attn_prep(pqkv, cos2, sin2, q_gain, k_gain):
    b, lt, wtot = pqkv.shape
    tr = _largest_tile(lt, 2176, LANES)
    qg = jnp.tile(q_gain * (HEAD_DIM ** -0.5 * math.log2(math.e)), ATTN_HEADS).reshape(1, ATTN_WIDTH)
    kg = jnp.tile(k_gain, ATTN_KV_HEADS).reshape(1, ATTN_KV_WIDTH)
    return pl.pallas_call(
        _attn_prep_kernel,
        out_shape=(jax.ShapeDtypeStruct((b, lt, ATTN_WIDTH), BF16),
                   jax.ShapeDtypeStruct((b, ATTN_KV_WIDTH, lt), BF16),
                   jax.ShapeDtypeStruct((b, ATTN_KV_HEADS, lt, 2 * HEAD_DIM), BF16)),
        grid=(b, lt // tr),
        in_specs=[pl.BlockSpec((1, tr, wtot), lambda i, r: (i, r, 0)),
                  pl.BlockSpec((tr, 2 * HEAD_DIM), lambda i, r: (r, 0)),
                  pl.BlockSpec((tr, 2 * HEAD_DIM), lambda i, r: (r, 0)),
                  pl.BlockSpec((1, ATTN_WIDTH), lambda i, r: (0, 0)),
                  pl.BlockSpec((1, ATTN_KV_WIDTH), lambda i, r: (0, 0)),
                  pl.BlockSpec((ATTN_WIDTH, ATTN_WIDTH), lambda i, r: (0, 0)),
                  pl.BlockSpec((ATTN_KV_WIDTH, ATTN_KV_WIDTH), lambda i, r: (0, 0))],
        out_specs=(pl.BlockSpec((1, tr, ATTN_WIDTH), lambda i, r: (i, r, 0)),
                   pl.BlockSpec((1, ATTN_KV_WIDTH, tr), lambda i, r: (i, 0, r)),
                   pl.BlockSpec((1, ATTN_KV_HEADS, tr, 2 * HEAD_DIM), lambda i, r: (i, 0, r, 0))),
        compiler_params=_cparams("parallel", "parallel"),
        name="attn_prep",
    )(pqkv, cos2, sin2, qg, kg, _head_ones(ATTN_WIDTH), _head_ones(ATTN_KV_WIDTH))


def _attn_kernel(*refs, nq, nk):
    q_refs, (kt_ref, v_ref, o_ref) = refs[:nq], refs[nq:]
    outs = []
    for h in range(ATTN_HEADS):
        kv = h // ATTN_GROUP
        hs = slice(h * HEAD_DIM, (h + 1) * HEAD_DIM)
        qh = jnp.concatenate([q_ref[0, :, hs] for q_ref in q_refs], axis=0)
        s = _dot(qh, kt_ref[0, kv * HEAD_DIM:(kv + 1) * HEAD_DIM, :nk])
        m = jnp.max(s, axis=-1, keepdims=True)
        p = jnp.exp2(s - m)
        o = _dot(p.astype(BF16), v_ref[0, kv, :nk, :])
        outs.append(o[:, :HEAD_DIM] / o[:, HEAD_DIM:])
    o_ref[0] = jnp.concatenate(outs, axis=-1).astype(o_ref.dtype)


def _attention(qn, kt, vx, cl, rows, nk):
    b, lt, _ = qn.shape
    blk = 256 if (cl % 256 == 0 and lt % 256 == 0) else 128
    if rows == 'ctx':
        n_rows, first, nq = cl, 0, 1
    else:
        n_rows, first = lt - cl, cl // blk
        nq = ATTN_Q_BLOCKS if n_rows % (ATTN_Q_BLOCKS * blk) == 0 else 1
    tq = nq * blk
    q_specs = [pl.BlockSpec((1, blk, ATTN_WIDTH), lambda i, t, j=j: (i, first + nq * t + j, 0)) for j in range(nq)]
    return pl.pallas_call(
        functools.partial(_attn_kernel, nq=nq, nk=nk),
        out_shape=jax.ShapeDtypeStruct((b, n_rows, ATTN_WIDTH), BF16),
        grid=(b, n_rows // tq),
        in_specs=q_specs + [pl.BlockSpec((1, ATTN_KV_WIDTH, lt), lambda i, t: (i, 0, 0)),
                            pl.BlockSpec((1, ATTN_KV_HEADS, lt, 2 * HEAD_DIM), lambda i, t: (i, 0, 0, 0))],
        out_specs=pl.BlockSpec((1, tq, ATTN_WIDTH), lambda i, t: (i, t, 0)),
        compiler_params=_cparams("parallel", "parallel"),
        name="attention_" + rows,
    )(*([qn] * nq), kt, vx)


def _halo_specs(tr, lt, width, lead):
    per = tr // SUBLANES
    last = lt // SUBLANES - 1
    nlead = len(lead)

    def prev_map(*ids):
        return (*ids[:nlead], jnp.maximum(ids[nlead] * per - 1, 0), 0)

    def next_map(*ids):
        return (*ids[:nlead], jnp.minimum((ids[nlead] + 1) * per, last), 0)

    blk = (*lead, SUBLANES, width)
    return pl.BlockSpec(blk, prev_map), pl.BlockSpec(blk, next_map)


def _shift_rows(x, halo, offset, rows, cl, lt):
    tr = x.shape[0]
    local = lax.broadcasted_iota(jnp.int32, (tr, 1), 0)
    if offset < 0:
        y = pltpu.roll(x, -offset, 0)
        y = jnp.where(local == 0, halo, y)
        bad = (rows == 0) | (rows == cl)
    else:
        y = pltpu.roll(x, tr - offset, 0)
        for j in range(offset):
            y = jnp.where(local == tr - offset + j, halo[j:j + 1, :], y)
        bad = (rows >= lt - offset) | ((rows >= cl - offset) & (rows < cl))
    return jnp.where(bad, 0.0, y)


def _hyena_pre_kernel(p_ref, pv_ref, nx_ref, w_ref, b_ref, v_ref, x1_ref, x2_ref, *, cl, lt, tr):
    rows = _row_ids(tr, pl.program_id(1))
    p = p_ref[0].astype(F32)
    pm = _shift_rows(p, pv_ref[0, SUBLANES - 1:, :].astype(F32), -1, rows, cl, lt)
    pp = _shift_rows(p, nx_ref[0, :1, :].astype(F32), 1, rows, cl, lt)
    z = pm * w_ref[0:1, :] + p * w_ref[1:2, :] + pp * w_ref[2:3, :] + b_ref[...]
    c = HYENA_WIDTH
    v_ref[0] = z[:, :c].astype(v_ref.dtype)
    x1_ref[0] = z[:, c:2 * c].astype(x1_ref.dtype)
    x2_ref[0] = z[:, 2 * c:].astype(x2_ref.dtype)


def _hyena_pre(phy, conv_w, conv_b, cl):
    b, lt, w = phy.shape
    tr = _largest_tile(lt, 1088, 16)
    prev_spec, next_spec = _halo_specs(tr, lt, w, (1,))
    out = jax.ShapeDtypeStruct((b, lt, HYENA_WIDTH), BF16)
    ospec = pl.BlockSpec((1, tr, HYENA_WIDTH), lambda i, r: (i, r, 0))
    return pl.pallas_call(
        functools.partial(_hyena_pre_kernel, cl=cl, lt=lt, tr=tr),
        out_shape=(out, out, out),
        grid=(b, lt // tr),
        in_specs=[pl.BlockSpec((1, tr, w), lambda i, r: (i, r, 0)), prev_spec, next_spec,
                  pl.BlockSpec(conv_w.shape, lambda i, r: (0, 0)),
                  pl.BlockSpec((1, w), lambda i, r: (0, 0))],
        out_specs=(ospec, ospec, ospec),
        compiler_params=_cparams("parallel", "parallel"),
        name="hyena_pre",
    )(phy, phy, phy, conv_w, conv_b.reshape(1, w))


def _trig_mats_kernel(cb_ref, sb_ref, co_ref, so_ref, c_ref, s_ref):
    cb, sb, co, so = cb_ref[0], sb_ref[0], co_ref[...], so_ref[...]
    c_ref[...] = (cb * co - sb * so).astype(c_ref.dtype)
    s_ref[...] = (sb * co + cb * so).astype(s_ref.dtype)


def _trig_mats(n, h, row_mult, col_mult):
    tm = _largest_tile(h, 256, 16)
    tn = _largest_tile(h, 1024, LANES) if h % LANES == 0 else h
    b = col_mult(jnp.arange(h, dtype=jnp.int32))[None, :]

    def tables(a):
        ang = ((a[:, None] * b) % (2 * n)).astype(F32) * (math.pi / n)
        return jnp.cos(ang), jnp.sin(ang)

    r0 = jnp.arange(0, h, tm, dtype=jnp.int32)
    cb, sb = tables(row_mult(r0))
    co, so = tables(row_mult(jnp.arange(tm, dtype=jnp.int32)) - row_mult(jnp.zeros((tm,), jnp.int32)))
    base = pl.BlockSpec((1, 1, tn), lambda j, c: (j, 0, c))
    off = pl.BlockSpec((tm, tn), lambda j, c: (0, c))
    out = jax.ShapeDtypeStruct((h, h), BF16)
    ospec = pl.BlockSpec((tm, tn), lambda j, c: (j, c))
    return pl.pallas_call(
        _trig_mats_kernel, out_shape=(out, out), grid=(h // tm, h // tn),
        in_specs=[base, base, off, off], out_specs=(ospec, ospec),
        compiler_params=_cparams("parallel", "parallel"), name="trig_mats",
    )(cb.reshape(h // tm, 1, h), sb.reshape(h // tm, 1, h), co, so)


def _dft_mats(n):
    h = n // 2
    ident = lambda r: r
    ce, se = _trig_mats(n, h, ident, lambda j: 2 * j)
    co, so = _trig_mats(n, h, ident, lambda j: 2 * j + 1)
    cot, sot = _trig_mats(n, h, lambda j: 2 * j + 1, ident)
    return ce, se, co, so, cot, sot


def _alt_sum(z):
    j = lax.broadcasted_iota(jnp.int32, (z.shape[0], 1), 0)
    return jnp.sum(z * (1 - 2 * (j % 2)).astype(F32), axis=0, keepdims=True)


def _half_spectra(ce_ref, se_ref, co_ref, so_ref, z_ev, z_od):
    p, q = _dot(ce_ref[...], z_ev), _dot(co_ref[...], z_od)
    ps, qs = _dot(se_ref[...], z_ev), _dot(so_ref[...], z_od)
    return p + q, ps + qs, p - q, qs - ps


def _dft_raw_kernel(ce_ref, se_ref, co_ref, so_ref, z_ref, rl_ref, il_ref, ru_ref, iu_ref, mid_ref):
    z_ev, z_od = z_ref[0], z_ref[1]
    rl_ref[...], il_ref[...], ru_ref[...], iu_ref[...] = _half_spectra(ce_ref, se_ref, co_ref, so_ref, z_ev, z_od)

    @pl.when(pl.program_id(0) == 0)
    def _():
        mid = jnp.concatenate([_alt_sum(z_ev.astype(F32)), _alt_sum(z_od.astype(F32))], axis=0)
        mid_ref[...] = jnp.concatenate([mid, jnp.zeros((SUBLANES - 2, mid.shape[1]), F32)], axis=0)


def _dft_raw(mats, zs):
    ce, se, co, so, _, _ = mats
    _, h, c = zs.shape
    tm = _largest_tile(h, 256, 16)
    mat = pl.BlockSpec((tm, h), lambda j: (j, 0))
    out = jax.ShapeDtypeStruct((h, c), F32)
    ospec = pl.BlockSpec((tm, c), lambda j: (j, 0))
    return pl.pallas_call(
        _dft_raw_kernel, out_shape=(out, out, out, out, jax.ShapeDtypeStruct((SUBLANES, c), F32)), grid=(h // tm,),
        in_specs=[mat, mat, mat, mat, pl.BlockSpec((2, h, c), lambda j: (0, 0, 0))],
        out_specs=(ospec, ospec, ospec, ospec, pl.BlockSpec((SUBLANES, c), lambda j: (0, 0))),
        compiler_params=_cparams("arbitrary"), name="dft_raw",
    )(ce, se, co, so, zs)


def _dft_fwd_kernel(ce_ref, se_ref, co_ref, so_ref, z_ref, hrl_ref, hil_ref, hru_ref, hiu_ref, hmid_ref,
                    ea_ref, eb_ref, oa_ref, ob_ref, mid_ref, *, bb):
    hrl, hil, hru, hiu = hrl_ref[...], hil_ref[...], hru_ref[...], hiu_ref[...]
    for i in range(bb):
        z_ev, z_od = z_ref[i, 0], z_ref[i, 1]
        zrl, zil, zru, ziu = _half_spectra(ce_ref, se_ref, co_ref, so_ref, z_ev, z_od)
        yrl, yil = zrl * hrl + zil * hil, zil * hrl - zrl * hil
        yru, yiu = zru * hru + ziu * hiu, ziu * hru - zru * hiu
        ea_ref[i] = (yrl + yru).astype(ea_ref.dtype)
        eb_ref[i] = (yil - yiu).astype(eb_ref.dtype)
        oa_ref[i] = (yrl - yru).astype(oa_ref.dtype)
        ob_ref[i] = (yil + yiu).astype(ob_ref.dtype)

    @pl.when(pl.program_id(1) == 0)
    def _():
        hr, hi = hmid_ref[0:1, :], hmid_ref[1:2, :]
        for i in range(bb):
            zr, zi = _alt_sum(z_ref[i, 0].astype(F32)), _alt_sum(z_ref[i, 1].astype(F32))
            mid = jnp.concatenate([zr * hr + zi * hi, zi * hr - zr * hi], axis=0)
            mid_ref[i] = jnp.concatenate([mid, jnp.zeros((SUBLANES - 2, mid.shape[1]), F32)], axis=0)


def _dft_inv_kernel(ce_ref, se_ref, cot_ref, sot_ref, ea_ref, eb_ref, oa_ref, ob_ref, mid_ref, z_ref, g_ref, skip_ref,
                    o_ref, *, bb, tm):
    alt = (1 - 2 * (_row_ids(tm, pl.program_id(1)) % 2)).astype(F32)
    for i in range(bb):
        y_ev = _dot(ce_ref[...], ea_ref[i]) + _dot(se_ref[...], eb_ref[i]) + alt * mid_ref[i, 0:1, :]
        y_od = _dot(cot_ref[...], oa_ref[i]) + _dot(sot_ref[...], ob_ref[i]) + alt * mid_ref[i, 1:2, :]
        for par, y in enumerate((y_ev, y_od)):
            y = y + z_ref[i, par].astype(F32) * skip_ref[...]
            o_ref[i, par] = (g_ref[i, par].astype(F32) * y).astype(o_ref.dtype)


def _longconv_gated(zs, gates, spectrum, skip, mats):
    ce, se, co, so, cot, sot = mats
    b, _, h, c = zs.shape
    bb = 2 if b % 2 == 0 else 1
    tm = _largest_tile(h, 256, 16)
    mat = pl.BlockSpec((tm, h), lambda i, j: (j, 0))
    full4 = pl.BlockSpec((bb, 2, h, c), lambda i, j: (i, 0, 0, 0))
    full = pl.BlockSpec((bb, h, c), lambda i, j: (i, 0, 0))
    tile = pl.BlockSpec((bb, tm, c), lambda i, j: (i, j, 0))
    tile4 = pl.BlockSpec((bb, 2, tm, c), lambda i, j: (i, 0, j, 0))
    filt = pl.BlockSpec((tm, c), lambda i, j: (j, 0))
    mid_spec = pl.BlockSpec((bb, SUBLANES, c), lambda i, j: (i, 0, 0))
    half = jax.ShapeDtypeStruct((b, h, c), BF16)
    *combos, mid = pl.pallas_call(
        functools.partial(_dft_fwd_kernel, bb=bb),
        out_shape=(half, half, half, half, jax.ShapeDtypeStruct((b, SUBLANES, c), F32)), grid=(b // bb, h // tm),
        in_specs=[mat, mat, mat, mat, full4, filt, filt, filt, filt, pl.BlockSpec((SUBLANES, c), lambda i, j: (0, 0))],
        out_specs=(tile, tile, tile, tile, mid_spec),
        compiler_params=_cparams("parallel", "arbitrary"), name="dft_fwd",
    )(ce, se, co, so, zs, *spectrum)
    return pl.pallas_call(
        functools.partial(_dft_inv_kernel, bb=bb, tm=tm),
        out_shape=jax.ShapeDtypeStruct((b, 2, h, c), BF16), grid=(b // bb, h // tm),
        in_specs=[mat, mat, mat, mat, full, full, full, full, mid_spec, tile4, tile4,
                  pl.BlockSpec((1, c), lambda i, j: (0, 0))],
        out_specs=tile4,
        compiler_params=_cparams("parallel", "parallel"), name="dft_inv",
    )(ce, se, cot, sot, *combos, mid, zs, gates, skip.reshape(1, c))


def _hyena_filters(n, f1, fb1, f2, fb2, f3):
    t = jnp.arange(n, dtype=F32) / n
    bands = jnp.arange(1, HYENA_BANDS + 1, dtype=F32)
    ang = 2.0 * math.pi * t[:, None] * bands
    feat = jnp.concatenate([t[:, None], jnp.sin(ang), jnp.cos(ang)], axis=-1)
    hp = lax.Precision.HIGHEST
    h = jnp.sin(jnp.dot(feat, f1, precision=hp) + fb1)
    h = jnp.sin(jnp.dot(h, f2, precision=hp) + fb2)
    h = jnp.dot(h, f3, precision=hp).reshape(n, HYENA_ORDER, 2, HYENA_WIDTH)
    deltas = jnp.linspace(-math.log(HYENA_DECAY_TARGET) / HYENA_SLOW_DECAY,
                          -math.log(HYENA_DECAY_TARGET) / HYENA_FAST_DECAY, HYENA_WIDTH, dtype=F32)
    h = h * jnp.exp(-t[:, None] * deltas)[:, None, None, :]
    return h / jnp.sum(jnp.abs(h), axis=(0, 2), keepdims=True)


def _parity_split(t):
    *lead, n, c = t.shape
    return jnp.swapaxes(t.reshape(*lead, n // 2, 2, c), -2, -3)


def _parity_merge(t):
    *lead, _, h, c = t.shape
    return jnp.swapaxes(t, -2, -3).reshape(*lead, 2 * h, c)


def _hyena_spectra(n, filt_params, mats):
    h = _hyena_filters(n, *filt_params)
    oc = HYENA_ORDER * HYENA_WIDTH
    hf = h[:, :, 0].reshape(n, oc)
    hb = h[:, :, 1].reshape(n, oc)
    hb = jnp.where(jnp.arange(n)[:, None] == 0, 0.0, hb)
    sig = _parity_split(jnp.concatenate([hf + hb, hb - hf], axis=1).astype(BF16))
    rl, il, ru, iu, mid = _dft_raw(mats, sig)
    scale = 1.0 / n
    ends = jnp.where(jnp.arange(n // 2)[:, None] == 0, 0.5 * scale, scale)
    spectra = []
    for o in range(HYENA_ORDER):
        re = slice(o * HYENA_WIDTH, (o + 1) * HYENA_WIDTH)
        im = slice(oc + o * HYENA_WIDTH, oc + (o + 1) * HYENA_WIDTH)
        hmid = jnp.zeros((SUBLANES, HYENA_WIDTH), F32).at[0].set(mid[0, re] * scale).at[1].set(mid[1, im] * scale)
        spectra.append((rl[:, re] * ends, il[:, im] * scale, ru[:, re] * ends, iu[:, im] * scale, hmid))
    return spectra


def _hyena_run(v, x1, x2, spectra, skip, mats):
    y = _parity_split(v)
    for o, gate in enumerate((x1, x2)):
        y = _longconv_gated(y, _parity_split(gate), spectra[o], skip[o], mats)
    return _parity_merge(y)


def _blockdiag_dense(w):
    nb, blk = w.shape[1], w.shape[2]
    eye = jnp.eye(nb, dtype=w.dtype)
    return jnp.einsum('dncf,nm->dncmf', w, eye).reshape(w.shape[0], nb * blk, nb * blk)


def _lru_pre_kernel(p_ref, pv_ref, nx_ref, cw_ref, cb_ref, wa_ref, ba_ref, wx_ref, bx_ref, lam_ref,
                    a_ref, b_ref, *, cl, lt, tr):
    rows = _row_ids(tr, pl.program_id(1))
    c = LRU_WIDTH
    x = p_ref[0][:, c:].astype(F32)
    pv = pv_ref[0][SUBLANES - 1:, c:].astype(F32)
    nx = nx_ref[0][:, c:].astype(F32)
    xc = (_shift_rows(x, pv, -1, rows, cl, lt) * cw_ref[0:1, :] + x * cw_ref[1:2, :]
          + _shift_rows(x, nx[:1], 1, rows, cl, lt) * cw_ref[2:3, :]
          + _shift_rows(x, nx[:2], 2, rows, cl, lt) * cw_ref[3:4, :] + cb_ref[...])
    xcb = xc.astype(BF16)
    for d in range(2):
        r = _sigmoid(_dot(xcb, wa_ref[d]) + ba_ref[d])
        i = _sigmoid(_dot(xcb, wx_ref[d]) + bx_ref[d])
        log_a = -LRU_C * r * _softplus(-lam_ref[d])
        a_ref[d, 0] = jnp.exp(log_a)
        b_ref[d, 0] = jnp.sqrt(1.0 - jnp.exp(2.0 * log_a)) * (i * xc)


def _lru_pre(plr, conv_w, conv_b, wa, ba, wx, bx, lam, cl):
    b, lt, w = plr.shape
    c = LRU_WIDTH
    tr = _largest_tile(lt, 1088, 16)
    prev_spec, next_spec = _halo_specs(tr, lt, w, (1,))
    out = jax.ShapeDtypeStruct((2, b, lt, c), F32)
    ospec = pl.BlockSpec((2, 1, tr, c), lambda i, r: (0, i, r, 0))
    const2 = lambda shape: pl.BlockSpec(shape, lambda i, r: (0,) * len(shape))
    return pl.pallas_call(
        functools.partial(_lru_pre_kernel, cl=cl, lt=lt, tr=tr),
        out_shape=(out, out), grid=(b, lt // tr),
        in_specs=[pl.BlockSpec((1, tr, w), lambda i, r: (i, r, 0)), prev_spec, next_spec,
                  const2(conv_w.shape), const2((1, c)), const2((2, c, c)), const2((2, 1, c)),
                  const2((2, c, c)), const2((2, 1, c)), const2((2, 1, c))],
        out_specs=(ospec, ospec),
        compiler_params=_cparams("parallel", "parallel"), name="lru_pre",
    )(plr, plr, plr, conv_w, conv_b.reshape(1, c), _blockdiag_dense(wa).astype(BF16), ba.reshape(2, 1, c),
      _blockdiag_dense(wx).astype(BF16), bx.reshape(2, 1, c), lam.reshape(2, 1, c))


def _gelu_tanh(x):
    return 0.5 * x * (1.0 + jnp.tanh(math.sqrt(2.0 / math.pi) * (x + 0.044715 * (x * x * x))))


def _lru_scan_kernel(af_ref, bf_ref, ar_ref, br_ref, g_ref, o_ref, acc_ref, *, cl, lt):
    row = lax.broadcasted_iota(jnp.int32, (SUBLANES, LANES), 0)

    def group_scan(a, b, reverse):
        for s in (1, 2, 4):
            if reverse:
                keep = row < SUBLANES - s
                a_s = jnp.where(keep, pltpu.roll(a, SUBLANES - s, 0), 1.0)
                b_s = jnp.where(keep, pltpu.roll(b, SUBLANES - s, 0), 0.0)
            else:
                keep = row >= s
                a_s = jnp.where(keep, pltpu.roll(a, s, 0), 1.0)
                b_s = jnp.where(keep, pltpu.roll(b, s, 0), 0.0)
            b = a * b_s + b
            a = a * a_s
        return a, b

    ng = math.gcd(math.gcd(cl // SUBLANES, (lt - cl) // SUBLANES), LRU_GROUPS_PER_STEP)
    span = ng * SUBLANES

    def fwd_body(i, h):
        sl = pl.ds(pl.multiple_of(i * span, span), span)
        a_all, b_all = af_ref[0, 0, sl, :], bf_ref[0, 0, sl, :]
        scans = [group_scan(a_all[j * SUBLANES:(j + 1) * SUBLANES], b_all[j * SUBLANES:(j + 1) * SUBLANES], False)
                 for j in range(ng)]
        outs = []
        for a, b in scans:
            hh = a * h + b
            outs.append(hh)
            h = hh[SUBLANES - 1:SUBLANES, :]
        acc_ref[sl, :] = jnp.concatenate(outs, axis=0)
        return h

    lax.fori_loop(0, lt // span, fwd_body, jnp.zeros((1, LANES), F32))

    def rev_body(i, h, top):
        sl = pl.ds(pl.multiple_of((top - 1 - i) * span, span), span)
        a_all, b_all = ar_ref[0, 0, sl, :], br_ref[0, 0, sl, :]
        scans = [group_scan(a_all[j * SUBLANES:(j + 1) * SUBLANES], b_all[j * SUBLANES:(j + 1) * SUBLANES], True)
                 for j in range(ng)]
        outs = [None] * ng
        for j in reversed(range(ng)):
            a, b = scans[j]
            hh = a * h + b
            outs[j] = hh
            h = hh[0:1, :]
        gate = g_ref[0, sl, :].astype(F32)
        o_ref[0, sl, :] = ((acc_ref[sl, :] + jnp.concatenate(outs, axis=0)) * _gelu_tanh(gate)).astype(o_ref.dtype)
        return h

    h = lax.fori_loop(0, cl // span, functools.partial(rev_body, top=cl // span), jnp.zeros((1, LANES), F32))
    lax.fori_loop(0, (lt - cl) // span, functools.partial(rev_body, top=lt // span), h)


def _lru_scan(a, b_, plr, cl):
    _, b, lt, c = a.shape
    nl = c // LANES
    fwd = pl.BlockSpec((1, 1, lt, LANES), lambda i, j: (0, i, 0, j))
    rev = pl.BlockSpec((1, 1, lt, LANES), lambda i, j: (1, i, 0, j))
    return pl.pallas_call(
        functools.partial(_lru_scan_kernel, cl=cl, lt=lt),
        out_shape=jax.ShapeDtypeStruct((b, lt, c), BF16), grid=(b, nl),
        in_specs=[fwd, fwd, rev, rev, pl.BlockSpec((1, lt, LANES), lambda i, j: (i, 0, j))],
        out_specs=pl.BlockSpec((1, lt, LANES), lambda i, j: (i, 0, j)),
        scratch_shapes=[pltpu.VMEM((lt, LANES), F32)],
        compiler_params=_cparams("parallel", "parallel"), name="lru_scan",
    )(a, b_, a, b_, plr)


def _row_perm(tr, rev):
    t = lax.broadcasted_iota(jnp.int32, (tr, tr), 0)
    s = lax.broadcasted_iota(jnp.int32, (tr, tr), 1)
    return jnp.where(s == jnp.where(rev, tr - 1 - t, t), 1.0, 0.0).astype(BF16)


def _mirror_tile(r, rev, n_ctx_tiles, n_tiles):
    m = jnp.where(r < n_ctx_tiles, n_ctx_tiles - 1 - r, n_tiles - 1 + n_ctx_tiles - r)
    return jnp.where(rev, m, r)


def _chunk_masks(tr):
    t = lax.broadcasted_iota(jnp.int32, (tr, tr), 0)
    s = lax.broadcasted_iota(jnp.int32, (tr, tr), 1)
    same = (t // RWKV_CHUNK) == (s // RWKV_CHUNK)
    return same, same & (s <= t), same & (s < t)


def _rwkv_prep_kernel(p_ref, pv_ref, nx_ref, mu_ref, w0_ref, w2_ref, a0_ref, a2_ref, g2_ref, kk_ref, ka_ref,
                      rk_ref, ones_ref,
                      aq_ref, vp_ref, y0_ref, rt_ref, mrb_ref, bht_ref, gm_ref, pc_ref, g_ref, bonus_ref,
                      *, cl, lt, tr):
    c = RWKV_WIDTH
    rows = _row_ids(tr, pl.program_id(2))
    rev = pl.program_id(0) == 1
    p = _dot(_row_perm(tr, rev), p_ref[0])
    before = jnp.where(rev, nx_ref[0, :1, :], pv_ref[0, SUBLANES - 1:, :]).astype(F32)
    after = jnp.where(rev, pv_ref[0, SUBLANES - 1:, :], nx_ref[0, :1, :]).astype(F32)
    prev = _shift_rows(p, before, -1, rows, cl, lt)
    nxt = _shift_rows(p, after, 1, rows, cl, lt)
    xm = p + (prev - p) * mu_ref[0, 0:1, :] + (nxt - p) * mu_ref[0, 1:2, :]
    r, k, v = xm[:, :c], xm[:, c:2 * c], xm[:, 2 * c:3 * c]
    o = 3 * c
    w1 = xm[:, o:o + 2 * RWKV_DECAY_RANK]
    a1 = xm[:, o + 2 * RWKV_DECAY_RANK:o + 2 * RWKV_DECAY_RANK + 2 * RWKV_ICLR_RANK]
    g1 = xm[:, o + 2 * RWKV_DECAY_RANK + 2 * RWKV_ICLR_RANK:]
    wlog = -_softplus(-(w0_ref[0] + _dot(jnp.tanh(w1).astype(BF16), w2_ref[0]))) - 0.5
    ld = -jnp.exp(wlog)
    a = _sigmoid(a0_ref[0] + _dot(a1.astype(BF16), a2_ref[0]))
    g_ref[0, 0] = _dot(_sigmoid(g1).astype(BF16), g2_ref[...])
    kk = k * kk_ref[...]
    kk = kk * lax.rsqrt(_dot_exact_rhs(kk * kk, ones_ref[...]) + 1e-12)
    kd = k * (1.0 + (a - 1.0) * ka_ref[...])
    bonus_ref[0, 0] = _dot_exact_rhs(r * kd * rk_ref[...], ones_ref[...]) * v
    beta = kk * a

    _rwkv_chunk_stage(ld, kk, r, kd, beta, v, aq_ref, vp_ref, y0_ref, rt_ref, mrb_ref, bht_ref, gm_ref, pc_ref)


def _rwkv_chunk_stage(ld, kk, r, kd, beta, v, aq_ref, vp_ref, y0_ref, rt_ref, mrb_ref, bht_ref, gm_ref, pc_ref):
    tr = ld.shape[0]
    n = min(RWKV_STAGE_ROWS, tr)
    nparts = tr // n
    hd = HEAD_DIM
    ch = RWKV_CHUNK
    nch = n // ch
    same_t, incl_t, _ = _chunk_masks(tr)
    cum = _dot_exact_lhs(jnp.where(incl_t, 1.0, 0.0).astype(BF16), ld)
    tot = _dot_exact_lhs(jnp.where(same_t, 1.0, 0.0).astype(BF16), ld)
    _, incl, strict = _chunk_masks(n)
    alpha_t = kk * jnp.exp(cum - ld)
    r_t = r * jnp.exp(cum)
    e_neg = jnp.exp(-cum)
    k_t = kd * e_neg
    b_t = beta * e_neg
    e_rem = jnp.exp(tot - cum)
    k_hat_t = (kd * e_rem).T
    b_hat_t = (beta * e_rem).T
    pc_t = jnp.exp(tot).T
    t_i = lax.broadcasted_iota(jnp.int32, (n, n), 0)
    s_i = lax.broadcasted_iota(jnp.int32, (n, n), 1)
    eye_f = jnp.where(t_i == s_i, 1.0, 0.0)
    same_blk = []
    size = RWKV_INV_BASE
    while size <= ch:
        same_blk.append((t_i // size) == (s_i // size))
        size *= 2
    col_chunk = lax.broadcasted_iota(jnp.int32, (n, nch * hd), 1) // hd
    row_chunk = lax.broadcasted_iota(jnp.int32, (n, nch * hd), 0) // ch

    def diag_blocks(m):
        out = m[:, :ch]
        for j in range(1, nch):
            out = out + m[:, j * ch:(j + 1) * ch]
        return out

    chains = [(p, h) for p in range(nparts) for h in range(RWKV_HEADS)]

    def blk(t, p, h):
        return t[p * n:(p + 1) * n, h * hd:(h + 1) * hd]

    prods = [_dot_nt(jnp.concatenate([blk(alpha_t, p, h), blk(r_t, p, h)], axis=0).astype(BF16),
                     jnp.concatenate([blk(b_t, p, h), blk(k_t, p, h)], axis=0).astype(BF16))
             for p, h in chains]
    l_ab = [jnp.where(strict, pr[:n, :n], 0.0) for pr in prods]
    pw = [jnp.where(same_blk[0], l, 0.0) for l in l_ab]
    t_inv = [eye_f - l for l in pw]
    for _ in range(int(math.log2(RWKV_INV_BASE)) - 1):
        pw = [_dot_bf(q, q) for q in pw]
        t_inv = [t + _dot_bf(t, q) for t, q in zip(t_inv, pw)]
    for lvl in range(1, len(same_blk)):
        off = same_blk[lvl] & jnp.logical_not(same_blk[lvl - 1])
        half = [_dot_bf(t, jnp.where(off, l, 0.0)) for t, l in zip(t_inv, l_ab)]
        t_inv = [t - _dot_bf(hf, t) for t, hf in zip(t_inv, half)]
    vh = [blk(v, p, h).astype(BF16) for p, h in chains]
    lakv = [_dot(jnp.where(strict, pr[:n, n:], 0.0).astype(BF16), vv) for pr, vv in zip(prods, vh)]
    x = [_dot(t.astype(BF16), jnp.concatenate([blk(alpha_t, p, h), lv], axis=1).astype(BF16))
         for t, (p, h), lv in zip(t_inv, chains, lakv)]
    y0 = [_dot(jnp.where(incl, pr[n:, n:], 0.0).astype(BF16), vv) for pr, vv in zip(prods, vh)]
    mrb = [diag_blocks(jnp.where(incl, pr[n:, :n], 0.0)) for pr in prods]
    gms = [_dot(k_hat_t[h * hd:(h + 1) * hd, p * n:(p + 1) * n].astype(BF16),
                jnp.where(col_chunk == row_chunk, jnp.concatenate([blk(v, p, h)] * nch, axis=1), 0.0).astype(BF16))
           for p, h in chains]

    def assemble(parts):
        return jnp.concatenate([jnp.concatenate(parts[p * RWKV_HEADS:(p + 1) * RWKV_HEADS], axis=1)
                                for p in range(nparts)], axis=0)

    aq_ref[0, 0] = assemble([t[:, :hd] for t in x]).astype(aq_ref.dtype)
    vp_ref[0, 0] = assemble([t[:, hd:] for t in x])
    y0_ref[0, 0] = assemble(y0)
    rt_ref[0, 0] = r_t.astype(rt_ref.dtype)
    mrb_ref[0, 0] = assemble(mrb).astype(mrb_ref.dtype)
    for p in range(nparts):
        for j in range(nch):
            jj = p * nch + j
            cs = slice(jj * ch, (jj + 1) * ch)
            bht_ref[0, 0, jj] = jnp.concatenate([b_hat_t[h * hd:(h + 1) * hd, cs] for h in range(RWKV_HEADS)],
                                                axis=1).astype(bht_ref.dtype)
            gm_ref[0, 0, jj] = jnp.concatenate([gms[p * RWKV_HEADS + h][:, j * hd:(j + 1) * hd]
                                                for h in range(RWKV_HEADS)], axis=1)
            pc_ref[0, 0, jj] = jnp.concatenate([pc_t[h * hd:(h + 1) * hd, cs] for h in range(RWKV_HEADS)], axis=1)


def _rwkv_tile(lt, cl):
    tr = 256 if (lt % 256 == 0 and cl % 256 == 0) else 128
    assert lt % tr == 0 and cl % tr == 0
    return tr


def _rwkv_prep(prw, mud, w0, w2p, a0, a2p, g2, k_k, k_a, r_k, cl):
    b, lt, w = prw.shape
    c = RWKV_WIDTH
    tr = _rwkv_tile(lt, cl)
    nch = tr // RWKV_CHUNK
    per = tr // SUBLANES
    src = lambda d, r: _mirror_tile(r, d == 1, cl // tr, lt // tr)
    prev_spec = pl.BlockSpec((1, SUBLANES, w), lambda d, i, r: (i, jnp.maximum(src(d, r) * per - 1, 0), 0))
    next_spec = pl.BlockSpec((1, SUBLANES, w),
                             lambda d, i, r: (i, jnp.minimum((src(d, r) + 1) * per, lt // SUBLANES - 1), 0))
    tile = pl.BlockSpec((1, 1, tr, c), lambda d, i, r: (d, i, r, 0))
    per_dir = lambda shape: pl.BlockSpec((1, *shape), lambda d, i, r: (d,) + (0,) * len(shape))
    const = lambda shape: pl.BlockSpec(shape, lambda d, i, r: (0,) * len(shape))
    seq = lambda dt: jax.ShapeDtypeStruct((2, b, lt, c), dt)
    chunked = jax.ShapeDtypeStruct((2, b, lt // RWKV_CHUNK, HEAD_DIM, c), F32)
    chunk_spec = pl.BlockSpec((1, 1, nch, HEAD_DIM, c), lambda d, i, r: (d, i, r, 0, 0))
    return pl.pallas_call(
        functools.partial(_rwkv_prep_kernel, cl=cl, lt=lt, tr=tr),
        out_shape=(seq(BF16), seq(F32), seq(F32), seq(BF16), seq(BF16),
                   jax.ShapeDtypeStruct(chunked.shape, BF16), chunked, chunked, seq(F32), seq(F32)),
        grid=(2, b, lt // tr),
        in_specs=[pl.BlockSpec((1, tr, w), lambda d, i, r: (i, src(d, r), 0)), prev_spec, next_spec,
                  per_dir((2, w)), per_dir((1, c)), per_dir((2 * RWKV_DECAY_RANK, c)), per_dir((1, c)),
                  per_dir((2 * RWKV_ICLR_RANK, c)), const((RWKV_GATE_RANK, c)), const((1, c)), const((1, c)),
                  const((1, c)), const((c, c))],
        out_specs=(tile, tile, tile, tile, tile, chunk_spec, chunk_spec, chunk_spec, tile, tile),
        compiler_params=_cparams("parallel", "parallel", "parallel"), name="rwkv_prep",
    )(prw, prw, prw, mud, w0.reshape(2, 1, c), w2p, a0.reshape(2, 1, c), a2p, g2.astype(BF16), k_k.reshape(1, c),
      k_a.reshape(1, c), r_k.reshape(1, c), _head_ones(c))


def _rwkv_scan_kernel(aq_ref, vp_ref, y0_ref, rt_ref, mrb_ref, bht_ref, gm_ref, pc_ref, y_ref, s_ref, *, ns):
    @pl.when(pl.program_id(1) == 0)
    def _():
        s_ref[...] = jnp.zeros_like(s_ref)

    c = RWKV_WIDTH
    ch = RWKV_CHUNK
    same_head = (lax.broadcasted_iota(jnp.int32, (c, c), 0) // HEAD_DIM
                 == lax.broadcasted_iota(jnp.int32, (c, c), 1) // HEAD_DIM)

    def head_blockdiag(t):
        return jnp.where(same_head, jnp.concatenate([t] * RWKV_HEADS, axis=0), 0.0).astype(BF16)

    for i in range(ns):
        s0 = s_ref[i]
        r1 = _dot(jnp.concatenate([aq_ref[i, 0], rt_ref[i, 0]], axis=0), head_blockdiag(s0))
        u = r1[:ch] + vp_ref[i, 0]
        r2 = _dot(jnp.concatenate([mrb_ref[i, 0], bht_ref[i, 0, 0]], axis=0), head_blockdiag(u))
        y_ref[i, 0] = r1[ch:] + y0_ref[i, 0] - r2[:ch]
        s_ref[i] = pc_ref[i, 0, 0] * s0 + gm_ref[i, 0, 0] - r2[ch:]


def _rwkv_scan(aq, vp, y0, rt, mrb, bht, gm, pc):
    _, b, lt, c = aq.shape
    nstream = 2 * b
    ns = nstream
    ch = RWKV_CHUNK
    merge = lambda t: t.reshape(nstream, 1, *t.shape[2:])
    tile = pl.BlockSpec((ns, 1, ch, c), lambda s, j: (s, 0, j, 0))
    chunk_spec = pl.BlockSpec((ns, 1, 1, HEAD_DIM, c), lambda s, j: (s, 0, j, 0, 0))
    y = pl.pallas_call(
        functools.partial(_rwkv_scan_kernel, ns=ns),
        out_shape=jax.ShapeDtypeStruct((nstream, 1, lt, c), F32), grid=(nstream // ns, lt // ch),
        in_specs=[tile, tile, tile, tile, tile, chunk_spec, chunk_spec, chunk_spec],
        out_specs=tile,
        scratch_shapes=[pltpu.VMEM((ns, HEAD_DIM, c), F32)],
        compiler_params=_cparams("parallel", "arbitrary"), name="rwkv_scan",
    )(*(merge(t) for t in (aq, vp, y0, rt, mrb, bht, gm, pc)))
    return y.reshape(2, b, lt, c)


def _rwkv_readout_kernel(yf_ref, yr_ref, bf_ref, br_ref, g_ref, lw_ref, lb_ref, ones_ref, o_ref, *, tr):
    unflip = _row_perm(tr, True)
    y = yf_ref[0, 0] + _dot_exact_lhs(unflip, yr_ref[0, 0])
    bonus = bf_ref[0, 0] + _dot_exact_lhs(unflip, br_ref[0, 0])
    inv = 1.0 / HEAD_DIM
    mu = _dot_exact_rhs(y, ones_ref[...]) * inv
    yc = y - mu
    var = _dot_exact_rhs(yc * yc, ones_ref[...]) * inv
    yn = yc * lax.rsqrt(var + RWKV_GN_EPS) * lw_ref[...] + lb_ref[...]
    o_ref[0] = ((yn + bonus) * g_ref[0, 0]).astype(o_ref.dtype)


def _rwkv_readout(y, bonus, g, ln_w, ln_b, cl):
    _, b, lt, c = y.shape
    tr = _rwkv_tile(lt, cl)
    fwd = pl.BlockSpec((1, 1, tr, c), lambda i, r: (0, i, r, 0))
    rev = pl.BlockSpec((1, 1, tr, c), lambda i, r: (1, i, _mirror_tile(r, True, cl // tr, lt // tr), 0))
    row = pl.BlockSpec((1, c), lambda i, r: (0, 0))
    return pl.pallas_call(
        functools.partial(_rwkv_readout_kernel, tr=tr),
        out_shape=jax.ShapeDtypeStruct((b, lt, c), BF16), grid=(b, lt // tr),
        in_specs=[fwd, rev, fwd, rev, fwd, row, row, pl.BlockSpec((c, c), lambda i, r: (0, 0))],
        out_specs=pl.BlockSpec((1, tr, c), lambda i, r: (i, r, 0)),
        compiler_params=_cparams("parallel", "parallel"), name="rwkv_readout",
    )(y, y, bonus, bonus, g, ln_w.reshape(1, c), ln_b.reshape(1, c), _head_ones(c))


def _pad_rank_rows(w):
    z = jnp.zeros_like(w[0])
    return jnp.stack([jnp.concatenate([w[0], z], 0), jnp.concatenate([z, w[1]], 0)])


def _rwkv_mixer(prw, mu, w0, w2, a0, a2, g2, k_k, k_a, r_k, ln_w, ln_b, cl):
    mud = jnp.stack([mu, mu[::-1]])
    outs = _rwkv_prep(prw, mud, w0, _pad_rank_rows(w2).astype(BF16), a0, _pad_rank_rows(a2).astype(BF16), g2,
                      k_k, k_a, r_k, cl)
    *scan_in, g, bonus = outs
    y = _rwkv_scan(*scan_in)
    return _rwkv_readout(y, bonus, g, ln_w, ln_b, cl)


def _merge_kernel(ya_ref, yh_ref, yr_ref, yl_ref, gt_ref, x_ref, modx_ref, modc_ref,
                  wa_ref, wh_ref, wr_ref, wl_ref, wo_ref, rth_ref, rtl_ref, rtb_ref, o_ref, h_ref, cmb_ref, *, cl, tm, d):
    is_ctx = _row_ids(tm, pl.program_id(1)) < cl
    m = None
    for i, (y_ref, w_ref) in enumerate(((ya_ref, wa_ref), (yh_ref, wh_ref), (yr_ref, wr_ref), (yl_ref, wl_ref))):
        gate = _sigmoid(gt_ref[0, :, i * d:(i + 1) * d].astype(F32))
        term = gate * _dot(y_ref[0], w_ref[...])
        m = term if m is None else m + term
    g1 = _mod_rows(modx_ref, modc_ref, 2, is_ctx)
    x_new = x_ref[0] + g1 * _dot(m.astype(BF16), wo_ref[...])
    o_ref[0] = x_new
    h, cmb = _route_rows(x_new, is_ctx, modx_ref, modc_ref, rth_ref, rtl_ref, rtb_ref)
    h_ref[0] = h
    cmb_ref[0] = cmb


def _merge(ys, gates, xc, mod, w_brs, w_out, router, cl):
    b, lt, d = xc.shape
    tm = _largest_tile(lt, 544, 16)
    row = lambda w: pl.BlockSpec((1, tm, w), lambda i, r: (i, r, 0))
    const = lambda a: pl.BlockSpec(a.shape, lambda i, r: (0, 0))
    ws = [w.astype(BF16) for w in w_brs] + [w_out.astype(BF16)] + list(router)
    return pl.pallas_call(
        functools.partial(_merge_kernel, cl=cl, tm=tm, d=d),
        out_shape=(jax.ShapeDtypeStruct((b, lt, d), F32), jax.ShapeDtypeStruct((b, lt, d), BF16),
                   jax.ShapeDtypeStruct((b, lt, LANES), F32)),
        grid=(b, lt // tm),
        in_specs=[row(y.shape[-1]) for y in ys] + [row(N_BRANCH * d), row(d),
                  pl.BlockSpec((1, 6, d), lambda i, r: (i, 0, 0)), pl.BlockSpec((1, 6, d), lambda i, r: (b, 0, 0))]
                 + [const(w) for w in ws],
        out_specs=(row(d), row(d), row(LANES)), compiler_params=_cparams("parallel", "parallel"), name="merge",
    )(*ys, gates, xc, mod, mod, *ws)


def _route_rows(x, is_ctx, modx_ref, modc_ref, wrh_ref, wrl_ref, br_ref):
    h = _rms_modulate(x, _mod_rows(modx_ref, modc_ref, 3, is_ctx), _mod_rows(modx_ref, modc_ref, 4, is_ctx))
    h_hi = h.astype(BF16)
    h_lo = (h - h_hi.astype(F32)).astype(BF16)
    lg = _dot(h_hi, wrh_ref[...]) + (_dot(h_hi, wrl_ref[...]) + _dot(h_lo, wrh_ref[...])) + br_ref[...]
    lane = lax.broadcasted_iota(jnp.int32, lg.shape, 1)
    lane_f = lane.astype(F32)
    neg = -jnp.inf
    big = 1e9

    def first_lane(cond):
        return jnp.min(jnp.where(cond, lane_f, big), axis=-1, keepdims=True)

    is_grp = (lane >= N_EXPERTS) & (lane < N_EXPERTS + N_GROUPS)
    gl = jnp.where(is_grp, lg, neg)
    gmax = jnp.max(gl, axis=-1, keepdims=True)
    ge = jnp.where(is_grp, jnp.exp(gl - gmax), 0.0)
    gp = ge / jnp.sum(ge, axis=-1, keepdims=True)
    g_val = jnp.max(gp, axis=-1, keepdims=True)
    g_idx = first_lane(is_grp & (gp == g_val)) - N_EXPERTS
    lo = g_idx * EXPERTS_PER_GROUP
    in_grp = (lane_f >= lo) & (lane_f < lo + EXPERTS_PER_GROUP)
    el = jnp.where(in_grp, lg, neg)
    emax = jnp.max(el, axis=-1, keepdims=True)
    ee = jnp.where(in_grp, jnp.exp(el - emax), 0.0)
    pe = ee / jnp.sum(ee, axis=-1, keepdims=True)
    v1 = jnp.max(jnp.where(in_grp, pe, -1.0), axis=-1, keepdims=True)
    i1 = first_lane(in_grp & (pe == v1))
    rest = in_grp & (lane_f != i1)
    v2 = jnp.max(jnp.where(rest, pe, -1.0), axis=-1, keepdims=True)
    i2 = first_lane(rest & (pe == v2))
    den = v1 + v2
    cmb = (jnp.where(lane_f == i1, g_val * v1 / den, 0.0) + jnp.where(lane_f == i2, g_val * v2 / den, 0.0)
           + jnp.where(lane == N_EXPERTS, g_idx, 0.0))
    return h_hi, cmb


def _router_weights(w_grp, b_grp, w_rt, b_rt):
    d = w_rt.shape[0]
    wr = jnp.zeros((d, LANES), F32).at[:, :N_EXPERTS].set(w_rt).at[:, N_EXPERTS:N_EXPERTS + N_GROUPS].set(w_grp)
    br = jnp.zeros((1, LANES), F32).at[0, :N_EXPERTS].set(b_rt).at[0, N_EXPERTS:N_EXPERTS + N_GROUPS].set(b_grp)
    hi = wr.astype(BF16)
    return hi, (wr - hi.astype(F32)).astype(BF16), br


def _moe_kernel(grp_ref, nvalid_ref, h_ref, cmb_ref, w1_ref, w3_ref, w2_ref, o_ref, acc_ref, *, bm):
    j = pl.program_id(0)
    e = pl.program_id(1)

    @pl.when(e == 0)
    def _():
        acc_ref[...] = jnp.zeros_like(acc_ref)

    @pl.when(j < nvalid_ref[0])
    def _():
        h = h_ref[...]
        t = _silu(_dot(h, w1_ref[0, 0])) * _dot(h, w3_ref[0, 0])
        y = _dot(t.astype(BF16), w2_ref[0, 0])
        lane = lax.broadcasted_iota(jnp.int32, (bm, LANES), 1)
        expert = grp_ref[j] * EXPERTS_PER_GROUP + e
        wcol = jnp.sum(jnp.where(lane == expert, cmb_ref[...], 0.0), axis=-1, keepdims=True)
        acc_ref[...] += wcol * y

    @pl.when(e == pl.num_programs(1) - 1)
    def _():
        o_ref[...] = acc_ref[...].astype(o_ref.dtype)


def _moe_grouped(hs, ws, blk_group, nvalid, w1, w3, w2, layer):
    s_rows, d = hs.shape
    hid = w1.shape[3]
    bm = MOE_BLOCK_ROWS
    wspec = lambda shape: pl.BlockSpec(shape, lambda j, e, grp, nv: (layer, grp[j] * EXPERTS_PER_GROUP + e, 0, 0))
    return pl.pallas_call(
        functools.partial(_moe_kernel, bm=bm),
        out_shape=jax.ShapeDtypeStruct((s_rows, d), BF16),
        grid_spec=pltpu.PrefetchScalarGridSpec(
            num_scalar_prefetch=2, grid=(s_rows // bm, EXPERTS_PER_GROUP),
            in_specs=[pl.BlockSpec((bm, d), lambda j, e, grp, nv: (j, 0)),
                      pl.BlockSpec((bm, LANES), lambda j, e, grp, nv: (j, 0)),
                      wspec((1, 1, d, hid)), wspec((1, 1, d, hid)), wspec((1, 1, hid, d))],
            out_specs=pl.BlockSpec((bm, d), lambda j, e, grp, nv: (j, 0)),
            scratch_shapes=[pltpu.VMEM((bm, d), F32)]),
        compiler_params=_cparams("parallel", "arbitrary"), name="moe_experts",
    )(blk_group, nvalid, hs, ws, w1, w3, w2)


def _moe_finish_kernel(x_ref, y_ref, modx_ref, modc_ref, o_ref, *, cl, tm, first):
    is_ctx = _row_ids(tm, pl.program_id(1) + first) < cl
    o_ref[0] = x_ref[0] + _mod_rows(modx_ref, modc_ref, 5, is_ctx) * y_ref[0].astype(F32)


def _moe_finish_next_kernel(x_ref, y_ref, modx_ref, modc_ref, nmodx_ref, nmodc_ref, o_ref, h_ref, *, cl, tm):
    is_ctx = _row_ids(tm, pl.program_id(1)) < cl
    x_new = x_ref[0] + _mod_rows(modx_ref, modc_ref, 5, is_ctx) * y_ref[0].astype(F32)
    o_ref[0] = x_new
    h = _rms_modulate(x_new, _mod_rows(nmodx_ref, nmodc_ref, 0, is_ctx), _mod_rows(nmodx_ref, nmodc_ref, 1, is_ctx))
    h_ref[0] = h.astype(h_ref.dtype)


def _moe_finish_next(xc, y, mod, next_mod, cl):
    b, lt, d = xc.shape
    tm = _largest_tile(lt, 1088, 16)
    tile = pl.BlockSpec((1, tm, d), lambda i, r: (i, r, 0))
    modx = pl.BlockSpec((1, 6, d), lambda i, r: (i, 0, 0))
    modc = pl.BlockSpec((1, 6, d), lambda i, r: (b, 0, 0))
    return pl.pallas_call(
        functools.partial(_moe_finish_next_kernel, cl=cl, tm=tm),
        out_shape=(jax.ShapeDtypeStruct((b, lt, d), F32), jax.ShapeDtypeStruct((b, lt, d), BF16)), grid=(b, lt // tm),
        in_specs=[tile, tile, modx, modc, modx, modc], out_specs=(tile, tile),
        compiler_params=_cparams("parallel", "parallel"), name="moe_finish_next",
    )(xc, y, mod, mod, next_mod, next_mod)


def _moe_finish(xc, y, mod, cl, latent_only):
    b, lt, d = xc.shape
    if latent_only:
        tm = _largest_tile(math.gcd(cl, lt - cl), 1088, 16)
        first = cl // tm
    else:
        tm, first = _largest_tile(lt, 1088, 16), 0
    src = pl.BlockSpec((1, tm, d), lambda i, r: (i, r + first, 0))
    n_rows = lt - first * tm
    return pl.pallas_call(
        functools.partial(_moe_finish_kernel, cl=cl, tm=tm, first=first),
        out_shape=jax.ShapeDtypeStruct((b, n_rows, d), F32), grid=(b, n_rows // tm),
        in_specs=[src, src, pl.BlockSpec((1, 6, d), lambda i, r: (i, 0, 0)),
                  pl.BlockSpec((1, 6, d), lambda i, r: (b, 0, 0))],
        out_specs=pl.BlockSpec((1, tm, d), lambda i, r: (i, r, 0)),
        compiler_params=_cparams("parallel", "parallel"), name="moe_finish",
    )(xc, y, mod, mod)


def _moe(h2, cmb, xc, mod, w1, w3, w2, layer, cl, next_mod):
    b, lt, d = xc.shape
    t = b * lt
    bm = MOE_BLOCK_ROWS
    s_rows = -(-t // bm) * bm + N_GROUPS * bm
    cmb2 = cmb.reshape(t, LANES)
    gid = cmb2[:, N_EXPERTS].astype(jnp.int32)
    onehot = (gid[:, None] == jnp.arange(N_GROUPS, dtype=jnp.int32)).astype(jnp.int32)
    csum = jnp.cumsum(onehot, axis=0)
    rank = jnp.sum(onehot * csum, axis=1) - 1
    padded = -(-csum[-1] // bm) * bm
    ends = jnp.cumsum(padded)
    pos = (ends - padded)[gid] + rank
    slot_token = jnp.zeros((s_rows,), jnp.int32).at[pos].set(jnp.arange(t, dtype=jnp.int32))
    starts = jnp.arange(s_rows // bm, dtype=jnp.int32) * bm
    blk_group = jnp.minimum(jnp.sum((starts[:, None] >= ends[None, :]).astype(jnp.int32), axis=1), N_GROUPS - 1)
    nvalid = (ends[-1] // bm).reshape(1).astype(jnp.int32)
    take_rows = lambda a, idx: a.at[idx].get(mode="promise_in_bounds")
    hs = take_rows(h2.reshape(t, d), slot_token)
    ws = take_rows(cmb2, slot_token)
    ys = _moe_grouped(hs, ws, blk_group.astype(jnp.int32), nvalid, w1, w3, w2, layer)
    y = take_rows(ys, pos).reshape(b, lt, d)
    if next_mod is None:
        return _moe_finish(xc, y, mod, cl, True), None
    return _moe_finish_next(xc, y, mod, next_mod, cl)


def kernel(x, c, ctx, c_ctx, ada_w, ada_b, w_in, q_norm, k_norm, hy_conv_w, hy_conv_b, hy_f1, hy_fb1, hy_f2, hy_fb2, hy_f3, hy_skip, rw_mu, rw_w0, rw_w2, rw_a0, rw_a2, rw_g2, rw_k_k, rw_k_a, rw_r_k, rw_ln_w, rw_ln_b, lru_conv_w, lru_conv_b, lru_wa, lru_ba, lru_wx, lru_bx, lru_lambda, w_br_attn, w_br_hyena, w_br_rwkv, w_br_lru, w_out, moe_w_grp, moe_b_grp, moe_w_rt, moe_b_rt, moe_w1, moe_w3, moe_w2):
    b, l, d = x.shape
    cl = ctx.shape[1]
    depth = ada_w.shape[0]
    assert b < MOD_ROWS and cl % RWKV_CHUNK == 0 and l % RWKV_CHUNK == 0

    xc = jnp.concatenate([ctx, x], axis=1)
    cc = jnp.zeros((MOD_ROWS, d), F32).at[:b].set(c).at[b].set(c_ctx)
    mod_all = _ada_mod(cc, ada_w, ada_b).reshape(depth, MOD_ROWS, 6, d)

    moe_w1b, moe_w3b, moe_w2b = (w.astype(BF16) for w in (moe_w1, moe_w3, moe_w2))
    cos2, sin2 = _rope_tables(l, cl)
    mats_x = _dft_mats(l)
    mats_c = _dft_mats(cl)
    qkv_w = ATTN_WIDTH + 2 * ATTN_KV_WIDTH
    col = np.cumsum([0, qkv_w, 3 * HYENA_WIDTH, RWKV_PROJ, 2 * LRU_WIDTH, N_BRANCH * d])

    for i in range(depth):
        need_ctx = i < depth - 1
        mod = mod_all[i]
        w_i = w_in[i].astype(BF16)
        if i == 0:
            h1 = _modnorm(xc, mod, cl)
        pqkv, phy, prw, plr, gates = (_proj(h1, w_i[:, col[j]:col[j + 1]]) for j in range(5))

        qn, kt, vx = _attn_prep(pqkv, cos2, sin2, q_norm[i], k_norm[i])
        y_att_x = _attention(qn, kt, vx, cl, 'x', cl + l)
        if need_ctx:
            y_att_c = _attention(qn, kt, vx, cl, 'ctx', cl)
        else:
            y_att_c = jnp.zeros((b, cl, ATTN_WIDTH), BF16)
        y_att = jnp.concatenate([y_att_c, y_att_x], axis=1)

        hv, hx1, hx2 = _hyena_pre(phy, hy_conv_w[i], hy_conv_b[i], cl)
        filt = (hy_f1[i], hy_fb1[i], hy_f2[i], hy_fb2[i], hy_f3[i])
        y_hx = _hyena_run(hv[:, cl:], hx1[:, cl:], hx2[:, cl:], _hyena_spectra(l, filt, mats_x), hy_skip[i], mats_x)
        if need_ctx:
            y_hc = _hyena_run(hv[:, :cl], hx1[:, :cl], hx2[:, :cl], _hyena_spectra(cl, filt, mats_c), hy_skip[i],
                              mats_c)
        else:
            y_hc = jnp.zeros((b, cl, HYENA_WIDTH), BF16)
        y_hy = jnp.concatenate([y_hc, y_hx], axis=1)

        y_rw = _rwkv_mixer(prw, rw_mu[i], rw_w0[i], rw_w2[i], rw_a0[i], rw_a2[i], rw_g2[i], rw_k_k[i], rw_k_a[i],
                           rw_r_k[i].reshape(-1), rw_ln_w[i], rw_ln_b[i], cl)

        la, lb = _lru_pre(plr, lru_conv_w[i], lru_conv_b[i], lru_wa[i], lru_ba[i], lru_wx[i], lru_bx[i],
                          lru_lambda[i], cl)
        y_lr = _lru_scan(la, lb, plr, cl)

        xc, h2, cmb = _merge((y_att, y_hy, y_rw, y_lr), gates, xc, mod,
                             (w_br_attn[i], w_br_hyena[i], w_br_rwkv[i], w_br_lru[i]), w_out[i],
                             _router_weights(moe_w_grp[i], moe_b_grp[i], moe_w_rt[i], moe_b_rt[i]), cl)
        xc, h1 = _moe(h2, cmb, xc, mod, moe_w1b, moe_w3b, moe_w2b, i, cl, mod_all[i + 1] if need_ctx else None)
    return xc
```
